```python
import math
import jax
import jax.numpy as jnp
from jax import lax
import numpy as np

D_MODEL = 1024
BATCH = 8
SEQ = 2048
DEPTH = 2

F32 = jnp.float32
GRID_W = 64
CTX_LEN = 256
NORM_EPS = 1e-6

S5_WIDTH = D_MODEL // 4
S5_GROUP_CH = 16
S5_GROUPS = S5_WIDTH // S5_GROUP_CH
S5_STATE = 64

MLA_HEADS = D_MODEL // 128
MLA_NOPE = 64
MLA_ROPE = 32
MLA_V = 64
MLA_Q_LORA = D_MODEL // 4
MLA_KV_LORA = D_MODEL // 8
MLA_Q_BLOCK = 128
MLA_SCALE = 1.0 / math.sqrt(MLA_NOPE + MLA_ROPE)
ROPE_PAIRS = MLA_ROPE // 4
ROPE_BASE = 10000.0

HG_HEADS = D_MODEL // 256
HG_K = 128
HG_V = 64
HG_CHUNK = 64

MOE_GROUPS = 4
MOE_EXPERTS_PER_GROUP = 8
MOE_EXPERTS = MOE_GROUPS * MOE_EXPERTS_PER_GROUP
MOE_TOP_K = 2
MOE_HIDDEN = D_MODEL // 2
MOE_BLOCK = 128

IN_SPLITS = (S5_WIDTH, MLA_Q_LORA, MLA_KV_LORA, MLA_ROPE,
             HG_HEADS * HG_K, HG_HEADS * HG_K, HG_HEADS * HG_K,
             HG_HEADS * HG_V, HG_HEADS * HG_V,
             D_MODEL, D_MODEL, D_MODEL)
IN_COLS = sum(IN_SPLITS)

kernel_name = 'hybrid_s5_mla_hgrn2_hmoe_dit'


def rms_norm(x, g):
    xf = x.astype(F32)
    y = xf * lax.rsqrt(jnp.mean(xf * xf, axis=-1, keepdims=True) + NORM_EPS)
    return (y * g.astype(F32)).astype(x.dtype)


def modulate(h, shift, scale):
    return h * (1 + scale) + shift


def split_columns(z):
    cuts = [int(v) for v in np.cumsum(IN_SPLITS)[:-1]]
    return jnp.split(z, cuts, axis=-1)


def rev(a, direction, axis):
    return jnp.flip(a, axis=axis) if direction == 1 else a


def axial_rope_angles(n_tokens):
    rows = n_tokens // GRID_W
    row = jnp.repeat(jnp.arange(rows, dtype=F32), GRID_W)
    col = jnp.tile(jnp.arange(GRID_W, dtype=F32), rows)
    inv = ROPE_BASE ** (-jnp.arange(ROPE_PAIRS, dtype=F32) / ROPE_PAIRS)
    ang = jnp.stack([row[:, None] * inv, col[:, None] * inv], axis=1)
    return jnp.cos(ang), jnp.sin(ang)


def apply_axial_rope(x, cos, sin):
    shp = x.shape
    xr = x.astype(F32).reshape(shp[:-1] + (2, 2, ROPE_PAIRS))
    x1, x2 = xr[..., 0, :], xr[..., 1, :]
    out = jnp.stack([x1 * cos - x2 * sin, x1 * sin + x2 * cos], axis=-2)
    return out.reshape(shp).astype(x.dtype)


def s5_discretise(lam_re, lam_im, b_re, b_im, log_step):
    lam_re, lam_im = lam_re.astype(F32), lam_im.astype(F32)
    b_re, b_im = b_re.astype(F32), b_im.astype(F32)
    dt = jnp.exp(log_step.astype(F32))[:, None]
    mag = jnp.exp(lam_re * dt)
    lb_re, lb_im = mag * jnp.cos(lam_im * dt), mag * jnp.sin(lam_im * dt)
    den = lam_re * lam_re + lam_im * lam_im
    fr = ((lb_re - 1) * lam_re + lb_im * lam_im) / den
    fi = (lb_im * lam_re - (lb_re - 1) * lam_im) / den
    bb_re = fr[..., None] * b_re - fi[..., None] * b_im
    bb_im = fr[..., None] * b_im + fi[..., None] * b_re
    return lb_re, lb_im, bb_re, bb_im


def s5_states(u, lb_re, lb_im, bb_re, bb_im, h0_re, h0_im):
    bu_re = jnp.einsum('blgh,gph->blgp', u, bb_re)
    bu_im = jnp.einsum('blgh,gph->blgp', u, bb_im)
    bu_re = bu_re.at[:, 0].add(lb_re * h0_re - lb_im * h0_im)
    bu_im = bu_im.at[:, 0].add(lb_re * h0_im + lb_im * h0_re)
    a_re = jnp.broadcast_to(lb_re, bu_re.shape)
    a_im = jnp.broadcast_to(lb_im, bu_im.shape)

    def combine(e1, e2):
        a1r, a1i, b1r, b1i = e1
        a2r, a2i, b2r, b2i = e2
        return (a2r * a1r - a2i * a1i, a2r * a1i + a2i * a1r,
                a2r * b1r - a2i * b1i + b2r, a2r * b1i + a2i * b1r + b2i)

    _, _, h_re, h_im = lax.associative_scan(combine, (a_re, a_im, bu_re, bu_im), axis=1)
    return h_re, h_im


def s5_readout(h_re, h_im, c_re, c_im):
    return jnp.einsum('blgp,ghp->blgh', h_re, c_re) - jnp.einsum('blgp,ghp->blgh', h_im, c_im)


def s5_mixer(u_lat, u_ctx, lam_re, lam_im, b_re, b_im, c_re, c_im, log_step, d_skip,
             glu_w, glu_b, need_ctx):
    def grouped(u):
        return u.astype(F32).reshape(u.shape[0], u.shape[1], S5_GROUPS, S5_GROUP_CH)

    ul, uc = grouped(u_lat), grouped(u_ctx)
    zero = jnp.zeros((ul.shape[0], S5_GROUPS, S5_STATE), F32)
    y_lat, y_ctx = 0.0, 0.0
    for d in range(2):
        lb_re, lb_im, bb_re, bb_im = s5_discretise(lam_re[d], lam_im[d], b_re[d], b_im[d], log_step[d])
        cr, ci = c_re[d].astype(F32), c_im[d].astype(F32)
        hc_re, hc_im = s5_states(rev(uc, d, 1), lb_re, lb_im, bb_re, bb_im, zero, zero)
        hl_re, hl_im = s5_states(rev(ul, d, 1), lb_re, lb_im, bb_re, bb_im, hc_re[:, -1], hc_im[:, -1])
        y_lat = y_lat + rev(s5_readout(hl_re, hl_im, cr, ci), d, 1)
        if need_ctx:
            y_ctx = y_ctx + rev(s5_readout(hc_re, hc_im, cr, ci), d, 1)

    def finish(y, u, like):
        y = jax.nn.gelu(y + d_skip.astype(F32).reshape(S5_GROUPS, S5_GROUP_CH) * u).reshape(like.shape)
        y = y * jax.nn.sigmoid(y @ glu_w.astype(F32) + glu_b.astype(F32))
        return y.astype(like.dtype)

    return finish(y_lat, ul, u_lat), (finish(y_ctx, uc, u_ctx) if need_ctx else None)


def mla_queries(cq, qa_g, w_uq, cos, sin):
    n_b, n_t, _ = cq.shape
    q = (rms_norm(cq, qa_g) @ w_uq).reshape(n_b, n_t, MLA_HEADS, MLA_NOPE + MLA_ROPE)
    q_nope, q_pe = q[..., :MLA_NOPE], q[..., MLA_NOPE:]
    if cos is not None:
        q_pe = apply_axial_rope(q_pe, cos[:, None], sin[:, None])
    return q_nope, q_pe


def mla_keys_values(ckv, k_pe, kva_g, w_uk, w_uv, cos, sin):
    n_b, n_t, _ = ckv.shape
    ckv = rms_norm(ckv, kva_g)
    k_nope = (ckv @ w_uk).reshape(n_b, n_t, MLA_HEADS, MLA_NOPE)
    v = (ckv @ w_uv).reshape(n_b, n_t, MLA_HEADS, MLA_V)
    if cos is not None:
        k_pe = apply_axial_rope(k_pe, cos, sin)
    return k_nope, k_pe, v


def mla_attend(q_nope, q_pe, k_nope, k_pe, v):
    s = (jnp.einsum('bqhd,bkhd->bhqk', q_nope, k_nope)
         + jnp.einsum('bqhr,bkr->bhqk', q_pe, k_pe)).astype(F32)
    p = jax.nn.softmax(s * MLA_SCALE, axis=-1).astype(v.dtype)
    return jnp.einsum('bhqk,bkhd->bqhd', p, v)


def mla_attend_blocked(q_nope, q_pe, k_nope, k_pe, v):
    n_b, n_t = q_nope.shape[:2]
    n_blk = n_t // MLA_Q_BLOCK

    def blocks(a):
        return jnp.moveaxis(a.reshape((n_b, n_blk, MLA_Q_BLOCK) + a.shape[2:]), 1, 0)

    out = lax.map(lambda qb: mla_attend(qb[0], qb[1], k_nope, k_pe, v), (blocks(q_nope), blocks(q_pe)))
    return jnp.moveaxis(out, 0, 1).reshape(n_b, n_t, MLA_HEADS * MLA_V)


def hgrn2_chunk_scan(q, k, v, log_f, s0):
    n_b, n_h, n_t, _ = q.shape
    d_v = v.shape[-1]
    n_c = n_t // HG_CHUNK

    def chunks(a):
        return jnp.moveaxis(a.reshape(n_b, n_h, n_c, HG_CHUNK, a.shape[-1]), 2, 0)

    lower = jnp.tril(jnp.ones((HG_CHUNK, HG_CHUNK), bool))[:, :, None]

    def step(state, inp):
        qc, kc, vc, gc = inp
        b = jnp.cumsum(gc, axis=2)
        o_inter = jnp.einsum('bhtk,bhkv->bhtv', qc * jnp.exp(b), state)
        decay = jnp.exp(jnp.where(lower, b[:, :, :, None, :] - b[:, :, None, :, :], -jnp.inf))
        scores = jnp.einsum('bhtk,bhsk,bhtsk->bhts', qc, kc, decay)
        o = o_inter + jnp.einsum('bhts,bhsv->bhtv', scores, vc)
        b_last = b[:, :, -1:, :]
        state = (jnp.exp(b_last[:, :, 0, :, None]) * state
                 + jnp.einsum('bhsk,bhsv->bhkv', kc * jnp.exp(b_last - b), vc))
        return state, o

    state, o = lax.scan(step, s0, (chunks(q), chunks(k), chunks(v), chunks(log_f)))
    return jnp.moveaxis(o, 0, 2).reshape(n_b, n_h, n_t, d_v), state


def hgrn2_mixer(parts_lat, parts_ctx, lb, norm_g, need_ctx):
    def heads(a, dh):
        return a.astype(F32).reshape(a.shape[0], a.shape[1], HG_HEADS, dh).transpose(0, 2, 1, 3)

    def forget(z):
        log_f = jnp.logaddexp(jnp.log(lb), jnp.log1p(-lb) + jax.nn.log_sigmoid(heads(z, HG_K)))
        return -jnp.expm1(log_f), log_f

    q_l, i_l = heads(parts_lat[0], HG_K), heads(parts_lat[3], HG_V)
    q_c, i_c = heads(parts_ctx[0], HG_K), heads(parts_ctx[3], HG_V)
    state0 = jnp.zeros((q_l.shape[0], HG_HEADS, HG_K, HG_V), F32)
    o_lat, o_ctx = 0.0, 0.0
    for d in range(2):
        k_c, lf_c = forget(parts_ctx[1 + d])
        k_l, lf_l = forget(parts_lat[1 + d])
        oc, s_ctx = hgrn2_chunk_scan(rev(q_c, d, 2), rev(k_c, d, 2), rev(i_c, d, 2), rev(lf_c, d, 2), state0)
        ol, _ = hgrn2_chunk_scan(rev(q_l, d, 2), rev(k_l, d, 2), rev(i_l, d, 2), rev(lf_l, d, 2), s_ctx)
        o_lat = o_lat + rev(ol, d, 2)
        if need_ctx:
            o_ctx = o_ctx + rev(oc, d, 2)

    def finish(o, g):
        o = o.transpose(0, 2, 1, 3)
        o = o * lax.rsqrt(jnp.mean(o * o, axis=-1, keepdims=True) + NORM_EPS)
        o = o * norm_g.astype(F32).reshape(HG_HEADS, HG_V) * jax.nn.silu(g.astype(F32)).reshape(o.shape)
        return o.reshape(g.shape).astype(g.dtype)

    return finish(o_lat, parts_lat[4]), (finish(o_ctx, parts_ctx[4]) if need_ctx else None)


def merge_branches(gates, y_s5, y_mla, y_hg, w_pa, w_pb, w_pc, w_out):
    g_a, g_b, g_c = gates
    merged = (jax.nn.sigmoid(g_a) * (y_s5 @ w_pa)
              + jax.nn.sigmoid(g_b) * (y_mla @ w_pb)
              + jax.nn.sigmoid(g_c) * (y_hg @ w_pc))
    return merged @ w_out


def hier_moe(h, w_group, b_group, w_expert, b_expert, w1, w3, w2):
    n_tok = h.shape[0]
    hf = h.astype(F32)
    g_prob = jax.nn.softmax(hf @ w_group.astype(F32) + b_group.astype(F32), axis=-1)
    g_w, g_idx = lax.top_k(g_prob, 1)
    e_logits = (hf @ w_expert.astype(F32) + b_expert.astype(F32)).reshape(
        n_tok, MOE_GROUPS, MOE_EXPERTS_PER_GROUP)
    e_logits = jnp.take_along_axis(e_logits, g_idx[:, :, None], axis=1)[:, 0]
    top_v, top_i = lax.top_k(e_logits, MOE_TOP_K)
    gate = jax.nn.softmax(top_v, axis=-1) * g_w
    expert = g_idx * MOE_EXPERTS_PER_GROUP + top_i

    n_assign = n_tok * MOE_TOP_K
    flat_e = expert.reshape(n_assign)
    order = jnp.argsort(flat_e)
    s_exp = flat_e[order]
    s_tok = order // MOE_TOP_K
    s_gate = gate.reshape(n_assign)[order]
    counts = jnp.bincount(flat_e, length=MOE_EXPERTS)
    starts = jnp.cumsum(counts) - counts
    padded = (counts + MOE_BLOCK - 1) // MOE_BLOCK * MOE_BLOCK
    p_ends = jnp.cumsum(padded)
    dest = (p_ends - padded)[s_exp] + jnp.arange(n_assign) - starts[s_exp]
    n_blocks = (n_assign + MOE_EXPERTS * (MOE_BLOCK - 1) + MOE_BLOCK - 1) // MOE_BLOCK
    buf = jnp.zeros((n_blocks * MOE_BLOCK, h.shape[1]), h.dtype).at[dest].set(h[s_tok])
    block_expert = jnp.minimum(
        jnp.searchsorted(p_ends, jnp.arange(n_blocks) * MOE_BLOCK, side='right'), MOE_EXPERTS - 1)

    def expert_block(args):
        xb, e = args
        return (jax.nn.silu(xb @ w1[e]) * (xb @ w3[e])) @ w2[e]

    y = lax.map(expert_block, (buf.reshape(n_blocks, MOE_BLOCK, -1), block_expert))
    y = y.reshape(n_blocks * MOE_BLOCK, -1)[dest].astype(F32) * s_gate[:, None]
    return jnp.zeros(h.shape, F32).at[s_tok].add(y).astype(h.dtype)


def setup_inputs(seed: int = 0) -> dict:
    key = jax.random.key(seed)
    ks = iter(jax.random.split(key, 48))

    def nrm(shape, scale):
        return jax.random.normal(next(ks), shape, F32) * scale

    L, G, P, H5, E = DEPTH, S5_GROUPS, S5_STATE, S5_GROUP_CH, MOE_EXPERTS
    n_idx = jnp.arange(S5_STATE, dtype=F32)
    return {
        'x': nrm((BATCH, SEQ, D_MODEL), 1.0),
        'c': nrm((BATCH, D_MODEL), 1.0),
        'ctx': nrm((BATCH, CTX_LEN, D_MODEL), 1.0),
        'c_ctx': nrm((D_MODEL,), 1.0),
        'mod_w': nrm((L, D_MODEL, 6 * D_MODEL), 0.5 * D_MODEL ** -0.5),
        'mod_b': nrm((L, 6 * D_MODEL), 0.01),
        'norm1_g': 1.0 + nrm((L, D_MODEL), 0.05),
        'norm2_g': 1.0 + nrm((L, D_MODEL), 0.05),
        'w_in': nrm((L, D_MODEL, IN_COLS), D_MODEL ** -0.5),
        's5_lam_re': -0.5 + nrm((L, 2, G, P), 0.01),
        's5_lam_im': math.pi * n_idx + nrm((L, 2, G, P), 0.01),
        's5_b_re': nrm((L, 2, G, P, H5), (2 * H5) ** -0.5),
        's5_b_im': nrm((L, 2, G, P, H5), (2 * H5) ** -0.5),
        's5_c_re': nrm((L, 2, G, H5, P), 0.5),
        's5_c_im': nrm((L, 2, G, H5, P), 0.5),
        's5_log_step': jax.random.uniform(next(ks), (L, 2, G), F32, math.log(1e-3), math.log(1e-1)),
        's5_d': nrm((L, S5_WIDTH), 1.0),
        's5_glu_w': nrm((L, S5_WIDTH, S5_WIDTH), S5_WIDTH ** -0.5),
        's5_glu_b': nrm((L, S5_WIDTH), 0.01),
        'mla_qa_g': 1.0 + nrm((L, MLA_Q_LORA), 0.05),
        'mla_kva_g': 1.0 + nrm((L, MLA_KV_LORA), 0.05),
        'mla_w_uq': nrm((L, MLA_Q_LORA, MLA_HEADS * (MLA_NOPE + MLA_ROPE)), MLA_Q_LORA ** -0.5),
        'mla_w_uk': nrm((L, MLA_KV_LORA, MLA_HEADS * MLA_NOPE), MLA_KV_LORA ** -0.5),
        'mla_w_uv': nrm((L, MLA_KV_LORA, MLA_HEADS * MLA_V), MLA_KV_LORA ** -0.5),
        'hg_lb_logits': nrm((L,), 1.0),
        'hg_norm_g': 1.0 + nrm((L, HG_HEADS * HG_V), 0.05),
        'w_pa': nrm((L, S5_WIDTH, D_MODEL), S5_WIDTH ** -0.5),
        'w_pb': nrm((L, MLA_HEADS * MLA_V, D_MODEL), (MLA_HEADS * MLA_V) ** -0.5),
        'w_pc': nrm((L, HG_HEADS * HG_V, D_MODEL), (HG_HEADS * HG_V) ** -0.5),
        'w_out': nrm((L, D_MODEL, D_MODEL), D_MODEL ** -0.5),
        'moe_w_group': nrm((L, D_MODEL, MOE_GROUPS), D_MODEL ** -0.5),
        'moe_b_group': nrm((L, MOE_GROUPS), 0.01),
        'moe_w_expert': nrm((L, D_MODEL, E), D_MODEL ** -0.5),
        'moe_b_expert': nrm((L, E), 0.01),
        'moe_w1': nrm((L, E, D_MODEL, MOE_HIDDEN), D_MODEL ** -0.5),
        'moe_w3': nrm((L, E, D_MODEL, MOE_HIDDEN), D_MODEL ** -0.5),
        'moe_w2': nrm((L, E, MOE_HIDDEN, D_MODEL), MOE_HIDDEN ** -0.5),
        'final_norm_g': 1.0 + nrm((D_MODEL,), 0.05),
    }


def reference(x, c, ctx, c_ctx, mod_w, mod_b, norm1_g, norm2_g, w_in,
              s5_lam_re, s5_lam_im, s5_b_re, s5_b_im, s5_c_re, s5_c_im, s5_log_step, s5_d,
              s5_glu_w, s5_glu_b,
              mla_qa_g, mla_kva_g, mla_w_uq, mla_w_uk, mla_w_uv,
              hg_lb_logits, hg_norm_g,
              w_pa, w_pb, w_pc, w_out,
              moe_w_group, moe_b_group, moe_w_expert, moe_b_expert, moe_w1, moe_w3, moe_w2,
              final_norm_g):
    n_b, n_lat, _ = x.shape
    cos, sin = axial_rope_angles(n_lat)
    lb_all = jnp.cumsum(jax.nn.softmax(hg_lb_logits.astype(F32)))
    lb_all = lb_all - lb_all[0]
    silu_c = jax.nn.silu(c)
    silu_cc = jax.nn.silu(c_ctx)
    x_lat, x_ctx = x, ctx
    for layer in range(DEPTH):
        need_ctx = layer < DEPTH - 1
        mod_l = jnp.split((silu_c @ mod_w[layer] + mod_b[layer])[:, None, :], 6, axis=-1)
        mod_c = jnp.split(silu_cc @ mod_w[layer] + mod_b[layer], 6, axis=-1)

        h_lat = modulate(rms_norm(x_lat, norm1_g[layer]), mod_l[0], mod_l[1])
        h_ctx = modulate(rms_norm(x_ctx, norm1_g[layer]), mod_c[0], mod_c[1])
        z_lat = split_columns(h_lat @ w_in[layer])
        z_ctx = split_columns(h_ctx @ w_in[layer])

        s5_lat, s5_ctx = s5_mixer(z_lat[0], z_ctx[0], s5_lam_re[layer], s5_lam_im[layer],
                                  s5_b_re[layer], s5_b_im[layer], s5_c_re[layer], s5_c_im[layer],
                                  s5_log_step[layer], s5_d[layer], s5_glu_w[layer], s5_glu_b[layer],
                                  need_ctx)

        kn_l, kp_l, v_l = mla_keys_values(z_lat[2], z_lat[3], mla_kva_g[layer], mla_w_uk[layer],
                                          mla_w_uv[layer], cos, sin)
        kn_c, kp_c, v_c = mla_keys_values(z_ctx[2], z_ctx[3], mla_kva_g[layer], mla_w_uk[layer],
                                          mla_w_uv[layer], None, None)
        qn_l, qp_l = mla_queries(z_lat[1], mla_qa_g[layer], mla_w_uq[layer], cos, sin)
        mla_lat = mla_attend_blocked(qn_l, qp_l,
                                     jnp.concatenate([kn_l, kn_c], axis=1),
                                     jnp.concatenate([kp_l, kp_c], axis=1),
                                     jnp.concatenate([v_l, v_c], axis=1))

        hg_lat, hg_ctx = hgrn2_mixer(tuple(z_lat[4:9]), tuple(z_ctx[4:9]), lb_all[layer],
                                     hg_norm_g[layer], need_ctx)

        y_lat = merge_branches(z_lat[9:12], s5_lat, mla_lat, hg_lat,
                               w_pa[layer], w_pb[layer], w_pc[layer], w_out[layer])
        x_lat = x_lat + mod_l[2] * y_lat
        if need_ctx:
            qn_c, qp_c = mla_queries(z_ctx[1], mla_qa_g[layer], mla_w_uq[layer], None, None)
            mla_ctx = mla_attend(qn_c, qp_c, kn_c, kp_c, v_c).reshape(n_b, -1, MLA_HEADS * MLA_V)
            y_ctx = merge_branches(z_ctx[9:12], s5_ctx, mla_ctx, hg_ctx,
                                   w_pa[layer], w_pb[layer], w_pc[layer], w_out[layer])
            x_ctx = x_ctx + mod_c[2] * y_ctx

        h2_lat = modulate(rms_norm(x_lat, norm2_g[layer]), mod_l[3], mod_l[4]).reshape(-1, D_MODEL)
        n_lat_tok = h2_lat.shape[0]
        if need_ctx:
            h2_ctx = modulate(rms_norm(x_ctx, norm2_g[layer]), mod_c[3], mod_c[4]).reshape(-1, D_MODEL)
            tokens = jnp.concatenate([h2_lat, h2_ctx], axis=0)
        else:
            tokens = h2_lat
        f = hier_moe(tokens, moe_w_group[layer], moe_b_group[layer], moe_w_expert[layer],
                     moe_b_expert[layer], moe_w1[layer], moe_w3[layer], moe_w2[layer])
        x_lat = x_lat + mod_l[5] * f[:n_lat_tok].reshape(x_lat.shape)
        if need_ctx:
            x_ctx = x_ctx + mod_c[5] * f[n_lat_tok:].reshape(x_ctx.shape)
    return rms_norm(x_lat, final_norm_g)
```

```python
import functools
import math

import jax
import jax.numpy as jnp
from jax import lax
import numpy as np
from jax.experimental import pallas as pl
from jax.experimental.pallas import tpu as pltpu

F32 = jnp.float32
BF16 = jnp.bfloat16
HIGHEST = lax.Precision.HIGHEST

D_MODEL = 1024
BATCH = 8
SEQ = 2048
DEPTH = 2
GRID_W = 64
CTX_LEN = 256
NORM_EPS = 1e-6

S5_WIDTH = D_MODEL // 4
S5_GROUP_CH = 16
S5_GROUPS = S5_WIDTH // S5_GROUP_CH
S5_STATE = 64
S5_LANES = S5_GROUPS * S5_STATE

MLA_HEADS = D_MODEL // 128
MLA_NOPE = 64
MLA_ROPE = 32
MLA_V = 64
MLA_Q_LORA = D_MODEL // 4
MLA_KV_LORA = D_MODEL // 8
MLA_SCALE = 1.0 / math.sqrt(MLA_NOPE + MLA_ROPE)
MLA_HEAD_PAD = 128
ROPE_PAIRS = MLA_ROPE // 4
ROPE_BASE = 10000.0

HG_HEADS = D_MODEL // 256
HG_K = 128
HG_V = 64
HG_CHUNK = 64
HG_LEVELS = 6
HG_SETS = 2 + HG_LEVELS

MOE_GROUPS = 4
MOE_EXPERTS_PER_GROUP = 8
MOE_EXPERTS = MOE_GROUPS * MOE_EXPERTS_PER_GROUP
MOE_TOP_K = 2
MOE_HIDDEN = D_MODEL // 2
MOE_ROWS = 256

N_LAT = BATCH * SEQ
N_CTX = BATCH * CTX_LEN
N_TOK = N_LAT + N_CTX
RB = CTX_LEN
NB_LAT = N_LAT // RB
NB_CTX = N_CTX // RB
NB_TOK = N_TOK // RB
LAT_BLOCKS = SEQ // RB
SEQ_BLOCKS = LAT_BLOCKS + 1

C_S5 = 0
C_CQ = 256
C_KV = 512
C_HQ = 768
C_ZF = 1280
C_ZB = 1792
C_HI = 2304
C_HG = 2560
C_GATE = 2816
W_IN_COLS = C_GATE + 3 * D_MODEL

VMEM_LIMIT = 56 * 1024 * 1024


def _params(*sem):
    return pltpu.CompilerParams(dimension_semantics=sem, vmem_limit_bytes=VMEM_LIMIT)


def _rms(v):
    return v * lax.rsqrt(jnp.mean(v * v, axis=-1, keepdims=True) + NORM_EPS)


def _mod_row(i):
    return jnp.where(i < NB_LAT, i // LAT_BLOCKS, BATCH)


def _mod_kernel(c_ref, w_ref, b_ref, o_ref):
    c = c_ref[...]
    s = c * jax.nn.sigmoid(c)
    o_ref[0] = jnp.dot(s, w_ref[0], precision=HIGHEST, preferred_element_type=F32) + b_ref[0]


def _modulation(c_rows, mod_w, mod_b):
    bn = 1536
    return pl.pallas_call(
        _mod_kernel,
        grid=(DEPTH, 6 * D_MODEL // bn),
        in_specs=[
            pl.BlockSpec((16, D_MODEL), lambda l, j: (0, 0)),
            pl.BlockSpec((1, D_MODEL, bn), lambda l, j: (l, 0, j)),
            pl.BlockSpec((1, 1, bn), lambda l, j: (l, 0, j)),
        ],
        out_specs=pl.BlockSpec((1, 16, bn), lambda l, j: (l, 0, j)),
        out_shape=jax.ShapeDtypeStruct((DEPTH, 16, 6 * D_MODEL), F32),
        compiler_params=_params("arbitrary", "arbitrary"),
        name="adaln_mod",
    )(c_rows, mod_w, mod_b.reshape(DEPTH, 1, 6 * D_MODEL))


def _log_sigmoid(z):
    return jnp.minimum(z, 0.0) - jnp.log1p(jnp.exp(-jnp.abs(z)))


def _in_kernel(hs_ref, x_ref, mod_ref, g1_ref, w_ref, qag_ref, kvg_ref,
               s5u_ref, cqn_ref, kvn_ref, hq_ref, lff_ref, lfb_ref, kkf_ref, kkb_ref,
               hi_ref, hgs_ref, gates_ref):
    x = x_ref[...]
    xn = _rms(x) * g1_ref[...]
    h = (xn * (1.0 + mod_ref[0, 1:2, :]) + mod_ref[0, 0:1, :]).astype(BF16)

    def mm(a, b):
        return jnp.dot(h, w_ref[:, a:b], preferred_element_type=F32)

    s5u_ref[...] = mm(C_S5, C_CQ).astype(BF16)
    cqn_ref[...] = (_rms(mm(C_CQ, C_KV)) * qag_ref[...]).astype(BF16)
    kv = mm(C_KV, C_HQ)
    ckvn = _rms(kv[:, :MLA_KV_LORA]) * kvg_ref[...]
    kvn_ref[...] = jnp.concatenate([ckvn, kv[:, MLA_KV_LORA:]], axis=1).astype(BF16)
    hq_ref[...] = mm(C_HQ, C_ZF).astype(BF16)

    log_lb = hs_ref[0]
    log_1m_lb = hs_ref[1]
    one_m_lb = hs_ref[2]
    for c0, lf_ref, kk_ref in ((C_ZF, lff_ref, kkf_ref), (C_ZB, lfb_ref, kkb_ref)):
        z = mm(c0, c0 + HG_HEADS * HG_K)
        b = log_1m_lb + _log_sigmoid(z)
        m = jnp.maximum(b, log_lb)
        lf_ref[...] = m + jnp.log1p(jnp.exp(-jnp.abs(b - log_lb)))
        kk_ref[...] = (one_m_lb * jax.nn.sigmoid(-z)).astype(BF16)

    hi_ref[...] = mm(C_HI, C_HG).astype(BF16)
    g = mm(C_HG, C_GATE)
    hgs_ref[...] = (g * jax.nn.sigmoid(g)).astype(BF16)
    for j in range(3):
        c0 = C_GATE + j * D_MODEL
        gates_ref[:, j * D_MODEL:(j + 1) * D_MODEL] = jax.nn.sigmoid(mm(c0, c0 + D_MODEL)).astype(BF16)


def _input_projection(x, mod3, hg_scal, g1, w_in, qa_g, kva_g):
    row = lambda w: pl.BlockSpec((RB, w), lambda i: (i, 0))
    full = lambda a: pl.BlockSpec(a.shape, lambda i: (0,) * a.ndim)
    shapes = [
        (S5_WIDTH, BF16), (MLA_Q_LORA, BF16), (256, BF16), (HG_HEADS * HG_K, BF16),
        (HG_HEADS * HG_K, F32), (HG_HEADS * HG_K, F32), (HG_HEADS * HG_K, BF16), (HG_HEADS * HG_K, BF16),
        (HG_HEADS * HG_V, BF16), (HG_HEADS * HG_V, BF16), (3 * D_MODEL, BF16),
    ]
    return pl.pallas_call(
        _in_kernel,
        grid=(NB_TOK,),
        in_specs=[
            pl.BlockSpec(memory_space=pltpu.SMEM),
            row(D_MODEL),
            pl.BlockSpec((1, 6, D_MODEL), lambda i: (_mod_row(i), 0, 0)),
            full(g1), full(w_in), full(qa_g), full(kva_g),
        ],
        out_specs=[row(w) for w, _ in shapes],
        out_shape=[jax.ShapeDtypeStruct((N_TOK, w), dt) for w, dt in shapes],
        compiler_params=_params("arbitrary"),
        name="input_projection",
    )(hg_scal, x, mod3, g1, w_in, qa_g, kva_g)


def _seq_block(b, j):
    return jnp.where(j == 0, NB_LAT + b, b * LAT_BLOCKS + j - 1)


def _seq_block_rev(b, j):
    return jnp.where(j == 0, NB_LAT + b, b * LAT_BLOCKS + LAT_BLOCKS - j)


def _s5_scan(bu_ref, apow_ref, d, carry, reverse):
    n_groups = RB // 8

    def body(gi, carry):
        cre, cim = carry
        g = (n_groups - 1 - gi) if reverse else gi
        r0 = pl.multiple_of(g * 8, 8)
        xre = bu_ref[pl.ds(r0, 8), 0:S5_LANES]
        xim = bu_ref[pl.ds(r0, 8), S5_LANES:2 * S5_LANES]
        for k, sh in enumerate((1, 2, 4)):
            shift = (8 - sh) if reverse else sh
            sre = pltpu.roll(xre, shift, 0)
            sim = pltpu.roll(xim, shift, 0)
            are = apow_ref[d, k, 0]
            aim = apow_ref[d, k, 1]
            xre, xim = xre + are * sre - aim * sim, xim + are * sim + aim * sre
        are = apow_ref[d, 3, 0]
        aim = apow_ref[d, 3, 1]
        xre, xim = xre + are * cre - aim * cim, xim + are * cim + aim * cre
        bu_ref[pl.ds(r0, 8), 0:S5_LANES] = xre
        bu_ref[pl.ds(r0, 8), S5_LANES:2 * S5_LANES] = xim
        if reverse:
            return xre[0:1, :], xim[0:1, :]
        return xre[7:8, :], xim[7:8, :]

    return lax.fori_loop(0, n_groups, body, carry)


def _s5_kernel(uf_ref, ub_ref, bmat_ref, cmat_ref, apow_ref, yf_ref, yb_ref, bu_ref, carry_ref):
    j = pl.program_id(1)

    @pl.when(j == 0)
    def _():
        carry_ref[...] = jnp.zeros_like(carry_ref)

    for d, (u_ref, y_ref) in enumerate(((uf_ref, yf_ref), (ub_ref, yb_ref))):
        bu_ref[...] = jnp.dot(u_ref[...], bmat_ref[d], preferred_element_type=F32)
        carry = (carry_ref[2 * d:2 * d + 1, :], carry_ref[2 * d + 1:2 * d + 2, :])
        cre, cim = _s5_scan(bu_ref, apow_ref, d, carry, reverse=(d == 1))
        carry_ref[2 * d:2 * d + 1, :] = cre
        carry_ref[2 * d + 1:2 * d + 2, :] = cim
        y_ref[...] = jnp.dot(bu_ref[...].astype(BF16), cmat_ref[d], preferred_element_type=F32)


def _s5_states(s5u, bmat, cmat, apow):
    full = lambda a: pl.BlockSpec(a.shape, lambda b, j: (0,) * a.ndim)
    return pl.pallas_call(
        _s5_kernel,
        grid=(BATCH, SEQ_BLOCKS),
        in_specs=[
            pl.BlockSpec((RB, S5_WIDTH), lambda b, j: (_seq_block(b, j), 0)),
            pl.BlockSpec((RB, S5_WIDTH), lambda b, j: (_seq_block_rev(b, j), 0)),
            full(bmat), full(cmat), full(apow),
        ],
        out_specs=[
            pl.BlockSpec((RB, S5_WIDTH), lambda b, j: (_seq_block(b, j), 0)),
            pl.BlockSpec((RB, S5_WIDTH), lambda b, j: (_seq_block_rev(b, j), 0)),
        ],
        out_shape=[jax.ShapeDtypeStruct((N_TOK, S5_WIDTH), F32)] * 2,
        scratch_shapes=[pltpu.VMEM((RB, 2 * S5_LANES), F32), pltpu.VMEM((8, S5_LANES), F32)],
        compiler_params=_params("arbitrary", "arbitrary"),
        name="s5_scan",
    )(s5u, s5u, bmat, cmat, apow)


def _qkv_kernel(cqn_ref, kvn_ref, cos_ref, sin_ref, wq_ref, wk_ref, wv_ref, q_ref, k_ref, v_ref):
    cos = jnp.concatenate([cos_ref[...]] * MLA_HEADS, axis=1)
    sin = jnp.concatenate([sin_ref[...]] * MLA_HEADS, axis=1)
    w = MLA_HEADS * MLA_HEAD_PAD
    q2 = jnp.dot(cqn_ref[...], wq_ref[...], preferred_element_type=F32)
    q_ref[...] = ((q2[:, :w] * cos + q2[:, w:] * sin) * MLA_SCALE).astype(BF16)
    kv = kvn_ref[...]
    k2 = jnp.dot(kv, wk_ref[...], preferred_element_type=F32)
    k_ref[...] = (k2[:, :w] * cos + k2[:, w:] * sin).astype(BF16)
    v_ref[...] = jnp.dot(kv, wv_ref[...], preferred_element_type=F32).astype(BF16)


def _qkv(cqn, kvn, cos_t, sin_t, wq, wk, wv):
    row = lambda w: pl.BlockSpec((RB, w), lambda i: (i, 0))
    full = lambda a: pl.BlockSpec(a.shape, lambda i: (0,) * a.ndim)
    pos = pl.BlockSpec((RB, MLA_HEAD_PAD), lambda i: (jnp.where(i < NB_LAT, i % LAT_BLOCKS, LAT_BLOCKS), 0))
    w = MLA_HEADS * MLA_HEAD_PAD
    return pl.pallas_call(
        _qkv_kernel,
        grid=(NB_TOK,),
        in_specs=[row(MLA_Q_LORA), row(256), pos, pos, full(wq), full(wk), full(wv)],
        out_specs=[row(w), row(w), row(MLA_HEADS * MLA_V)],
        out_shape=[jax.ShapeDtypeStruct((N_TOK, w), BF16), jax.ShapeDtypeStruct((N_TOK, w), BF16),
                   jax.ShapeDtypeStruct((N_TOK, MLA_HEADS * MLA_V), BF16)],
        compiler_params=_params("arbitrary"),
        name="mla_qkv",
    )(cqn, kvn, cos_t, sin_t, wq, wk, wv)


def _attn_kernel(*refs, n_kv):
    q_ref = refs[0]
    k_refs = refs[1:1 + n_kv]
    v_refs = refs[1 + n_kv:1 + 2 * n_kv]
    o_ref = refs[1 + 2 * n_kv]
    nt = (((1,), (1,)), ((), ()))
    for h in range(MLA_HEADS):
        q = q_ref[:, h * MLA_HEAD_PAD:(h + 1) * MLA_HEAD_PAD]
        s = [lax.dot_general(q, k_ref[:, h * MLA_HEAD_PAD:(h + 1) * MLA_HEAD_PAD], nt,
                             preferred_element_type=F32) for k_ref in k_refs]
        m = functools.reduce(jnp.maximum, [jnp.max(si, axis=-1, keepdims=True) for si in s])
        p = [jnp.exp(si - m) for si in s]
        l = functools.reduce(jnp.add, [jnp.sum(pi, axis=-1, keepdims=True) for pi in p])
        o = functools.reduce(jnp.add, [
            jnp.dot(pi.astype(BF16), v_ref[:, h * MLA_V:(h + 1) * MLA_V], preferred_element_type=F32)
            for pi, v_ref in zip(p, v_refs)])
        o_ref[:, h * MLA_V:(h + 1) * MLA_V] = (o / l).astype(BF16)


def _attention(q, k, v, latent):
    w = MLA_HEADS * MLA_HEAD_PAD
    wv = MLA_HEADS * MLA_V
    if latent:
        tq = RB
        nq = SEQ // tq
        q_spec = pl.BlockSpec((tq, w), lambda b, j: (b * nq + j, 0))
        k_specs = [pl.BlockSpec((SEQ, w), lambda b, j: (b, 0)),
                   pl.BlockSpec((CTX_LEN, w), lambda b, j: (NB_LAT + b, 0))]
        v_specs = [pl.BlockSpec((SEQ, wv), lambda b, j: (b, 0)),
                   pl.BlockSpec((CTX_LEN, wv), lambda b, j: (NB_LAT + b, 0))]
        o_spec = pl.BlockSpec((tq, wv), lambda b, j: (b * nq + j, 0))
        n_out = N_LAT
    else:
        tq = CTX_LEN
        nq = 1
        q_spec = pl.BlockSpec((tq, w), lambda b, j: (NB_LAT + b, 0))
        k_specs = [pl.BlockSpec((CTX_LEN, w), lambda b, j: (NB_LAT + b, 0))]
        v_specs = [pl.BlockSpec((CTX_LEN, wv), lambda b, j: (NB_LAT + b, 0))]
        o_spec = pl.BlockSpec((tq, wv), lambda b, j: (b, 0))
        n_out = N_CTX
    n_kv = len(k_specs)
    return pl.pallas_call(
        functools.partial(_attn_kernel, n_kv=n_kv),
        grid=(BATCH, nq),
        in_specs=[q_spec] + k_specs + v_specs,
        out_specs=o_spec,
        out_shape=jax.ShapeDtypeStruct((n_out, wv), BF16),
        compiler_params=_params("arbitrary", "arbitrary"),
        name="mla_attention_lat" if latent else "mla_attention_ctx",
    )(q, *([k] * n_kv), *([v] * n_kv))


def _hgrn_tables():
    c = HG_CHUNK
    w = np.zeros((2, HG_SETS * c, c), np.float32)
    mask = np.zeros((2, HG_LEVELS, c, c), np.float32)
    idx = np.arange(c)
    for r in range(c):
        w[0, r, :r + 1] = 1
        w[1, r, r:] = 1
        w[0, c + r, r + 1:] = 1
        w[1, c + r, :r] = 1
    for l in range(HG_LEVELS):
        m = c >> (l + 1)
        for r in range(c):
            base = (r // (2 * m)) * 2 * m
            mid = base + m
            later = r >= mid
            row = (2 + l) * c + r
            if later:
                w[0, row, mid:r + 1] = 1
                w[1, row, mid:r] = 1
            else:
                w[0, row, r + 1:mid] = 1
                w[1, row, r:mid] = 1
        same = (idx[:, None] // (2 * m)) == (idx[None, :] // (2 * m))
        q_later = (idx[:, None] % (2 * m)) >= m
        k_later = (idx[None, :] % (2 * m)) >= m
        mask[0, l] = same & q_later & ~k_later
        mask[1, l] = same & ~q_later & k_later
    return w, mask


def _hgrn_chunk(d, r0, q_ref, k_ref, lf_ref, v_ref, w_ref, mask_ref, st_ref, o_ref):
    c = HG_CHUNK
    nt = (((1,), (1,)), ((), ()))
    tn = (((0,), (0,)), ((), ()))
    rows = pl.ds(r0, c)
    wsel = w_ref[d]
    last = c - 1 if d == 0 else 0
    for h in range(HG_HEADS):
        ks = slice(h * HG_K, (h + 1) * HG_K)
        vs = slice(h * HG_V, (h + 1) * HG_V)
        g = lf_ref[rows, ks]
        g_hi = g.astype(BF16)
        g_lo = (g - g_hi.astype(F32)).astype(BF16)
        expo = (jnp.dot(wsel, g_hi, preferred_element_type=F32)
                + jnp.dot(wsel, g_lo, preferred_element_type=F32))
        e = jnp.exp(expo)
        q = q_ref[rows, ks].astype(F32)
        k = k_ref[rows, ks].astype(F32)
        v = v_ref[rows, vs]
        scores = jnp.zeros((c, c), F32)
        for l in range(HG_LEVELS):
            el = e[(2 + l) * c:(3 + l) * c]
            p = lax.dot_general((q * el).astype(BF16), (k * el).astype(BF16), nt,
                                preferred_element_type=F32)
            scores = scores + jnp.where(mask_ref[d, l] > 0.5, p, 0.0)
        st = st_ref[d, h]
        o = jnp.dot(scores.astype(BF16), v, preferred_element_type=F32)
        o = o + lax.dot_general((q * e[0:c]).astype(BF16), st.astype(BF16), nt,
                                preferred_element_type=F32)
        o = o + jnp.sum(q * k, axis=-1, keepdims=True) * v.astype(F32)
        o_ref[rows, vs] = o
        decay = e[last:last + 1]
        st_ref[d, h] = st * decay + lax.dot_general(v, (k * e[c:2 * c]).astype(BF16), tn,
                                                    preferred_element_type=F32)


def _hgrn_kernel(qf_ref, kf_ref, lff_ref, vf_ref, qb_ref, kb_ref, lfb_ref, vb_ref, w_ref, mask_ref,
                 of_ref, ob_ref, st_ref):
    j = pl.program_id(1)

    @pl.when(j == 0)
    def _():
        st_ref[...] = jnp.zeros_like(st_ref)

    n_chunks = RB // HG_CHUNK

    def body(ci, carry):
        r0 = pl.multiple_of(ci * HG_CHUNK, HG_CHUNK)
        _hgrn_chunk(0, r0, qf_ref, kf_ref, lff_ref, vf_ref, w_ref, mask_ref, st_ref, of_ref)
        r1 = pl.multiple_of((n_chunks - 1 - ci) * HG_CHUNK, HG_CHUNK)
        _hgrn_chunk(1, r1, qb_ref, kb_ref, lfb_ref, vb_ref, w_ref, mask_ref, st_ref, ob_ref)
        return carry

    lax.fori_loop(0, n_chunks, body, 0)


def _hgrn(hq, kkf, kkb, lff, lfb, hi, w_tab, mask_tab):
    full = lambda a: pl.BlockSpec(a.shape, lambda b, j: (0,) * a.ndim)
    fwd = lambda w: pl.BlockSpec((RB, w), lambda b, j: (_seq_block(b, j), 0))
    bwd = lambda w: pl.BlockSpec((RB, w), lambda b, j: (_seq_block_rev(b, j), 0))
    wk = HG_HEADS * HG_K
    wv = HG_HEADS * HG_V
    return pl.pallas_call(
        _hgrn_kernel,
        grid=(BATCH, SEQ_BLOCKS),
        in_specs=[fwd(wk), fwd(wk), fwd(wk), fwd(wv), bwd(wk), bwd(wk), bwd(wk), bwd(wv),
                  full(w_tab), full(mask_tab)],
        out_specs=[fwd(wv), bwd(wv)],
        out_shape=[jax.ShapeDtypeStruct((N_TOK, wv), F32)] * 2,
        scratch_shapes=[pltpu.VMEM((2, HG_HEADS, HG_V, HG_K), F32)],
        compiler_params=_params("arbitrary", "arbitrary"),
        name="hgrn2_scan",
    )(hq, kkf, lff, hi, hq, kkb, lfb, hi, w_tab, mask_tab)


def _gelu_tanh(x):
    return 0.5 * x * (1.0 + jnp.tanh(math.sqrt(2.0 / math.pi) * (x + 0.044715 * (x * x * x))))


def _merge_kernel(x_ref, mod_ref, yf_ref, yb_ref, u_ref, of_ref, ob_ref, hgs_ref, att_ref, gates_ref,
                  s5d_ref, gluw_ref, glub_ref, hgg_ref, hsum_ref, wpa_ref, wpb_ref, wpc_ref, wout_ref,
                  g2_ref, wr_ref, br_ref,
                  x1_ref, h2_ref, eid_ref, gate_ref):
    y = yf_ref[...] + yb_ref[...] + s5d_ref[...] * u_ref[...].astype(F32)
    y = _gelu_tanh(y)
    y = y * jax.nn.sigmoid(jnp.dot(y.astype(BF16), gluw_ref[...], preferred_element_type=F32) + glub_ref[...])
    o = of_ref[...] + ob_ref[...]
    ms = jnp.dot(o * o, hsum_ref[...], precision=HIGHEST, preferred_element_type=F32) * (1.0 / HG_V)
    o = o * lax.rsqrt(ms + NORM_EPS) * hgg_ref[...] * hgs_ref[...].astype(F32)

    d = D_MODEL
    merged = (gates_ref[:, 0:d].astype(F32) * jnp.dot(y.astype(BF16), wpa_ref[...], preferred_element_type=F32)
              + gates_ref[:, d:2 * d].astype(F32) * jnp.dot(att_ref[...], wpb_ref[...], preferred_element_type=F32)
              + gates_ref[:, 2 * d:3 * d].astype(F32) * jnp.dot(o.astype(BF16), wpc_ref[...], preferred_element_type=F32))
    y_out = jnp.dot(merged.astype(BF16), wout_ref[...], preferred_element_type=F32)
    x1 = x_ref[...] + mod_ref[0, 2:3, :] * y_out
    x1_ref[...] = x1
    h2 = _rms(x1) * g2_ref[...] * (1.0 + mod_ref[0, 4:5, :]) + mod_ref[0, 3:4, :]
    h2_ref[...] = h2

    logits = jnp.dot(h2, wr_ref[...], precision=HIGHEST, preferred_element_type=F32) + br_ref[...]
    lane = lax.broadcasted_iota(jnp.int32, logits.shape, 1).astype(F32)
    neg = -jnp.inf
    glog = jnp.where(lane < MOE_GROUPS, logits, neg)
    gmax = jnp.max(glog, axis=-1, keepdims=True)
    gidx = jnp.min(jnp.where(glog == gmax, lane, 1e9), axis=-1, keepdims=True)
    g_w = 1.0 / jnp.sum(jnp.exp(glog - gmax), axis=-1, keepdims=True)
    e_lo = 32.0 + gidx * MOE_EXPERTS_PER_GROUP
    elog = jnp.where((lane >= e_lo) & (lane < e_lo + MOE_EXPERTS_PER_GROUP), logits, neg)
    v1 = jnp.max(elog, axis=-1, keepdims=True)
    i1 = jnp.min(jnp.where(elog == v1, lane, 1e9), axis=-1, keepdims=True)
    elog2 = jnp.where(lane == i1, neg, elog)
    v2 = jnp.max(elog2, axis=-1, keepdims=True)
    i2 = jnp.min(jnp.where(elog2 == v2, lane, 1e9), axis=-1, keepdims=True)
    e2 = jnp.exp(v2 - v1)
    gate1 = g_w / (1.0 + e2)
    gate2 = g_w * e2 / (1.0 + e2)
    eid_ref[...] = jnp.where(lane == 0.0, i1 - 32.0, jnp.where(lane == 1.0, i2 - 32.0, 0.0)).astype(jnp.int32)
    gate_ref[...] = jnp.where(lane == 0.0, gate1, jnp.where(lane == 1.0, gate2, 0.0))


def _merge(n_blocks, x, mod3, yf, yb, s5u, of, ob, hgs, att_lat, att_ctx, gates, consts):
    row = lambda w: pl.BlockSpec((RB, w), lambda i: (i, 0))
    full = lambda a: pl.BlockSpec(a.shape, lambda i: (0,) * a.ndim)
    n = n_blocks * RB
    wv = MLA_HEADS * MLA_V
    if att_ctx is None:
        att = att_lat
    else:
        att = jnp.concatenate([att_lat, att_ctx], axis=0)
    return pl.pallas_call(
        _merge_kernel,
        grid=(n_blocks,),
        in_specs=[row(D_MODEL), pl.BlockSpec((1, 6, D_MODEL), lambda i: (_mod_row(i), 0, 0)),
                  row(S5_WIDTH), row(S5_WIDTH), row(S5_WIDTH), row(HG_HEADS * HG_V), row(HG_HEADS * HG_V),
                  row(HG_HEADS * HG_V), row(wv), row(3 * D_MODEL)] + [full(a) for a in consts],
        out_specs=[row(D_MODEL), row(D_MODEL), row(128), row(128)],
        out_shape=[jax.ShapeDtypeStruct((n, D_MODEL), F32), jax.ShapeDtypeStruct((n, D_MODEL), F32),
                   jax.ShapeDtypeStruct((n, 128), jnp.int32), jax.ShapeDtypeStruct((n, 128), F32)],
        compiler_params=_params("arbitrary"),
        name="merge_router",
    )(x, mod3, yf, yb, s5u, of, ob, hgs, att, gates, *consts)


def _moe_kernel(bexp_ref, nblk_ref, src_ref, dst_ref, h_hbm, w13_ref, w2_ref, y_hbm, xbuf, ybuf, sem):
    del bexp_ref
    i = pl.program_id(0)

    @pl.when(i == 0)
    def _():
        xbuf[...] = jnp.zeros_like(xbuf)

    def gather(r):
        return pltpu.make_async_copy(h_hbm.at[pl.ds(src_ref[0, 0, r], 1), :], xbuf.at[pl.ds(r, 1), :], sem.at[0])

    def scatter(r):
        return pltpu.make_async_copy(ybuf.at[pl.ds(r, 1), :], y_hbm.at[pl.ds(dst_ref[0, 0, r], 1), :], sem.at[1])

    def each_valid_row(fn):
        def body(r, c):
            @pl.when(dst_ref[0, 0, r] >= 0)
            def _():
                fn(r)
            return c
        lax.fori_loop(0, MOE_ROWS, body, 0)

    @pl.when(i < nblk_ref[0])
    def _():
        each_valid_row(lambda r: gather(r).start())
        each_valid_row(lambda r: gather(r).wait())
        x = xbuf[...].astype(BF16)
        h13 = jnp.dot(x, w13_ref[0], preferred_element_type=F32)
        a = h13[:, :MOE_HIDDEN]
        hid = (a * jax.nn.sigmoid(a) * h13[:, MOE_HIDDEN:]).astype(BF16)
        ybuf[...] = jnp.dot(hid, w2_ref[0], preferred_element_type=F32)
        each_valid_row(lambda r: scatter(r).start())
        each_valid_row(lambda r: scatter(r).wait())


def _moe(h2, eid, w13, w2):
    n = h2.shape[0]
    n_assign = n * MOE_TOP_K
    n_blocks = (n_assign + MOE_EXPERTS * (MOE_ROWS - 1) + MOE_ROWS - 1) // MOE_ROWS
    n_rows = n_blocks * MOE_ROWS
    flat_e = eid.reshape(n_assign)
    order = jnp.argsort(flat_e, stable=True).astype(jnp.int32)
    s_exp = flat_e[order]
    counts = jnp.bincount(flat_e, length=MOE_EXPERTS).astype(jnp.int32)
    starts = jnp.cumsum(counts) - counts
    padded = (counts + MOE_ROWS - 1) // MOE_ROWS * MOE_ROWS
    p_ends = jnp.cumsum(padded)
    dest = (p_ends - padded)[s_exp] + jnp.arange(n_assign, dtype=jnp.int32) - starts[s_exp]
    row_assign = jnp.full((n_rows,), -1, jnp.int32).at[dest].set(order)
    n_used = (p_ends[-1] // MOE_ROWS).astype(jnp.int32).reshape(1)
    blk_start = jnp.arange(n_blocks, dtype=jnp.int32) * MOE_ROWS
    last_row = jnp.maximum(p_ends[-1] - 1, 0)
    block_expert = jnp.minimum(jnp.searchsorted(p_ends, jnp.minimum(blk_start, last_row), side='right'),
                               MOE_EXPERTS - 1).astype(jnp.int32)
    src = jnp.maximum(row_assign, 0) // MOE_TOP_K
    smem_rows = pl.BlockSpec((1, 1, MOE_ROWS), lambda i, be, nb: (i, 0, 0), memory_space=pltpu.SMEM)
    return pl.pallas_call(
        _moe_kernel,
        grid_spec=pltpu.PrefetchScalarGridSpec(
            num_scalar_prefetch=2,
            grid=(n_blocks,),
            in_specs=[
                smem_rows, smem_rows,
                pl.BlockSpec(memory_space=pl.ANY),
                pl.BlockSpec((1, D_MODEL, 2 * MOE_HIDDEN), lambda i, be, nb: (be[i], 0, 0)),
                pl.BlockSpec((1, MOE_HIDDEN, D_MODEL), lambda i, be, nb: (be[i], 0, 0)),
            ],
            out_specs=pl.BlockSpec(memory_space=pl.ANY),
            scratch_shapes=[pltpu.VMEM((MOE_ROWS, D_MODEL), F32), pltpu.VMEM((MOE_ROWS, D_MODEL), F32),
                            pltpu.SemaphoreType.DMA((2,))],
        ),
        out_shape=jax.ShapeDtypeStruct((n_assign, D_MODEL), F32),
        compiler_params=_params("arbitrary"),
        name="moe_experts",
    )(block_expert, n_used, src.reshape(n_blocks, 1, MOE_ROWS), row_assign.reshape(n_blocks, 1, MOE_ROWS),
      h2, w13, w2)


def _combine_kernel(x1_ref, y_ref, gate_ref, mod_ref, g_ref, o_ref, *, final):
    g0 = gate_ref[:, 0:1]
    g1 = gate_ref[:, 1:2]
    f = g0 * y_ref[:, 0:D_MODEL] + g1 * y_ref[:, D_MODEL:2 * D_MODEL]
    x2 = x1_ref[...] + mod_ref[0, 5:6, :] * f
    if final:
        x2 = _rms(x2) * g_ref[...]
    o_ref[...] = x2


def _combine(n_blocks, x1, yflat, gate, mod3, g_final, final):
    row = lambda w: pl.BlockSpec((RB, w), lambda i: (i, 0))
    n = n_blocks * RB
    return pl.pallas_call(
        functools.partial(_combine_kernel, final=final),
        grid=(n_blocks,),
        in_specs=[row(D_MODEL), row(2 * D_MODEL), row(128),
                  pl.BlockSpec((1, 6, D_MODEL), lambda i: (_mod_row(i), 0, 0)),
                  pl.BlockSpec((1, D_MODEL), lambda i: (0, 0))],
        out_specs=row(D_MODEL),
        out_shape=jax.ShapeDtypeStruct((n, D_MODEL), F32),
        compiler_params=_params("arbitrary"),
        name="moe_combine_final" if final else "moe_combine",
    )(x1, yflat.reshape(n, 2 * D_MODEL), gate, mod3, g_final)


def _rope_rot_cols(w):
    p = ROPE_PAIRS
    return jnp.concatenate([-w[..., p:2 * p], w[..., 0:p], -w[..., 3 * p:4 * p], w[..., 2 * p:3 * p]], axis=-1)


def _pack_w_in(w_in):
    cuts = np.cumsum([S5_WIDTH, MLA_Q_LORA, MLA_KV_LORA, MLA_ROPE, HG_HEADS * HG_K, HG_HEADS * HG_K,
                      HG_HEADS * HG_K, HG_HEADS * HG_V, HG_HEADS * HG_V, D_MODEL, D_MODEL])
    (w_s5, w_cq, w_ckv, w_kpe, w_hq, w_zf, w_zb, w_hi, w_hg, w_ga, w_gb, w_gc) = jnp.split(w_in, cuts, axis=1)
    pad = jnp.zeros((D_MODEL, 256 - MLA_KV_LORA - 2 * MLA_ROPE), w_in.dtype)
    packed = jnp.concatenate([w_s5, w_cq, w_ckv, w_kpe, _rope_rot_cols(w_kpe), pad,
                              w_hq, w_zf, w_zb, w_hi, w_hg, w_ga, w_gb, w_gc], axis=1)
    assert packed.shape[1] == W_IN_COLS
    return packed.astype(BF16)


def _pack_mla(w_uq, w_uk, w_uv):
    hp, n, r = MLA_HEAD_PAD, MLA_NOPE, MLA_ROPE
    wq = w_uq.reshape(MLA_Q_LORA, MLA_HEADS, n + r)
    zq = jnp.zeros((MLA_Q_LORA, MLA_HEADS, hp - n - r), F32)
    q1 = jnp.concatenate([wq, zq], axis=-1).reshape(MLA_Q_LORA, MLA_HEADS * hp)
    q2 = jnp.concatenate([jnp.zeros((MLA_Q_LORA, MLA_HEADS, n), F32), _rope_rot_cols(wq[..., n:]), zq],
                         axis=-1).reshape(MLA_Q_LORA, MLA_HEADS * hp)
    wq_packed = jnp.concatenate([q1, q2], axis=1).astype(BF16)

    wk = w_uk.reshape(MLA_KV_LORA, MLA_HEADS, n)
    k_nope = jnp.concatenate([wk, jnp.zeros((MLA_KV_LORA, MLA_HEADS, hp - n), F32)], axis=-1)
    eye = jnp.eye(r, dtype=F32)
    pe_slot = jnp.concatenate([jnp.zeros((r, n), F32), eye, jnp.zeros((r, hp - n - r), F32)], axis=-1)
    pe_all = jnp.tile(pe_slot[:, None, :], (1, MLA_HEADS, 1))
    zero_pe = jnp.zeros_like(pe_all)
    tail = jnp.zeros((256 - MLA_KV_LORA - 2 * r, MLA_HEADS, hp), F32)
    k1 = jnp.concatenate([k_nope, pe_all, zero_pe, tail], axis=0).reshape(256, MLA_HEADS * hp)
    k2 = jnp.concatenate([jnp.zeros_like(k_nope), zero_pe, pe_all, tail], axis=0).reshape(256, MLA_HEADS * hp)
    wk_packed = jnp.concatenate([k1, k2], axis=1).astype(BF16)
    wv_packed = jnp.concatenate([w_uv, jnp.zeros((256 - MLA_KV_LORA, MLA_HEADS * MLA_V), F32)], axis=0).astype(BF16)
    return wq_packed, wk_packed, wv_packed


def _rope_tables():
    rows = SEQ // GRID_W
    row = jnp.repeat(jnp.arange(rows, dtype=F32), GRID_W)
    col = jnp.tile(jnp.arange(GRID_W, dtype=F32), rows)
    inv = ROPE_BASE ** (-jnp.arange(ROPE_PAIRS, dtype=F32) / ROPE_PAIRS)
    ar, ac = row[:, None] * inv, col[:, None] * inv
    cos = jnp.concatenate([jnp.cos(ar), jnp.cos(ar), jnp.cos(ac), jnp.cos(ac)], axis=1)
    sin = jnp.concatenate([jnp.sin(ar), jnp.sin(ar), jnp.sin(ac), jnp.sin(ac)], axis=1)
    ones = jnp.ones((SEQ, MLA_NOPE), F32)
    tail = MLA_HEAD_PAD - MLA_NOPE - MLA_ROPE
    cos_t = jnp.concatenate([ones, cos, jnp.ones((SEQ, tail), F32)], axis=1)
    sin_t = jnp.concatenate([jnp.zeros((SEQ, MLA_NOPE), F32), sin, jnp.zeros((SEQ, tail), F32)], axis=1)
    cos_t = jnp.concatenate([cos_t, jnp.ones((CTX_LEN, MLA_HEAD_PAD), F32)], axis=0)
    sin_t = jnp.concatenate([sin_t, jnp.zeros((CTX_LEN, MLA_HEAD_PAD), F32)], axis=0)
    return cos_t, sin_t


def _cmul(ar, ai, br, bi):
    return ar * br - ai * bi, ar * bi + ai * br


def _pack_s5(lam_re, lam_im, b_re, b_im, c_re, c_im, log_step):
    bmats, cmats, apows = [], [], []
    eye = jnp.eye(S5_GROUPS, dtype=F32)
    for d in range(2):
        dt = jnp.exp(log_step[d])[:, None]
        mag = jnp.exp(lam_re[d] * dt)
        lb_re, lb_im = mag * jnp.cos(lam_im[d] * dt), mag * jnp.sin(lam_im[d] * dt)
        den = lam_re[d] * lam_re[d] + lam_im[d] * lam_im[d]
        fr = ((lb_re - 1) * lam_re[d] + lb_im * lam_im[d]) / den
        fi = (lb_im * lam_re[d] - (lb_re - 1) * lam_im[d]) / den
        bb_re = fr[..., None] * b_re[d] - fi[..., None] * b_im[d]
        bb_im = fr[..., None] * b_im[d] + fi[..., None] * b_re[d]

        def in_mat(bb):
            return jnp.einsum('gph,gk->ghkp', bb, eye).reshape(S5_WIDTH, S5_LANES)

        def out_mat(cc):
            return jnp.einsum('ghp,gk->gpkh', cc, eye).reshape(S5_LANES, S5_WIDTH)

        bmats.append(jnp.concatenate([in_mat(bb_re), in_mat(bb_im)], axis=1))
        cmats.append(jnp.concatenate([out_mat(c_re[d]), -out_mat(c_im[d])], axis=0))

        a1 = (lb_re.reshape(1, S5_LANES), lb_im.reshape(1, S5_LANES))
        pw = [a1]
        for _ in range(7):
            pw.append(_cmul(*pw[-1], *a1))
        r = jnp.arange(8)[:, None]
        if d == 0:
            keep = lambda sh: (r >= sh).astype(F32)
            carry_pow = lambda c: jnp.concatenate([pw[k][c] for k in range(8)], axis=0)
        else:
            keep = lambda sh: (r < 8 - sh).astype(F32)
            carry_pow = lambda c: jnp.concatenate([pw[7 - k][c] for k in range(8)], axis=0)
        tabs = [jnp.stack([keep(sh) * pw[sh - 1][0], keep(sh) * pw[sh - 1][1]]) for sh in (1, 2, 4)]
        tabs.append(jnp.stack([carry_pow(0), carry_pow(1)]))
        apows.append(jnp.stack(tabs))
    return jnp.stack(bmats).astype(BF16), jnp.stack(cmats).astype(BF16), jnp.stack(apows)


def kernel(x, c, ctx, c_ctx, mod_w, mod_b, norm1_g, norm2_g, w_in, s5_lam_re, s5_lam_im, s5_b_re, s5_b_im, s5_c_re, s5_c_im, s5_log_step, s5_d, s5_glu_w, s5_glu_b, mla_qa_g, mla_kva_g, mla_w_uq, mla_w_uk, mla_w_uv, hg_lb_logits, hg_norm_g, w_pa, w_pb, w_pc, w_out, moe_w_group, moe_b_group, moe_w_expert, moe_b_expert, moe_w1, moe_w3, moe_w2, final_norm_g):
    xt = jnp.concatenate([x.reshape(N_LAT, D_MODEL), ctx.reshape(N_CTX, D_MODEL)], axis=0)
    c_rows = jnp.concatenate([c, c_ctx[None, :], jnp.zeros((16 - BATCH - 1, D_MODEL), F32)], axis=0)
    mod = _modulation(c_rows, mod_w, mod_b).reshape(DEPTH, 16, 6, D_MODEL)

    lb_all = jnp.cumsum(jax.nn.softmax(hg_lb_logits.astype(F32)))
    lb_all = lb_all - lb_all[0]
    cos_t, sin_t = _rope_tables()
    w_tab_np, mask_tab_np = _hgrn_tables()
    w_tab = jnp.asarray(w_tab_np, BF16)
    mask_tab = jnp.asarray(mask_tab_np, F32)
    head_sum = jnp.asarray(np.kron(np.eye(HG_HEADS), np.ones((HG_V, HG_V))), F32)

    out = None
    for layer in range(DEPTH):
        last = layer == DEPTH - 1
        mod3 = mod[layer]
        lb = lb_all[layer]
        hg_scal = jnp.stack([jnp.log(lb), jnp.log1p(-lb), 1.0 - lb]).astype(F32)
        (s5u, cqn, kvn, hq, lff, lfb, kkf, kkb, hi, hgs, gates) = _input_projection(
            xt, mod3, hg_scal, norm1_g[layer][None, :], _pack_w_in(w_in[layer]),
            mla_qa_g[layer][None, :], mla_kva_g[layer][None, :])

        bmat, cmat, apow = _pack_s5(s5_lam_re[layer], s5_lam_im[layer], s5_b_re[layer], s5_b_im[layer],
                                    s5_c_re[layer], s5_c_im[layer], s5_log_step[layer])
        yf, yb = _s5_states(s5u, bmat, cmat, apow)

        wq, wk, wv = _pack_mla(mla_w_uq[layer], mla_w_uk[layer], mla_w_uv[layer])
        q, k, v = _qkv(cqn, kvn, cos_t, sin_t, wq, wk, wv)
        att_lat = _attention(q, k, v, latent=True)
        att_ctx = None if last else _attention(q, k, v, latent=False)

        of, ob = _hgrn(hq, kkf, kkb, lff, lfb, hi, w_tab, mask_tab)

        n_blocks = NB_LAT if last else NB_TOK
        w_route = jnp.zeros((D_MODEL, 128), F32)
        w_route = w_route.at[:, :MOE_GROUPS].set(moe_w_group[layer])
        w_route = w_route.at[:, 32:32 + MOE_EXPERTS].set(moe_w_expert[layer])
        b_route = jnp.zeros((1, 128), F32)
        b_route = b_route.at[0, :MOE_GROUPS].set(moe_b_group[layer])
        b_route = b_route.at[0, 32:32 + MOE_EXPERTS].set(moe_b_expert[layer])
        consts = [s5_d[layer][None, :], s5_glu_w[layer].astype(BF16), s5_glu_b[layer][None, :],
                  hg_norm_g[layer][None, :], head_sum, w_pa[layer].astype(BF16), w_pb[layer].astype(BF16),
                  w_pc[layer].astype(BF16), w_out[layer].astype(BF16), norm2_g[layer][None, :],
                  w_route, b_route]
        x1, h2, eid, gate = _merge(n_blocks, xt, mod3, yf, yb, s5u, of, ob, hgs, att_lat, att_ctx, gates, consts)

        w13 = jnp.concatenate([moe_w1[layer], moe_w3[layer]], axis=2).astype(BF16)
        yflat = _moe(h2, eid[:, :MOE_TOP_K], w13, moe_w2[layer].astype(BF16))
        res = _combine(n_blocks, x1, yflat, gate, mod3, final_norm_g[None, :], final=last)
        if last:
            out = res
        else:
            xt = res
    return out.reshape(BATCH, SEQ, D_MODEL)
```

```python
import functools
import math

import jax
import jax.numpy as jnp
from jax import lax
import numpy as np
from jax.experimental import pallas as pl
from jax.experimental.pallas import tpu as pltpu

F32 = jnp.float32
BF16 = jnp.bfloat16
HIGHEST = lax.Precision.HIGHEST

D_MODEL = 1024
BATCH = 8
SEQ = 2048
DEPTH = 2
GRID_W = 64
CTX_LEN = 256
NORM_EPS = 1e-6

S5_WIDTH = D_MODEL // 4
S5_GROUP_CH = 16
S5_GROUPS = S5_WIDTH // S5_GROUP_CH
S5_STATE = 64
S5_LANES = S5_GROUPS * S5_STATE

MLA_HEADS = D_MODEL // 128
MLA_NOPE = 64
MLA_ROPE = 32
MLA_V = 64
MLA_Q_LORA = D_MODEL // 4
MLA_KV_LORA = D_MODEL // 8
MLA_SCALE = 1.0 / math.sqrt(MLA_NOPE + MLA_ROPE)
MLA_HEAD_PAD = 128
ROPE_PAIRS = MLA_ROPE // 4
ROPE_BASE = 10000.0

HG_HEADS = D_MODEL // 256
HG_K = 128
HG_V = 64
HG_CHUNK = 64
HG_LEVELS = 6
HG_SETS = 2 + HG_LEVELS

MOE_GROUPS = 4
MOE_EXPERTS_PER_GROUP = 8
MOE_EXPERTS = MOE_GROUPS * MOE_EXPERTS_PER_GROUP
MOE_TOP_K = 2
MOE_HIDDEN = D_MODEL // 2
MOE_ROWS = 256

N_LAT = BATCH * SEQ
N_CTX = BATCH * CTX_LEN
N_TOK = N_LAT + N_CTX
RB = CTX_LEN
NB_LAT = N_LAT // RB
NB_CTX = N_CTX // RB
NB_TOK = N_TOK // RB
LAT_BLOCKS = SEQ // RB
SEQ_BLOCKS = LAT_BLOCKS + 1

C_S5 = 0
C_CQ = 256
C_KV = 512
C_HQ = 768
C_ZF = 1280
C_ZB = 1792
C_HI = 2304
C_HG = 2560
C_GATE = 2816
W_IN_COLS = C_GATE + 3 * D_MODEL

VMEM_LIMIT = 56 * 1024 * 1024


def _params(*sem):
    return pltpu.CompilerParams(dimension_semantics=sem, vmem_limit_bytes=VMEM_LIMIT)


def _rms(v):
    return v * lax.rsqrt(jnp.mean(v * v, axis=-1, keepdims=True) + NORM_EPS)


def _mod_row(i):
    return jnp.where(i < NB_LAT, i // LAT_BLOCKS, BATCH)


def _mod_kernel(c_ref, w_ref, b_ref, o_ref):
    c = c_ref[...]
    s = c * jax.nn.sigmoid(c)
    o_ref[0] = jnp.dot(s, w_ref[0], precision=HIGHEST, preferred_element_type=F32) + b_ref[0]


def _modulation(c_rows, mod_w, mod_b):
    bn = 1536
    return pl.pallas_call(
        _mod_kernel,
        grid=(DEPTH, 6 * D_MODEL // bn),
        in_specs=[
            pl.BlockSpec((16, D_MODEL), lambda l, j: (0, 0)),
            pl.BlockSpec((1, D_MODEL, bn), lambda l, j: (l, 0, j)),
            pl.BlockSpec((1, 1, bn), lambda l, j: (l, 0, j)),
        ],
        out_specs=pl.BlockSpec((1, 16, bn), lambda l, j: (l, 0, j)),
        out_shape=jax.ShapeDtypeStruct((DEPTH, 16, 6 * D_MODEL), F32),
        compiler_params=_params("arbitrary", "arbitrary"),
        name="adaln_mod",
    )(c_rows, mod_w, mod_b.reshape(DEPTH, 1, 6 * D_MODEL))


def _log_sigmoid(z):
    return jnp.minimum(z, 0.0) - jnp.log1p(jnp.exp(-jnp.abs(z)))


def _in_kernel(hs_ref, x_ref, mod_ref, g1_ref, w_ref, qag_ref, kvg_ref,
               s5u_ref, cqn_ref, kvn_ref, hq_ref, lff_ref, lfb_ref, kkf_ref, kkb_ref,
               hi_ref, hgs_ref, gates_ref):
    x = x_ref[...]
    xn = _rms(x) * g1_ref[...]
    h = (xn * (1.0 + mod_ref[0, 1:2, :]) + mod_ref[0, 0:1, :]).astype(BF16)

    def mm(a, b):
        return jnp.dot(h, w_ref[:, a:b], preferred_element_type=F32)

    s5u_ref[...] = mm(C_S5, C_CQ).astype(BF16)
    cqn_ref[...] = (_rms(mm(C_CQ, C_KV)) * qag_ref[...]).astype(BF16)
    kv = mm(C_KV, C_HQ)
    ckvn = _rms(kv[:, :MLA_KV_LORA]) * kvg_ref[...]
    kvn_ref[...] = jnp.concatenate([ckvn, kv[:, MLA_KV_LORA:]], axis=1).astype(BF16)
    hq_ref[...] = mm(C_HQ, C_ZF).astype(BF16)

    log_lb = hs_ref[0]
    log_1m_lb = hs_ref[1]
    one_m_lb = hs_ref[2]
    for c0, lf_ref, kk_ref in ((C_ZF, lff_ref, kkf_ref), (C_ZB, lfb_ref, kkb_ref)):
        z = mm(c0, c0 + HG_HEADS * HG_K)
        b = log_1m_lb + _log_sigmoid(z)
        m = jnp.maximum(b, log_lb)
        lf_ref[...] = m + jnp.log1p(jnp.exp(-jnp.abs(b - log_lb)))
        kk_ref[...] = (one_m_lb * jax.nn.sigmoid(-z)).astype(BF16)

    hi_ref[...] = mm(C_HI, C_HG).astype(BF16)
    g = mm(C_HG, C_GATE)
    hgs_ref[...] = (g * jax.nn.sigmoid(g)).astype(BF16)
    for j in range(3):
        c0 = C_GATE + j * D_MODEL
        gates_ref[:, j * D_MODEL:(j + 1) * D_MODEL] = jax.nn.sigmoid(mm(c0, c0 + D_MODEL)).astype(BF16)


def _input_projection(x, mod3, hg_scal, g1, w_in, qa_g, kva_g):
    row = lambda w: pl.BlockSpec((RB, w), lambda i: (i, 0))
    full = lambda a: pl.BlockSpec(a.shape, lambda i: (0,) * a.ndim)
    shapes = [
        (S5_WIDTH, BF16), (MLA_Q_LORA, BF16), (256, BF16), (HG_HEADS * HG_K, BF16),
        (HG_HEADS * HG_K, F32), (HG_HEADS * HG_K, F32), (HG_HEADS * HG_K, BF16), (HG_HEADS * HG_K, BF16),
        (HG_HEADS * HG_V, BF16), (HG_HEADS * HG_V, BF16), (3 * D_MODEL, BF16),
    ]
    return pl.pallas_call(
        _in_kernel,
        grid=(NB_TOK,),
        in_specs=[
            pl.BlockSpec(memory_space=pltpu.SMEM),
            row(D_MODEL),
            pl.BlockSpec((1, 6, D_MODEL), lambda i: (_mod_row(i), 0, 0)),
            full(g1), full(w_in), full(qa_g), full(kva_g),
        ],
        out_specs=[row(w) for w, _ in shapes],
        out_shape=[jax.ShapeDtypeStruct((N_TOK, w), dt) for w, dt in shapes],
        compiler_params=_params("arbitrary"),
        name="input_projection",
    )(hg_scal, x, mod3, g1, w_in, qa_g, kva_g)


def _seq_block(b, j):
    return jnp.where(j == 0, NB_LAT + b, b * LAT_BLOCKS + j - 1)


def _seq_block_rev(b, j):
    return jnp.where(j == 0, NB_LAT + b, b * LAT_BLOCKS + LAT_BLOCKS - j)


def _s5_scan(bu_ref, apow_ref, d, carry, reverse):
    n_groups = RB // 8

    def body(gi, carry):
        cre, cim = carry
        g = (n_groups - 1 - gi) if reverse else gi
        r0 = pl.multiple_of(g * 8, 8)
        xre = bu_ref[pl.ds(r0, 8), 0:S5_LANES]
        xim = bu_ref[pl.ds(r0, 8), S5_LANES:2 * S5_LANES]
        for k, sh in enumerate((1, 2, 4)):
            shift = (8 - sh) if reverse else sh
            sre = pltpu.roll(xre, shift, 0)
            sim = pltpu.roll(xim, shift, 0)
            are = apow_ref[d, k, 0]
            aim = apow_ref[d, k, 1]
            xre, xim = xre + are * sre - aim * sim, xim + are * sim + aim * sre
        are = apow_ref[d, 3, 0]
        aim = apow_ref[d, 3, 1]
        xre, xim = xre + are * cre - aim * cim, xim + are * cim + aim * cre
        bu_ref[pl.ds(r0, 8), 0:S5_LANES] = xre
        bu_ref[pl.ds(r0, 8), S5_LANES:2 * S5_LANES] = xim
        if reverse:
            return xre[0:1, :], xim[0:1, :]
        return xre[7:8, :], xim[7:8, :]

    return lax.fori_loop(0, n_groups, body, carry)


def _s5_kernel(uf_ref, ub_ref, bmat_ref, cmat_ref, apow_ref, yf_ref, yb_ref, bu_ref, carry_ref):
    j = pl.program_id(1)

    @pl.when(j == 0)
    def _():
        carry_ref[...] = jnp.zeros_like(carry_ref)

    for d, (u_ref, y_ref) in enumerate(((uf_ref, yf_ref), (ub_ref, yb_ref))):
        bu_ref[...] = jnp.dot(u_ref[...], bmat_ref[d], preferred_element_type=F32)
        carry = (carry_ref[2 * d:2 * d + 1, :], carry_ref[2 * d + 1:2 * d + 2, :])
        cre, cim = _s5_scan(bu_ref, apow_ref, d, carry, reverse=(d == 1))
        carry_ref[2 * d:2 * d + 1, :] = cre
        carry_ref[2 * d + 1:2 * d + 2, :] = cim
        y_ref[...] = jnp.dot(bu_ref[...].astype(BF16), cmat_ref[d], preferred_element_type=F32)


def _s5_states(s5u, bmat, cmat, apow):
    full = lambda a: pl.BlockSpec(a.shape, lambda b, j: (0,) * a.ndim)
    return pl.pallas_call(
        _s5_kernel,
        grid=(BATCH, SEQ_BLOCKS),
        in_specs=[
            pl.BlockSpec((RB, S5_WIDTH), lambda b, j: (_seq_block(b, j), 0)),
            pl.BlockSpec((RB, S5_WIDTH), lambda b, j: (_seq_block_rev(b, j), 0)),
            full(bmat), full(cmat), full(apow),
        ],
        out_specs=[
            pl.BlockSpec((RB, S5_WIDTH), lambda b, j: (_seq_block(b, j), 0)),
            pl.BlockSpec((RB, S5_WIDTH), lambda b, j: (_seq_block_rev(b, j), 0)),
        ],
        out_shape=[jax.ShapeDtypeStruct((N_TOK, S5_WIDTH), F32)] * 2,
        scratch_shapes=[pltpu.VMEM((RB, 2 * S5_LANES), F32), pltpu.VMEM((8, S5_LANES), F32)],
        compiler_params=_params("arbitrary", "arbitrary"),
        name="s5_scan",
    )(s5u, s5u, bmat, cmat, apow)


def _qkv_kernel(cqn_ref, kvn_ref, cos_ref, sin_ref, wq_ref, wk_ref, wv_ref, q_ref, k_ref, v_ref):
    cos = jnp.concatenate([cos_ref[...]] * MLA_HEADS, axis=1)
    sin = jnp.concatenate([sin_ref[...]] * MLA_HEADS, axis=1)
    w = MLA_HEADS * MLA_HEAD_PAD
    q2 = jnp.dot(cqn_ref[...], wq_ref[...], preferred_element_type=F32)
    q_ref[...] = ((q2[:, :w] * cos + q2[:, w:] * sin) * MLA_SCALE).astype(BF16)
    kv = kvn_ref[...]
    k2 = jnp.dot(kv, wk_ref[...], preferred_element_type=F32)
    k_ref[...] = (k2[:, :w] * cos + k2[:, w:] * sin).astype(BF16)
    v_ref[...] = jnp.dot(kv, wv_ref[...], preferred_element_type=F32).astype(BF16)


def _qkv(cqn, kvn, cos_t, sin_t, wq, wk, wv):
    row = lambda w: pl.BlockSpec((RB, w), lambda i: (i, 0))
    full = lambda a: pl.BlockSpec(a.shape, lambda i: (0,) * a.ndim)
    pos = pl.BlockSpec((RB, MLA_HEAD_PAD), lambda i: (jnp.where(i < NB_LAT, i % LAT_BLOCKS, LAT_BLOCKS), 0))
    w = MLA_HEADS * MLA_HEAD_PAD
    return pl.pallas_call(
        _qkv_kernel,
        grid=(NB_TOK,),
        in_specs=[row(MLA_Q_LORA), row(256), pos, pos, full(wq), full(wk), full(wv)],
        out_specs=[row(w), row(w), row(MLA_HEADS * MLA_V)],
        out_shape=[jax.ShapeDtypeStruct((N_TOK, w), BF16), jax.ShapeDtypeStruct((N_TOK, w), BF16),
                   jax.ShapeDtypeStruct((N_TOK, MLA_HEADS * MLA_V), BF16)],
        compiler_params=_params("arbitrary"),
        name="mla_qkv",
    )(cqn, kvn, cos_t, sin_t, wq, wk, wv)


def _attn_kernel(*refs, n_kv):
    q_ref = refs[0]
    k_refs = refs[1:1 + n_kv]
    v_refs = refs[1 + n_kv:1 + 2 * n_kv]
    o_ref = refs[1 + 2 * n_kv]
    nt = (((1,), (1,)), ((), ()))
    for h in range(MLA_HEADS):
        q = q_ref[:, h * MLA_HEAD_PAD:(h + 1) * MLA_HEAD_PAD]
        s = [lax.dot_general(q, k_ref[:, h * MLA_HEAD_PAD:(h + 1) * MLA_HEAD_PAD], nt,
                             preferred_element_type=F32) for k_ref in k_refs]
        m = functools.reduce(jnp.maximum, [jnp.max(si, axis=-1, keepdims=True) for si in s])
        p = [jnp.exp(si - m) for si in s]
        l = functools.reduce(jnp.add, [jnp.sum(pi, axis=-1, keepdims=True) for pi in p])
        o = functools.reduce(jnp.add, [
            jnp.dot(pi.astype(BF16), v_ref[:, h * MLA_V:(h + 1) * MLA_V], preferred_element_type=F32)
            for pi, v_ref in zip(p, v_refs)])
        o_ref[:, h * MLA_V:(h + 1) * MLA_V] = (o / l).astype(BF16)


def _attention(q, k, v, latent):
    w = MLA_HEADS * MLA_HEAD_PAD
    wv = MLA_HEADS * MLA_V
    if latent:
        tq = RB
        nq = SEQ // tq
        q_spec = pl.BlockSpec((tq, w), lambda b, j: (b * nq + j, 0))
        k_specs = [pl.BlockSpec((SEQ, w), lambda b, j: (b, 0)),
                   pl.BlockSpec((CTX_LEN, w), lambda b, j: (NB_LAT + b, 0))]
        v_specs = [pl.BlockSpec((SEQ, wv), lambda b, j: (b, 0)),
                   pl.BlockSpec((CTX_LEN, wv), lambda b, j: (NB_LAT + b, 0))]
        o_spec = pl.BlockSpec((tq, wv), lambda b, j: (b * nq + j, 0))
        n_out = N_LAT
    else:
        tq = CTX_LEN
        nq = 1
        q_spec = pl.BlockSpec((tq, w), lambda b, j: (NB_LAT + b, 0))
        k_specs = [pl.BlockSpec((CTX_LEN, w), lambda b, j: (NB_LAT + b, 0))]
        v_specs = [pl.BlockSpec((CTX_LEN, wv), lambda b, j: (NB_LAT + b, 0))]
        o_spec = pl.BlockSpec((tq, wv), lambda b, j: (b, 0))
        n_out = N_CTX
    n_kv = len(k_specs)
    return pl.pallas_call(
        functools.partial(_attn_kernel, n_kv=n_kv),
        grid=(BATCH, nq),
        in_specs=[q_spec] + k_specs + v_specs,
        out_specs=o_spec,
        out_shape=jax.ShapeDtypeStruct((n_out, wv), BF16),
        compiler_params=_params("arbitrary", "arbitrary"),
        name="mla_attention_lat" if latent else "mla_attention_ctx",
    )(q, *([k] * n_kv), *([v] * n_kv))


def _hgrn_tables():
    c = HG_CHUNK
    w = np.zeros((2, HG_SETS * c, c), np.float32)
    mask = np.zeros((2, HG_LEVELS, c, c), np.float32)
    idx = np.arange(c)
    for r in range(c):
        w[0, r, :r + 1] = 1
        w[1, r, r:] = 1
        w[0, c + r, r + 1:] = 1
        w[1, c + r, :r] = 1
    for l in range(HG_LEVELS):
        m = c >> (l + 1)
        for r in range(c):
            base = (r // (2 * m)) * 2 * m
            mid = base + m
            later = r >= mid
            row = (2 + l) * c + r
            if later:
                w[0, row, mid:r + 1] = 1
                w[1, row, mid:r] = 1
            else:
                w[0, row, r + 1:mid] = 1
                w[1, row, r:mid] = 1
        same = (idx[:, None] // (2 * m)) == (idx[None, :] // (2 * m))
        q_later = (idx[:, None] % (2 * m)) >= m
        k_later = (idx[None, :] % (2 * m)) >= m
        mask[0, l] = same & q_later & ~k_later
        mask[1, l] = same & ~q_later & k_later
    return w, mask


def _hgrn_chunk(d, r0, q_ref, k_ref, lf_ref, v_ref, w_ref, mask_ref, st_ref, o_ref):
    c = HG_CHUNK
    nt = (((1,), (1,)), ((), ()))
    tn = (((0,), (0,)), ((), ()))
    rows = pl.ds(r0, c)
    wsel = w_ref[d]
    last = c - 1 if d == 0 else 0
    for h in range(HG_HEADS):
        ks = slice(h * HG_K, (h + 1) * HG_K)
        vs = slice(h * HG_V, (h + 1) * HG_V)
        g = lf_ref[rows, ks]
        g_hi = g.astype(BF16)
        g_lo = (g - g_hi.astype(F32)).astype(BF16)
        expo = (jnp.dot(wsel, g_hi, preferred_element_type=F32)
                + jnp.dot(wsel, g_lo, preferred_element_type=F32))
        e = jnp.exp(expo)
        q = q_ref[rows, ks].astype(F32)
        k = k_ref[rows, ks].astype(F32)
        v = v_ref[rows, vs]
        scores = jnp.zeros((c, c), F32)
        for l in range(HG_LEVELS):
            el = e[(2 + l) * c:(3 + l) * c]
            p = lax.dot_general((q * el).astype(BF16), (k * el).astype(BF16), nt,
                                preferred_element_type=F32)
            scores = scores + jnp.where(mask_ref[d, l] > 0.5, p, 0.0)
        st = st_ref[d, h]
        o = jnp.dot(scores.astype(BF16), v, preferred_element_type=F32)
        o = o + lax.dot_general((q * e[0:c]).astype(BF16), st.astype(BF16), nt,
                                preferred_element_type=F32)
        o = o + jnp.sum(q * k, axis=-1, keepdims=True) * v.astype(F32)
        o_ref[rows, vs] = o
        decay = e[last:last + 1]
        st_ref[d, h] = st * decay + lax.dot_general(v, (k * e[c:2 * c]).astype(BF16), tn,
                                                    preferred_element_type=F32)


def _hgrn_kernel(qf_ref, kf_ref, lff_ref, vf_ref, qb_ref, kb_ref, lfb_ref, vb_ref, w_ref, mask_ref,
                 of_ref, ob_ref, st_ref):
    j = pl.program_id(1)

    @pl.when(j == 0)
    def _():
        st_ref[...] = jnp.zeros_like(st_ref)

    n_chunks = RB // HG_CHUNK

    def body(ci, carry):
        r0 = pl.multiple_of(ci * HG_CHUNK, HG_CHUNK)
        _hgrn_chunk(0, r0, qf_ref, kf_ref, lff_ref, vf_ref, w_ref, mask_ref, st_ref, of_ref)
        r1 = pl.multiple_of((n_chunks - 1 - ci) * HG_CHUNK, HG_CHUNK)
        _hgrn_chunk(1, r1, qb_ref, kb_ref, lfb_ref, vb_ref, w_ref, mask_ref, st_ref, ob_ref)
        return carry

    lax.fori_loop(0, n_chunks, body, 0)


def _hgrn(hq, kkf, kkb, lff, lfb, hi, w_tab, mask_tab):
    full = lambda a: pl.BlockSpec(a.shape, lambda b, j: (0,) * a.ndim)
    fwd = lambda w: pl.BlockSpec((RB, w), lambda b, j: (_seq_block(b, j), 0))
    bwd = lambda w: pl.BlockSpec((RB, w), lambda b, j: (_seq_block_rev(b, j), 0))
    wk = HG_HEADS * HG_K
    wv = HG_HEADS * HG_V
    return pl.pallas_call(
        _hgrn_kernel,
        grid=(BATCH, SEQ_BLOCKS),
        in_specs=[fwd(wk), fwd(wk), fwd(wk), fwd(wv), bwd(wk), bwd(wk), bwd(wk), bwd(wv),
                  full(w_tab), full(mask_tab)],
        out_specs=[fwd(wv), bwd(wv)],
        out_shape=[jax.ShapeDtypeStruct((N_TOK, wv), F32)] * 2,
        scratch_shapes=[pltpu.VMEM((2, HG_HEADS, HG_V, HG_K), F32)],
        compiler_params=_params("arbitrary", "arbitrary"),
        name="hgrn2_scan",
    )(hq, kkf, lff, hi, hq, kkb, lfb, hi, w_tab, mask_tab)


def _gelu_tanh(x):
    return 0.5 * x * (1.0 + jnp.tanh(math.sqrt(2.0 / math.pi) * (x + 0.044715 * (x * x * x))))


def _merge_kernel(x_ref, mod_ref, yf_ref, yb_ref, u_ref, of_ref, ob_ref, hgs_ref, att_ref, gates_ref,
                  s5d_ref, gluw_ref, glub_ref, hgg_ref, hsum_ref, wpa_ref, wpb_ref, wpc_ref, wout_ref,
                  g2_ref, wr_ref, br_ref,
                  x1_ref, h2_ref, eid_ref, gate_ref):
    y = yf_ref[...] + yb_ref[...] + s5d_ref[...] * u_ref[...].astype(F32)
    y = _gelu_tanh(y)
    y = y * jax.nn.sigmoid(jnp.dot(y.astype(BF16), gluw_ref[...], preferred_element_type=F32) + glub_ref[...])
    o = of_ref[...] + ob_ref[...]
    ms = jnp.dot(o * o, hsum_ref[...], precision=HIGHEST, preferred_element_type=F32) * (1.0 / HG_V)
    o = o * lax.rsqrt(ms + NORM_EPS) * hgg_ref[...] * hgs_ref[...].astype(F32)

    d = D_MODEL
    merged = (gates_ref[:, 0:d].astype(F32) * jnp.dot(y.astype(BF16), wpa_ref[...], preferred_element_type=F32)
              + gates_ref[:, d:2 * d].astype(F32) * jnp.dot(att_ref[...], wpb_ref[...], preferred_element_type=F32)
              + gates_ref[:, 2 * d:3 * d].astype(F32) * jnp.dot(o.astype(BF16), wpc_ref[...], preferred_element_type=F32))
    y_out = jnp.dot(merged.astype(BF16), wout_ref[...], preferred_element_type=F32)
    x1 = x_ref[...] + mod_ref[0, 2:3, :] * y_out
    x1_ref[...] = x1
    h2 = _rms(x1) * g2_ref[...] * (1.0 + mod_ref[0, 4:5, :]) + mod_ref[0, 3:4, :]
    h2_ref[...] = h2

    logits = jnp.dot(h2, wr_ref[...], precision=HIGHEST, preferred_element_type=F32) + br_ref[...]
    lane = lax.broadcasted_iota(jnp.int32, logits.shape, 1).astype(F32)
    neg = -jnp.inf
    glog = jnp.where(lane < MOE_GROUPS, logits, neg)
    gmax = jnp.max(glog, axis=-1, keepdims=True)
    gidx = jnp.min(jnp.where(glog == gmax, lane, 1e9), axis=-1, keepdims=True)
    g_w = 1.0 / jnp.sum(jnp.exp(glog - gmax), axis=-1, keepdims=True)
    e_lo = 32.0 + gidx * MOE_EXPERTS_PER_GROUP
    elog = jnp.where((lane >= e_lo) & (lane < e_lo + MOE_EXPERTS_PER_GROUP), logits, neg)
    v1 = jnp.max(elog, axis=-1, keepdims=True)
    i1 = jnp.min(jnp.where(elog == v1, lane, 1e9), axis=-1, keepdims=True)
    elog2 = jnp.where(lane == i1, neg, elog)
    v2 = jnp.max(elog2, axis=-1, keepdims=True)
    i2 = jnp.min(jnp.where(elog2 == v2, lane, 1e9), axis=-1, keepdims=True)
    e2 = jnp.exp(v2 - v1)
    gate1 = g_w / (1.0 + e2)
    gate2 = g_w * e2 / (1.0 + e2)
    eid_ref[...] = jnp.where(lane == 0.0, i1 - 32.0, jnp.where(lane == 1.0, i2 - 32.0, 0.0)).astype(jnp.int32)
    gate_ref[...] = jnp.where(lane == 0.0, gate1, jnp.where(lane == 1.0, gate2, 0.0))


def _merge(n_blocks, x, mod3, yf, yb, s5u, of, ob, hgs, att_lat, att_ctx, gates, consts):
    row = lambda w: pl.BlockSpec((RB, w), lambda i: (i, 0))
    full = lambda a: pl.BlockSpec(a.shape, lambda i: (0,) * a.ndim)
    n = n_blocks * RB
    wv = MLA_HEADS * MLA_V
    if att_ctx is None:
        att = att_lat
    else:
        att = jnp.concatenate([att_lat, att_ctx], axis=0)
    return pl.pallas_call(
        _merge_kernel,
        grid=(n_blocks,),
        in_specs=[row(D_MODEL), pl.BlockSpec((1, 6, D_MODEL), lambda i: (_mod_row(i), 0, 0)),
                  row(S5_WIDTH), row(S5_WIDTH), row(S5_WIDTH), row(HG_HEADS * HG_V), row(HG_HEADS * HG_V),
                  row(HG_HEADS * HG_V), row(wv), row(3 * D_MODEL)] + [full(a) for a in consts],
        out_specs=[row(D_MODEL), row(D_MODEL), row(128), row(128)],
        out_shape=[jax.ShapeDtypeStruct((n, D_MODEL), F32), jax.ShapeDtypeStruct((n, D_MODEL), F32),
                   jax.ShapeDtypeStruct((n, 128), jnp.int32), jax.ShapeDtypeStruct((n, 128), F32)],
        compiler_params=_params("arbitrary"),
        name="merge_router",
    )(x, mod3, yf, yb, s5u, of, ob, hgs, att, gates, *consts)


def _moe_kernel(bexp_ref, nblk_ref, src_ref, srcn_ref, dst_ref, h_hbm, w1_ref, w3_ref, w2_ref, y_hbm,
                xbuf, ybuf, w1b, w3b, w2b, gsem, ssem):
    i = pl.program_id(0)
    n_used = nblk_ref[0]
    slot = lax.rem(i, 2)

    def gather_start(idx_ref, s):
        for r in range(MOE_ROWS):
            pltpu.make_async_copy(h_hbm.at[pl.ds(idx_ref[0, 0, r], 1), :], xbuf.at[s, pl.ds(r, 1), :],
                                  gsem.at[s]).start()

    def gather_wait(s):
        pltpu.make_async_copy(h_hbm.at[pl.ds(0, MOE_ROWS), :], xbuf.at[s], gsem.at[s]).wait()

    def scatter_start(s):
        for r in range(MOE_ROWS):
            pltpu.make_async_copy(ybuf.at[s, pl.ds(r, 1), :], y_hbm.at[pl.ds(dst_ref[0, 0, r], 1), :],
                                  ssem.at[s]).start()

    def scatter_wait(s):
        pltpu.make_async_copy(ybuf.at[s], y_hbm.at[pl.ds(0, MOE_ROWS), :], ssem.at[s]).wait()

    @pl.when(i < n_used)
    def _():
        @pl.when(i == 0)
        def _():
            gather_start(src_ref, 0)
            ybuf[1] = jnp.zeros((MOE_ROWS, D_MODEL), F32)
            n_real = y_hbm.shape[0] - MOE_ROWS
            fill = pltpu.make_async_copy(ybuf.at[1], y_hbm.at[pl.ds(n_real, MOE_ROWS), :], ssem.at[1])
            fill.start()
            fill.wait()

        @pl.when(i + 1 < n_used)
        def _():
            gather_start(srcn_ref, 1 - slot)

        @pl.when((i == 0) | (bexp_ref[i] != bexp_ref[jnp.maximum(i - 1, 0)]))
        def _():
            w1b[...] = w1_ref[0].astype(BF16)
            w3b[...] = w3_ref[0].astype(BF16)
            w2b[...] = w2_ref[0].astype(BF16)

        gather_wait(slot)

        @pl.when(i >= 2)
        def _():
            scatter_wait(slot)

        x = xbuf[slot].astype(BF16)
        a = jnp.dot(x, w1b[...], preferred_element_type=F32)
        g = jnp.dot(x, w3b[...], preferred_element_type=F32)
        hid = (a * jax.nn.sigmoid(a) * g).astype(BF16)
        ybuf[slot] = jnp.dot(hid, w2b[...], preferred_element_type=F32)
        scatter_start(slot)

        @pl.when(i == n_used - 1)
        def _():
            @pl.when(i >= 1)
            def _():
                scatter_wait(1 - slot)
            scatter_wait(slot)


def _moe(h2, eid, w1, w3, w2):
    n = h2.shape[0]
    n_assign = n * MOE_TOP_K
    n_blocks = (n_assign + MOE_EXPERTS * (MOE_ROWS - 1) + MOE_ROWS - 1) // MOE_ROWS
    flat_e = eid.reshape(n_assign)
    order = jnp.argsort(flat_e, stable=True).astype(jnp.int32)
    experts = jnp.arange(MOE_EXPERTS, dtype=jnp.int32)
    counts = jnp.sum((flat_e[:, None] == experts[None, :]).astype(jnp.int32), axis=0)
    starts = jnp.cumsum(counts) - counts
    padded = (counts + MOE_ROWS - 1) // MOE_ROWS * MOE_ROWS
    p_ends = jnp.cumsum(padded)
    p_starts = p_ends - padded
    n_used_s = p_ends[-1] // MOE_ROWS
    n_used = n_used_s.astype(jnp.int32).reshape(1)
    blk = jnp.arange(n_blocks, dtype=jnp.int32)
    blk_expert = jnp.sum((p_ends[None, :] <= (blk * MOE_ROWS)[:, None]).astype(jnp.int32), axis=1)
    blk_expert = jnp.minimum(blk_expert, MOE_EXPERTS - 1)
    block_expert = jnp.where(blk < n_used_s, blk_expert, blk_expert[jnp.maximum(n_used_s - 1, 0)])
    local = jnp.arange(MOE_ROWS, dtype=jnp.int32)[None, :]
    rank = blk[:, None] * MOE_ROWS + local - p_starts[blk_expert][:, None]
    valid = (rank < counts[blk_expert][:, None]) & (blk < n_used_s)[:, None]
    pos = jnp.clip(starts[blk_expert][:, None] + rank, 0, n_assign - 1)
    assign = order[pos]
    tok = jnp.where(valid, assign // MOE_TOP_K, 0)
    dst = jnp.where(valid, (assign % MOE_TOP_K) * n + assign // MOE_TOP_K, n_assign + local)
    rows_now = pl.BlockSpec((1, 1, MOE_ROWS), lambda i, be, nb: (i, 0, 0), memory_space=pltpu.SMEM)
    rows_next = pl.BlockSpec((1, 1, MOE_ROWS), lambda i, be, nb: (jnp.minimum(i + 1, n_blocks - 1), 0, 0),
                             memory_space=pltpu.SMEM)
    src3 = tok.reshape(n_blocks, 1, MOE_ROWS)
    return pl.pallas_call(
        _moe_kernel,
        grid_spec=pltpu.PrefetchScalarGridSpec(
            num_scalar_prefetch=2,
            grid=(n_blocks,),
            in_specs=[
                rows_now, rows_next, rows_now,
                pl.BlockSpec(memory_space=pl.ANY),
                pl.BlockSpec((1, D_MODEL, MOE_HIDDEN), lambda i, be, nb: (be[i], 0, 0)),
                pl.BlockSpec((1, D_MODEL, MOE_HIDDEN), lambda i, be, nb: (be[i], 0, 0)),
                pl.BlockSpec((1, MOE_HIDDEN, D_MODEL), lambda i, be, nb: (be[i], 0, 0)),
            ],
            out_specs=pl.BlockSpec(memory_space=pl.ANY),
            scratch_shapes=[pltpu.VMEM((2, MOE_ROWS, D_MODEL), F32), pltpu.VMEM((2, MOE_ROWS, D_MODEL), F32),
                            pltpu.VMEM((D_MODEL, MOE_HIDDEN), BF16), pltpu.VMEM((D_MODEL, MOE_HIDDEN), BF16),
                            pltpu.VMEM((MOE_HIDDEN, D_MODEL), BF16),
                            pltpu.SemaphoreType.DMA((2,)), pltpu.SemaphoreType.DMA((2,))],
        ),
        out_shape=jax.ShapeDtypeStruct((n_assign + MOE_ROWS, D_MODEL), F32),
        compiler_params=_params("arbitrary"),
        name="moe_experts",
    )(block_expert, n_used, src3, src3, dst.reshape(n_blocks, 1, MOE_ROWS), h2, w1, w3, w2)


def _combine_kernel(x1_ref, y0_ref, y1_ref, gate_ref, mod_ref, g_ref, o_ref, *, final):
    f = gate_ref[:, 0:1] * y0_ref[...] + gate_ref[:, 1:2] * y1_ref[...]
    x2 = x1_ref[...] + mod_ref[0, 5:6, :] * f
    if final:
        x2 = _rms(x2) * g_ref[...]
    o_ref[...] = x2


def _combine(n_blocks, x1, y, gate, mod3, g_final, final):
    row = lambda w: pl.BlockSpec((RB, w), lambda i: (i, 0))
    n = n_blocks * RB
    return pl.pallas_call(
        functools.partial(_combine_kernel, final=final),
        grid=(n_blocks,),
        in_specs=[row(D_MODEL), row(D_MODEL), pl.BlockSpec((RB, D_MODEL), lambda i: (i + n_blocks, 0)), row(128),
                  pl.BlockSpec((1, 6, D_MODEL), lambda i: (_mod_row(i), 0, 0)),
                  pl.BlockSpec((1, D_MODEL), lambda i: (0, 0))],
        out_specs=row(D_MODEL),
        out_shape=jax.ShapeDtypeStruct((n, D_MODEL), F32),
        compiler_params=_params("arbitrary"),
        name="moe_combine_final" if final else "moe_combine",
    )(x1, y, y, gate, mod3, g_final)


def _rope_rot_cols(w):
    p = ROPE_PAIRS
    return jnp.concatenate([-w[..., p:2 * p], w[..., 0:p], -w[..., 3 * p:4 * p], w[..., 2 * p:3 * p]], axis=-1)


def _pack_w_in(w_in):
    cuts = np.cumsum([S5_WIDTH, MLA_Q_LORA, MLA_KV_LORA, MLA_ROPE, HG_HEADS * HG_K, HG_HEADS * HG_K,
                      HG_HEADS * HG_K, HG_HEADS * HG_V, HG_HEADS * HG_V, D_MODEL, D_MODEL])
    (w_s5, w_cq, w_ckv, w_kpe, w_hq, w_zf, w_zb, w_hi, w_hg, w_ga, w_gb, w_gc) = jnp.split(w_in, cuts, axis=1)
    pad = jnp.zeros((D_MODEL, 256 - MLA_KV_LORA - 2 * MLA_ROPE), w_in.dtype)
    packed = jnp.concatenate([w_s5, w_cq, w_ckv, w_kpe, _rope_rot_cols(w_kpe), pad,
                              w_hq, w_zf, w_zb, w_hi, w_hg, w_ga, w_gb, w_gc], axis=1)
    assert packed.shape[1] == W_IN_COLS
    return packed.astype(BF16)


def _pack_mla(w_uq, w_uk, w_uv):
    hp, n, r = MLA_HEAD_PAD, MLA_NOPE, MLA_ROPE
    wq = w_uq.reshape(MLA_Q_LORA, MLA_HEADS, n + r)
    zq = jnp.zeros((MLA_Q_LORA, MLA_HEADS, hp - n - r), F32)
    q1 = jnp.concatenate([wq, zq], axis=-1).reshape(MLA_Q_LORA, MLA_HEADS * hp)
    q2 = jnp.concatenate([jnp.zeros((MLA_Q_LORA, MLA_HEADS, n), F32), _rope_rot_cols(wq[..., n:]), zq],
                         axis=-1).reshape(MLA_Q_LORA, MLA_HEADS * hp)
    wq_packed = jnp.concatenate([q1, q2], axis=1).astype(BF16)

    wk = w_uk.reshape(MLA_KV_LORA, MLA_HEADS, n)
    k_nope = jnp.concatenate([wk, jnp.zeros((MLA_KV_LORA, MLA_HEADS, hp - n), F32)], axis=-1)
    eye = jnp.eye(r, dtype=F32)
    pe_slot = jnp.concatenate([jnp.zeros((r, n), F32), eye, jnp.zeros((r, hp - n - r), F32)], axis=-1)
    pe_all = jnp.tile(pe_slot[:, None, :], (1, MLA_HEADS, 1))
    zero_pe = jnp.zeros_like(pe_all)
    tail = jnp.zeros((256 - MLA_KV_LORA - 2 * r, MLA_HEADS, hp), F32)
    k1 = jnp.concatenate([k_nope, pe_all, zero_pe, tail], axis=0).reshape(256, MLA_HEADS * hp)
    k2 = jnp.concatenate([jnp.zeros_like(k_nope), zero_pe, pe_all, tail], axis=0).reshape(256, MLA_HEADS * hp)
    wk_packed = jnp.concatenate([k1, k2], axis=1).astype(BF16)
    wv_packed = jnp.concatenate([w_uv, jnp.zeros((256 - MLA_KV_LORA, MLA_HEADS * MLA_V), F32)], axis=0).astype(BF16)
    return wq_packed, wk_packed, wv_packed


def _rope_tables():
    rows = SEQ // GRID_W
    row = jnp.repeat(jnp.arange(rows, dtype=F32), GRID_W)
    col = jnp.tile(jnp.arange(GRID_W, dtype=F32), rows)
    inv = ROPE_BASE ** (-jnp.arange(ROPE_PAIRS, dtype=F32) / ROPE_PAIRS)
    ar, ac = row[:, None] * inv, col[:, None] * inv
    cos = jnp.concatenate([jnp.cos(ar), jnp.cos(ar), jnp.cos(ac), jnp.cos(ac)], axis=1)
    sin = jnp.concatenate([jnp.sin(ar), jnp.sin(ar), jnp.sin(ac), jnp.sin(ac)], axis=1)
    ones = jnp.ones((SEQ, MLA_NOPE), F32)
    tail = MLA_HEAD_PAD - MLA_NOPE - MLA_ROPE
    cos_t = jnp.concatenate([ones, cos, jnp.ones((SEQ, tail), F32)], axis=1)
    sin_t = jnp.concatenate([jnp.zeros((SEQ, MLA_NOPE), F32), sin, jnp.zeros((SEQ, tail), F32)], axis=1)
    cos_t = jnp.concatenate([cos_t, jnp.ones((CTX_LEN, MLA_HEAD_PAD), F32)], axis=0)
    sin_t = jnp.concatenate([sin_t, jnp.zeros((CTX_LEN, MLA_HEAD_PAD), F32)], axis=0)
    return cos_t, sin_t


def _cmul(ar, ai, br, bi):
    return ar * br - ai * bi, ar * bi + ai * br


def _pack_s5(lam_re, lam_im, b_re, b_im, c_re, c_im, log_step):
    bmats, cmats, apows = [], [], []
    eye = jnp.eye(S5_GROUPS, dtype=F32)
    for d in range(2):
        dt = jnp.exp(log_step[d])[:, None]
        mag = jnp.exp(lam_re[d] * dt)
        lb_re, lb_im = mag * jnp.cos(lam_im[d] * dt), mag * jnp.sin(lam_im[d] * dt)
        den = lam_re[d] * lam_re[d] + lam_im[d] * lam_im[d]
        fr = ((lb_re - 1) * lam_re[d] + lb_im * lam_im[d]) / den
        fi = (lb_im * lam_re[d] - (lb_re - 1) * lam_im[d]) / den
        bb_re = fr[..., None] * b_re[d] - fi[..., None] * b_im[d]
        bb_im = fr[..., None] * b_im[d] + fi[..., None] * b_re[d]

        def in_mat(bb):
            return jnp.einsum('gph,gk->ghkp', bb, eye).reshape(S5_WIDTH, S5_LANES)

        def out_mat(cc):
            return jnp.einsum('ghp,gk->gpkh', cc, eye).reshape(S5_LANES, S5_WIDTH)

        bmats.append(jnp.concatenate([in_mat(bb_re), in_mat(bb_im)], axis=1))
        cmats.append(jnp.concatenate([out_mat(c_re[d]), -out_mat(c_im[d])], axis=0))

        a1 = (lb_re.reshape(1, S5_LANES), lb_im.reshape(1, S5_LANES))
        pw = [a1]
        for _ in range(7):
            pw.append(_cmul(*pw[-1], *a1))
        r = jnp.arange(8)[:, None]
        if d == 0:
            keep = lambda sh: (r >= sh).astype(F32)
            carry_pow = lambda c: jnp.concatenate([pw[k][c] for k in range(8)], axis=0)
        else:
            keep = lambda sh: (r < 8 - sh).astype(F32)
            carry_pow = lambda c: jnp.concatenate([pw[7 - k][c] for k in range(8)], axis=0)
        tabs = [jnp.stack([keep(sh) * pw[sh - 1][0], keep(sh) * pw[sh - 1][1]]) for sh in (1, 2, 4)]
        tabs.append(jnp.stack([carry_pow(0), carry_pow(1)]))
        apows.append(jnp.stack(tabs))
    return jnp.stack(bmats).astype(BF16), jnp.stack(cmats).astype(BF16), jnp.stack(apows)


def kernel(x, c, ctx, c_ctx, mod_w, mod_b, norm1_g, norm2_g, w_in, s5_lam_re, s5_lam_im, s5_b_re, s5_b_im, s5_c_re, s5_c_im, s5_log_step, s5_d, s5_glu_w, s5_glu_b, mla_qa_g, mla_kva_g, mla_w_uq, mla_w_uk, mla_w_uv, hg_lb_logits, hg_norm_g, w_pa, w_pb, w_pc, w_out, moe_w_group, moe_b_group, moe_w_expert, moe_b_expert, moe_w1, moe_w3, moe_w2, final_norm_g):
    xt = jnp.concatenate([x.reshape(N_LAT, D_MODEL), ctx.reshape(N_CTX, D_MODEL)], axis=0)
    c_rows = jnp.concatenate([c, c_ctx[None, :], jnp.zeros((16 - BATCH - 1, D_MODEL), F32)], axis=0)
    mod = _modulation(c_rows, mod_w, mod_b).reshape(DEPTH, 16, 6, D_MODEL)

    lb_all = jnp.cumsum(jax.nn.softmax(hg_lb_logits.astype(F32)))
    lb_all = lb_all - lb_all[0]
    cos_t, sin_t = _rope_tables()
    w_tab_np, mask_tab_np = _hgrn_tables()
    w_tab = jnp.asarray(w_tab_np, BF16)
    mask_tab = jnp.asarray(mask_tab_np, F32)
    head_sum = jnp.asarray(np.kron(np.eye(HG_HEADS), np.ones((HG_V, HG_V))), F32)

    out = None
    for layer in range(DEPTH):
        last = layer == DEPTH - 1
        mod3 = mod[layer]
        lb = lb_all[layer]
        hg_scal = jnp.stack([jnp.log(lb), jnp.log1p(-lb), 1.0 - lb]).astype(F32)
        (s5u, cqn, kvn, hq, lff, lfb, kkf, kkb, hi, hgs, gates) = _input_projection(
            xt, mod3, hg_scal, norm1_g[layer][None, :], _pack_w_in(w_in[layer]),
            mla_qa_g[layer][None, :], mla_kva_g[layer][None, :])

        bmat, cmat, apow = _pack_s5(s5_lam_re[layer], s5_lam_im[layer], s5_b_re[layer], s5_b_im[layer],
                                    s5_c_re[layer], s5_c_im[layer], s5_log_step[layer])
        yf, yb = _s5_states(s5u, bmat, cmat, apow)

        wq, wk, wv = _pack_mla(mla_w_uq[layer], mla_w_uk[layer], mla_w_uv[layer])
        q, k, v = _qkv(cqn, kvn, cos_t, sin_t, wq, wk, wv)
        att_lat = _attention(q, k, v, latent=True)
        att_ctx = None if last else _attention(q, k, v, latent=False)

        of, ob = _hgrn(hq, kkf, kkb, lff, lfb, hi, w_tab, mask_tab)

        n_blocks = NB_LAT if last else NB_TOK
        w_route = jnp.zeros((D_MODEL, 128), F32)
        w_route = w_route.at[:, :MOE_GROUPS].set(moe_w_group[layer])
        w_route = w_route.at[:, 32:32 + MOE_EXPERTS].set(moe_w_expert[layer])
        b_route = jnp.zeros((1, 128), F32)
        b_route = b_route.at[0, :MOE_GROUPS].set(moe_b_group[layer])
        b_route = b_route.at[0, 32:32 + MOE_EXPERTS].set(moe_b_expert[layer])
        consts = [s5_d[layer][None, :], s5_glu_w[layer].astype(BF16), s5_glu_b[layer][None, :],
                  hg_norm_g[layer][None, :], head_sum, w_pa[layer].astype(BF16), w_pb[layer].astype(BF16),
                  w_pc[layer].astype(BF16), w_out[layer].astype(BF16), norm2_g[layer][None, :],
                  w_route, b_route]
        x1, h2, eid, gate = _merge(n_blocks, xt, mod3, yf, yb, s5u, of, ob, hgs, att_lat, att_ctx, gates, consts)

        y_moe = _moe(h2, eid[:, :MOE_TOP_K], moe_w1[layer], moe_w3[layer], moe_w2[layer])
        res = _combine(n_blocks, x1, y_moe, gate, mod3, final_norm_g[None, :], final=last)
        if last:
            out = res
        else:
            xt = res
    return out.reshape(BATCH, SEQ, D_MODEL)
```

```python
import functools
import math

import jax
import jax.numpy as jnp
from jax import lax
import numpy as np
from jax.experimental import pallas as pl
from jax.experimental.pallas import tpu as pltpu

F32 = jnp.float32
BF16 = jnp.bfloat16
HIGHEST = lax.Precision.HIGHEST

D_MODEL = 1024
BATCH = 8
SEQ = 2048
DEPTH = 2
GRID_W = 64
CTX_LEN = 256
NORM_EPS = 1e-6

S5_WIDTH = D_MODEL // 4
S5_GROUP_CH = 16
S5_GROUPS = S5_WIDTH // S5_GROUP_CH
S5_STATE = 64
S5_LANES = S5_GROUPS * S5_STATE

MLA_HEADS = D_MODEL // 128
MLA_NOPE = 64
MLA_ROPE = 32
MLA_V = 64
MLA_Q_LORA = D_MODEL // 4
MLA_KV_LORA = D_MODEL // 8
MLA_SCALE = 1.0 / math.sqrt(MLA_NOPE + MLA_ROPE)
MLA_HEAD_PAD = 128
ROPE_PAIRS = MLA_ROPE // 4
ROPE_BASE = 10000.0

HG_HEADS = D_MODEL // 256
HG_K = 128
HG_V = 64
HG_CHUNK = 64
HG_LEVELS = 6
HG_SETS = 2 + HG_LEVELS

MOE_GROUPS = 4
MOE_EXPERTS_PER_GROUP = 8
MOE_EXPERTS = MOE_GROUPS * MOE_EXPERTS_PER_GROUP
MOE_TOP_K = 2
MOE_HIDDEN = D_MODEL // 2
MOE_ROWS = 256

N_LAT = BATCH * SEQ
N_CTX = BATCH * CTX_LEN
N_TOK = N_LAT + N_CTX
RB = CTX_LEN
WB = 2 * RB
NB_LAT = N_LAT // RB
NB_CTX = N_CTX // RB
NB_TOK = N_TOK // RB
LAT_BLOCKS = SEQ // RB
SEQ_BLOCKS = LAT_BLOCKS + 1

C_S5 = 0
C_CQ = 256
C_KV = 512
C_HQ = 768
C_ZF = 1280
C_ZB = 1792
C_HI = 2304
C_HG = 2560
C_GATE = 2816
W_IN_COLS = C_GATE + 3 * D_MODEL

VMEM_LIMIT = 56 * 1024 * 1024


def _params(*sem):
    return pltpu.CompilerParams(dimension_semantics=sem, vmem_limit_bytes=VMEM_LIMIT)


def _rms(v):
    return v * lax.rsqrt(jnp.mean(v * v, axis=-1, keepdims=True) + NORM_EPS)


def _mod_row(i, rb=RB):
    return jnp.where(i < N_LAT // rb, i // (SEQ // rb), BATCH)


def _resident(a):
    return pl.BlockSpec(a.shape, lambda *_: (0,) * a.ndim, pipeline_mode=pl.Buffered(1))


def _mod_kernel(c_ref, w_ref, b_ref, o_ref):
    c = c_ref[...]
    s = c * jax.nn.sigmoid(c)
    o_ref[0] = jnp.dot(s, w_ref[0], precision=HIGHEST, preferred_element_type=F32) + b_ref[0]


def _modulation(c_rows, mod_w, mod_b):
    bn = 1536
    return pl.pallas_call(
        _mod_kernel,
        grid=(DEPTH, 6 * D_MODEL // bn),
        in_specs=[
            pl.BlockSpec((16, D_MODEL), lambda l, j: (0, 0)),
            pl.BlockSpec((1, D_MODEL, bn), lambda l, j: (l, 0, j)),
            pl.BlockSpec((1, 1, bn), lambda l, j: (l, 0, j)),
        ],
        out_specs=pl.BlockSpec((1, 16, bn), lambda l, j: (l, 0, j)),
        out_shape=jax.ShapeDtypeStruct((DEPTH, 16, 6 * D_MODEL), F32),
        compiler_params=_params("arbitrary", "arbitrary"),
        name="adaln_mod",
    )(c_rows, mod_w, mod_b.reshape(DEPTH, 1, 6 * D_MODEL))


def _log_sigmoid(z):
    return jnp.minimum(z, 0.0) - jnp.log1p(jnp.exp(-jnp.abs(z)))


def _in_kernel(hs_ref, x_ref, mod_ref, g1_ref, w_ref, qag_ref, kvg_ref,
               s5u_ref, cqn_ref, kvn_ref, hq_ref, lff_ref, lfb_ref, kkf_ref, kkb_ref,
               hi_ref, hgs_ref, gates_ref):
    x = x_ref[...]
    xn = _rms(x) * g1_ref[...]
    h = (xn * (1.0 + mod_ref[0, 1:2, :]) + mod_ref[0, 0:1, :]).astype(BF16)

    def mm(a, b):
        return jnp.dot(h, w_ref[:, a:b], preferred_element_type=F32)

    s5u_ref[...] = mm(C_S5, C_CQ).astype(BF16)
    cqn_ref[...] = (_rms(mm(C_CQ, C_KV)) * qag_ref[...]).astype(BF16)
    kv = mm(C_KV, C_HQ)
    ckvn = _rms(kv[:, :MLA_KV_LORA]) * kvg_ref[...]
    kvn_ref[...] = jnp.concatenate([ckvn, kv[:, MLA_KV_LORA:]], axis=1).astype(BF16)
    hq_ref[...] = mm(C_HQ, C_ZF).astype(BF16)

    log_lb = hs_ref[0]
    log_1m_lb = hs_ref[1]
    one_m_lb = hs_ref[2]
    for c0, lf_ref, kk_ref in ((C_ZF, lff_ref, kkf_ref), (C_ZB, lfb_ref, kkb_ref)):
        z = mm(c0, c0 + HG_HEADS * HG_K)
        b = log_1m_lb + _log_sigmoid(z)
        m = jnp.maximum(b, log_lb)
        lf_ref[...] = m + jnp.log1p(jnp.exp(-jnp.abs(b - log_lb)))
        kk_ref[...] = (one_m_lb * jax.nn.sigmoid(-z)).astype(BF16)

    hi_ref[...] = mm(C_HI, C_HG).astype(BF16)
    g = mm(C_HG, C_GATE)
    hgs_ref[...] = (g * jax.nn.sigmoid(g)).astype(BF16)
    for j in range(3):
        c0 = C_GATE + j * D_MODEL
        gates_ref[:, j * D_MODEL:(j + 1) * D_MODEL] = jax.nn.sigmoid(mm(c0, c0 + D_MODEL)).astype(BF16)


def _input_projection(x, mod3, hg_scal, g1, w_in, qa_g, kva_g):
    row = lambda w: pl.BlockSpec((WB, w), lambda i: (i, 0))
    shapes = [
        (S5_WIDTH, BF16), (MLA_Q_LORA, BF16), (256, BF16), (HG_HEADS * HG_K, BF16),
        (HG_HEADS * HG_K, F32), (HG_HEADS * HG_K, F32), (HG_HEADS * HG_K, BF16), (HG_HEADS * HG_K, BF16),
        (HG_HEADS * HG_V, BF16), (HG_HEADS * HG_V, BF16), (3 * D_MODEL, BF16),
    ]
    return pl.pallas_call(
        _in_kernel,
        grid=(N_TOK // WB,),
        in_specs=[
            pl.BlockSpec(memory_space=pltpu.SMEM),
            row(D_MODEL),
            pl.BlockSpec((1, 6, D_MODEL), lambda i: (_mod_row(i, WB), 0, 0)),
            _resident(g1), _resident(w_in), _resident(qa_g), _resident(kva_g),
        ],
        out_specs=[row(w) for w, _ in shapes],
        out_shape=[jax.ShapeDtypeStruct((N_TOK, w), dt) for w, dt in shapes],
        compiler_params=_params("arbitrary"),
        name="input_projection",
    )(hg_scal, x, mod3, g1, w_in, qa_g, kva_g)


def _seq_block(b, j):
    return jnp.where(j == 0, NB_LAT + b, b * LAT_BLOCKS + j - 1)


def _seq_block_rev(b, j):
    return jnp.where(j == 0, NB_LAT + b, b * LAT_BLOCKS + LAT_BLOCKS - j)


S5_TB = 64
S5_ROWS = S5_TB * BATCH
S5_STEPS = (CTX_LEN + SEQ) // S5_TB
S5_CTX_STEPS = CTX_LEN // S5_TB


def _s5_kernel(uf_ref, ub_ref, bmat_ref, cmat_ref, a_ref, yf_ref, yb_ref, bu_ref, carry_ref):
    j = pl.program_id(0)

    @pl.when(j == 0)
    def _():
        carry_ref[...] = jnp.zeros_like(carry_ref)

    for d, (u_ref, y_ref) in enumerate(((uf_ref, yf_ref), (ub_ref, yb_ref))):
        bu_ref[...] = jnp.dot(u_ref[...], bmat_ref[d], preferred_element_type=F32)
        are = a_ref[d, 0]
        aim = a_ref[d, 1]

        def body(t, carry, d=d, are=are, aim=aim):
            hre, him = carry
            tt = (S5_TB - 1 - t) if d == 1 else t
            r0 = pl.multiple_of(tt * BATCH, BATCH)
            nre = are * hre - aim * him + bu_ref[pl.ds(r0, BATCH), 0:S5_LANES]
            nim = are * him + aim * hre + bu_ref[pl.ds(r0, BATCH), S5_LANES:2 * S5_LANES]
            bu_ref[pl.ds(r0, BATCH), 0:S5_LANES] = nre
            bu_ref[pl.ds(r0, BATCH), S5_LANES:2 * S5_LANES] = nim
            return nre, nim

        hre, him = lax.fori_loop(0, S5_TB, body, (carry_ref[d, 0], carry_ref[d, 1]), unroll=4)
        carry_ref[d, 0] = hre
        carry_ref[d, 1] = him
        y_ref[...] = jnp.dot(bu_ref[...].astype(BF16), cmat_ref[d], preferred_element_type=F32)


def _s5_block_rev(j):
    return jnp.where(j < S5_CTX_STEPS, S5_CTX_STEPS - 1 - j, S5_STEPS - 1 + S5_CTX_STEPS - j)


def _s5_states(u_tm, bmat, cmat, a_tab):
    full = _resident
    fwd = pl.BlockSpec((S5_ROWS, S5_WIDTH), lambda j: (j, 0))
    bwd = pl.BlockSpec((S5_ROWS, S5_WIDTH), lambda j: (_s5_block_rev(j), 0))
    return pl.pallas_call(
        _s5_kernel,
        grid=(S5_STEPS,),
        in_specs=[fwd, bwd, full(bmat), full(cmat), full(a_tab)],
        out_specs=[fwd, bwd],
        out_shape=[jax.ShapeDtypeStruct((S5_STEPS * S5_ROWS, S5_WIDTH), F32)] * 2,
        scratch_shapes=[pltpu.VMEM((S5_ROWS, 2 * S5_LANES), F32), pltpu.VMEM((2, 2, BATCH, S5_LANES), F32)],
        compiler_params=_params("arbitrary"),
        name="s5_scan",
    )(u_tm, u_tm, bmat, cmat, a_tab)


def _to_time_major(a):
    w = a.shape[1]
    seq = jnp.concatenate([a[N_LAT:].reshape(BATCH, CTX_LEN, w), a[:N_LAT].reshape(BATCH, SEQ, w)], axis=1)
    return seq.transpose(1, 0, 2).reshape((CTX_LEN + SEQ) * BATCH, w)


def _from_time_major(a):
    w = a.shape[1]
    seq = a.reshape(CTX_LEN + SEQ, BATCH, w).transpose(1, 0, 2)
    return jnp.concatenate([seq[:, CTX_LEN:].reshape(N_LAT, w), seq[:, :CTX_LEN].reshape(N_CTX, w)], axis=0)


def _qkv_kernel(cqn_ref, kvn_ref, cos_ref, sin_ref, wq_ref, wk_ref, wv_ref, q_ref, k_ref, v_ref):
    cos = jnp.concatenate([cos_ref[...]] * MLA_HEADS, axis=1)
    sin = jnp.concatenate([sin_ref[...]] * MLA_HEADS, axis=1)
    w = MLA_HEADS * MLA_HEAD_PAD
    q2 = jnp.dot(cqn_ref[...], wq_ref[...], preferred_element_type=F32)
    q_ref[...] = ((q2[:, :w] * cos + q2[:, w:] * sin) * MLA_SCALE).astype(BF16)
    kv = kvn_ref[...]
    k2 = jnp.dot(kv, wk_ref[...], preferred_element_type=F32)
    k_ref[...] = (k2[:, :w] * cos + k2[:, w:] * sin).astype(BF16)
    v_ref[...] = jnp.dot(kv, wv_ref[...], preferred_element_type=F32).astype(BF16)


def _qkv(cqn, kvn, cos_t, sin_t, wq, wk, wv):
    row = lambda w: pl.BlockSpec((RB, w), lambda i: (i, 0))
    full = _resident
    pos = pl.BlockSpec((RB, MLA_HEAD_PAD), lambda i: (jnp.where(i < NB_LAT, i % LAT_BLOCKS, LAT_BLOCKS), 0))
    w = MLA_HEADS * MLA_HEAD_PAD
    return pl.pallas_call(
        _qkv_kernel,
        grid=(NB_TOK,),
        in_specs=[row(MLA_Q_LORA), row(256), pos, pos, full(wq), full(wk), full(wv)],
        out_specs=[row(w), row(w), row(MLA_HEADS * MLA_V)],
        out_shape=[jax.ShapeDtypeStruct((N_TOK, w), BF16), jax.ShapeDtypeStruct((N_TOK, w), BF16),
                   jax.ShapeDtypeStruct((N_TOK, MLA_HEADS * MLA_V), BF16)],
        compiler_params=_params("arbitrary"),
        name="mla_qkv",
    )(cqn, kvn, cos_t, sin_t, wq, wk, wv)


def _attn_kernel(*refs, n_kv):
    q_ref = refs[0]
    k_refs = refs[1:1 + n_kv]
    v_refs = refs[1 + n_kv:1 + 2 * n_kv]
    o_ref = refs[1 + 2 * n_kv]
    nt = (((1,), (1,)), ((), ()))
    for h in range(MLA_HEADS):
        q = q_ref[:, h * MLA_HEAD_PAD:(h + 1) * MLA_HEAD_PAD]
        s = [lax.dot_general(q, k_ref[:, h * MLA_HEAD_PAD:(h + 1) * MLA_HEAD_PAD], nt,
                             preferred_element_type=F32) for k_ref in k_refs]
        m = functools.reduce(jnp.maximum, [jnp.max(si, axis=-1, keepdims=True) for si in s])
        p = [jnp.exp(si - m) for si in s]
        l = functools.reduce(jnp.add, [jnp.sum(pi, axis=-1, keepdims=True) for pi in p])
        o = functools.reduce(jnp.add, [
            jnp.dot(pi.astype(BF16), v_ref[:, h * MLA_V:(h + 1) * MLA_V], preferred_element_type=F32)
            for pi, v_ref in zip(p, v_refs)])
        o_ref[:, h * MLA_V:(h + 1) * MLA_V] = (o / l).astype(BF16)


def _attention(q, k, v, latent):
    w = MLA_HEADS * MLA_HEAD_PAD
    wv = MLA_HEADS * MLA_V
    if latent:
        tq = WB
        nq = SEQ // tq
        q_spec = pl.BlockSpec((tq, w), lambda b, j: (b * nq + j, 0))
        k_specs = [pl.BlockSpec((SEQ, w), lambda b, j: (b, 0)),
                   pl.BlockSpec((CTX_LEN, w), lambda b, j: (NB_LAT + b, 0))]
        v_specs = [pl.BlockSpec((SEQ, wv), lambda b, j: (b, 0)),
                   pl.BlockSpec((CTX_LEN, wv), lambda b, j: (NB_LAT + b, 0))]
        o_spec = pl.BlockSpec((tq, wv), lambda b, j: (b * nq + j, 0))
        n_out = N_LAT
    else:
        tq = CTX_LEN
        nq = 1
        q_spec = pl.BlockSpec((tq, w), lambda b, j: (NB_LAT + b, 0))
        k_specs = [pl.BlockSpec((CTX_LEN, w), lambda b, j: (NB_LAT + b, 0))]
        v_specs = [pl.BlockSpec((CTX_LEN, wv), lambda b, j: (NB_LAT + b, 0))]
        o_spec = pl.BlockSpec((tq, wv), lambda b, j: (b, 0))
        n_out = N_CTX
    n_kv = len(k_specs)
    return pl.pallas_call(
        functools.partial(_attn_kernel, n_kv=n_kv),
        grid=(BATCH, nq),
        in_specs=[q_spec] + k_specs + v_specs,
        out_specs=o_spec,
        out_shape=jax.ShapeDtypeStruct((n_out, wv), BF16),
        compiler_params=_params("arbitrary", "arbitrary"),
        name="mla_attention_lat" if latent else "mla_attention_ctx",
    )(q, *([k] * n_kv), *([v] * n_kv))


def _hgrn_tables():
    c = HG_CHUNK
    w = np.zeros((2, HG_SETS * c, c), np.float32)
    mask = np.zeros((2, HG_LEVELS, c, c), np.float32)
    idx = np.arange(c)
    for r in range(c):
        w[0, r, :r + 1] = 1
        w[1, r, r:] = 1
        w[0, c + r, r + 1:] = 1
        w[1, c + r, :r] = 1
    for l in range(HG_LEVELS):
        m = c >> (l + 1)
        for r in range(c):
            base = (r // (2 * m)) * 2 * m
            mid = base + m
            later = r >= mid
            row = (2 + l) * c + r
            if later:
                w[0, row, mid:r + 1] = 1
                w[1, row, mid:r] = 1
            else:
                w[0, row, r + 1:mid] = 1
                w[1, row, r:mid] = 1
        same = (idx[:, None] // (2 * m)) == (idx[None, :] // (2 * m))
        q_later = (idx[:, None] % (2 * m)) >= m
        k_later = (idx[None, :] % (2 * m)) >= m
        mask[0, l] = same & q_later & ~k_later
        mask[1, l] = same & ~q_later & k_later
    return w, mask


def _hgrn_dir(d, q_ref, k_ref, lf_ref, v_ref, w_ref, mask_ref, st_ref, o_ref):
    c = HG_CHUNK
    n_chunks = RB // c
    nt = (((1,), (1,)), ((), ()))
    tn = (((0,), (0,)), ((), ()))
    last = c - 1 if d == 0 else 0
    wsel = w_ref[d]
    e = []
    for ci in range(n_chunks):
        g = lf_ref[ci * c:(ci + 1) * c, :]
        g_hi = g.astype(BF16)
        g_lo = (g - g_hi.astype(F32)).astype(BF16)
        expo = jnp.dot(wsel, jnp.concatenate([g_hi, g_lo], axis=0), preferred_element_type=F32)
        e.append(jnp.exp(expo))

    def rows(s):
        return jnp.concatenate([e[ci][s * c:(s + 1) * c] for ci in range(n_chunks)], axis=0)

    q = q_ref[...].astype(F32)
    k = k_ref[...].astype(F32)
    q_state = (q * rows(0)).astype(BF16)
    k_state = (k * rows(1)).astype(BF16)
    q_lvl, k_lvl = [], []
    for l in range(HG_LEVELS):
        el = rows(2 + l)
        q_lvl.append((q * el).astype(BF16))
        k_lvl.append((k * el).astype(BF16))
    qk = q * k
    order = range(n_chunks) if d == 0 else range(n_chunks - 1, -1, -1)
    for h in range(HG_HEADS):
        ks = slice(h * HG_K, (h + 1) * HG_K)
        vs = slice(h * HG_V, (h + 1) * HG_V)
        v = v_ref[:, vs]
        scores = jnp.zeros((RB, RB), F32)
        for l in range(HG_LEVELS):
            p = lax.dot_general(q_lvl[l][:, ks], k_lvl[l][:, ks], nt, preferred_element_type=F32)
            scores = scores + p * mask_ref[d, l]
        o = jnp.dot(scores.astype(BF16), v, preferred_element_type=F32)
        o = o + jnp.sum(qk[:, ks], axis=-1, keepdims=True) * v.astype(F32)
        st = st_ref[d, h]
        o_state = [None] * n_chunks
        for ci in order:
            r = slice(ci * c, (ci + 1) * c)
            o_state[ci] = lax.dot_general(q_state[r, ks], st.astype(BF16), nt, preferred_element_type=F32)
            inc = lax.dot_general(v[r], k_state[r, ks], tn, preferred_element_type=F32)
            st = st * e[ci][last:last + 1, ks] + inc
        st_ref[d, h] = st
        o_ref[:, vs] = o + jnp.concatenate(o_state, axis=0)


def _hgrn_kernel(qf_ref, kf_ref, lff_ref, vf_ref, qb_ref, kb_ref, lfb_ref, vb_ref, w_ref, mask_ref,
                 of_ref, ob_ref, st_ref):
    j = pl.program_id(1)

    @pl.when(j == 0)
    def _():
        st_ref[...] = jnp.zeros_like(st_ref)

    _hgrn_dir(0, qf_ref, kf_ref, lff_ref, vf_ref, w_ref, mask_ref, st_ref, of_ref)
    _hgrn_dir(1, qb_ref, kb_ref, lfb_ref, vb_ref, w_ref, mask_ref, st_ref, ob_ref)


def _hgrn(hq, kkf, kkb, lff, lfb, hi, w_tab, mask_tab):
    full = _resident
    fwd = lambda w: pl.BlockSpec((RB, w), lambda b, j: (_seq_block(b, j), 0))
    bwd = lambda w: pl.BlockSpec((RB, w), lambda b, j: (_seq_block_rev(b, j), 0))
    wk = HG_HEADS * HG_K
    wv = HG_HEADS * HG_V
    return pl.pallas_call(
        _hgrn_kernel,
        grid=(BATCH, SEQ_BLOCKS),
        in_specs=[fwd(wk), fwd(wk), fwd(wk), fwd(wv), bwd(wk), bwd(wk), bwd(wk), bwd(wv),
                  full(w_tab), full(mask_tab)],
        out_specs=[fwd(wv), bwd(wv)],
        out_shape=[jax.ShapeDtypeStruct((N_TOK, wv), F32)] * 2,
        scratch_shapes=[pltpu.VMEM((2, HG_HEADS, HG_V, HG_K), F32)],
        compiler_params=_params("arbitrary", "arbitrary"),
        name="hgrn2_scan",
    )(hq, kkf, lff, hi, hq, kkb, lfb, hi, w_tab, mask_tab)


def _gelu_tanh(x):
    return 0.5 * x * (1.0 + jnp.tanh(math.sqrt(2.0 / math.pi) * (x + 0.044715 * (x * x * x))))


def _merge_kernel(x_ref, mod_ref, yf_ref, yb_ref, u_ref, of_ref, ob_ref, hgs_ref, att_ref, gates_ref,
                  s5d_ref, gluw_ref, glub_ref, hgg_ref, hsum_ref, wpa_ref, wpb_ref, wpc_ref, wout_ref,
                  g2_ref, wr_ref, br_ref,
                  x1_ref, h2_ref, eid_ref, gate_ref):
    y = yf_ref[...] + yb_ref[...] + s5d_ref[...] * u_ref[...].astype(F32)
    y = _gelu_tanh(y)
    y = y * jax.nn.sigmoid(jnp.dot(y.astype(BF16), gluw_ref[...], preferred_element_type=F32) + glub_ref[...])
    o = of_ref[...] + ob_ref[...]
    ms = jnp.dot(o * o, hsum_ref[...], precision=HIGHEST, preferred_element_type=F32) * (1.0 / HG_V)
    o = o * lax.rsqrt(ms + NORM_EPS) * hgg_ref[...] * hgs_ref[...].astype(F32)

    d = D_MODEL
    merged = (gates_ref[:, 0:d].astype(F32) * jnp.dot(y.astype(BF16), wpa_ref[...], preferred_element_type=F32)
              + gates_ref[:, d:2 * d].astype(F32) * jnp.dot(att_ref[...], wpb_ref[...], preferred_element_type=F32)
              + gates_ref[:, 2 * d:3 * d].astype(F32) * jnp.dot(o.astype(BF16), wpc_ref[...], preferred_element_type=F32))
    y_out = jnp.dot(merged.astype(BF16), wout_ref[...], preferred_element_type=F32)
    x1 = x_ref[...] + mod_ref[0, 2:3, :] * y_out
    x1_ref[...] = x1
    h2 = _rms(x1) * g2_ref[...] * (1.0 + mod_ref[0, 4:5, :]) + mod_ref[0, 3:4, :]
    h2_ref[...] = h2

    logits = jnp.dot(h2, wr_ref[...], precision=HIGHEST, preferred_element_type=F32) + br_ref[...]
    lane = lax.broadcasted_iota(jnp.int32, logits.shape, 1).astype(F32)
    neg = -jnp.inf
    glog = jnp.where(lane < MOE_GROUPS, logits, neg)
    gmax = jnp.max(glog, axis=-1, keepdims=True)
    gidx = jnp.min(jnp.where(glog == gmax, lane, 1e9), axis=-1, keepdims=True)
    g_w = 1.0 / jnp.sum(jnp.exp(glog - gmax), axis=-1, keepdims=True)
    e_lo = 32.0 + gidx * MOE_EXPERTS_PER_GROUP
    elog = jnp.where((lane >= e_lo) & (lane < e_lo + MOE_EXPERTS_PER_GROUP), logits, neg)
    v1 = jnp.max(elog, axis=-1, keepdims=True)
    i1 = jnp.min(jnp.where(elog == v1, lane, 1e9), axis=-1, keepdims=True)
    elog2 = jnp.where(lane == i1, neg, elog)
    v2 = jnp.max(elog2, axis=-1, keepdims=True)
    i2 = jnp.min(jnp.where(elog2 == v2, lane, 1e9), axis=-1, keepdims=True)
    e2 = jnp.exp(v2 - v1)
    gate1 = g_w / (1.0 + e2)
    gate2 = g_w * e2 / (1.0 + e2)
    eid_ref[...] = jnp.where(lane == 0.0, i1 - 32.0, jnp.where(lane == 1.0, i2 - 32.0, 0.0)).astype(jnp.int32)
    gate_ref[...] = jnp.where(lane == 0.0, gate1, jnp.where(lane == 1.0, gate2, 0.0))


def _merge(n_blocks, x, mod3, yf, yb, s5u, of, ob, hgs, att_lat, att_ctx, gates, consts):
    row = lambda w: pl.BlockSpec((WB, w), lambda i: (i, 0))
    n = n_blocks * RB
    wv = MLA_HEADS * MLA_V
    if att_ctx is None:
        att = att_lat
    else:
        att = jnp.concatenate([att_lat, att_ctx], axis=0)
    return pl.pallas_call(
        _merge_kernel,
        grid=(n // WB,),
        in_specs=[row(D_MODEL), pl.BlockSpec((1, 6, D_MODEL), lambda i: (_mod_row(i, WB), 0, 0)),
                  row(S5_WIDTH), row(S5_WIDTH), row(S5_WIDTH), row(HG_HEADS * HG_V), row(HG_HEADS * HG_V),
                  row(HG_HEADS * HG_V), row(wv), row(3 * D_MODEL)] + [_resident(a) for a in consts],
        out_specs=[row(D_MODEL), row(D_MODEL), row(128), row(128)],
        out_shape=[jax.ShapeDtypeStruct((n, D_MODEL), F32), jax.ShapeDtypeStruct((n, D_MODEL), F32),
                   jax.ShapeDtypeStruct((n, 128), jnp.int32), jax.ShapeDtypeStruct((n, 128), F32)],
        compiler_params=_params("arbitrary"),
        name="merge_router",
    )(x, mod3, yf, yb, s5u, of, ob, hgs, att, gates, *consts)


def _moe_kernel(bexp_ref, nblk_ref, src_ref, srcn_ref, dst_ref, h_hbm, w1_ref, w3_ref, w2_ref, y_hbm,
                xbuf, ybuf, w1b, w3b, w2b, gsem, ssem):
    i = pl.program_id(0)
    n_used = nblk_ref[0]
    slot = lax.rem(i, 2)

    def gather_start(idx_ref, s):
        for r in range(MOE_ROWS):
            pltpu.make_async_copy(h_hbm.at[pl.ds(idx_ref[0, 0, r], 1), :], xbuf.at[s, pl.ds(r, 1), :],
                                  gsem.at[s]).start()

    def gather_wait(s):
        pltpu.make_async_copy(h_hbm.at[pl.ds(0, MOE_ROWS), :], xbuf.at[s], gsem.at[s]).wait()

    def scatter_start(s):
        for r in range(MOE_ROWS):
            pltpu.make_async_copy(ybuf.at[s, pl.ds(r, 1), :], y_hbm.at[pl.ds(dst_ref[0, 0, r], 1), :],
                                  ssem.at[s]).start()

    def scatter_wait(s):
        pltpu.make_async_copy(ybuf.at[s], y_hbm.at[pl.ds(0, MOE_ROWS), :], ssem.at[s]).wait()

    @pl.when(i < n_used)
    def _():
        @pl.when(i == 0)
        def _():
            gather_start(src_ref, 0)
            ybuf[1] = jnp.zeros((MOE_ROWS, D_MODEL), F32)
            n_real = y_hbm.shape[0] - MOE_ROWS
            fill = pltpu.make_async_copy(ybuf.at[1], y_hbm.at[pl.ds(n_real, MOE_ROWS), :], ssem.at[1])
            fill.start()
            fill.wait()

        @pl.when(i + 1 < n_used)
        def _():
            gather_start(srcn_ref, 1 - slot)

        @pl.when((i == 0) | (bexp_ref[i] != bexp_ref[jnp.maximum(i - 1, 0)]))
        def _():
            w1b[...] = w1_ref[0, 0].astype(BF16)
            w3b[...] = w3_ref[0, 0].astype(BF16)
            w2b[...] = w2_ref[0, 0].astype(BF16)

        gather_wait(slot)

        @pl.when(i >= 2)
        def _():
            scatter_wait(slot)

        x = xbuf[slot].astype(BF16)
        a = jnp.dot(x, w1b[...], preferred_element_type=F32)
        g = jnp.dot(x, w3b[...], preferred_element_type=F32)
        hid = (a * jax.nn.sigmoid(a) * g).astype(BF16)
        ybuf[slot] = jnp.dot(hid, w2b[...], preferred_element_type=F32)
        scatter_start(slot)

        @pl.when(i == n_used - 1)
        def _():
            @pl.when(i >= 1)
            def _():
                scatter_wait(1 - slot)
            scatter_wait(slot)


def _moe(h2, eid, w1, w3, w2, layer):
    n = h2.shape[0]
    n_assign = n * MOE_TOP_K
    n_blocks = (n_assign + MOE_EXPERTS * (MOE_ROWS - 1) + MOE_ROWS - 1) // MOE_ROWS
    flat_e = eid.reshape(n_assign)
    order = jnp.argsort(flat_e, stable=True).astype(jnp.int32)
    experts = jnp.arange(MOE_EXPERTS, dtype=jnp.int32)
    counts = jnp.sum((flat_e[:, None] == experts[None, :]).astype(jnp.int32), axis=0)
    starts = jnp.cumsum(counts) - counts
    padded = (counts + MOE_ROWS - 1) // MOE_ROWS * MOE_ROWS
    p_ends = jnp.cumsum(padded)
    p_starts = p_ends - padded
    n_used_s = p_ends[-1] // MOE_ROWS
    n_used = n_used_s.astype(jnp.int32).reshape(1)
    blk = jnp.arange(n_blocks, dtype=jnp.int32)
    blk_expert = jnp.sum((p_ends[None, :] <= (blk * MOE_ROWS)[:, None]).astype(jnp.int32), axis=1)
    blk_expert = jnp.minimum(blk_expert, MOE_EXPERTS - 1)
    block_expert = jnp.where(blk < n_used_s, blk_expert, blk_expert[jnp.maximum(n_used_s - 1, 0)])
    local = jnp.arange(MOE_ROWS, dtype=jnp.int32)[None, :]
    rank = blk[:, None] * MOE_ROWS + local - p_starts[blk_expert][:, None]
    valid = (rank < counts[blk_expert][:, None]) & (blk < n_used_s)[:, None]
    pos = jnp.clip(starts[blk_expert][:, None] + rank, 0, n_assign - 1)
    assign = order[pos]
    tok = jnp.where(valid, assign // MOE_TOP_K, 0)
    dst = jnp.where(valid, (assign % MOE_TOP_K) * n + assign // MOE_TOP_K, n_assign + local)
    rows_now = pl.BlockSpec((1, 1, MOE_ROWS), lambda i, be, nb: (i, 0, 0), memory_space=pltpu.SMEM)
    rows_next = pl.BlockSpec((1, 1, MOE_ROWS), lambda i, be, nb: (jnp.minimum(i + 1, n_blocks - 1), 0, 0),
                             memory_space=pltpu.SMEM)
    src3 = tok.reshape(n_blocks, 1, MOE_ROWS)
    return pl.pallas_call(
        _moe_kernel,
        grid_spec=pltpu.PrefetchScalarGridSpec(
            num_scalar_prefetch=2,
            grid=(n_blocks,),
            in_specs=[
                rows_now, rows_next, rows_now,
                pl.BlockSpec(memory_space=pl.ANY),
                pl.BlockSpec((1, 1, D_MODEL, MOE_HIDDEN), lambda i, be, nb: (layer, be[i], 0, 0)),
                pl.BlockSpec((1, 1, D_MODEL, MOE_HIDDEN), lambda i, be, nb: (layer, be[i], 0, 0)),
                pl.BlockSpec((1, 1, MOE_HIDDEN, D_MODEL), lambda i, be, nb: (layer, be[i], 0, 0)),
            ],
            out_specs=pl.BlockSpec(memory_space=pl.ANY),
            scratch_shapes=[pltpu.VMEM((2, MOE_ROWS, D_MODEL), F32), pltpu.VMEM((2, MOE_ROWS, D_MODEL), F32),
                            pltpu.VMEM((D_MODEL, MOE_HIDDEN), BF16), pltpu.VMEM((D_MODEL, MOE_HIDDEN), BF16),
                            pltpu.VMEM((MOE_HIDDEN, D_MODEL), BF16),
                            pltpu.SemaphoreType.DMA((2,)), pltpu.SemaphoreType.DMA((2,))],
        ),
        out_shape=jax.ShapeDtypeStruct((n_assign + MOE_ROWS, D_MODEL), F32),
        compiler_params=_params("arbitrary"),
        name="moe_experts",
    )(block_expert, n_used, src3, src3, dst.reshape(n_blocks, 1, MOE_ROWS), h2, w1, w3, w2)


def _combine_kernel(x1_ref, y0_ref, y1_ref, gate_ref, mod_ref, g_ref, o_ref, *, final):
    f = gate_ref[:, 0:1] * y0_ref[...] + gate_ref[:, 1:2] * y1_ref[...]
    x2 = x1_ref[...] + mod_ref[0, 5:6, :] * f
    if final:
        x2 = _rms(x2) * g_ref[...]
    o_ref[...] = x2


def _combine(n_blocks, x1, y, gate, mod3, g_final, final):
    row = lambda w: pl.BlockSpec((RB, w), lambda i: (i, 0))
    n = n_blocks * RB
    return pl.pallas_call(
        functools.partial(_combine_kernel, final=final),
        grid=(n_blocks,),
        in_specs=[row(D_MODEL), row(D_MODEL), pl.BlockSpec((RB, D_MODEL), lambda i: (i + n_blocks, 0)), row(128),
                  pl.BlockSpec((1, 6, D_MODEL), lambda i: (_mod_row(i), 0, 0)),
                  pl.BlockSpec((1, D_MODEL), lambda i: (0, 0))],
        out_specs=row(D_MODEL),
        out_shape=jax.ShapeDtypeStruct((n, D_MODEL), F32),
        compiler_params=_params("arbitrary"),
        name="moe_combine_final" if final else "moe_combine",
    )(x1, y, y, gate, mod3, g_final)


def _rope_rot_cols(w):
    p = ROPE_PAIRS
    return jnp.concatenate([-w[..., p:2 * p], w[..., 0:p], -w[..., 3 * p:4 * p], w[..., 2 * p:3 * p]], axis=-1)


def _pack_w_in(w_in):
    cuts = np.cumsum([S5_WIDTH, MLA_Q_LORA, MLA_KV_LORA, MLA_ROPE, HG_HEADS * HG_K, HG_HEADS * HG_K,
                      HG_HEADS * HG_K, HG_HEADS * HG_V, HG_HEADS * HG_V, D_MODEL, D_MODEL])
    (w_s5, w_cq, w_ckv, w_kpe, w_hq, w_zf, w_zb, w_hi, w_hg, w_ga, w_gb, w_gc) = jnp.split(w_in, cuts, axis=1)
    pad = jnp.zeros((D_MODEL, 256 - MLA_KV_LORA - 2 * MLA_ROPE), w_in.dtype)
    packed = jnp.concatenate([w_s5, w_cq, w_ckv, w_kpe, _rope_rot_cols(w_kpe), pad,
                              w_hq, w_zf, w_zb, w_hi, w_hg, w_ga, w_gb, w_gc], axis=1)
    assert packed.shape[1] == W_IN_COLS
    return packed.astype(BF16)


def _pack_mla(w_uq, w_uk, w_uv):
    hp, n, r = MLA_HEAD_PAD, MLA_NOPE, MLA_ROPE
    wq = w_uq.reshape(MLA_Q_LORA, MLA_HEADS, n + r)
    zq = jnp.zeros((MLA_Q_LORA, MLA_HEADS, hp - n - r), F32)
    q1 = jnp.concatenate([wq, zq], axis=-1).reshape(MLA_Q_LORA, MLA_HEADS * hp)
    q2 = jnp.concatenate([jnp.zeros((MLA_Q_LORA, MLA_HEADS, n), F32), _rope_rot_cols(wq[..., n:]), zq],
                         axis=-1).reshape(MLA_Q_LORA, MLA_HEADS * hp)
    wq_packed = jnp.concatenate([q1, q2], axis=1).astype(BF16)

    wk = w_uk.reshape(MLA_KV_LORA, MLA_HEADS, n)
    k_nope = jnp.concatenate([wk, jnp.zeros((MLA_KV_LORA, MLA_HEADS, hp - n), F32)], axis=-1)
    eye = jnp.eye(r, dtype=F32)
    pe_slot = jnp.concatenate([jnp.zeros((r, n), F32), eye, jnp.zeros((r, hp - n - r), F32)], axis=-1)
    pe_all = jnp.tile(pe_slot[:, None, :], (1, MLA_HEADS, 1))
    zero_pe = jnp.zeros_like(pe_all)
    tail = jnp.zeros((256 - MLA_KV_LORA - 2 * r, MLA_HEADS, hp), F32)
    k1 = jnp.concatenate([k_nope, pe_all, zero_pe, tail], axis=0).reshape(256, MLA_HEADS * hp)
    k2 = jnp.concatenate([jnp.zeros_like(k_nope), zero_pe, pe_all, tail], axis=0).reshape(256, MLA_HEADS * hp)
    wk_packed = jnp.concatenate([k1, k2], axis=1).astype(BF16)
    wv_packed = jnp.concatenate([w_uv, jnp.zeros((256 - MLA_KV_LORA, MLA_HEADS * MLA_V), F32)], axis=0).astype(BF16)
    return wq_packed, wk_packed, wv_packed


def _rope_tables():
    rows = SEQ // GRID_W
    row = jnp.repeat(jnp.arange(rows, dtype=F32), GRID_W)
    col = jnp.tile(jnp.arange(GRID_W, dtype=F32), rows)
    inv = ROPE_BASE ** (-jnp.arange(ROPE_PAIRS, dtype=F32) / ROPE_PAIRS)
    ar, ac = row[:, None] * inv, col[:, None] * inv
    cos = jnp.concatenate([jnp.cos(ar), jnp.cos(ar), jnp.cos(ac), jnp.cos(ac)], axis=1)
    sin = jnp.concatenate([jnp.sin(ar), jnp.sin(ar), jnp.sin(ac), jnp.sin(ac)], axis=1)
    ones = jnp.ones((SEQ, MLA_NOPE), F32)
    tail = MLA_HEAD_PAD - MLA_NOPE - MLA_ROPE
    cos_t = jnp.concatenate([ones, cos, jnp.ones((SEQ, tail), F32)], axis=1)
    sin_t = jnp.concatenate([jnp.zeros((SEQ, MLA_NOPE), F32), sin, jnp.zeros((SEQ, tail), F32)], axis=1)
    cos_t = jnp.concatenate([cos_t, jnp.ones((CTX_LEN, MLA_HEAD_PAD), F32)], axis=0)
    sin_t = jnp.concatenate([sin_t, jnp.zeros((CTX_LEN, MLA_HEAD_PAD), F32)], axis=0)
    return cos_t, sin_t


def _pack_s5(lam_re, lam_im, b_re, b_im, c_re, c_im, log_step):
    bmats, cmats, a_tabs = [], [], []
    eye = jnp.eye(S5_GROUPS, dtype=F32)
    for d in range(2):
        dt = jnp.exp(log_step[d])[:, None]
        mag = jnp.exp(lam_re[d] * dt)
        lb_re, lb_im = mag * jnp.cos(lam_im[d] * dt), mag * jnp.sin(lam_im[d] * dt)
        den = lam_re[d] * lam_re[d] + lam_im[d] * lam_im[d]
        fr = ((lb_re - 1) * lam_re[d] + lb_im * lam_im[d]) / den
        fi = (lb_im * lam_re[d] - (lb_re - 1) * lam_im[d]) / den
        bb_re = fr[..., None] * b_re[d] - fi[..., None] * b_im[d]
        bb_im = fr[..., None] * b_im[d] + fi[..., None] * b_re[d]

        def in_mat(bb):
            return jnp.einsum('gph,gk->ghkp', bb, eye).reshape(S5_WIDTH, S5_LANES)

        def out_mat(cc):
            return jnp.einsum('ghp,gk->gpkh', cc, eye).reshape(S5_LANES, S5_WIDTH)

        bmats.append(jnp.concatenate([in_mat(bb_re), in_mat(bb_im)], axis=1))
        cmats.append(jnp.concatenate([out_mat(c_re[d]), -out_mat(c_im[d])], axis=0))

        a_tabs.append(jnp.stack([jnp.broadcast_to(lb_re.reshape(1, S5_LANES), (BATCH, S5_LANES)),
                                 jnp.broadcast_to(lb_im.reshape(1, S5_LANES), (BATCH, S5_LANES))]))
    return jnp.stack(bmats).astype(BF16), jnp.stack(cmats).astype(BF16), jnp.stack(a_tabs)


def kernel(x, c, ctx, c_ctx, mod_w, mod_b, norm1_g, norm2_g, w_in, s5_lam_re, s5_lam_im, s5_b_re, s5_b_im, s5_c_re, s5_c_im, s5_log_step, s5_d, s5_glu_w, s5_glu_b, mla_qa_g, mla_kva_g, mla_w_uq, mla_w_uk, mla_w_uv, hg_lb_logits, hg_norm_g, w_pa, w_pb, w_pc, w_out, moe_w_group, moe_b_group, moe_w_expert, moe_b_expert, moe_w1, moe_w3, moe_w2, final_norm_g):
    xt = jnp.concatenate([x.reshape(N_LAT, D_MODEL), ctx.reshape(N_CTX, D_MODEL)], axis=0)
    c_rows = jnp.concatenate([c, c_ctx[None, :], jnp.zeros((16 - BATCH - 1, D_MODEL), F32)], axis=0)
    mod = _modulation(c_rows, mod_w, mod_b).reshape(DEPTH, 16, 6, D_MODEL)

    lb_all = jnp.cumsum(jax.nn.softmax(hg_lb_logits.astype(F32)))
    lb_all = lb_all - lb_all[0]
    cos_t, sin_t = _rope_tables()
    w_tab_np, mask_tab_np = _hgrn_tables()
    w_tab = jnp.asarray(np.concatenate([w_tab_np, w_tab_np], axis=-1), BF16)
    mask_tab = jnp.asarray(np.kron(np.eye(RB // HG_CHUNK, dtype=np.float32), mask_tab_np), F32)
    head_sum = jnp.asarray(np.kron(np.eye(HG_HEADS), np.ones((HG_V, HG_V))), F32)

    out = None
    for layer in range(DEPTH):
        last = layer == DEPTH - 1
        mod3 = mod[layer]
        lb = lb_all[layer]
        hg_scal = jnp.stack([jnp.log(lb), jnp.log1p(-lb), 1.0 - lb]).astype(F32)
        (s5u, cqn, kvn, hq, lff, lfb, kkf, kkb, hi, hgs, gates) = _input_projection(
            xt, mod3, hg_scal, norm1_g[layer][None, :], _pack_w_in(w_in[layer]),
            mla_qa_g[layer][None, :], mla_kva_g[layer][None, :])

        bmat, cmat, a_tab = _pack_s5(s5_lam_re[layer], s5_lam_im[layer], s5_b_re[layer], s5_b_im[layer],
                                     s5_c_re[layer], s5_c_im[layer], s5_log_step[layer])
        yf, yb = [_from_time_major(y) for y in _s5_states(_to_time_major(s5u), bmat, cmat, a_tab)]

        wq, wk, wv = _pack_mla(mla_w_uq[layer], mla_w_uk[layer], mla_w_uv[layer])
        q, k, v = _qkv(cqn, kvn, cos_t, sin_t, wq, wk, wv)
        att_lat = _attention(q, k, v, latent=True)
        att_ctx = None if last else _attention(q, k, v, latent=False)

        of, ob = _hgrn(hq, kkf, kkb, lff, lfb, hi, w_tab, mask_tab)

        n_blocks = NB_LAT if last else NB_TOK
        w_route = jnp.zeros((D_MODEL, 128), F32)
        w_route = w_route.at[:, :MOE_GROUPS].set(moe_w_group[layer])
        w_route = w_route.at[:, 32:32 + MOE_EXPERTS].set(moe_w_expert[layer])
        b_route = jnp.zeros((1, 128), F32)
        b_route = b_route.at[0, :MOE_GROUPS].set(moe_b_group[layer])
        b_route = b_route.at[0, 32:32 + MOE_EXPERTS].set(moe_b_expert[layer])
        consts = [s5_d[layer][None, :], s5_glu_w[layer].astype(BF16), s5_glu_b[layer][None, :],
                  hg_norm_g[layer][None, :], head_sum, w_pa[layer].astype(BF16), w_pb[layer].astype(BF16),
                  w_pc[layer].astype(BF16), w_out[layer].astype(BF16), norm2_g[layer][None, :],
                  w_route, b_route]
        x1, h2, eid, gate = _merge(n_blocks, xt, mod3, yf, yb, s5u, of, ob, hgs, att_lat, att_ctx, gates, consts)

        y_moe = _moe(h2, eid[:, :MOE_TOP_K], moe_w1, moe_w3, moe_w2, layer)
        res = _combine(n_blocks, x1, y_moe, gate, mod3, final_norm_g[None, :], final=last)
        if last:
            out = res
        else:
            xt = res
    return out.reshape(BATCH, SEQ, D_MODEL)
```

```python
import functools
import math

import jax
import jax.numpy as jnp
from jax import lax
import numpy as np
from jax.experimental import pallas as pl
from jax.experimental.pallas import tpu as pltpu

F32 = jnp.float32
BF16 = jnp.bfloat16
HIGHEST = lax.Precision.HIGHEST

D_MODEL = 1024
BATCH = 8
SEQ = 2048
DEPTH = 2
GRID_W = 64
CTX_LEN = 256
NORM_EPS = 1e-6

S5_WIDTH = D_MODEL // 4
S5_GROUP_CH = 16
S5_GROUPS = S5_WIDTH // S5_GROUP_CH
S5_STATE = 64
S5_LANES = S5_GROUPS * S5_STATE

MLA_HEADS = D_MODEL // 128
MLA_NOPE = 64
MLA_ROPE = 32
MLA_V = 64
MLA_Q_LORA = D_MODEL // 4
MLA_KV_LORA = D_MODEL // 8
MLA_SCALE = 1.0 / math.sqrt(MLA_NOPE + MLA_ROPE)
MLA_HEAD_PAD = 128
ROPE_PAIRS = MLA_ROPE // 4
ROPE_BASE = 10000.0

HG_HEADS = D_MODEL // 256
HG_K = 128
HG_V = 64
HG_CHUNK = 64
HG_LEVELS = 6
HG_SETS = 2 + HG_LEVELS

MOE_GROUPS = 4
MOE_EXPERTS_PER_GROUP = 8
MOE_EXPERTS = MOE_GROUPS * MOE_EXPERTS_PER_GROUP
MOE_TOP_K = 2
MOE_HIDDEN = D_MODEL // 2
MOE_ROWS = 256

N_LAT = BATCH * SEQ
N_CTX = BATCH * CTX_LEN
N_TOK = N_LAT + N_CTX
RB = CTX_LEN
WB = 2 * RB
NB_LAT = N_LAT // RB
NB_CTX = N_CTX // RB
NB_TOK = N_TOK // RB
LAT_BLOCKS = SEQ // RB
SEQ_BLOCKS = LAT_BLOCKS + 1

C_S5 = 0
C_CQ = 256
C_KV = 512
C_HQ = 768
C_ZF = 1280
C_ZB = 1792
C_HI = 2304
C_HG = 2560
C_GATE = 2816
W_IN_COLS = C_GATE + 3 * D_MODEL

VMEM_LIMIT = 56 * 1024 * 1024


def _params(*sem):
    return pltpu.CompilerParams(dimension_semantics=sem, vmem_limit_bytes=VMEM_LIMIT)


def _rms(v):
    return v * lax.rsqrt(jnp.mean(v * v, axis=-1, keepdims=True) + NORM_EPS)


def _mod_row(i, rb=RB):
    return jnp.where(i < N_LAT // rb, i // (SEQ // rb), BATCH)


def _resident(a):
    return pl.BlockSpec(a.shape, lambda *_: (0,) * a.ndim, pipeline_mode=pl.Buffered(1))


def _layer_spec(a, layer):
    return pl.BlockSpec((1,) + a.shape[1:], lambda *_: (layer,) + (0,) * (a.ndim - 1),
                        pipeline_mode=pl.Buffered(1))


def _split_row_specs(rb, w, ctx_row0):
    n_lat = N_LAT // rb
    lat = pl.BlockSpec((rb, w), lambda i: (jnp.minimum(i, n_lat - 1), 0))
    ctx = pl.BlockSpec((rb, w), lambda i: (ctx_row0 // rb + jnp.maximum(i - n_lat, 0), 0))
    return lat, ctx


def _pick_rows(lat_ref, ctx_ref, rb):
    return jnp.where(pl.program_id(0) < N_LAT // rb, lat_ref[...], ctx_ref[...])


def _mod_kernel(c_ref, w_ref, b_ref, o_ref):
    c = c_ref[...]
    s = c * jax.nn.sigmoid(c)
    o_ref[0] = jnp.dot(s, w_ref[0], precision=HIGHEST, preferred_element_type=F32) + b_ref[0]


def _modulation(c_rows, mod_w, mod_b):
    bn = 1536
    return pl.pallas_call(
        _mod_kernel,
        grid=(DEPTH, 6 * D_MODEL // bn),
        in_specs=[
            pl.BlockSpec((16, D_MODEL), lambda l, j: (0, 0)),
            pl.BlockSpec((1, D_MODEL, bn), lambda l, j: (l, 0, j)),
            pl.BlockSpec((1, 1, bn), lambda l, j: (l, 0, j)),
        ],
        out_specs=pl.BlockSpec((1, 16, bn), lambda l, j: (l, 0, j)),
        out_shape=jax.ShapeDtypeStruct((DEPTH, 16, 6 * D_MODEL), F32),
        compiler_params=_params("arbitrary", "arbitrary"),
        name="adaln_mod",
    )(c_rows, mod_w, mod_b.reshape(DEPTH, 1, 6 * D_MODEL))


def _log_sigmoid(z):
    return jnp.minimum(z, 0.0) - jnp.log1p(jnp.exp(-jnp.abs(z)))


def _in_kernel(hs_ref, xl_ref, xc_ref, mod_ref, g1_ref, w_ref, qag_ref, kvg_ref,
               s5u_ref, cqn_ref, kvn_ref, hq_ref, lff_ref, lfb_ref, kkf_ref, kkb_ref,
               hi_ref, hgs_ref, gates_ref, *, layer):
    x = _pick_rows(xl_ref, xc_ref, WB)
    xn = _rms(x) * g1_ref[0]
    h = (xn * (1.0 + mod_ref[0, 1:2, :]) + mod_ref[0, 0:1, :]).astype(BF16)

    def mm(a, b):
        return jnp.dot(h, w_ref[0, :, a:b], preferred_element_type=F32)

    s5u_ref[...] = mm(C_S5, C_CQ).astype(BF16)
    cqn_ref[...] = (_rms(mm(C_CQ, C_KV)) * qag_ref[0]).astype(BF16)
    kv = mm(C_KV, C_HQ)
    ckvn = _rms(kv[:, :MLA_KV_LORA]) * kvg_ref[0]
    kvn_ref[...] = jnp.concatenate([ckvn, kv[:, MLA_KV_LORA:]], axis=1).astype(BF16)
    hq_ref[...] = mm(C_HQ, C_ZF).astype(BF16)

    log_lb = hs_ref[3 * layer]
    log_1m_lb = hs_ref[3 * layer + 1]
    one_m_lb = hs_ref[3 * layer + 2]
    for c0, lf_ref, kk_ref in ((C_ZF, lff_ref, kkf_ref), (C_ZB, lfb_ref, kkb_ref)):
        z = mm(c0, c0 + HG_HEADS * HG_K)
        b = log_1m_lb + _log_sigmoid(z)
        m = jnp.maximum(b, log_lb)
        lf_ref[...] = m + jnp.log1p(jnp.exp(-jnp.abs(b - log_lb)))
        kk_ref[...] = (one_m_lb * jax.nn.sigmoid(-z)).astype(BF16)

    hi_ref[...] = mm(C_HI, C_HG).astype(BF16)
    g = mm(C_HG, C_GATE)
    hgs_ref[...] = (g * jax.nn.sigmoid(g)).astype(BF16)
    for j in range(3):
        c0 = C_GATE + j * D_MODEL
        gates_ref[:, j * D_MODEL:(j + 1) * D_MODEL] = jax.nn.sigmoid(mm(c0, c0 + D_MODEL)).astype(BF16)


def _input_projection(layer, x_lat, x_ctx, ctx_row0, mod3, hg_scal, g1, w_in, qa_g, kva_g):
    row = lambda w: pl.BlockSpec((WB, w), lambda i: (i, 0))
    x_specs = _split_row_specs(WB, D_MODEL, ctx_row0)
    shapes = [
        (S5_WIDTH, BF16), (MLA_Q_LORA, BF16), (256, BF16), (HG_HEADS * HG_K, BF16),
        (HG_HEADS * HG_K, F32), (HG_HEADS * HG_K, F32), (HG_HEADS * HG_K, BF16), (HG_HEADS * HG_K, BF16),
        (HG_HEADS * HG_V, BF16), (HG_HEADS * HG_V, BF16), (3 * D_MODEL, BF16),
    ]
    return pl.pallas_call(
        functools.partial(_in_kernel, layer=layer),
        grid=(N_TOK // WB,),
        in_specs=[
            pl.BlockSpec(memory_space=pltpu.SMEM),
            *x_specs,
            pl.BlockSpec((1, 6, D_MODEL), lambda i: (_mod_row(i, WB), 0, 0)),
            _layer_spec(g1, layer), _layer_spec(w_in, layer), _layer_spec(qa_g, layer), _layer_spec(kva_g, layer),
        ],
        out_specs=[row(w) for w, _ in shapes],
        out_shape=[jax.ShapeDtypeStruct((N_TOK, w), dt) for w, dt in shapes],
        compiler_params=_params("arbitrary"),
        name="input_projection",
    )(hg_scal, x_lat, x_ctx, mod3, g1, w_in, qa_g, kva_g)


def _seq_block(b, j):
    return jnp.where(j == 0, NB_LAT + b, b * LAT_BLOCKS + j - 1)


def _seq_block_rev(b, j):
    return jnp.where(j == 0, NB_LAT + b, b * LAT_BLOCKS + LAT_BLOCKS - j)


S5_TB = 64
S5_ROWS = S5_TB * BATCH
S5_STEPS = (CTX_LEN + SEQ) // S5_TB
S5_CTX_STEPS = CTX_LEN // S5_TB


def _s5_kernel(uf_ref, ub_ref, bmat_ref, cmat_ref, a_ref, yf_ref, yb_ref, bu_ref, carry_ref):
    j = pl.program_id(0)

    @pl.when(j == 0)
    def _():
        carry_ref[...] = jnp.zeros_like(carry_ref)

    for d, (u_ref, y_ref) in enumerate(((uf_ref, yf_ref), (ub_ref, yb_ref))):
        bu_ref[...] = jnp.dot(u_ref[...], bmat_ref[0, d], preferred_element_type=F32)
        are = a_ref[0, d, 0]
        aim = a_ref[0, d, 1]

        def body(t, carry, d=d, are=are, aim=aim):
            hre, him = carry
            tt = (S5_TB - 1 - t) if d == 1 else t
            r0 = pl.multiple_of(tt * BATCH, BATCH)
            nre = are * hre - aim * him + bu_ref[pl.ds(r0, BATCH), 0:S5_LANES]
            nim = are * him + aim * hre + bu_ref[pl.ds(r0, BATCH), S5_LANES:2 * S5_LANES]
            bu_ref[pl.ds(r0, BATCH), 0:S5_LANES] = nre
            bu_ref[pl.ds(r0, BATCH), S5_LANES:2 * S5_LANES] = nim
            return nre, nim

        hre, him = lax.fori_loop(0, S5_TB, body, (carry_ref[d, 0], carry_ref[d, 1]), unroll=4)
        carry_ref[d, 0] = hre
        carry_ref[d, 1] = him
        y_ref[...] = jnp.dot(bu_ref[...].astype(BF16), cmat_ref[0, d], preferred_element_type=F32)


def _s5_block_rev(j):
    return jnp.where(j < S5_CTX_STEPS, S5_CTX_STEPS - 1 - j, S5_STEPS - 1 + S5_CTX_STEPS - j)


def _s5_states(layer, u_tm, bmat, cmat, a_tab):
    full = lambda a: _layer_spec(a, layer)
    fwd = pl.BlockSpec((S5_ROWS, S5_WIDTH), lambda j: (j, 0))
    bwd = pl.BlockSpec((S5_ROWS, S5_WIDTH), lambda j: (_s5_block_rev(j), 0))
    return pl.pallas_call(
        _s5_kernel,
        grid=(S5_STEPS,),
        in_specs=[fwd, bwd, full(bmat), full(cmat), full(a_tab)],
        out_specs=[fwd, bwd],
        out_shape=[jax.ShapeDtypeStruct((S5_STEPS * S5_ROWS, S5_WIDTH), F32)] * 2,
        scratch_shapes=[pltpu.VMEM((S5_ROWS, 2 * S5_LANES), F32), pltpu.VMEM((2, 2, BATCH, S5_LANES), F32)],
        compiler_params=_params("arbitrary"),
        name="s5_scan",
    )(u_tm, u_tm, bmat, cmat, a_tab)


def _to_time_major(a):
    w = a.shape[1]
    seq = jnp.concatenate([a[N_LAT:].reshape(BATCH, CTX_LEN, w), a[:N_LAT].reshape(BATCH, SEQ, w)], axis=1)
    return seq.transpose(1, 0, 2).reshape((CTX_LEN + SEQ) * BATCH, w)


def _from_time_major(a):
    w = a.shape[1]
    seq = a.reshape(CTX_LEN + SEQ, BATCH, w).transpose(1, 0, 2)
    return jnp.concatenate([seq[:, CTX_LEN:].reshape(N_LAT, w), seq[:, :CTX_LEN].reshape(N_CTX, w)], axis=0)


def _qkv_kernel(cqn_ref, kvn_ref, cos_ref, sin_ref, wq_ref, wk_ref, wv_ref, q_ref, k_ref, v_ref):
    cos = jnp.concatenate([cos_ref[...]] * MLA_HEADS, axis=1)
    sin = jnp.concatenate([sin_ref[...]] * MLA_HEADS, axis=1)
    w = MLA_HEADS * MLA_HEAD_PAD
    q2 = jnp.dot(cqn_ref[...], wq_ref[0], preferred_element_type=F32)
    q_ref[...] = ((q2[:, :w] * cos + q2[:, w:] * sin) * MLA_SCALE).astype(BF16)
    kv = kvn_ref[...]
    k2 = jnp.dot(kv, wk_ref[0], preferred_element_type=F32)
    k_ref[...] = (k2[:, :w] * cos + k2[:, w:] * sin).astype(BF16)
    v_ref[...] = jnp.dot(kv, wv_ref[0], preferred_element_type=F32).astype(BF16)


def _qkv(layer, cqn, kvn, cos_t, sin_t, wq, wk, wv):
    row = lambda w: pl.BlockSpec((RB, w), lambda i: (i, 0))
    full = lambda a: _layer_spec(a, layer)
    pos = pl.BlockSpec((RB, MLA_HEAD_PAD), lambda i: (jnp.where(i < NB_LAT, i % LAT_BLOCKS, LAT_BLOCKS), 0))
    w = MLA_HEADS * MLA_HEAD_PAD
    return pl.pallas_call(
        _qkv_kernel,
        grid=(NB_TOK,),
        in_specs=[row(MLA_Q_LORA), row(256), pos, pos, full(wq), full(wk), full(wv)],
        out_specs=[row(w), row(w), row(MLA_HEADS * MLA_V)],
        out_shape=[jax.ShapeDtypeStruct((N_TOK, w), BF16), jax.ShapeDtypeStruct((N_TOK, w), BF16),
                   jax.ShapeDtypeStruct((N_TOK, MLA_HEADS * MLA_V), BF16)],
        compiler_params=_params("arbitrary"),
        name="mla_qkv",
    )(cqn, kvn, cos_t, sin_t, wq, wk, wv)


def _attn_kernel(*refs, n_kv):
    q_ref = refs[0]
    k_refs = refs[1:1 + n_kv]
    v_refs = refs[1 + n_kv:1 + 2 * n_kv]
    o_ref = refs[1 + 2 * n_kv]
    nt = (((1,), (1,)), ((), ()))
    for h in range(MLA_HEADS):
        q = q_ref[:, h * MLA_HEAD_PAD:(h + 1) * MLA_HEAD_PAD]
        s = [lax.dot_general(q, k_ref[:, h * MLA_HEAD_PAD:(h + 1) * MLA_HEAD_PAD], nt,
                             preferred_element_type=F32) for k_ref in k_refs]
        m = functools.reduce(jnp.maximum, [jnp.max(si, axis=-1, keepdims=True) for si in s])
        p = [jnp.exp(si - m) for si in s]
        l = functools.reduce(jnp.add, [jnp.sum(pi, axis=-1, keepdims=True) for pi in p])
        o = functools.reduce(jnp.add, [
            jnp.dot(pi.astype(BF16), v_ref[:, h * MLA_V:(h + 1) * MLA_V], preferred_element_type=F32)
            for pi, v_ref in zip(p, v_refs)])
        o_ref[:, h * MLA_V:(h + 1) * MLA_V] = (o / l).astype(BF16)


def _attention(q, k, v, latent):
    w = MLA_HEADS * MLA_HEAD_PAD
    wv = MLA_HEADS * MLA_V
    if latent:
        tq = WB
        nq = SEQ // tq
        q_spec = pl.BlockSpec((tq, w), lambda b, j: (b * nq + j, 0))
        k_specs = [pl.BlockSpec((SEQ, w), lambda b, j: (b, 0)),
                   pl.BlockSpec((CTX_LEN, w), lambda b, j: (NB_LAT + b, 0))]
        v_specs = [pl.BlockSpec((SEQ, wv), lambda b, j: (b, 0)),
                   pl.BlockSpec((CTX_LEN, wv), lambda b, j: (NB_LAT + b, 0))]
        o_spec = pl.BlockSpec((tq, wv), lambda b, j: (b * nq + j, 0))
        n_out = N_LAT
    else:
        tq = CTX_LEN
        nq = 1
        q_spec = pl.BlockSpec((tq, w), lambda b, j: (NB_LAT + b, 0))
        k_specs = [pl.BlockSpec((CTX_LEN, w), lambda b, j: (NB_LAT + b, 0))]
        v_specs = [pl.BlockSpec((CTX_LEN, wv), lambda b, j: (NB_LAT + b, 0))]
        o_spec = pl.BlockSpec((tq, wv), lambda b, j: (b, 0))
        n_out = N_CTX
    n_kv = len(k_specs)
    return pl.pallas_call(
        functools.partial(_attn_kernel, n_kv=n_kv),
        grid=(BATCH, nq),
        in_specs=[q_spec] + k_specs + v_specs,
        out_specs=o_spec,
        out_shape=jax.ShapeDtypeStruct((n_out, wv), BF16),
        compiler_params=_params("arbitrary", "arbitrary"),
        name="mla_attention_lat" if latent else "mla_attention_ctx",
    )(q, *([k] * n_kv), *([v] * n_kv))


def _hgrn_tables():
    c = HG_CHUNK
    w = np.zeros((2, HG_SETS * c, c), np.float32)
    mask = np.zeros((2, HG_LEVELS, c, c), np.float32)
    idx = np.arange(c)
    for r in range(c):
        w[0, r, :r + 1] = 1
        w[1, r, r:] = 1
        w[0, c + r, r + 1:] = 1
        w[1, c + r, :r] = 1
    for l in range(HG_LEVELS):
        m = c >> (l + 1)
        for r in range(c):
            base = (r // (2 * m)) * 2 * m
            mid = base + m
            later = r >= mid
            row = (2 + l) * c + r
            if later:
                w[0, row, mid:r + 1] = 1
                w[1, row, mid:r] = 1
            else:
                w[0, row, r + 1:mid] = 1
                w[1, row, r:mid] = 1
        same = (idx[:, None] // (2 * m)) == (idx[None, :] // (2 * m))
        q_later = (idx[:, None] % (2 * m)) >= m
        k_later = (idx[None, :] % (2 * m)) >= m
        mask[0, l] = same & q_later & ~k_later
        mask[1, l] = same & ~q_later & k_later
    return w, mask


def _hgrn_dir(d, q_ref, k_ref, lf_ref, v_ref, w_ref, mask_ref, st_ref, o_ref):
    c = HG_CHUNK
    n_chunks = RB // c
    nt = (((1,), (1,)), ((), ()))
    tn = (((0,), (0,)), ((), ()))
    last = c - 1 if d == 0 else 0
    wsel = w_ref[d]
    e = []
    for ci in range(n_chunks):
        g = lf_ref[ci * c:(ci + 1) * c, :]
        g_hi = g.astype(BF16)
        g_lo = (g - g_hi.astype(F32)).astype(BF16)
        expo = jnp.dot(wsel, jnp.concatenate([g_hi, g_lo], axis=0), preferred_element_type=F32)
        e.append(jnp.exp(expo))

    def rows(s):
        return jnp.concatenate([e[ci][s * c:(s + 1) * c] for ci in range(n_chunks)], axis=0)

    q = q_ref[...].astype(F32)
    k = k_ref[...].astype(F32)
    q_state = (q * rows(0)).astype(BF16)
    k_state = (k * rows(1)).astype(BF16)
    q_lvl, k_lvl = [], []
    for l in range(HG_LEVELS):
        el = rows(2 + l)
        q_lvl.append((q * el).astype(BF16))
        k_lvl.append((k * el).astype(BF16))
    qk = q * k
    order = range(n_chunks) if d == 0 else range(n_chunks - 1, -1, -1)
    for h in range(HG_HEADS):
        ks = slice(h * HG_K, (h + 1) * HG_K)
        vs = slice(h * HG_V, (h + 1) * HG_V)
        v = v_ref[:, vs]
        scores = jnp.zeros((RB, RB), F32)
        for l in range(HG_LEVELS):
            p = lax.dot_general(q_lvl[l][:, ks], k_lvl[l][:, ks], nt, preferred_element_type=F32)
            scores = scores + p * mask_ref[d, l]
        o = jnp.dot(scores.astype(BF16), v, preferred_element_type=F32)
        o = o + jnp.sum(qk[:, ks], axis=-1, keepdims=True) * v.astype(F32)
        st = st_ref[d, h]
        o_state = [None] * n_chunks
        for ci in order:
            r = slice(ci * c, (ci + 1) * c)
            o_state[ci] = lax.dot_general(q_state[r, ks], st.astype(BF16), nt, preferred_element_type=F32)
            inc = lax.dot_general(v[r], k_state[r, ks], tn, preferred_element_type=F32)
            st = st * e[ci][last:last + 1, ks] + inc
        st_ref[d, h] = st
        o_ref[:, vs] = o + jnp.concatenate(o_state, axis=0)


def _hgrn_kernel(qf_ref, kf_ref, lff_ref, vf_ref, qb_ref, kb_ref, lfb_ref, vb_ref, w_ref, mask_ref,
                 of_ref, ob_ref, st_ref):
    j = pl.program_id(1)

    @pl.when(j == 0)
    def _():
        st_ref[...] = jnp.zeros_like(st_ref)

    _hgrn_dir(0, qf_ref, kf_ref, lff_ref, vf_ref, w_ref, mask_ref, st_ref, of_ref)
    _hgrn_dir(1, qb_ref, kb_ref, lfb_ref, vb_ref, w_ref, mask_ref, st_ref, ob_ref)


def _hgrn(hq, kkf, kkb, lff, lfb, hi, w_tab, mask_tab):
    full = _resident
    fwd = lambda w: pl.BlockSpec((RB, w), lambda b, j: (_seq_block(b, j), 0))
    bwd = lambda w: pl.BlockSpec((RB, w), lambda b, j: (_seq_block_rev(b, j), 0))
    wk = HG_HEADS * HG_K
    wv = HG_HEADS * HG_V
    return pl.pallas_call(
        _hgrn_kernel,
        grid=(BATCH, SEQ_BLOCKS),
        in_specs=[fwd(wk), fwd(wk), fwd(wk), fwd(wv), bwd(wk), bwd(wk), bwd(wk), bwd(wv),
                  full(w_tab), full(mask_tab)],
        out_specs=[fwd(wv), bwd(wv)],
        out_shape=[jax.ShapeDtypeStruct((N_TOK, wv), F32)] * 2,
        scratch_shapes=[pltpu.VMEM((2, HG_HEADS, HG_V, HG_K), F32)],
        compiler_params=_params("arbitrary", "arbitrary"),
        name="hgrn2_scan",
    )(hq, kkf, lff, hi, hq, kkb, lfb, hi, w_tab, mask_tab)


def _gelu_tanh(x):
    return 0.5 * x * (1.0 + jnp.tanh(math.sqrt(2.0 / math.pi) * (x + 0.044715 * (x * x * x))))


def _split_bf16(a):
    hi = a.astype(BF16)
    return hi, (a - hi.astype(F32)).astype(BF16)


def _merge_kernel(xl_ref, xc_ref, mod_ref, yf_ref, yb_ref, u_ref, of_ref, ob_ref, hgs_ref, attl_ref, attc_ref,
                  gates_ref, s5d_ref, gluw_ref, glub_ref, hgg_ref, hsum_ref, wpa_ref, wpb_ref, wpc_ref, wout_ref,
                  g2_ref, wrh_ref, wrl_ref, br_ref,
                  x1_ref, h2_ref, eid_ref, gate_ref):
    y = yf_ref[...] + yb_ref[...] + s5d_ref[0] * u_ref[...].astype(F32)
    y = _gelu_tanh(y)
    y = y * jax.nn.sigmoid(jnp.dot(y.astype(BF16), gluw_ref[0], preferred_element_type=F32) + glub_ref[0])
    o = of_ref[...] + ob_ref[...]
    sq_hi, sq_lo = _split_bf16(o * o)
    ms = (jnp.dot(sq_hi, hsum_ref[...], preferred_element_type=F32)
          + jnp.dot(sq_lo, hsum_ref[...], preferred_element_type=F32)) * (1.0 / HG_V)
    o = o * lax.rsqrt(ms + NORM_EPS) * hgg_ref[0] * hgs_ref[...].astype(F32)

    d = D_MODEL
    att = _pick_rows(attl_ref, attc_ref, WB)
    merged = (gates_ref[:, 0:d].astype(F32) * jnp.dot(y.astype(BF16), wpa_ref[0], preferred_element_type=F32)
              + gates_ref[:, d:2 * d].astype(F32) * jnp.dot(att, wpb_ref[0], preferred_element_type=F32)
              + gates_ref[:, 2 * d:3 * d].astype(F32) * jnp.dot(o.astype(BF16), wpc_ref[0], preferred_element_type=F32))
    y_out = jnp.dot(merged.astype(BF16), wout_ref[0], preferred_element_type=F32)
    x1 = _pick_rows(xl_ref, xc_ref, WB) + mod_ref[0, 2:3, :] * y_out
    x1_ref[...] = x1
    h2 = _rms(x1) * g2_ref[0] * (1.0 + mod_ref[0, 4:5, :]) + mod_ref[0, 3:4, :]
    h2_ref[...] = h2

    h_hi, h_lo = _split_bf16(h2)
    logits = (jnp.dot(h_hi, wrh_ref[0], preferred_element_type=F32)
              + jnp.dot(h_lo, wrh_ref[0], preferred_element_type=F32)
              + jnp.dot(h_hi, wrl_ref[0], preferred_element_type=F32)) + br_ref[0]
    lane = lax.broadcasted_iota(jnp.int32, logits.shape, 1).astype(F32)
    neg = -jnp.inf
    glog = jnp.where(lane < MOE_GROUPS, logits, neg)
    gmax = jnp.max(glog, axis=-1, keepdims=True)
    gidx = jnp.min(jnp.where(glog == gmax, lane, 1e9), axis=-1, keepdims=True)
    g_w = 1.0 / jnp.sum(jnp.exp(glog - gmax), axis=-1, keepdims=True)
    e_lo = 32.0 + gidx * MOE_EXPERTS_PER_GROUP
    elog = jnp.where((lane >= e_lo) & (lane < e_lo + MOE_EXPERTS_PER_GROUP), logits, neg)
    v1 = jnp.max(elog, axis=-1, keepdims=True)
    i1 = jnp.min(jnp.where(elog == v1, lane, 1e9), axis=-1, keepdims=True)
    elog2 = jnp.where(lane == i1, neg, elog)
    v2 = jnp.max(elog2, axis=-1, keepdims=True)
    i2 = jnp.min(jnp.where(elog2 == v2, lane, 1e9), axis=-1, keepdims=True)
    e2 = jnp.exp(v2 - v1)
    gate1 = g_w / (1.0 + e2)
    gate2 = g_w * e2 / (1.0 + e2)
    eid_ref[...] = jnp.where(lane == 0.0, i1 - 32.0, jnp.where(lane == 1.0, i2 - 32.0, 0.0)).astype(jnp.int32)
    gate_ref[...] = jnp.where(lane == 0.0, gate1, jnp.where(lane == 1.0, gate2, 0.0))


def _merge(layer, n_blocks, x_lat, x_ctx, ctx_row0, mod3, yf, yb, s5u, of, ob, hgs, att_lat, att_ctx, gates,
           head_sum, consts):
    row = lambda w: pl.BlockSpec((WB, w), lambda i: (i, 0))
    n = n_blocks * RB
    wv = MLA_HEADS * MLA_V
    if att_ctx is None:
        att_ctx = att_lat
    const_specs = [_layer_spec(a, layer) for a in consts]
    const_specs.insert(4, _resident(head_sum))
    return pl.pallas_call(
        _merge_kernel,
        grid=(n // WB,),
        in_specs=[*_split_row_specs(WB, D_MODEL, ctx_row0),
                  pl.BlockSpec((1, 6, D_MODEL), lambda i: (_mod_row(i, WB), 0, 0)),
                  row(S5_WIDTH), row(S5_WIDTH), row(S5_WIDTH), row(HG_HEADS * HG_V), row(HG_HEADS * HG_V),
                  row(HG_HEADS * HG_V), *_split_row_specs(WB, wv, 0), row(3 * D_MODEL)] + const_specs,
        out_specs=[row(D_MODEL), row(D_MODEL), row(128), row(128)],
        out_shape=[jax.ShapeDtypeStruct((n, D_MODEL), F32), jax.ShapeDtypeStruct((n, D_MODEL), F32),
                   jax.ShapeDtypeStruct((n, 128), jnp.int32), jax.ShapeDtypeStruct((n, 128), F32)],
        compiler_params=_params("arbitrary"),
        name="merge_router",
    )(x_lat, x_ctx, mod3, yf, yb, s5u, of, ob, hgs, att_lat, att_ctx, gates,
      *consts[:4], head_sum, *consts[4:])


def _moe_kernel(bexp_ref, nblk_ref, src_ref, srcn_ref, dst_ref, h_hbm, w1_ref, w3_ref, w2_ref, y_hbm,
                xbuf, ybuf, w1b, w3b, w2b, gsem, ssem):
    i = pl.program_id(0)
    n_used = nblk_ref[0]
    slot = lax.rem(i, 2)

    def gather_start(idx_ref, s):
        for r in range(MOE_ROWS):
            pltpu.make_async_copy(h_hbm.at[pl.ds(idx_ref[0, 0, r], 1), :], xbuf.at[s, pl.ds(r, 1), :],
                                  gsem.at[s]).start(priority=r % 2)

    def gather_wait(s):
        pltpu.make_async_copy(h_hbm.at[pl.ds(0, MOE_ROWS), :], xbuf.at[s], gsem.at[s]).wait()

    def scatter_start(s):
        for r in range(MOE_ROWS):
            pltpu.make_async_copy(ybuf.at[s, pl.ds(r, 1), :], y_hbm.at[pl.ds(dst_ref[0, 0, r], 1), :],
                                  ssem.at[s]).start(priority=r % 2)

    def scatter_wait(s):
        pltpu.make_async_copy(ybuf.at[s], y_hbm.at[pl.ds(0, MOE_ROWS), :], ssem.at[s]).wait()

    @pl.when(i < n_used)
    def _():
        @pl.when(i == 0)
        def _():
            gather_start(src_ref, 0)
            ybuf[1] = jnp.zeros((MOE_ROWS, D_MODEL), F32)
            n_real = y_hbm.shape[0] - MOE_ROWS
            fill = pltpu.make_async_copy(ybuf.at[1], y_hbm.at[pl.ds(n_real, MOE_ROWS), :], ssem.at[1])
            fill.start()
            fill.wait()

        @pl.when(i + 1 < n_used)
        def _():
            gather_start(srcn_ref, 1 - slot)

        @pl.when((i == 0) | (bexp_ref[i] != bexp_ref[jnp.maximum(i - 1, 0)]))
        def _():
            w1b[...] = w1_ref[0, 0].astype(BF16)
            w3b[...] = w3_ref[0, 0].astype(BF16)
            w2b[...] = w2_ref[0, 0].astype(BF16)

        gather_wait(slot)

        @pl.when(i >= 2)
        def _():
            scatter_wait(slot)

        x = xbuf[slot].astype(BF16)
        a = jnp.dot(x, w1b[...], preferred_element_type=F32)
        g = jnp.dot(x, w3b[...], preferred_element_type=F32)
        hid = (a * jax.nn.sigmoid(a) * g).astype(BF16)
        ybuf[slot] = jnp.dot(hid, w2b[...], preferred_element_type=F32)
        scatter_start(slot)

        @pl.when(i == n_used - 1)
        def _():
            @pl.when(i >= 1)
            def _():
                scatter_wait(1 - slot)
            scatter_wait(slot)


def _moe(h2, eid, w1, w3, w2, layer):
    n = h2.shape[0]
    n_assign = n * MOE_TOP_K
    n_blocks = (n_assign + MOE_EXPERTS * (MOE_ROWS - 1) + MOE_ROWS - 1) // MOE_ROWS
    flat_e = eid.reshape(n_assign)
    order = jnp.argsort(flat_e, stable=True).astype(jnp.int32)
    experts = jnp.arange(MOE_EXPERTS, dtype=jnp.int32)
    counts = jnp.sum((flat_e[:, None] == experts[None, :]).astype(jnp.int32), axis=0)
    starts = jnp.cumsum(counts) - counts
    padded = (counts + MOE_ROWS - 1) // MOE_ROWS * MOE_ROWS
    p_ends = jnp.cumsum(padded)
    p_starts = p_ends - padded
    n_used_s = p_ends[-1] // MOE_ROWS
    n_used = n_used_s.astype(jnp.int32).reshape(1)
    blk = jnp.arange(n_blocks, dtype=jnp.int32)
    blk_expert = jnp.sum((p_ends[None, :] <= (blk * MOE_ROWS)[:, None]).astype(jnp.int32), axis=1)
    last_expert = jnp.max(jnp.where(counts > 0, experts, 0))
    block_expert = jnp.where(blk < n_used_s, jnp.minimum(blk_expert, MOE_EXPERTS - 1), last_expert)
    onehot = (block_expert[:, None] == experts[None, :]).astype(jnp.int32)
    pick = lambda table: jnp.sum(onehot * table[None, :], axis=1)
    local = jnp.arange(MOE_ROWS, dtype=jnp.int32)[None, :]
    rank = blk[:, None] * MOE_ROWS + local - pick(p_starts)[:, None]
    valid = (rank < pick(counts)[:, None]) & (blk < n_used_s)[:, None]
    pos = jnp.clip(pick(starts)[:, None] + rank, 0, n_assign - 1)
    assign = order[pos]
    tok = jnp.where(valid, assign // MOE_TOP_K, 0)
    dst = jnp.where(valid, (assign % MOE_TOP_K) * n + assign // MOE_TOP_K, n_assign + local)
    rows_now = pl.BlockSpec((1, 1, MOE_ROWS), lambda i, be, nb: (i, 0, 0), memory_space=pltpu.SMEM)
    rows_next = pl.BlockSpec((1, 1, MOE_ROWS), lambda i, be, nb: (jnp.minimum(i + 1, n_blocks - 1), 0, 0),
                             memory_space=pltpu.SMEM)
    src3 = tok.reshape(n_blocks, 1, MOE_ROWS)
    return pl.pallas_call(
        _moe_kernel,
        grid_spec=pltpu.PrefetchScalarGridSpec(
            num_scalar_prefetch=2,
            grid=(n_blocks,),
            in_specs=[
                rows_now, rows_next, rows_now,
                pl.BlockSpec(memory_space=pl.ANY),
                pl.BlockSpec((1, 1, D_MODEL, MOE_HIDDEN), lambda i, be, nb: (layer, be[i], 0, 0)),
                pl.BlockSpec((1, 1, D_MODEL, MOE_HIDDEN), lambda i, be, nb: (layer, be[i], 0, 0)),
                pl.BlockSpec((1, 1, MOE_HIDDEN, D_MODEL), lambda i, be, nb: (layer, be[i], 0, 0)),
            ],
            out_specs=pl.BlockSpec(memory_space=pl.ANY),
            scratch_shapes=[pltpu.VMEM((2, MOE_ROWS, D_MODEL), F32), pltpu.VMEM((2, MOE_ROWS, D_MODEL), F32),
                            pltpu.VMEM((D_MODEL, MOE_HIDDEN), BF16), pltpu.VMEM((D_MODEL, MOE_HIDDEN), BF16),
                            pltpu.VMEM((MOE_HIDDEN, D_MODEL), BF16),
                            pltpu.SemaphoreType.DMA((2,)), pltpu.SemaphoreType.DMA((2,))],
        ),
        out_shape=jax.ShapeDtypeStruct((n_assign + MOE_ROWS, D_MODEL), F32),
        compiler_params=_params("arbitrary"),
        name="moe_experts",
    )(block_expert, n_used, src3, src3, dst.reshape(n_blocks, 1, MOE_ROWS), h2, w1, w3, w2)


def _combine_kernel(x1_ref, y0_ref, y1_ref, gate_ref, mod_ref, g_ref, o_ref, *, final):
    f = gate_ref[:, 0:1] * y0_ref[...] + gate_ref[:, 1:2] * y1_ref[...]
    x2 = x1_ref[...] + mod_ref[0, 5:6, :] * f
    if final:
        x2 = _rms(x2) * g_ref[...]
    o_ref[...] = x2


def _combine(n_blocks, x1, y, gate, mod3, g_final, final):
    row = lambda w: pl.BlockSpec((RB, w), lambda i: (i, 0))
    n = n_blocks * RB
    return pl.pallas_call(
        functools.partial(_combine_kernel, final=final),
        grid=(n_blocks,),
        in_specs=[row(D_MODEL), row(D_MODEL), pl.BlockSpec((RB, D_MODEL), lambda i: (i + n_blocks, 0)), row(128),
                  pl.BlockSpec((1, 6, D_MODEL), lambda i: (_mod_row(i), 0, 0)),
                  pl.BlockSpec((1, D_MODEL), lambda i: (0, 0))],
        out_specs=row(D_MODEL),
        out_shape=jax.ShapeDtypeStruct((n, D_MODEL), F32),
        compiler_params=_params("arbitrary"),
        name="moe_combine_final" if final else "moe_combine",
    )(x1, y, y, gate, mod3, g_final)


def _rope_rot_cols(w):
    p = ROPE_PAIRS
    return jnp.concatenate([-w[..., p:2 * p], w[..., 0:p], -w[..., 3 * p:4 * p], w[..., 2 * p:3 * p]], axis=-1)


def _pack_w_in(w_in):
    cuts = np.cumsum([S5_WIDTH, MLA_Q_LORA, MLA_KV_LORA, MLA_ROPE, HG_HEADS * HG_K, HG_HEADS * HG_K,
                      HG_HEADS * HG_K, HG_HEADS * HG_V, HG_HEADS * HG_V, D_MODEL, D_MODEL])
    (w_s5, w_cq, w_ckv, w_kpe, w_hq, w_zf, w_zb, w_hi, w_hg, w_ga, w_gb, w_gc) = jnp.split(w_in, cuts, axis=-1)
    pad = jnp.zeros(w_in.shape[:-1] + (256 - MLA_KV_LORA - 2 * MLA_ROPE,), w_in.dtype)
    packed = jnp.concatenate([w_s5, w_cq, w_ckv, w_kpe, _rope_rot_cols(w_kpe), pad,
                              w_hq, w_zf, w_zb, w_hi, w_hg, w_ga, w_gb, w_gc], axis=-1)
    assert packed.shape[-1] == W_IN_COLS
    return packed.astype(BF16)


def _pack_mla(w_uq, w_uk, w_uv):
    hp, n, r = MLA_HEAD_PAD, MLA_NOPE, MLA_ROPE
    nl = w_uq.shape[0]
    wq = w_uq.reshape(nl, MLA_Q_LORA, MLA_HEADS, n + r)
    zq = jnp.zeros((nl, MLA_Q_LORA, MLA_HEADS, hp - n - r), F32)
    q1 = jnp.concatenate([wq, zq], axis=-1).reshape(nl, MLA_Q_LORA, MLA_HEADS * hp)
    q2 = jnp.concatenate([jnp.zeros((nl, MLA_Q_LORA, MLA_HEADS, n), F32), _rope_rot_cols(wq[..., n:]), zq],
                         axis=-1).reshape(nl, MLA_Q_LORA, MLA_HEADS * hp)
    wq_packed = jnp.concatenate([q1, q2], axis=-1).astype(BF16)

    wk = w_uk.reshape(nl, MLA_KV_LORA, MLA_HEADS, n)
    k_nope = jnp.concatenate([wk, jnp.zeros((nl, MLA_KV_LORA, MLA_HEADS, hp - n), F32)], axis=-1)
    pe_slot = np.concatenate([np.zeros((r, n), np.float32), np.eye(r, dtype=np.float32),
                              np.zeros((r, hp - n - r), np.float32)], axis=-1)
    pe_all = jnp.asarray(np.tile(pe_slot[None, :, None, :], (nl, 1, MLA_HEADS, 1)))
    zero_pe = jnp.zeros_like(pe_all)
    tail = jnp.zeros((nl, 256 - MLA_KV_LORA - 2 * r, MLA_HEADS, hp), F32)
    k1 = jnp.concatenate([k_nope, pe_all, zero_pe, tail], axis=1).reshape(nl, 256, MLA_HEADS * hp)
    k2 = jnp.concatenate([jnp.zeros_like(k_nope), zero_pe, pe_all, tail], axis=1).reshape(nl, 256, MLA_HEADS * hp)
    wk_packed = jnp.concatenate([k1, k2], axis=-1).astype(BF16)
    wv_packed = jnp.concatenate([w_uv, jnp.zeros((nl, 256 - MLA_KV_LORA, MLA_HEADS * MLA_V), F32)],
                                axis=1).astype(BF16)
    return wq_packed, wk_packed, wv_packed


def _rope_tables():
    rows = SEQ // GRID_W
    row = np.repeat(np.arange(rows, dtype=np.float32), GRID_W)
    col = np.tile(np.arange(GRID_W, dtype=np.float32), rows)
    inv = (np.float32(ROPE_BASE) ** (-np.arange(ROPE_PAIRS, dtype=np.float32) / np.float32(ROPE_PAIRS))).astype(np.float32)
    ar, ac = row[:, None] * inv, col[:, None] * inv
    cos = np.concatenate([np.cos(ar), np.cos(ar), np.cos(ac), np.cos(ac)], axis=1)
    sin = np.concatenate([np.sin(ar), np.sin(ar), np.sin(ac), np.sin(ac)], axis=1)
    tail = MLA_HEAD_PAD - MLA_NOPE - MLA_ROPE
    cos_t = np.concatenate([np.ones((SEQ, MLA_NOPE)), cos, np.ones((SEQ, tail))], axis=1)
    sin_t = np.concatenate([np.zeros((SEQ, MLA_NOPE)), sin, np.zeros((SEQ, tail))], axis=1)
    cos_t = np.concatenate([cos_t, np.ones((CTX_LEN, MLA_HEAD_PAD))], axis=0)
    sin_t = np.concatenate([sin_t, np.zeros((CTX_LEN, MLA_HEAD_PAD))], axis=0)
    return jnp.asarray(cos_t, F32), jnp.asarray(sin_t, F32)


def _pack_s5(lam_re, lam_im, b_re, b_im, c_re, c_im, log_step):
    eye = jnp.eye(S5_GROUPS, dtype=F32)
    dt = jnp.exp(log_step)[..., None]
    mag = jnp.exp(lam_re * dt)
    lb_re, lb_im = mag * jnp.cos(lam_im * dt), mag * jnp.sin(lam_im * dt)
    den = lam_re * lam_re + lam_im * lam_im
    fr = ((lb_re - 1) * lam_re + lb_im * lam_im) / den
    fi = (lb_im * lam_re - (lb_re - 1) * lam_im) / den
    bb_re = fr[..., None] * b_re - fi[..., None] * b_im
    bb_im = fr[..., None] * b_im + fi[..., None] * b_re
    lead = lam_re.shape[:2]

    def in_mat(bb):
        return jnp.einsum('ldgph,gk->ldghkp', bb, eye).reshape(lead + (S5_WIDTH, S5_LANES))

    def out_mat(cc):
        return jnp.einsum('ldghp,gk->ldgpkh', cc, eye).reshape(lead + (S5_LANES, S5_WIDTH))

    bmat = jnp.concatenate([in_mat(bb_re), in_mat(bb_im)], axis=-1).astype(BF16)
    cmat = jnp.concatenate([out_mat(c_re), -out_mat(c_im)], axis=-2).astype(BF16)
    a_tab = jnp.stack([jnp.broadcast_to(lb_re.reshape(lead + (1, S5_LANES)), lead + (BATCH, S5_LANES)),
                       jnp.broadcast_to(lb_im.reshape(lead + (1, S5_LANES)), lead + (BATCH, S5_LANES))], axis=2)
    return bmat, cmat, a_tab


def kernel(x, c, ctx, c_ctx, mod_w, mod_b, norm1_g, norm2_g, w_in, s5_lam_re, s5_lam_im, s5_b_re, s5_b_im, s5_c_re, s5_c_im, s5_log_step, s5_d, s5_glu_w, s5_glu_b, mla_qa_g, mla_kva_g, mla_w_uq, mla_w_uk, mla_w_uv, hg_lb_logits, hg_norm_g, w_pa, w_pb, w_pc, w_out, moe_w_group, moe_b_group, moe_w_expert, moe_b_expert, moe_w1, moe_w3, moe_w2, final_norm_g):
    x_lat, x_ctx, ctx_row0 = x.reshape(N_LAT, D_MODEL), ctx.reshape(N_CTX, D_MODEL), 0
    c_rows = jnp.concatenate([c, c_ctx[None, :], jnp.zeros((16 - BATCH - 1, D_MODEL), F32)], axis=0)
    mod = _modulation(c_rows, mod_w, mod_b).reshape(DEPTH, 16, 6, D_MODEL)

    lb_all = jnp.cumsum(jax.nn.softmax(hg_lb_logits.astype(F32)))
    lb_all = lb_all - lb_all[0]
    hg_scal = jnp.stack([jnp.log(lb_all), jnp.log1p(-lb_all), 1.0 - lb_all], axis=1).reshape(3 * DEPTH).astype(F32)
    cos_t, sin_t = _rope_tables()

    vec = lambda a: a[:, None, :]
    w_in_p = _pack_w_in(w_in)
    bmat, cmat, a_tab = _pack_s5(s5_lam_re, s5_lam_im, s5_b_re, s5_b_im, s5_c_re, s5_c_im, s5_log_step)
    wq, wk, wv = _pack_mla(mla_w_uq, mla_w_uk, mla_w_uv)
    w_route = jnp.concatenate([moe_w_group, jnp.zeros((DEPTH, D_MODEL, 32 - MOE_GROUPS), F32), moe_w_expert,
                               jnp.zeros((DEPTH, D_MODEL, 128 - 32 - MOE_EXPERTS), F32)], axis=-1)
    b_route = jnp.concatenate([moe_b_group, jnp.zeros((DEPTH, 32 - MOE_GROUPS), F32), moe_b_expert,
                               jnp.zeros((DEPTH, 128 - 32 - MOE_EXPERTS), F32)], axis=-1)
    w_route_hi = w_route.astype(BF16)
    w_route_lo = (w_route - w_route_hi.astype(F32)).astype(BF16)
    merge_consts = [vec(s5_d), s5_glu_w.astype(BF16), vec(s5_glu_b), vec(hg_norm_g),
                    w_pa.astype(BF16), w_pb.astype(BF16), w_pc.astype(BF16), w_out.astype(BF16), vec(norm2_g),
                    w_route_hi, w_route_lo, vec(b_route)]
    w_tab_np, mask_tab_np = _hgrn_tables()
    w_tab = jnp.asarray(np.concatenate([w_tab_np, w_tab_np], axis=-1), BF16)
    mask_tab = jnp.asarray(np.kron(np.eye(RB // HG_CHUNK, dtype=np.float32), mask_tab_np), F32)
    head_sum = jnp.asarray(np.kron(np.eye(HG_HEADS), np.ones((HG_V, HG_V))), F32)

    out = None
    for layer in range(DEPTH):
        last = layer == DEPTH - 1
        mod3 = mod[layer]
        (s5u, cqn, kvn, hq, lff, lfb, kkf, kkb, hi, hgs, gates) = _input_projection(
            layer, x_lat, x_ctx, ctx_row0, mod3, hg_scal, vec(norm1_g), w_in_p, vec(mla_qa_g), vec(mla_kva_g))

        yf, yb = [_from_time_major(y) for y in _s5_states(layer, _to_time_major(s5u), bmat, cmat, a_tab)]

        q, k, v = _qkv(layer, cqn, kvn, cos_t, sin_t, wq, wk, wv)
        att_lat = _attention(q, k, v, latent=True)
        att_ctx = None if last else _attention(q, k, v, latent=False)

        of, ob = _hgrn(hq, kkf, kkb, lff, lfb, hi, w_tab, mask_tab)

        n_blocks = NB_LAT if last else NB_TOK
        x1, h2, eid, gate = _merge(layer, n_blocks, x_lat, x_ctx, ctx_row0, mod3, yf, yb, s5u, of, ob, hgs,
                                   att_lat, att_ctx, gates, head_sum, merge_consts)

        y_moe = _moe(h2, eid[:, :MOE_TOP_K], moe_w1, moe_w3, moe_w2, layer)
        res = _combine(n_blocks, x1, y_moe, gate, mod3, final_norm_g[None, :], final=last)
        if last:
            out = res
        else:
            x_lat, x_ctx, ctx_row0 = res, res, N_LAT
    return out.reshape(BATCH, SEQ, D_MODEL)
```

```python
import functools
import math

import jax
import jax.numpy as jnp
from jax import lax
import numpy as np
from jax.experimental import pallas as pl
from jax.experimental.pallas import tpu as pltpu

F32 = jnp.float32
BF16 = jnp.bfloat16
HIGHEST = lax.Precision.HIGHEST

D_MODEL = 1024
BATCH = 8
SEQ = 2048
DEPTH = 2
GRID_W = 64
CTX_LEN = 256
NORM_EPS = 1e-6

S5_WIDTH = D_MODEL // 4
S5_GROUP_CH = 16
S5_GROUPS = S5_WIDTH // S5_GROUP_CH
S5_STATE = 64
S5_LANES = S5_GROUPS * S5_STATE

MLA_HEADS = D_MODEL // 128
MLA_NOPE = 64
MLA_ROPE = 32
MLA_V = 64
MLA_Q_LORA = D_MODEL // 4
MLA_KV_LORA = D_MODEL // 8
MLA_SCALE = 1.0 / math.sqrt(MLA_NOPE + MLA_ROPE)
MLA_HEAD_PAD = 128
ROPE_PAIRS = MLA_ROPE // 4
ROPE_BASE = 10000.0

HG_HEADS = D_MODEL // 256
HG_K = 128
HG_V = 64
HG_CHUNK = 64
HG_LEVELS = 6
HG_SETS = 2 + HG_LEVELS

MOE_GROUPS = 4
MOE_EXPERTS_PER_GROUP = 8
MOE_EXPERTS = MOE_GROUPS * MOE_EXPERTS_PER_GROUP
MOE_TOP_K = 2
MOE_HIDDEN = D_MODEL // 2
MOE_ROWS = 256

N_LAT = BATCH * SEQ
N_CTX = BATCH * CTX_LEN
N_TOK = N_LAT + N_CTX
RB = CTX_LEN
WB = 2 * RB
TOK_SUB = D_MODEL // 128
NB_LAT = N_LAT // RB
NB_CTX = N_CTX // RB
NB_TOK = N_TOK // RB
LAT_BLOCKS = SEQ // RB
SEQ_BLOCKS = LAT_BLOCKS + 1

C_S5 = 0
C_CQ = 256
C_KV = 512
C_HQ = 768
C_ZF = 1280
C_ZB = 1792
C_HI = 2304
C_HG = 2560
C_GATE = 2816
W_IN_COLS = C_GATE + 3 * D_MODEL

VMEM_LIMIT = 56 * 1024 * 1024


def _params(*sem):
    return pltpu.CompilerParams(dimension_semantics=sem, vmem_limit_bytes=VMEM_LIMIT)


def _rms(v):
    return v * lax.rsqrt(jnp.mean(v * v, axis=-1, keepdims=True) + NORM_EPS)


def _mod_row(i, rb=RB):
    return jnp.where(i < N_LAT // rb, i // (SEQ // rb), BATCH)


def _resident(a):
    return pl.BlockSpec(a.shape, lambda *_: (0,) * a.ndim, pipeline_mode=pl.Buffered(1))


def _layer_spec(a, layer):
    return pl.BlockSpec((1,) + a.shape[1:], lambda *_: (layer,) + (0,) * (a.ndim - 1),
                        pipeline_mode=pl.Buffered(1))


def _split_row_specs(rb, w, ctx_row0):
    n_lat = N_LAT // rb
    lat = pl.BlockSpec((rb, w), lambda i: (jnp.minimum(i, n_lat - 1), 0))
    ctx = pl.BlockSpec((rb, w), lambda i: (ctx_row0 // rb + jnp.maximum(i - n_lat, 0), 0))
    return lat, ctx


def _pick_rows(lat_ref, ctx_ref, rb):
    return jnp.where(pl.program_id(0) < N_LAT // rb, lat_ref[...], ctx_ref[...])


def _mod_kernel(c_ref, w_ref, b_ref, o_ref):
    c = c_ref[...]
    s = c * jax.nn.sigmoid(c)
    o_ref[0] = jnp.dot(s, w_ref[0], precision=HIGHEST, preferred_element_type=F32) + b_ref[0]


def _modulation(c_rows, mod_w, mod_b):
    bn = 1536
    return pl.pallas_call(
        _mod_kernel,
        grid=(DEPTH, 6 * D_MODEL // bn),
        in_specs=[
            pl.BlockSpec((16, D_MODEL), lambda l, j: (0, 0)),
            pl.BlockSpec((1, D_MODEL, bn), lambda l, j: (l, 0, j)),
            pl.BlockSpec((1, 1, bn), lambda l, j: (l, 0, j)),
        ],
        out_specs=pl.BlockSpec((1, 16, bn), lambda l, j: (l, 0, j)),
        out_shape=jax.ShapeDtypeStruct((DEPTH, 16, 6 * D_MODEL), F32),
        compiler_params=_params("arbitrary", "arbitrary"),
        name="adaln_mod",
    )(c_rows, mod_w, mod_b.reshape(DEPTH, 1, 6 * D_MODEL))


def _log_sigmoid(z):
    return jnp.minimum(z, 0.0) - jnp.log1p(jnp.exp(-jnp.abs(z)))


def _in_kernel(hs_ref, xl_ref, xc_ref, mod_ref, g1_ref, w_ref, qag_ref, kvg_ref,
               s5u_ref, cqn_ref, kvn_ref, hq_ref, lff_ref, lfb_ref, kkf_ref, kkb_ref,
               hi_ref, hgs_ref, gates_ref, *, layer):
    x = _pick_rows(xl_ref, xc_ref, WB)
    xn = _rms(x) * g1_ref[0]
    h = (xn * (1.0 + mod_ref[0, 1:2, :]) + mod_ref[0, 0:1, :]).astype(BF16)

    def mm(a, b):
        return jnp.dot(h, w_ref[0, :, a:b], preferred_element_type=F32)

    s5u_ref[...] = mm(C_S5, C_CQ).astype(BF16)
    cqn_ref[...] = (_rms(mm(C_CQ, C_KV)) * qag_ref[0]).astype(BF16)
    kv = mm(C_KV, C_HQ)
    ckvn = _rms(kv[:, :MLA_KV_LORA]) * kvg_ref[0]
    kvn_ref[...] = jnp.concatenate([ckvn, kv[:, MLA_KV_LORA:]], axis=1).astype(BF16)
    hq_ref[...] = mm(C_HQ, C_ZF).astype(BF16)

    log_lb = hs_ref[3 * layer]
    log_1m_lb = hs_ref[3 * layer + 1]
    one_m_lb = hs_ref[3 * layer + 2]
    for c0, lf_ref, kk_ref in ((C_ZF, lff_ref, kkf_ref), (C_ZB, lfb_ref, kkb_ref)):
        z = mm(c0, c0 + HG_HEADS * HG_K)
        b = log_1m_lb + _log_sigmoid(z)
        m = jnp.maximum(b, log_lb)
        lf_ref[...] = m + jnp.log1p(jnp.exp(-jnp.abs(b - log_lb)))
        kk_ref[...] = (one_m_lb * jax.nn.sigmoid(-z)).astype(BF16)

    hi_ref[...] = mm(C_HI, C_HG).astype(BF16)
    g = mm(C_HG, C_GATE)
    hgs_ref[...] = (g * jax.nn.sigmoid(g)).astype(BF16)
    for j in range(3):
        c0 = C_GATE + j * D_MODEL
        gates_ref[:, j * D_MODEL:(j + 1) * D_MODEL] = jax.nn.sigmoid(mm(c0, c0 + D_MODEL)).astype(BF16)


def _input_projection(layer, x_lat, x_ctx, ctx_row0, mod3, hg_scal, g1, w_in, qa_g, kva_g):
    row = lambda w: pl.BlockSpec((WB, w), lambda i: (i, 0))
    x_specs = _split_row_specs(WB, D_MODEL, ctx_row0)
    shapes = [
        (S5_WIDTH, BF16), (MLA_Q_LORA, BF16), (256, BF16), (HG_HEADS * HG_K, BF16),
        (HG_HEADS * HG_K, F32), (HG_HEADS * HG_K, F32), (HG_HEADS * HG_K, BF16), (HG_HEADS * HG_K, BF16),
        (HG_HEADS * HG_V, BF16), (HG_HEADS * HG_V, BF16), (3 * D_MODEL, BF16),
    ]
    return pl.pallas_call(
        functools.partial(_in_kernel, layer=layer),
        grid=(N_TOK // WB,),
        in_specs=[
            pl.BlockSpec(memory_space=pltpu.SMEM),
            *x_specs,
            pl.BlockSpec((1, 6, D_MODEL), lambda i: (_mod_row(i, WB), 0, 0)),
            _layer_spec(g1, layer), _layer_spec(w_in, layer), _layer_spec(qa_g, layer), _layer_spec(kva_g, layer),
        ],
        out_specs=[row(w) for w, _ in shapes],
        out_shape=[jax.ShapeDtypeStruct((N_TOK, w), dt) for w, dt in shapes],
        compiler_params=_params("arbitrary"),
        name="input_projection",
    )(hg_scal, x_lat, x_ctx, mod3, g1, w_in, qa_g, kva_g)


def _seq_block(b, j):
    return jnp.where(j == 0, NB_LAT + b, b * LAT_BLOCKS + j - 1)


def _seq_block_rev(b, j):
    return jnp.where(j == 0, NB_LAT + b, b * LAT_BLOCKS + LAT_BLOCKS - j)


S5_TB = 64
S5_ROWS = S5_TB * BATCH
S5_STEPS = (CTX_LEN + SEQ) // S5_TB
S5_CTX_STEPS = CTX_LEN // S5_TB


def _s5_kernel(uf_ref, ub_ref, bmat_ref, cmat_ref, a_ref, yf_ref, yb_ref, bu_ref, carry_ref):
    j = pl.program_id(0)

    @pl.when(j == 0)
    def _():
        carry_ref[...] = jnp.zeros_like(carry_ref)

    for d, (u_ref, y_ref) in enumerate(((uf_ref, yf_ref), (ub_ref, yb_ref))):
        bu_ref[...] = jnp.dot(u_ref[...], bmat_ref[0, d], preferred_element_type=F32)
        are = a_ref[0, d, 0]
        aim = a_ref[0, d, 1]

        def body(t, carry, d=d, are=are, aim=aim):
            hre, him = carry
            tt = (S5_TB - 1 - t) if d == 1 else t
            r0 = pl.multiple_of(tt * BATCH, BATCH)
            nre = are * hre - aim * him + bu_ref[pl.ds(r0, BATCH), 0:S5_LANES]
            nim = are * him + aim * hre + bu_ref[pl.ds(r0, BATCH), S5_LANES:2 * S5_LANES]
            bu_ref[pl.ds(r0, BATCH), 0:S5_LANES] = nre
            bu_ref[pl.ds(r0, BATCH), S5_LANES:2 * S5_LANES] = nim
            return nre, nim

        hre, him = lax.fori_loop(0, S5_TB, body, (carry_ref[d, 0], carry_ref[d, 1]), unroll=4)
        carry_ref[d, 0] = hre
        carry_ref[d, 1] = him
        y_ref[...] = jnp.dot(bu_ref[...].astype(BF16), cmat_ref[0, d], preferred_element_type=F32)


def _s5_block_rev(j):
    return jnp.where(j < S5_CTX_STEPS, S5_CTX_STEPS - 1 - j, S5_STEPS - 1 + S5_CTX_STEPS - j)


def _s5_states(layer, u_tm, bmat, cmat, a_tab):
    full = lambda a: _layer_spec(a, layer)
    fwd = pl.BlockSpec((S5_ROWS, S5_WIDTH), lambda j: (j, 0))
    bwd = pl.BlockSpec((S5_ROWS, S5_WIDTH), lambda j: (_s5_block_rev(j), 0))
    return pl.pallas_call(
        _s5_kernel,
        grid=(S5_STEPS,),
        in_specs=[fwd, bwd, full(bmat), full(cmat), full(a_tab)],
        out_specs=[fwd, bwd],
        out_shape=[jax.ShapeDtypeStruct((S5_STEPS * S5_ROWS, S5_WIDTH), F32)] * 2,
        scratch_shapes=[pltpu.VMEM((S5_ROWS, 2 * S5_LANES), F32), pltpu.VMEM((2, 2, BATCH, S5_LANES), F32)],
        compiler_params=_params("arbitrary"),
        name="s5_scan",
    )(u_tm, u_tm, bmat, cmat, a_tab)


def _to_time_major(a):
    w = a.shape[1]
    seq = jnp.concatenate([a[N_LAT:].reshape(BATCH, CTX_LEN, w), a[:N_LAT].reshape(BATCH, SEQ, w)], axis=1)
    return seq.transpose(1, 0, 2).reshape((CTX_LEN + SEQ) * BATCH, w)


def _from_time_major(a):
    w = a.shape[1]
    seq = a.reshape(CTX_LEN + SEQ, BATCH, w).transpose(1, 0, 2)
    return jnp.concatenate([seq[:, CTX_LEN:].reshape(N_LAT, w), seq[:, :CTX_LEN].reshape(N_CTX, w)], axis=0)


def _qkv_kernel(cqn_ref, kvn_ref, cos_ref, sin_ref, wq_ref, wk_ref, wv_ref, q_ref, k_ref, v_ref):
    cos = jnp.concatenate([cos_ref[...]] * MLA_HEADS, axis=1)
    sin = jnp.concatenate([sin_ref[...]] * MLA_HEADS, axis=1)
    w = MLA_HEADS * MLA_HEAD_PAD
    q2 = jnp.dot(cqn_ref[...], wq_ref[0], preferred_element_type=F32)
    q_ref[...] = ((q2[:, :w] * cos + q2[:, w:] * sin) * MLA_SCALE).astype(BF16)
    kv = kvn_ref[...]
    k2 = jnp.dot(kv, wk_ref[0], preferred_element_type=F32)
    k_ref[...] = (k2[:, :w] * cos + k2[:, w:] * sin).astype(BF16)
    lane = lax.broadcasted_iota(jnp.int32, (RB, w), 1)
    ones_lane = (lane % MLA_HEAD_PAD == MLA_V).astype(F32)
    v_ref[...] = (jnp.dot(kv, wv_ref[0], preferred_element_type=F32) + ones_lane).astype(BF16)


def _qkv(layer, cqn, kvn, cos_t, sin_t, wq, wk, wv):
    row = lambda w: pl.BlockSpec((RB, w), lambda i: (i, 0))
    full = lambda a: _layer_spec(a, layer)
    pos = pl.BlockSpec((RB, MLA_HEAD_PAD), lambda i: (jnp.where(i < NB_LAT, i % LAT_BLOCKS, LAT_BLOCKS), 0))
    w = MLA_HEADS * MLA_HEAD_PAD
    return pl.pallas_call(
        _qkv_kernel,
        grid=(NB_TOK,),
        in_specs=[row(MLA_Q_LORA), row(256), pos, pos, full(wq), full(wk), full(wv)],
        out_specs=[row(w), row(w), row(w)],
        out_shape=[jax.ShapeDtypeStruct((N_TOK, w), BF16)] * 3,
        compiler_params=_params("arbitrary"),
        name="mla_qkv",
    )(cqn, kvn, cos_t, sin_t, wq, wk, wv)


def _attn_kernel(*refs, n_kv):
    q_ref = refs[0]
    k_refs = refs[1:1 + n_kv]
    v_refs = refs[1 + n_kv:1 + 2 * n_kv]
    o_ref = refs[1 + 2 * n_kv]
    nt = (((1,), (1,)), ((), ()))
    for h in range(MLA_HEADS):
        q = q_ref[:, h * MLA_HEAD_PAD:(h + 1) * MLA_HEAD_PAD]
        s = [lax.dot_general(q, k_ref[:, h * MLA_HEAD_PAD:(h + 1) * MLA_HEAD_PAD], nt,
                             preferred_element_type=F32) for k_ref in k_refs]
        m = functools.reduce(jnp.maximum, [jnp.max(si, axis=-1, keepdims=True) for si in s])
        o = functools.reduce(jnp.add, [
            jnp.dot(jnp.exp(si - m).astype(BF16), v_ref[:, h * MLA_HEAD_PAD:(h + 1) * MLA_HEAD_PAD],
                    preferred_element_type=F32)
            for si, v_ref in zip(s, v_refs)])
        o_ref[:, h * MLA_V:(h + 1) * MLA_V] = (o[:, :MLA_V] / o[:, MLA_V:MLA_V + 1]).astype(BF16)


def _attention(q, k, v, latent):
    w = MLA_HEADS * MLA_HEAD_PAD
    wv = MLA_HEADS * MLA_V
    if latent:
        tq = WB
        nq = SEQ // tq
        q_spec = pl.BlockSpec((tq, w), lambda b, j: (b * nq + j, 0))
        k_specs = [pl.BlockSpec((SEQ, w), lambda b, j: (b, 0)),
                   pl.BlockSpec((CTX_LEN, w), lambda b, j: (NB_LAT + b, 0))]
        v_specs = [pl.BlockSpec((SEQ, w), lambda b, j: (b, 0)),
                   pl.BlockSpec((CTX_LEN, w), lambda b, j: (NB_LAT + b, 0))]
        o_spec = pl.BlockSpec((tq, wv), lambda b, j: (b * nq + j, 0))
        n_out = N_LAT
    else:
        tq = CTX_LEN
        nq = 1
        q_spec = pl.BlockSpec((tq, w), lambda b, j: (NB_LAT + b, 0))
        k_specs = [pl.BlockSpec((CTX_LEN, w), lambda b, j: (NB_LAT + b, 0))]
        v_specs = [pl.BlockSpec((CTX_LEN, w), lambda b, j: (NB_LAT + b, 0))]
        o_spec = pl.BlockSpec((tq, wv), lambda b, j: (b, 0))
        n_out = N_CTX
    n_kv = len(k_specs)
    return pl.pallas_call(
        functools.partial(_attn_kernel, n_kv=n_kv),
        grid=(BATCH, nq),
        in_specs=[q_spec] + k_specs + v_specs,
        out_specs=o_spec,
        out_shape=jax.ShapeDtypeStruct((n_out, wv), BF16),
        compiler_params=_params("arbitrary", "arbitrary"),
        name="mla_attention_lat" if latent else "mla_attention_ctx",
    )(q, *([k] * n_kv), *([v] * n_kv))


def _hgrn_tables():
    c = HG_CHUNK
    w = np.zeros((2, HG_SETS * c, c), np.float32)
    mask = np.zeros((2, HG_LEVELS, c, c), np.float32)
    idx = np.arange(c)
    for r in range(c):
        w[0, r, :r + 1] = 1
        w[1, r, r:] = 1
        w[0, c + r, r + 1:] = 1
        w[1, c + r, :r] = 1
    for l in range(HG_LEVELS):
        m = c >> (l + 1)
        for r in range(c):
            base = (r // (2 * m)) * 2 * m
            mid = base + m
            later = r >= mid
            row = (2 + l) * c + r
            if later:
                w[0, row, mid:r + 1] = 1
                w[1, row, mid:r] = 1
            else:
                w[0, row, r + 1:mid] = 1
                w[1, row, r:mid] = 1
        same = (idx[:, None] // (2 * m)) == (idx[None, :] // (2 * m))
        q_later = (idx[:, None] % (2 * m)) >= m
        k_later = (idx[None, :] % (2 * m)) >= m
        mask[0, l] = same & q_later & ~k_later
        mask[1, l] = same & ~q_later & k_later
    return w, mask


def _hgrn_dir(d, q_ref, k_ref, lf_ref, v_ref, w_ref, mask_ref, st_ref, o_ref):
    c = HG_CHUNK
    n_chunks = RB // c
    nt = (((1,), (1,)), ((), ()))
    tn = (((0,), (0,)), ((), ()))
    last = c - 1 if d == 0 else 0
    wsel = w_ref[d]
    e = []
    for ci in range(n_chunks):
        g = lf_ref[ci * c:(ci + 1) * c, :]
        g_hi = g.astype(BF16)
        g_lo = (g - g_hi.astype(F32)).astype(BF16)
        expo = jnp.dot(wsel, jnp.concatenate([g_hi, g_lo], axis=0), preferred_element_type=F32)
        e.append(jnp.exp(expo))

    def rows(s):
        return jnp.concatenate([e[ci][s * c:(s + 1) * c] for ci in range(n_chunks)], axis=0)

    q = q_ref[...].astype(F32)
    k = k_ref[...].astype(F32)
    q_state = (q * rows(0)).astype(BF16)
    k_state = (k * rows(1)).astype(BF16)
    q_lvl, k_lvl = [], []
    for l in range(HG_LEVELS):
        el = rows(2 + l)
        q_lvl.append((q * el).astype(BF16))
        k_lvl.append((k * el).astype(BF16))
    qk = q * k
    order = range(n_chunks) if d == 0 else range(n_chunks - 1, -1, -1)
    for h in range(HG_HEADS):
        ks = slice(h * HG_K, (h + 1) * HG_K)
        vs = slice(h * HG_V, (h + 1) * HG_V)
        v = v_ref[:, vs]
        scores = jnp.zeros((RB, RB), F32)
        for l in range(HG_LEVELS):
            p = lax.dot_general(q_lvl[l][:, ks], k_lvl[l][:, ks], nt, preferred_element_type=F32)
            scores = scores + p * mask_ref[d, l]
        o = jnp.dot(scores.astype(BF16), v, preferred_element_type=F32)
        o = o + jnp.sum(qk[:, ks], axis=-1, keepdims=True) * v.astype(F32)
        st = st_ref[d, h]
        o_state = [None] * n_chunks
        for ci in order:
            r = slice(ci * c, (ci + 1) * c)
            o_state[ci] = lax.dot_general(q_state[r, ks], st.astype(BF16), nt, preferred_element_type=F32)
            inc = lax.dot_general(v[r], k_state[r, ks], tn, preferred_element_type=F32)
            st = st * e[ci][last:last + 1, ks] + inc
        st_ref[d, h] = st
        o_ref[:, vs] = o + jnp.concatenate(o_state, axis=0)


def _hgrn_kernel(qf_ref, kf_ref, lff_ref, vf_ref, qb_ref, kb_ref, lfb_ref, vb_ref, w_ref, mask_ref,
                 of_ref, ob_ref, st_ref):
    j = pl.program_id(1)

    @pl.when(j == 0)
    def _():
        st_ref[...] = jnp.zeros_like(st_ref)

    _hgrn_dir(0, qf_ref, kf_ref, lff_ref, vf_ref, w_ref, mask_ref, st_ref, of_ref)
    _hgrn_dir(1, qb_ref, kb_ref, lfb_ref, vb_ref, w_ref, mask_ref, st_ref, ob_ref)


def _hgrn(hq, kkf, kkb, lff, lfb, hi, w_tab, mask_tab):
    full = _resident
    fwd = lambda w: pl.BlockSpec((RB, w), lambda b, j: (_seq_block(b, j), 0))
    bwd = lambda w: pl.BlockSpec((RB, w), lambda b, j: (_seq_block_rev(b, j), 0))
    wk = HG_HEADS * HG_K
    wv = HG_HEADS * HG_V
    return pl.pallas_call(
        _hgrn_kernel,
        grid=(BATCH, SEQ_BLOCKS),
        in_specs=[fwd(wk), fwd(wk), fwd(wk), fwd(wv), bwd(wk), bwd(wk), bwd(wk), bwd(wv),
                  full(w_tab), full(mask_tab)],
        out_specs=[fwd(wv), bwd(wv)],
        out_shape=[jax.ShapeDtypeStruct((N_TOK, wv), F32)] * 2,
        scratch_shapes=[pltpu.VMEM((2, HG_HEADS, HG_V, HG_K), F32)],
        compiler_params=_params("arbitrary", "arbitrary"),
        name="hgrn2_scan",
    )(hq, kkf, lff, hi, hq, kkb, lfb, hi, w_tab, mask_tab)


def _gelu_tanh(x):
    return 0.5 * x * (1.0 + jnp.tanh(math.sqrt(2.0 / math.pi) * (x + 0.044715 * (x * x * x))))


def _split_bf16(a):
    hi = a.astype(BF16)
    return hi, (a - hi.astype(F32)).astype(BF16)


def _merge_kernel(xl_ref, xc_ref, mod_ref, yf_ref, yb_ref, u_ref, of_ref, ob_ref, hgs_ref, attl_ref, attc_ref,
                  gates_ref, s5d_ref, gluw_ref, glub_ref, hgg_ref, hsum_ref, wpa_ref, wpb_ref, wpc_ref, wout_ref,
                  g2_ref, wrh_ref, wrl_ref, br_ref,
                  x1_ref, h2_ref, eid_ref, gate_ref):
    y = yf_ref[...] + yb_ref[...] + s5d_ref[0] * u_ref[...].astype(F32)
    y = _gelu_tanh(y)
    y = y * jax.nn.sigmoid(jnp.dot(y.astype(BF16), gluw_ref[0], preferred_element_type=F32) + glub_ref[0])
    o = of_ref[...] + ob_ref[...]
    sq_hi, sq_lo = _split_bf16(o * o)
    ms = (jnp.dot(sq_hi, hsum_ref[...], preferred_element_type=F32)
          + jnp.dot(sq_lo, hsum_ref[...], preferred_element_type=F32)) * (1.0 / HG_V)
    o = o * lax.rsqrt(ms + NORM_EPS) * hgg_ref[0] * hgs_ref[...].astype(F32)

    d = D_MODEL
    att = _pick_rows(attl_ref, attc_ref, WB)
    merged = (gates_ref[:, 0:d].astype(F32) * jnp.dot(y.astype(BF16), wpa_ref[0], preferred_element_type=F32)
              + gates_ref[:, d:2 * d].astype(F32) * jnp.dot(att, wpb_ref[0], preferred_element_type=F32)
              + gates_ref[:, 2 * d:3 * d].astype(F32) * jnp.dot(o.astype(BF16), wpc_ref[0], preferred_element_type=F32))
    y_out = jnp.dot(merged.astype(BF16), wout_ref[0], preferred_element_type=F32)
    x1 = _pick_rows(xl_ref, xc_ref, WB) + mod_ref[0, 2:3, :] * y_out
    x1_ref[...] = x1
    h2 = _rms(x1) * g2_ref[0] * (1.0 + mod_ref[0, 4:5, :]) + mod_ref[0, 3:4, :]
    for s in range(TOK_SUB):
        h2_ref[:, s, :] = h2[:, s * 128:(s + 1) * 128]

    h_hi, h_lo = _split_bf16(h2)
    logits = (jnp.dot(h_hi, wrh_ref[0], preferred_element_type=F32)
              + jnp.dot(h_lo, wrh_ref[0], preferred_element_type=F32)
              + jnp.dot(h_hi, wrl_ref[0], preferred_element_type=F32)) + br_ref[0]
    lane = lax.broadcasted_iota(jnp.int32, logits.shape, 1).astype(F32)
    neg = -jnp.inf
    glog = jnp.where(lane < MOE_GROUPS, logits, neg)
    gmax = jnp.max(glog, axis=-1, keepdims=True)
    gidx = jnp.min(jnp.where(glog == gmax, lane, 1e9), axis=-1, keepdims=True)
    g_w = 1.0 / jnp.sum(jnp.exp(glog - gmax), axis=-1, keepdims=True)
    e_lo = 32.0 + gidx * MOE_EXPERTS_PER_GROUP
    elog = jnp.where((lane >= e_lo) & (lane < e_lo + MOE_EXPERTS_PER_GROUP), logits, neg)
    v1 = jnp.max(elog, axis=-1, keepdims=True)
    i1 = jnp.min(jnp.where(elog == v1, lane, 1e9), axis=-1, keepdims=True)
    elog2 = jnp.where(lane == i1, neg, elog)
    v2 = jnp.max(elog2, axis=-1, keepdims=True)
    i2 = jnp.min(jnp.where(elog2 == v2, lane, 1e9), axis=-1, keepdims=True)
    e2 = jnp.exp(v2 - v1)
    gate1 = g_w / (1.0 + e2)
    gate2 = g_w * e2 / (1.0 + e2)
    eid_ref[...] = jnp.where(lane == 0.0, i1 - 32.0, jnp.where(lane == 1.0, i2 - 32.0, 0.0)).astype(jnp.int32)
    gate_ref[...] = jnp.where(lane == 0.0, gate1, jnp.where(lane == 1.0, gate2, 0.0))


def _merge(layer, n_blocks, x_lat, x_ctx, ctx_row0, mod3, yf, yb, s5u, of, ob, hgs, att_lat, att_ctx, gates,
           head_sum, consts):
    row = lambda w: pl.BlockSpec((WB, w), lambda i: (i, 0))
    n = n_blocks * RB
    wv = MLA_HEADS * MLA_V
    if att_ctx is None:
        att_ctx = att_lat
    const_specs = [_layer_spec(a, layer) for a in consts]
    const_specs.insert(4, _resident(head_sum))
    return pl.pallas_call(
        _merge_kernel,
        grid=(n // WB,),
        in_specs=[*_split_row_specs(WB, D_MODEL, ctx_row0),
                  pl.BlockSpec((1, 6, D_MODEL), lambda i: (_mod_row(i, WB), 0, 0)),
                  row(S5_WIDTH), row(S5_WIDTH), row(S5_WIDTH), row(HG_HEADS * HG_V), row(HG_HEADS * HG_V),
                  row(HG_HEADS * HG_V), *_split_row_specs(WB, wv, 0), row(3 * D_MODEL)] + const_specs,
        out_specs=[row(D_MODEL), pl.BlockSpec((WB, TOK_SUB, 128), lambda i: (i, 0, 0)), row(128), row(128)],
        out_shape=[jax.ShapeDtypeStruct((n, D_MODEL), F32), jax.ShapeDtypeStruct((n, TOK_SUB, 128), F32),
                   jax.ShapeDtypeStruct((n, 128), jnp.int32), jax.ShapeDtypeStruct((n, 128), F32)],
        compiler_params=_params("arbitrary"),
        name="merge_router",
    )(x_lat, x_ctx, mod3, yf, yb, s5u, of, ob, hgs, att_lat, att_ctx, gates,
      *consts[:4], head_sum, *consts[4:])


def _moe_kernel(bexp_ref, nblk_ref, src_ref, srcn_ref, dst_ref, h_hbm, w1_ref, w3_ref, w2_ref, y_hbm,
                xbuf, ybuf, w1b, w3b, w2b, gsem, ssem):
    i = pl.program_id(0)
    n_used = nblk_ref[0]
    slot = lax.rem(i, 2)

    def gather_start(idx_ref, s):
        for r in range(MOE_ROWS):
            pltpu.make_async_copy(h_hbm.at[idx_ref[0, 0, r]], xbuf.at[s, r], gsem.at[s]).start()

    def gather_wait(s):
        pltpu.make_async_copy(h_hbm.at[pl.ds(0, MOE_ROWS)], xbuf.at[s], gsem.at[s]).wait()

    def scatter_start(s):
        for r in range(MOE_ROWS):
            pltpu.make_async_copy(ybuf.at[s, r], y_hbm.at[dst_ref[0, 0, r]], ssem.at[s]).start()

    def scatter_wait(s):
        pltpu.make_async_copy(ybuf.at[s], y_hbm.at[pl.ds(0, MOE_ROWS)], ssem.at[s]).wait()

    @pl.when(i < n_used)
    def _():
        @pl.when(i == 0)
        def _():
            gather_start(src_ref, 0)
            ybuf[1] = jnp.zeros((MOE_ROWS, TOK_SUB, 128), F32)
            n_real = y_hbm.shape[0] - MOE_ROWS
            fill = pltpu.make_async_copy(ybuf.at[1], y_hbm.at[pl.ds(n_real, MOE_ROWS)], ssem.at[1])
            fill.start()
            fill.wait()

        @pl.when(i + 1 < n_used)
        def _():
            gather_start(srcn_ref, 1 - slot)

        @pl.when((i == 0) | (bexp_ref[i] != bexp_ref[jnp.maximum(i - 1, 0)]))
        def _():
            w1b[...] = w1_ref[0, 0].astype(BF16)
            w3b[...] = w3_ref[0, 0].astype(BF16)
            w2b[...] = w2_ref[0, 0].astype(BF16)

        gather_wait(slot)

        @pl.when(i >= 2)
        def _():
            scatter_wait(slot)

        xs = xbuf.at[slot]
        x = jnp.concatenate([xs[:, c, :] for c in range(TOK_SUB)], axis=1).astype(BF16)
        a = jnp.dot(x, w1b[...], preferred_element_type=F32)
        g = jnp.dot(x, w3b[...], preferred_element_type=F32)
        hid = (a * jax.nn.sigmoid(a) * g).astype(BF16)
        y = jnp.dot(hid, w2b[...], preferred_element_type=F32)
        ys = ybuf.at[slot]
        for c in range(TOK_SUB):
            ys[:, c, :] = y[:, c * 128:(c + 1) * 128]
        scatter_start(slot)

        @pl.when(i == n_used - 1)
        def _():
            @pl.when(i >= 1)
            def _():
                scatter_wait(1 - slot)
            scatter_wait(slot)


def _moe(h2, eid, w1, w3, w2, layer):
    n = h2.shape[0]
    n_assign = n * MOE_TOP_K
    n_blocks = (n_assign + MOE_EXPERTS * (MOE_ROWS - 1) + MOE_ROWS - 1) // MOE_ROWS
    flat_e = eid.reshape(n_assign)
    order = jnp.argsort(flat_e, stable=True).astype(jnp.int32)
    experts = jnp.arange(MOE_EXPERTS, dtype=jnp.int32)
    counts = jnp.sum((flat_e[:, None] == experts[None, :]).astype(jnp.int32), axis=0)
    starts = jnp.cumsum(counts) - counts
    padded = (counts + MOE_ROWS - 1) // MOE_ROWS * MOE_ROWS
    p_ends = jnp.cumsum(padded)
    p_starts = p_ends - padded
    n_used_s = p_ends[-1] // MOE_ROWS
    n_used = n_used_s.astype(jnp.int32).reshape(1)
    blk = jnp.arange(n_blocks, dtype=jnp.int32)
    blk_expert = jnp.sum((p_ends[None, :] <= (blk * MOE_ROWS)[:, None]).astype(jnp.int32), axis=1)
    last_expert = jnp.max(jnp.where(counts > 0, experts, 0))
    block_expert = jnp.where(blk < n_used_s, jnp.minimum(blk_expert, MOE_EXPERTS - 1), last_expert)
    onehot = (block_expert[:, None] == experts[None, :]).astype(jnp.int32)
    pick = lambda table: jnp.sum(onehot * table[None, :], axis=1)
    local = jnp.arange(MOE_ROWS, dtype=jnp.int32)[None, :]
    rank = blk[:, None] * MOE_ROWS + local - pick(p_starts)[:, None]
    valid = (rank < pick(counts)[:, None]) & (blk < n_used_s)[:, None]
    pos = jnp.clip(pick(starts)[:, None] + rank, 0, n_assign - 1)
    assign = order[pos]
    tok = jnp.where(valid, assign // MOE_TOP_K, 0)
    dst = jnp.where(valid, (assign % MOE_TOP_K) * n + assign // MOE_TOP_K, n_assign + local)
    rows_now = pl.BlockSpec((1, 1, MOE_ROWS), lambda i, be, nb: (i, 0, 0), memory_space=pltpu.SMEM)
    rows_next = pl.BlockSpec((1, 1, MOE_ROWS), lambda i, be, nb: (jnp.minimum(i + 1, n_blocks - 1), 0, 0),
                             memory_space=pltpu.SMEM)
    src3 = tok.reshape(n_blocks, 1, MOE_ROWS)
    return pl.pallas_call(
        _moe_kernel,
        grid_spec=pltpu.PrefetchScalarGridSpec(
            num_scalar_prefetch=2,
            grid=(n_blocks,),
            in_specs=[
                rows_now, rows_next, rows_now,
                pl.BlockSpec(memory_space=pl.ANY),
                pl.BlockSpec((1, 1, D_MODEL, MOE_HIDDEN), lambda i, be, nb: (layer, be[i], 0, 0)),
                pl.BlockSpec((1, 1, D_MODEL, MOE_HIDDEN), lambda i, be, nb: (layer, be[i], 0, 0)),
                pl.BlockSpec((1, 1, MOE_HIDDEN, D_MODEL), lambda i, be, nb: (layer, be[i], 0, 0)),
            ],
            out_specs=pl.BlockSpec(memory_space=pl.ANY),
            scratch_shapes=[pltpu.VMEM((2, MOE_ROWS, TOK_SUB, 128), F32), pltpu.VMEM((2, MOE_ROWS, TOK_SUB, 128), F32),
                            pltpu.VMEM((D_MODEL, MOE_HIDDEN), BF16), pltpu.VMEM((D_MODEL, MOE_HIDDEN), BF16),
                            pltpu.VMEM((MOE_HIDDEN, D_MODEL), BF16),
                            pltpu.SemaphoreType.DMA((2,)), pltpu.SemaphoreType.DMA((2,))],
        ),
        out_shape=jax.ShapeDtypeStruct((n_assign + MOE_ROWS, TOK_SUB, 128), F32),
        compiler_params=_params("arbitrary"),
        name="moe_experts",
    )(block_expert, n_used, src3, src3, dst.reshape(n_blocks, 1, MOE_ROWS), h2, w1, w3, w2)


def _combine_kernel(x1_ref, y0_ref, y1_ref, gate_ref, mod_ref, g_ref, o_ref, *, final):
    g0 = gate_ref[:, 0:1]
    g1 = gate_ref[:, 1:2]
    f = jnp.concatenate([g0 * y0_ref[:, c, :] + g1 * y1_ref[:, c, :] for c in range(TOK_SUB)], axis=1)
    x2 = x1_ref[...] + mod_ref[0, 5:6, :] * f
    if final:
        x2 = _rms(x2) * g_ref[...]
    o_ref[...] = x2


def _combine(n_blocks, x1, y, gate, mod3, g_final, final):
    row = lambda w: pl.BlockSpec((RB, w), lambda i: (i, 0))
    n = n_blocks * RB
    return pl.pallas_call(
        functools.partial(_combine_kernel, final=final),
        grid=(n_blocks,),
        in_specs=[row(D_MODEL), pl.BlockSpec((RB, TOK_SUB, 128), lambda i: (i, 0, 0)),
                  pl.BlockSpec((RB, TOK_SUB, 128), lambda i: (i + n_blocks, 0, 0)), row(128),
                  pl.BlockSpec((1, 6, D_MODEL), lambda i: (_mod_row(i), 0, 0)),
                  pl.BlockSpec((1, D_MODEL), lambda i: (0, 0))],
        out_specs=row(D_MODEL),
        out_shape=jax.ShapeDtypeStruct((n, D_MODEL), F32),
        compiler_params=_params("arbitrary"),
        name="moe_combine_final" if final else "moe_combine",
    )(x1, y, y, gate, mod3, g_final)


def _rope_rot_cols(w):
    p = ROPE_PAIRS
    return jnp.concatenate([-w[..., p:2 * p], w[..., 0:p], -w[..., 3 * p:4 * p], w[..., 2 * p:3 * p]], axis=-1)


def _pack_w_in(w_in):
    cuts = np.cumsum([S5_WIDTH, MLA_Q_LORA, MLA_KV_LORA, MLA_ROPE, HG_HEADS * HG_K, HG_HEADS * HG_K,
                      HG_HEADS * HG_K, HG_HEADS * HG_V, HG_HEADS * HG_V, D_MODEL, D_MODEL])
    (w_s5, w_cq, w_ckv, w_kpe, w_hq, w_zf, w_zb, w_hi, w_hg, w_ga, w_gb, w_gc) = jnp.split(w_in, cuts, axis=-1)
    pad = jnp.zeros(w_in.shape[:-1] + (256 - MLA_KV_LORA - 2 * MLA_ROPE,), w_in.dtype)
    packed = jnp.concatenate([w_s5, w_cq, w_ckv, w_kpe, _rope_rot_cols(w_kpe), pad,
                              w_hq, w_zf, w_zb, w_hi, w_hg, w_ga, w_gb, w_gc], axis=-1)
    assert packed.shape[-1] == W_IN_COLS
    return packed.astype(BF16)


def _pack_mla(w_uq, w_uk, w_uv):
    hp, n, r = MLA_HEAD_PAD, MLA_NOPE, MLA_ROPE
    nl = w_uq.shape[0]
    wq = w_uq.reshape(nl, MLA_Q_LORA, MLA_HEADS, n + r)
    zq = jnp.zeros((nl, MLA_Q_LORA, MLA_HEADS, hp - n - r), F32)
    q1 = jnp.concatenate([wq, zq], axis=-1).reshape(nl, MLA_Q_LORA, MLA_HEADS * hp)
    q2 = jnp.concatenate([jnp.zeros((nl, MLA_Q_LORA, MLA_HEADS, n), F32), _rope_rot_cols(wq[..., n:]), zq],
                         axis=-1).reshape(nl, MLA_Q_LORA, MLA_HEADS * hp)
    wq_packed = jnp.concatenate([q1, q2], axis=-1).astype(BF16)

    wk = w_uk.reshape(nl, MLA_KV_LORA, MLA_HEADS, n)
    k_nope = jnp.concatenate([wk, jnp.zeros((nl, MLA_KV_LORA, MLA_HEADS, hp - n), F32)], axis=-1)
    pe_slot = np.concatenate([np.zeros((r, n), np.float32), np.eye(r, dtype=np.float32),
                              np.zeros((r, hp - n - r), np.float32)], axis=-1)
    pe_all = jnp.asarray(np.tile(pe_slot[None, :, None, :], (nl, 1, MLA_HEADS, 1)))
    zero_pe = jnp.zeros_like(pe_all)
    tail = jnp.zeros((nl, 256 - MLA_KV_LORA - 2 * r, MLA_HEADS, hp), F32)
    k1 = jnp.concatenate([k_nope, pe_all, zero_pe, tail], axis=1).reshape(nl, 256, MLA_HEADS * hp)
    k2 = jnp.concatenate([jnp.zeros_like(k_nope), zero_pe, pe_all, tail], axis=1).reshape(nl, 256, MLA_HEADS * hp)
    wk_packed = jnp.concatenate([k1, k2], axis=-1).astype(BF16)
    wv = w_uv.reshape(nl, MLA_KV_LORA, MLA_HEADS, MLA_V)
    wv = jnp.concatenate([wv, jnp.zeros((nl, MLA_KV_LORA, MLA_HEADS, hp - MLA_V), F32)], axis=-1)
    wv_packed = jnp.concatenate([wv.reshape(nl, MLA_KV_LORA, MLA_HEADS * hp),
                                 jnp.zeros((nl, 256 - MLA_KV_LORA, MLA_HEADS * hp), F32)], axis=1).astype(BF16)
    return wq_packed, wk_packed, wv_packed


def _rope_tables():
    rows = SEQ // GRID_W
    row = np.repeat(np.arange(rows, dtype=np.float32), GRID_W)
    col = np.tile(np.arange(GRID_W, dtype=np.float32), rows)
    inv = (np.float32(ROPE_BASE) ** (-np.arange(ROPE_PAIRS, dtype=np.float32) / np.float32(ROPE_PAIRS))).astype(np.float32)
    ar, ac = row[:, None] * inv, col[:, None] * inv
    cos = np.concatenate([np.cos(ar), np.cos(ar), np.cos(ac), np.cos(ac)], axis=1)
    sin = np.concatenate([np.sin(ar), np.sin(ar), np.sin(ac), np.sin(ac)], axis=1)
    tail = MLA_HEAD_PAD - MLA_NOPE - MLA_ROPE
    cos_t = np.concatenate([np.ones((SEQ, MLA_NOPE)), cos, np.ones((SEQ, tail))], axis=1)
    sin_t = np.concatenate([np.zeros((SEQ, MLA_NOPE)), sin, np.zeros((SEQ, tail))], axis=1)
    cos_t = np.concatenate([cos_t, np.ones((CTX_LEN, MLA_HEAD_PAD))], axis=0)
    sin_t = np.concatenate([sin_t, np.zeros((CTX_LEN, MLA_HEAD_PAD))], axis=0)
    return jnp.asarray(cos_t, F32), jnp.asarray(sin_t, F32)


def _pack_s5(lam_re, lam_im, b_re, b_im, c_re, c_im, log_step):
    eye = jnp.eye(S5_GROUPS, dtype=F32)
    dt = jnp.exp(log_step)[..., None]
    mag = jnp.exp(lam_re * dt)
    lb_re, lb_im = mag * jnp.cos(lam_im * dt), mag * jnp.sin(lam_im * dt)
    den = lam_re * lam_re + lam_im * lam_im
    fr = ((lb_re - 1) * lam_re + lb_im * lam_im) / den
    fi = (lb_im * lam_re - (lb_re - 1) * lam_im) / den
    bb_re = fr[..., None] * b_re - fi[..., None] * b_im
    bb_im = fr[..., None] * b_im + fi[..., None] * b_re
    lead = lam_re.shape[:2]

    def in_mat(bb):
        return jnp.einsum('ldgph,gk->ldghkp', bb, eye).reshape(lead + (S5_WIDTH, S5_LANES))

    def out_mat(cc):
        return jnp.einsum('ldghp,gk->ldgpkh', cc, eye).reshape(lead + (S5_LANES, S5_WIDTH))

    bmat = jnp.concatenate([in_mat(bb_re), in_mat(bb_im)], axis=-1).astype(BF16)
    cmat = jnp.concatenate([out_mat(c_re), -out_mat(c_im)], axis=-2).astype(BF16)
    a_tab = jnp.stack([jnp.broadcast_to(lb_re.reshape(lead + (1, S5_LANES)), lead + (BATCH, S5_LANES)),
                       jnp.broadcast_to(lb_im.reshape(lead + (1, S5_LANES)), lead + (BATCH, S5_LANES))], axis=2)
    return bmat, cmat, a_tab


def kernel(x, c, ctx, c_ctx, mod_w, mod_b, norm1_g, norm2_g, w_in, s5_lam_re, s5_lam_im, s5_b_re, s5_b_im, s5_c_re, s5_c_im, s5_log_step, s5_d, s5_glu_w, s5_glu_b, mla_qa_g, mla_kva_g, mla_w_uq, mla_w_uk, mla_w_uv, hg_lb_logits, hg_norm_g, w_pa, w_pb, w_pc, w_out, moe_w_group, moe_b_group, moe_w_expert, moe_b_expert, moe_w1, moe_w3, moe_w2, final_norm_g):
    x_lat, x_ctx, ctx_row0 = x.reshape(N_LAT, D_MODEL), ctx.reshape(N_CTX, D_MODEL), 0
    c_rows = jnp.concatenate([c, c_ctx[None, :], jnp.zeros((16 - BATCH - 1, D_MODEL), F32)], axis=0)
    mod = _modulation(c_rows, mod_w, mod_b).reshape(DEPTH, 16, 6, D_MODEL)

    lb_all = jnp.cumsum(jax.nn.softmax(hg_lb_logits.astype(F32)))
    lb_all = lb_all - lb_all[0]
    hg_scal = jnp.stack([jnp.log(lb_all), jnp.log1p(-lb_all), 1.0 - lb_all], axis=1).reshape(3 * DEPTH).astype(F32)
    cos_t, sin_t = _rope_tables()

    vec = lambda a: a[:, None, :]
    w_in_p = _pack_w_in(w_in)
    bmat, cmat, a_tab = _pack_s5(s5_lam_re, s5_lam_im, s5_b_re, s5_b_im, s5_c_re, s5_c_im, s5_log_step)
    wq, wk, wv = _pack_mla(mla_w_uq, mla_w_uk, mla_w_uv)
    w_route = jnp.concatenate([moe_w_group, jnp.zeros((DEPTH, D_MODEL, 32 - MOE_GROUPS), F32), moe_w_expert,
                               jnp.zeros((DEPTH, D_MODEL, 128 - 32 - MOE_EXPERTS), F32)], axis=-1)
    b_route = jnp.concatenate([moe_b_group, jnp.zeros((DEPTH, 32 - MOE_GROUPS), F32), moe_b_expert,
                               jnp.zeros((DEPTH, 128 - 32 - MOE_EXPERTS), F32)], axis=-1)
    w_route_hi = w_route.astype(BF16)
    w_route_lo = (w_route - w_route_hi.astype(F32)).astype(BF16)
    merge_consts = [vec(s5_d), s5_glu_w.astype(BF16), vec(s5_glu_b), vec(hg_norm_g),
                    w_pa.astype(BF16), w_pb.astype(BF16), w_pc.astype(BF16), w_out.astype(BF16), vec(norm2_g),
                    w_route_hi, w_route_lo, vec(b_route)]
    w_tab_np, mask_tab_np = _hgrn_tables()
    w_tab = jnp.asarray(np.concatenate([w_tab_np, w_tab_np], axis=-1), BF16)
    mask_tab = jnp.asarray(np.kron(np.eye(RB // HG_CHUNK, dtype=np.float32), mask_tab_np), F32)
    head_sum = jnp.asarray(np.kron(np.eye(HG_HEADS), np.ones((HG_V, HG_V))), F32)

    out = None
    for layer in range(DEPTH):
        last = layer == DEPTH - 1
        mod3 = mod[layer]
        (s5u, cqn, kvn, hq, lff, lfb, kkf, kkb, hi, hgs, gates) = _input_projection(
            layer, x_lat, x_ctx, ctx_row0, mod3, hg_scal, vec(norm1_g), w_in_p, vec(mla_qa_g), vec(mla_kva_g))

        yf, yb = [_from_time_major(y) for y in _s5_states(layer, _to_time_major(s5u), bmat, cmat, a_tab)]

        q, k, v = _qkv(layer, cqn, kvn, cos_t, sin_t, wq, wk, wv)
        att_lat = _attention(q, k, v, latent=True)
        att_ctx = None if last else _attention(q, k, v, latent=False)

        of, ob = _hgrn(hq, kkf, kkb, lff, lfb, hi, w_tab, mask_tab)

        n_blocks = NB_LAT if last else NB_TOK
        x1, h2, eid, gate = _merge(layer, n_blocks, x_lat, x_ctx, ctx_row0, mod3, yf, yb, s5u, of, ob, hgs,
                                   att_lat, att_ctx, gates, head_sum, merge_consts)

        y_moe = _moe(h2, eid[:, :MOE_TOP_K], moe_w1, moe_w3, moe_w2, layer)
        res = _combine(n_blocks, x1, y_moe, gate, mod3, final_norm_g[None, :], final=last)
        if last:
            out = res
        else:
            x_lat, x_ctx, ctx_row0 = res, res, N_LAT
    return out.reshape(BATCH, SEQ, D_MODEL)
```

```python
import functools
import math

import jax
import jax.numpy as jnp
from jax import lax
import numpy as np
from jax.experimental import pallas as pl
from jax.experimental.pallas import tpu as pltpu

F32 = jnp.float32
BF16 = jnp.bfloat16
HIGHEST = lax.Precision.HIGHEST

D_MODEL = 1024
BATCH = 8
SEQ = 2048
DEPTH = 2
GRID_W = 64
CTX_LEN = 256
NORM_EPS = 1e-6

S5_WIDTH = D_MODEL // 4
S5_GROUP_CH = 16
S5_GROUPS = S5_WIDTH // S5_GROUP_CH
S5_STATE = 64
S5_LANES = S5_GROUPS * S5_STATE

MLA_HEADS = D_MODEL // 128
MLA_NOPE = 64
MLA_ROPE = 32
MLA_V = 64
MLA_Q_LORA = D_MODEL // 4
MLA_KV_LORA = D_MODEL // 8
MLA_SCALE = 1.0 / math.sqrt(MLA_NOPE + MLA_ROPE)
MLA_HEAD_PAD = 128
ROPE_PAIRS = MLA_ROPE // 4
ROPE_BASE = 10000.0

HG_HEADS = D_MODEL // 256
HG_K = 128
HG_V = 64
HG_CHUNK = 64
HG_LEVELS = 6
HG_SETS = 2 + HG_LEVELS
HG_FAST_CHUNK = 32
HG_FAST_MIN = -60.0

MOE_GROUPS = 4
MOE_EXPERTS_PER_GROUP = 8
MOE_EXPERTS = MOE_GROUPS * MOE_EXPERTS_PER_GROUP
MOE_TOP_K = 2
MOE_HIDDEN = D_MODEL // 2
MOE_ROWS = 256

N_LAT = BATCH * SEQ
N_CTX = BATCH * CTX_LEN
N_TOK = N_LAT + N_CTX
RB = CTX_LEN
WB = 2 * RB
NB_LAT = N_LAT // RB
NB_CTX = N_CTX // RB
NB_TOK = N_TOK // RB
LAT_BLOCKS = SEQ // RB
SEQ_BLOCKS = LAT_BLOCKS + 1

C_S5 = 0
C_CQ = 256
C_KV = 512
C_HQ = 768
C_ZF = 1280
C_ZB = 1792
C_HI = 2304
C_HG = 2560
C_GATE = 2816
W_IN_COLS = C_GATE + 3 * D_MODEL

VMEM_LIMIT = 56 * 1024 * 1024


def _params(*sem):
    return pltpu.CompilerParams(dimension_semantics=sem, vmem_limit_bytes=VMEM_LIMIT)


def _rms(v):
    return v * lax.rsqrt(jnp.mean(v * v, axis=-1, keepdims=True) + NORM_EPS)


def _mod_row(i, rb=RB):
    return jnp.where(i < N_LAT // rb, i // (SEQ // rb), BATCH)


def _resident(a):
    return pl.BlockSpec(a.shape, lambda *_: (0,) * a.ndim, pipeline_mode=pl.Buffered(1))


def _layer_spec(a, layer):
    return pl.BlockSpec((1,) + a.shape[1:], lambda *_: (layer,) + (0,) * (a.ndim - 1),
                        pipeline_mode=pl.Buffered(1))


def _split_row_specs(rb, w, ctx_row0):
    n_lat = N_LAT // rb
    lat = pl.BlockSpec((rb, w), lambda i: (jnp.minimum(i, n_lat - 1), 0))
    ctx = pl.BlockSpec((rb, w), lambda i: (ctx_row0 // rb + jnp.maximum(i - n_lat, 0), 0))
    return lat, ctx


def _pick_rows(lat_ref, ctx_ref, rb):
    return jnp.where(pl.program_id(0) < N_LAT // rb, lat_ref[...], ctx_ref[...])


def _mod_kernel(c_ref, w_ref, b_ref, o_ref):
    c = c_ref[...]
    s = c * jax.nn.sigmoid(c)
    o_ref[0] = jnp.dot(s, w_ref[0], precision=HIGHEST, preferred_element_type=F32) + b_ref[0]


def _modulation(c_rows, mod_w, mod_b):
    bn = 1536
    return pl.pallas_call(
        _mod_kernel,
        grid=(DEPTH, 6 * D_MODEL // bn),
        in_specs=[
            pl.BlockSpec((16, D_MODEL), lambda l, j: (0, 0)),
            pl.BlockSpec((1, D_MODEL, bn), lambda l, j: (l, 0, j)),
            pl.BlockSpec((1, 1, bn), lambda l, j: (l, 0, j)),
        ],
        out_specs=pl.BlockSpec((1, 16, bn), lambda l, j: (l, 0, j)),
        out_shape=jax.ShapeDtypeStruct((DEPTH, 16, 6 * D_MODEL), F32),
        compiler_params=_params("arbitrary", "arbitrary"),
        name="adaln_mod",
    )(c_rows, mod_w, mod_b.reshape(DEPTH, 1, 6 * D_MODEL))


def _log_sigmoid(z):
    return jnp.minimum(z, 0.0) - jnp.log1p(jnp.exp(-jnp.abs(z)))


def _in_kernel(hs_ref, xl_ref, xc_ref, mod_ref, g1_ref, w_ref, qag_ref, kvg_ref,
               s5u_ref, cqn_ref, kvn_ref, hq_ref, lff_ref, lfb_ref, kkf_ref, kkb_ref,
               hi_ref, hgs_ref, gates_ref, *, layer):
    x = _pick_rows(xl_ref, xc_ref, WB)
    xn = _rms(x) * g1_ref[0]
    h = (xn * (1.0 + mod_ref[0, 1:2, :]) + mod_ref[0, 0:1, :]).astype(BF16)

    def mm(a, b):
        return jnp.dot(h, w_ref[0, :, a:b], preferred_element_type=F32)

    s5u_ref[...] = mm(C_S5, C_CQ).astype(BF16)
    cqn_ref[...] = (_rms(mm(C_CQ, C_KV)) * qag_ref[0]).astype(BF16)
    kv = mm(C_KV, C_HQ)
    ckvn = _rms(kv[:, :MLA_KV_LORA]) * kvg_ref[0]
    kvn_ref[...] = jnp.concatenate([ckvn, kv[:, MLA_KV_LORA:]], axis=1).astype(BF16)
    hq_ref[...] = mm(C_HQ, C_ZF).astype(BF16)

    log_lb = hs_ref[3 * layer]
    log_1m_lb = hs_ref[3 * layer + 1]
    one_m_lb = hs_ref[3 * layer + 2]
    for c0, lf_ref, kk_ref in ((C_ZF, lff_ref, kkf_ref), (C_ZB, lfb_ref, kkb_ref)):
        z = mm(c0, c0 + HG_HEADS * HG_K)
        b = log_1m_lb + _log_sigmoid(z)
        m = jnp.maximum(b, log_lb)
        lf_ref[...] = m + jnp.log1p(jnp.exp(-jnp.abs(b - log_lb)))
        kk_ref[...] = (one_m_lb * jax.nn.sigmoid(-z)).astype(BF16)

    hi_ref[...] = mm(C_HI, C_HG).astype(BF16)
    g = mm(C_HG, C_GATE)
    hgs_ref[...] = (g * jax.nn.sigmoid(g)).astype(BF16)
    for j in range(3):
        c0 = C_GATE + j * D_MODEL
        gates_ref[:, j * D_MODEL:(j + 1) * D_MODEL] = jax.nn.sigmoid(mm(c0, c0 + D_MODEL)).astype(BF16)


def _input_projection(layer, x_lat, x_ctx, ctx_row0, mod3, hg_scal, g1, w_in, qa_g, kva_g):
    row = lambda w: pl.BlockSpec((WB, w), lambda i: (i, 0))
    x_specs = _split_row_specs(WB, D_MODEL, ctx_row0)
    shapes = [
        (S5_WIDTH, BF16), (MLA_Q_LORA, BF16), (256, BF16), (HG_HEADS * HG_K, BF16),
        (HG_HEADS * HG_K, F32), (HG_HEADS * HG_K, F32), (HG_HEADS * HG_K, BF16), (HG_HEADS * HG_K, BF16),
        (HG_HEADS * HG_V, BF16), (HG_HEADS * HG_V, BF16), (3 * D_MODEL, BF16),
    ]
    return pl.pallas_call(
        functools.partial(_in_kernel, layer=layer),
        grid=(N_TOK // WB,),
        in_specs=[
            pl.BlockSpec(memory_space=pltpu.SMEM),
            *x_specs,
            pl.BlockSpec((1, 6, D_MODEL), lambda i: (_mod_row(i, WB), 0, 0)),
            _layer_spec(g1, layer), _layer_spec(w_in, layer), _layer_spec(qa_g, layer), _layer_spec(kva_g, layer),
        ],
        out_specs=[row(w) for w, _ in shapes],
        out_shape=[jax.ShapeDtypeStruct((N_TOK, w), dt) for w, dt in shapes],
        compiler_params=_params("arbitrary"),
        name="input_projection",
    )(hg_scal, x_lat, x_ctx, mod3, g1, w_in, qa_g, kva_g)


def _seq_block(b, j):
    return jnp.where(j == 0, NB_LAT + b, b * LAT_BLOCKS + j - 1)


def _seq_block_rev(b, j):
    return jnp.where(j == 0, NB_LAT + b, b * LAT_BLOCKS + LAT_BLOCKS - j)


S5_TB = 64
S5_ROWS = S5_TB * BATCH
S5_STEPS = (CTX_LEN + SEQ) // S5_TB
S5_CTX_STEPS = CTX_LEN // S5_TB


def _s5_kernel(uf_ref, ub_ref, bmat_ref, cmat_ref, a_ref, yf_ref, yb_ref, bu_ref, carry_ref):
    j = pl.program_id(0)

    @pl.when(j == 0)
    def _():
        carry_ref[...] = jnp.zeros_like(carry_ref)

    for d, (u_ref, y_ref) in enumerate(((uf_ref, yf_ref), (ub_ref, yb_ref))):
        bu_ref[...] = jnp.dot(u_ref[...], bmat_ref[0, d], preferred_element_type=F32)
        are = a_ref[0, d, 0]
        aim = a_ref[0, d, 1]

        def body(t, carry, d=d, are=are, aim=aim):
            hre, him = carry
            tt = (S5_TB - 1 - t) if d == 1 else t
            r0 = pl.multiple_of(tt * BATCH, BATCH)
            nre = are * hre - aim * him + bu_ref[pl.ds(r0, BATCH), 0:S5_LANES]
            nim = are * him + aim * hre + bu_ref[pl.ds(r0, BATCH), S5_LANES:2 * S5_LANES]
            bu_ref[pl.ds(r0, BATCH), 0:S5_LANES] = nre
            bu_ref[pl.ds(r0, BATCH), S5_LANES:2 * S5_LANES] = nim
            return nre, nim

        hre, him = lax.fori_loop(0, S5_TB, body, (carry_ref[d, 0], carry_ref[d, 1]), unroll=4)
        carry_ref[d, 0] = hre
        carry_ref[d, 1] = him
        y_ref[...] = jnp.dot(bu_ref[...].astype(BF16), cmat_ref[0, d], preferred_element_type=F32)


def _s5_block_rev(j):
    return jnp.where(j < S5_CTX_STEPS, S5_CTX_STEPS - 1 - j, S5_STEPS - 1 + S5_CTX_STEPS - j)


def _s5_states(layer, u_tm, bmat, cmat, a_tab):
    full = lambda a: _layer_spec(a, layer)
    fwd = pl.BlockSpec((S5_ROWS, S5_WIDTH), lambda j: (j, 0))
    bwd = pl.BlockSpec((S5_ROWS, S5_WIDTH), lambda j: (_s5_block_rev(j), 0))
    return pl.pallas_call(
        _s5_kernel,
        grid=(S5_STEPS,),
        in_specs=[fwd, bwd, full(bmat), full(cmat), full(a_tab)],
        out_specs=[fwd, bwd],
        out_shape=[jax.ShapeDtypeStruct((S5_STEPS * S5_ROWS, S5_WIDTH), F32)] * 2,
        scratch_shapes=[pltpu.VMEM((S5_ROWS, 2 * S5_LANES), F32), pltpu.VMEM((2, 2, BATCH, S5_LANES), F32)],
        compiler_params=_params("arbitrary"),
        name="s5_scan",
    )(u_tm, u_tm, bmat, cmat, a_tab)


def _to_time_major(a):
    w = a.shape[1]
    seq = jnp.concatenate([a[N_LAT:].reshape(BATCH, CTX_LEN, w), a[:N_LAT].reshape(BATCH, SEQ, w)], axis=1)
    return seq.transpose(1, 0, 2).reshape((CTX_LEN + SEQ) * BATCH, w)


def _from_time_major(a):
    w = a.shape[1]
    seq = a.reshape(CTX_LEN + SEQ, BATCH, w).transpose(1, 0, 2)
    return jnp.concatenate([seq[:, CTX_LEN:].reshape(N_LAT, w), seq[:, :CTX_LEN].reshape(N_CTX, w)], axis=0)


def _qkv_kernel(cqn_ref, kvn_ref, cos_ref, sin_ref, wq_ref, wk_ref, wv_ref, q_ref, k_ref, v_ref):
    cos = jnp.concatenate([cos_ref[...]] * MLA_HEADS, axis=1)
    sin = jnp.concatenate([sin_ref[...]] * MLA_HEADS, axis=1)
    w = MLA_HEADS * MLA_HEAD_PAD
    q2 = jnp.dot(cqn_ref[...], wq_ref[0], preferred_element_type=F32)
    q_ref[...] = ((q2[:, :w] * cos + q2[:, w:] * sin) * MLA_SCALE).astype(BF16)
    kv = kvn_ref[...]
    k2 = jnp.dot(kv, wk_ref[0], preferred_element_type=F32)
    k_ref[...] = (k2[:, :w] * cos + k2[:, w:] * sin).astype(BF16)
    v_ref[...] = jnp.dot(kv, wv_ref[0], preferred_element_type=F32).astype(BF16)


def _qkv(layer, cqn, kvn, cos_t, sin_t, wq, wk, wv):
    row = lambda w: pl.BlockSpec((RB, w), lambda i: (i, 0))
    full = lambda a: _layer_spec(a, layer)
    pos = pl.BlockSpec((RB, MLA_HEAD_PAD), lambda i: (jnp.where(i < NB_LAT, i % LAT_BLOCKS, LAT_BLOCKS), 0))
    w = MLA_HEADS * MLA_HEAD_PAD
    return pl.pallas_call(
        _qkv_kernel,
        grid=(NB_TOK,),
        in_specs=[row(MLA_Q_LORA), row(256), pos, pos, full(wq), full(wk), full(wv)],
        out_specs=[row(w), row(w), row(MLA_HEADS * MLA_V)],
        out_shape=[jax.ShapeDtypeStruct((N_TOK, w), BF16), jax.ShapeDtypeStruct((N_TOK, w), BF16),
                   jax.ShapeDtypeStruct((N_TOK, MLA_HEADS * MLA_V), BF16)],
        compiler_params=_params("arbitrary"),
        name="mla_qkv",
    )(cqn, kvn, cos_t, sin_t, wq, wk, wv)


def _attn_kernel(*refs, n_kv):
    q_ref = refs[0]
    k_refs = refs[1:1 + n_kv]
    v_refs = refs[1 + n_kv:1 + 2 * n_kv]
    o_ref = refs[1 + 2 * n_kv]
    nt = (((1,), (1,)), ((), ()))
    for h in range(MLA_HEADS):
        q = q_ref[:, h * MLA_HEAD_PAD:(h + 1) * MLA_HEAD_PAD]
        s = [lax.dot_general(q, k_ref[:, h * MLA_HEAD_PAD:(h + 1) * MLA_HEAD_PAD], nt,
                             preferred_element_type=F32) for k_ref in k_refs]
        m = functools.reduce(jnp.maximum, [jnp.max(si, axis=-1, keepdims=True) for si in s])
        p = [jnp.exp(si - m) for si in s]
        l = functools.reduce(jnp.add, [jnp.sum(pi, axis=-1, keepdims=True) for pi in p])
        o = functools.reduce(jnp.add, [
            jnp.dot(pi.astype(BF16), v_ref[:, h * MLA_V:(h + 1) * MLA_V], preferred_element_type=F32)
            for pi, v_ref in zip(p, v_refs)])
        o_ref[:, h * MLA_V:(h + 1) * MLA_V] = (o / l).astype(BF16)


def _attention(q, k, v, latent):
    w = MLA_HEADS * MLA_HEAD_PAD
    wv = MLA_HEADS * MLA_V
    if latent:
        tq = WB
        nq = SEQ // tq
        q_spec = pl.BlockSpec((tq, w), lambda b, j: (b * nq + j, 0))
        k_specs = [pl.BlockSpec((SEQ, w), lambda b, j: (b, 0)),
                   pl.BlockSpec((CTX_LEN, w), lambda b, j: (NB_LAT + b, 0))]
        v_specs = [pl.BlockSpec((SEQ, wv), lambda b, j: (b, 0)),
                   pl.BlockSpec((CTX_LEN, wv), lambda b, j: (NB_LAT + b, 0))]
        o_spec = pl.BlockSpec((tq, wv), lambda b, j: (b * nq + j, 0))
        n_out = N_LAT
    else:
        tq = CTX_LEN
        nq = 1
        q_spec = pl.BlockSpec((tq, w), lambda b, j: (NB_LAT + b, 0))
        k_specs = [pl.BlockSpec((CTX_LEN, w), lambda b, j: (NB_LAT + b, 0))]
        v_specs = [pl.BlockSpec((CTX_LEN, wv), lambda b, j: (NB_LAT + b, 0))]
        o_spec = pl.BlockSpec((tq, wv), lambda b, j: (b, 0))
        n_out = N_CTX
    n_kv = len(k_specs)
    return pl.pallas_call(
        functools.partial(_attn_kernel, n_kv=n_kv),
        grid=(BATCH, nq),
        in_specs=[q_spec] + k_specs + v_specs,
        out_specs=o_spec,
        out_shape=jax.ShapeDtypeStruct((n_out, wv), BF16),
        compiler_params=_params("arbitrary", "arbitrary"),
        name="mla_attention_lat" if latent else "mla_attention_ctx",
    )(q, *([k] * n_kv), *([v] * n_kv))


def _hgrn_tables():
    c = HG_CHUNK
    w = np.zeros((2, HG_SETS * c, c), np.float32)
    mask = np.zeros((2, HG_LEVELS, c, c), np.float32)
    idx = np.arange(c)
    for r in range(c):
        w[0, r, :r + 1] = 1
        w[1, r, r:] = 1
        w[0, c + r, r + 1:] = 1
        w[1, c + r, :r] = 1
    for l in range(HG_LEVELS):
        m = c >> (l + 1)
        for r in range(c):
            base = (r // (2 * m)) * 2 * m
            mid = base + m
            later = r >= mid
            row = (2 + l) * c + r
            if later:
                w[0, row, mid:r + 1] = 1
                w[1, row, mid:r] = 1
            else:
                w[0, row, r + 1:mid] = 1
                w[1, row, r:mid] = 1
        same = (idx[:, None] // (2 * m)) == (idx[None, :] // (2 * m))
        q_later = (idx[:, None] % (2 * m)) >= m
        k_later = (idx[None, :] % (2 * m)) >= m
        mask[0, l] = same & q_later & ~k_later
        mask[1, l] = same & ~q_later & k_later
    return w, mask


def _hgrn_dir(d, q_ref, k_ref, lf_ref, v_ref, w_ref, mask_ref, st_ref, o_ref):
    c = HG_CHUNK
    n_chunks = RB // c
    nt = (((1,), (1,)), ((), ()))
    tn = (((0,), (0,)), ((), ()))
    last = c - 1 if d == 0 else 0
    wsel = w_ref[d]
    e = []
    for ci in range(n_chunks):
        g = lf_ref[ci * c:(ci + 1) * c, :]
        g_hi = g.astype(BF16)
        g_lo = (g - g_hi.astype(F32)).astype(BF16)
        expo = jnp.dot(wsel, jnp.concatenate([g_hi, g_lo], axis=0), preferred_element_type=F32)
        e.append(jnp.exp(expo))

    def rows(s):
        return jnp.concatenate([e[ci][s * c:(s + 1) * c] for ci in range(n_chunks)], axis=0)

    q = q_ref[...].astype(F32)
    k = k_ref[...].astype(F32)
    q_state = (q * rows(0)).astype(BF16)
    k_state = (k * rows(1)).astype(BF16)
    q_lvl, k_lvl = [], []
    for l in range(HG_LEVELS):
        el = rows(2 + l)
        q_lvl.append((q * el).astype(BF16))
        k_lvl.append((k * el).astype(BF16))
    qk = q * k
    order = range(n_chunks) if d == 0 else range(n_chunks - 1, -1, -1)
    for h in range(HG_HEADS):
        ks = slice(h * HG_K, (h + 1) * HG_K)
        vs = slice(h * HG_V, (h + 1) * HG_V)
        v = v_ref[:, vs]
        scores = jnp.zeros((RB, RB), F32)
        for l in range(HG_LEVELS):
            p = lax.dot_general(q_lvl[l][:, ks], k_lvl[l][:, ks], nt, preferred_element_type=F32)
            scores = scores + p * mask_ref[d, l]
        o = jnp.dot(scores.astype(BF16), v, preferred_element_type=F32)
        o = o + jnp.sum(qk[:, ks], axis=-1, keepdims=True) * v.astype(F32)
        st = st_ref[d, h]
        o_state = [None] * n_chunks
        for ci in order:
            r = slice(ci * c, (ci + 1) * c)
            o_state[ci] = lax.dot_general(q_state[r, ks], st.astype(BF16), nt, preferred_element_type=F32)
            inc = lax.dot_general(v[r], k_state[r, ks], tn, preferred_element_type=F32)
            st = st * e[ci][last:last + 1, ks] + inc
        st_ref[d, h] = st
        o_ref[:, vs] = o + jnp.concatenate(o_state, axis=0)


def _hgrn_fast_tables():
    c = HG_FAST_CHUNK
    idx = np.arange(RB)
    same = (idx[:, None] // c) == (idx[None, :] // c)
    le = idx[None, :] <= idx[:, None]
    ge = idx[None, :] >= idx[:, None]
    wc = np.stack([same & le, same & ge]).astype(np.float32)
    col_chunk = np.arange((RB // c) * HG_V) // HG_V
    vmask = ((idx[:, None] // c) == col_chunk[None, :]).astype(np.float32)
    return np.concatenate([wc, wc], axis=-1), wc, vmask


def _hgrn_dir_fast(d, q_ref, k_ref, cum, tot, v_ref, maskc_ref, vmask_ref, st_ref, o_ref):
    c = HG_FAST_CHUNK
    n_chunks = RB // c
    nt = (((1,), (1,)), ((), ()))
    tn = (((0,), (0,)), ((), ()))
    q = q_ref[...].astype(F32)
    k = k_ref[...].astype(F32)
    qa = (q * jnp.exp(cum)).astype(BF16)
    kb = (k * jnp.exp(-cum)).astype(BF16)
    k_state = (k * jnp.exp(tot - cum)).astype(BF16)
    decay = jnp.exp(tot)
    keep = maskc_ref[d] > 0.5
    vmask = vmask_ref[...]
    order = range(n_chunks) if d == 0 else range(n_chunks - 1, -1, -1)
    for h in range(HG_HEADS):
        ks = slice(h * HG_K, (h + 1) * HG_K)
        vs = slice(h * HG_V, (h + 1) * HG_V)
        v = v_ref[:, vs]
        p = lax.dot_general(qa[:, ks], kb[:, ks], nt, preferred_element_type=F32)
        o = jnp.dot(jnp.where(keep, p, 0.0).astype(BF16), v, preferred_element_type=F32)
        v_by_chunk = jnp.concatenate([v] * n_chunks, axis=1) * vmask
        inc = lax.dot_general(v_by_chunk, k_state[:, ks], tn, preferred_element_type=F32)
        st = st_ref[d, h]
        entering = [None] * n_chunks
        for ci in order:
            entering[ci] = st
            st = st * decay[ci * c:ci * c + 1, ks] + inc[ci * HG_V:(ci + 1) * HG_V]
        st_ref[d, h] = st
        s_all = jnp.concatenate(entering, axis=0).astype(BF16)
        o_all = lax.dot_general(qa[:, ks], s_all, nt, preferred_element_type=F32) * vmask.astype(F32)
        o_fold = functools.reduce(jnp.add, [o_all[:, j * 128:(j + 1) * 128] for j in range(n_chunks * HG_V // 128)])
        o_state = (o_fold + pltpu.roll(o_fold, HG_V, 1))[:, :HG_V]
        o_ref[:, vs] = o + o_state


def _hgrn_kernel(qf_ref, kf_ref, lff_ref, vf_ref, qb_ref, kb_ref, lfb_ref, vb_ref, w_ref, mask_ref,
                 wc_ref, maskc_ref, vmask_ref, of_ref, ob_ref, st_ref):
    j = pl.program_id(1)

    @pl.when(j == 0)
    def _():
        st_ref[...] = jnp.zeros_like(st_ref)

    c = HG_FAST_CHUNK
    dirs = ((0, qf_ref, kf_ref, lff_ref, vf_ref, of_ref), (1, qb_ref, kb_ref, lfb_ref, vb_ref, ob_ref))
    for d, q_ref, k_ref, lf_ref, v_ref, o_ref in dirs:
        g_hi, g_lo = _split_bf16(lf_ref[...])
        cum = jnp.dot(wc_ref[d], jnp.concatenate([g_hi, g_lo], axis=0), preferred_element_type=F32)
        edge = c - 1 if d == 0 else 0
        totals = [cum[ci * c + edge:ci * c + edge + 1] for ci in range(RB // c)]
        tot = jnp.concatenate([jnp.broadcast_to(t, (c, t.shape[1])) for t in totals], axis=0)
        safe = jnp.min(jnp.concatenate(totals, axis=0)) >= HG_FAST_MIN

        @pl.when(safe)
        def _(d=d, q_ref=q_ref, k_ref=k_ref, v_ref=v_ref, o_ref=o_ref, cum=cum, tot=tot):
            _hgrn_dir_fast(d, q_ref, k_ref, cum, tot, v_ref, maskc_ref, vmask_ref, st_ref, o_ref)

        @pl.when(jnp.logical_not(safe))
        def _(d=d, q_ref=q_ref, k_ref=k_ref, lf_ref=lf_ref, v_ref=v_ref, o_ref=o_ref):
            _hgrn_dir(d, q_ref, k_ref, lf_ref, v_ref, w_ref, mask_ref, st_ref, o_ref)


def _hgrn(hq, kkf, kkb, lff, lfb, hi, w_tab, mask_tab, wc_tab, maskc_tab, vmask_tab):
    full = _resident
    fwd = lambda w: pl.BlockSpec((RB, w), lambda b, j: (_seq_block(b, j), 0))
    bwd = lambda w: pl.BlockSpec((RB, w), lambda b, j: (_seq_block_rev(b, j), 0))
    wk = HG_HEADS * HG_K
    wv = HG_HEADS * HG_V
    return pl.pallas_call(
        _hgrn_kernel,
        grid=(BATCH, SEQ_BLOCKS),
        in_specs=[fwd(wk), fwd(wk), fwd(wk), fwd(wv), bwd(wk), bwd(wk), bwd(wk), bwd(wv),
                  full(w_tab), full(mask_tab), full(wc_tab), full(maskc_tab), full(vmask_tab)],
        out_specs=[fwd(wv), bwd(wv)],
        out_shape=[jax.ShapeDtypeStruct((N_TOK, wv), F32)] * 2,
        scratch_shapes=[pltpu.VMEM((2, HG_HEADS, HG_V, HG_K), F32)],
        compiler_params=_params("arbitrary", "arbitrary"),
        name="hgrn2_scan",
    )(hq, kkf, lff, hi, hq, kkb, lfb, hi, w_tab, mask_tab, wc_tab, maskc_tab, vmask_tab)


def _gelu_tanh(x):
    return 0.5 * x * (1.0 + jnp.tanh(math.sqrt(2.0 / math.pi) * (x + 0.044715 * (x * x * x))))


def _split_bf16(a):
    hi = a.astype(BF16)
    return hi, (a - hi.astype(F32)).astype(BF16)


def _merge_kernel(xl_ref, xc_ref, mod_ref, yf_ref, yb_ref, u_ref, of_ref, ob_ref, hgs_ref, attl_ref, attc_ref,
                  gates_ref, s5d_ref, gluw_ref, glub_ref, hgg_ref, hsum_ref, wpa_ref, wpb_ref, wpc_ref, wout_ref,
                  g2_ref, wrh_ref, wrl_ref, br_ref,
                  x1_ref, h2_ref, eid_ref, gate_ref):
    y = yf_ref[...] + yb_ref[...] + s5d_ref[0] * u_ref[...].astype(F32)
    y = _gelu_tanh(y)
    y = y * jax.nn.sigmoid(jnp.dot(y.astype(BF16), gluw_ref[0], preferred_element_type=F32) + glub_ref[0])
    o = of_ref[...] + ob_ref[...]
    sq_hi, sq_lo = _split_bf16(o * o)
    ms = (jnp.dot(sq_hi, hsum_ref[...], preferred_element_type=F32)
          + jnp.dot(sq_lo, hsum_ref[...], preferred_element_type=F32)) * (1.0 / HG_V)
    o = o * lax.rsqrt(ms + NORM_EPS) * hgg_ref[0] * hgs_ref[...].astype(F32)

    d = D_MODEL
    att = _pick_rows(attl_ref, attc_ref, WB)
    merged = (gates_ref[:, 0:d].astype(F32) * jnp.dot(y.astype(BF16), wpa_ref[0], preferred_element_type=F32)
              + gates_ref[:, d:2 * d].astype(F32) * jnp.dot(att, wpb_ref[0], preferred_element_type=F32)
              + gates_ref[:, 2 * d:3 * d].astype(F32) * jnp.dot(o.astype(BF16), wpc_ref[0], preferred_element_type=F32))
    y_out = jnp.dot(merged.astype(BF16), wout_ref[0], preferred_element_type=F32)
    x1 = _pick_rows(xl_ref, xc_ref, WB) + mod_ref[0, 2:3, :] * y_out
    x1_ref[...] = x1
    h2 = _rms(x1) * g2_ref[0] * (1.0 + mod_ref[0, 4:5, :]) + mod_ref[0, 3:4, :]
    h2_ref[...] = h2

    h_hi, h_lo = _split_bf16(h2)
    logits = (jnp.dot(h_hi, wrh_ref[0], preferred_element_type=F32)
              + jnp.dot(h_lo, wrh_ref[0], preferred_element_type=F32)
              + jnp.dot(h_hi, wrl_ref[0], preferred_element_type=F32)) + br_ref[0]
    lane = lax.broadcasted_iota(jnp.int32, logits.shape, 1).astype(F32)
    neg = -jnp.inf
    glog = jnp.where(lane < MOE_GROUPS, logits, neg)
    gmax = jnp.max(glog, axis=-1, keepdims=True)
    gidx = jnp.min(jnp.where(glog == gmax, lane, 1e9), axis=-1, keepdims=True)
    g_w = 1.0 / jnp.sum(jnp.exp(glog - gmax), axis=-1, keepdims=True)
    e_lo = 32.0 + gidx * MOE_EXPERTS_PER_GROUP
    elog = jnp.where((lane >= e_lo) & (lane < e_lo + MOE_EXPERTS_PER_GROUP), logits, neg)
    v1 = jnp.max(elog, axis=-1, keepdims=True)
    i1 = jnp.min(jnp.where(elog == v1, lane, 1e9), axis=-1, keepdims=True)
    elog2 = jnp.where(lane == i1, neg, elog)
    v2 = jnp.max(elog2, axis=-1, keepdims=True)
    i2 = jnp.min(jnp.where(elog2 == v2, lane, 1e9), axis=-1, keepdims=True)
    e2 = jnp.exp(v2 - v1)
    gate1 = g_w / (1.0 + e2)
    gate2 = g_w * e2 / (1.0 + e2)
    eid_ref[...] = jnp.where(lane == 0.0, i1 - 32.0, jnp.where(lane == 1.0, i2 - 32.0, 0.0)).astype(jnp.int32)
    gate_ref[...] = jnp.where(lane == 0.0, gate1, jnp.where(lane == 1.0, gate2, 0.0))


def _merge(layer, n_blocks, x_lat, x_ctx, ctx_row0, mod3, yf, yb, s5u, of, ob, hgs, att_lat, att_ctx, gates,
           head_sum, consts):
    row = lambda w: pl.BlockSpec((WB, w), lambda i: (i, 0))
    n = n_blocks * RB
    wv = MLA_HEADS * MLA_V
    if att_ctx is None:
        att_ctx = att_lat
    const_specs = [_layer_spec(a, layer) for a in consts]
    const_specs.insert(4, _resident(head_sum))
    return pl.pallas_call(
        _merge_kernel,
        grid=(n // WB,),
        in_specs=[*_split_row_specs(WB, D_MODEL, ctx_row0),
                  pl.BlockSpec((1, 6, D_MODEL), lambda i: (_mod_row(i, WB), 0, 0)),
                  row(S5_WIDTH), row(S5_WIDTH), row(S5_WIDTH), row(HG_HEADS * HG_V), row(HG_HEADS * HG_V),
                  row(HG_HEADS * HG_V), *_split_row_specs(WB, wv, 0), row(3 * D_MODEL)] + const_specs,
        out_specs=[row(D_MODEL), row(D_MODEL), row(128), row(128)],
        out_shape=[jax.ShapeDtypeStruct((n, D_MODEL), F32), jax.ShapeDtypeStruct((n, D_MODEL), F32),
                   jax.ShapeDtypeStruct((n, 128), jnp.int32), jax.ShapeDtypeStruct((n, 128), F32)],
        compiler_params=_params("arbitrary"),
        name="merge_router",
    )(x_lat, x_ctx, mod3, yf, yb, s5u, of, ob, hgs, att_lat, att_ctx, gates,
      *consts[:4], head_sum, *consts[4:])


def _moe_kernel(bexp_ref, nblk_ref, src_ref, srcn_ref, dst_ref, h_hbm, w1_ref, w3_ref, w2_ref, y_hbm,
                xbuf, ybuf, w1b, w3b, w2b, gsem, ssem):
    i = pl.program_id(0)
    n_used = nblk_ref[0]
    slot = lax.rem(i, 2)

    def gather_start(idx_ref, s):
        for r in range(MOE_ROWS):
            pltpu.make_async_copy(h_hbm.at[pl.ds(idx_ref[0, 0, r], 1), :], xbuf.at[s, pl.ds(r, 1), :],
                                  gsem.at[s]).start()

    def gather_wait(s):
        pltpu.make_async_copy(h_hbm.at[pl.ds(0, MOE_ROWS), :], xbuf.at[s], gsem.at[s]).wait()

    def scatter_start(s):
        for r in range(MOE_ROWS):
            pltpu.make_async_copy(ybuf.at[s, pl.ds(r, 1), :], y_hbm.at[pl.ds(dst_ref[0, 0, r], 1), :],
                                  ssem.at[s]).start()

    def scatter_wait(s):
        pltpu.make_async_copy(ybuf.at[s], y_hbm.at[pl.ds(0, MOE_ROWS), :], ssem.at[s]).wait()

    @pl.when(i < n_used)
    def _():
        @pl.when(i == 0)
        def _():
            gather_start(src_ref, 0)
            ybuf[1] = jnp.zeros((MOE_ROWS, D_MODEL), F32)
            n_real = y_hbm.shape[0] - MOE_ROWS
            fill = pltpu.make_async_copy(ybuf.at[1], y_hbm.at[pl.ds(n_real, MOE_ROWS), :], ssem.at[1])
            fill.start()
            fill.wait()

        @pl.when(i + 1 < n_used)
        def _():
            gather_start(srcn_ref, 1 - slot)

        @pl.when((i == 0) | (bexp_ref[i] != bexp_ref[jnp.maximum(i - 1, 0)]))
        def _():
            w1b[...] = w1_ref[0, 0].astype(BF16)
            w3b[...] = w3_ref[0, 0].astype(BF16)
            w2b[...] = w2_ref[0, 0].astype(BF16)

        gather_wait(slot)

        @pl.when(i >= 2)
        def _():
            scatter_wait(slot)

        x = xbuf[slot].astype(BF16)
        a = jnp.dot(x, w1b[...], preferred_element_type=F32)
        g = jnp.dot(x, w3b[...], preferred_element_type=F32)
        hid = (a * jax.nn.sigmoid(a) * g).astype(BF16)
        ybuf[slot] = jnp.dot(hid, w2b[...], preferred_element_type=F32)
        scatter_start(slot)

        @pl.when(i == n_used - 1)
        def _():
            @pl.when(i >= 1)
            def _():
                scatter_wait(1 - slot)
            scatter_wait(slot)


def _moe(h2, eid, w1, w3, w2, layer):
    n = h2.shape[0]
    n_assign = n * MOE_TOP_K
    n_blocks = (n_assign + MOE_EXPERTS * (MOE_ROWS - 1) + MOE_ROWS - 1) // MOE_ROWS
    flat_e = eid.reshape(n_assign)
    order = jnp.argsort(flat_e, stable=True).astype(jnp.int32)
    experts = jnp.arange(MOE_EXPERTS, dtype=jnp.int32)
    counts = jnp.sum((flat_e[:, None] == experts[None, :]).astype(jnp.int32), axis=0)
    starts = jnp.cumsum(counts) - counts
    padded = (counts + MOE_ROWS - 1) // MOE_ROWS * MOE_ROWS
    p_ends = jnp.cumsum(padded)
    p_starts = p_ends - padded
    n_used_s = p_ends[-1] // MOE_ROWS
    n_used = n_used_s.astype(jnp.int32).reshape(1)
    blk = jnp.arange(n_blocks, dtype=jnp.int32)
    blk_expert = jnp.sum((p_ends[None, :] <= (blk * MOE_ROWS)[:, None]).astype(jnp.int32), axis=1)
    last_expert = jnp.max(jnp.where(counts > 0, experts, 0))
    block_expert = jnp.where(blk < n_used_s, jnp.minimum(blk_expert, MOE_EXPERTS - 1), last_expert)
    onehot = (block_expert[:, None] == experts[None, :]).astype(jnp.int32)
    pick = lambda table: jnp.sum(onehot * table[None, :], axis=1)
    local = jnp.arange(MOE_ROWS, dtype=jnp.int32)[None, :]
    rank = blk[:, None] * MOE_ROWS + local - pick(p_starts)[:, None]
    valid = (rank < pick(counts)[:, None]) & (blk < n_used_s)[:, None]
    pos = jnp.clip(pick(starts)[:, None] + rank, 0, n_assign - 1)
    assign = order[pos]
    tok = jnp.where(valid, assign // MOE_TOP_K, 0)
    dst = jnp.where(valid, (assign % MOE_TOP_K) * n + assign // MOE_TOP_K, n_assign + local)
    rows_now = pl.BlockSpec((1, 1, MOE_ROWS), lambda i, be, nb: (i, 0, 0), memory_space=pltpu.SMEM)
    rows_next = pl.BlockSpec((1, 1, MOE_ROWS), lambda i, be, nb: (jnp.minimum(i + 1, n_blocks - 1), 0, 0),
                             memory_space=pltpu.SMEM)
    src3 = tok.reshape(n_blocks, 1, MOE_ROWS)
    return pl.pallas_call(
        _moe_kernel,
        grid_spec=pltpu.PrefetchScalarGridSpec(
            num_scalar_prefetch=2,
            grid=(n_blocks,),
            in_specs=[
                rows_now, rows_next, rows_now,
                pl.BlockSpec(memory_space=pl.ANY),
                pl.BlockSpec((1, 1, D_MODEL, MOE_HIDDEN), lambda i, be, nb: (layer, be[i], 0, 0)),
                pl.BlockSpec((1, 1, D_MODEL, MOE_HIDDEN), lambda i, be, nb: (layer, be[i], 0, 0)),
                pl.BlockSpec((1, 1, MOE_HIDDEN, D_MODEL), lambda i, be, nb: (layer, be[i], 0, 0)),
            ],
            out_specs=pl.BlockSpec(memory_space=pl.ANY),
            scratch_shapes=[pltpu.VMEM((2, MOE_ROWS, D_MODEL), F32), pltpu.VMEM((2, MOE_ROWS, D_MODEL), F32),
                            pltpu.VMEM((D_MODEL, MOE_HIDDEN), BF16), pltpu.VMEM((D_MODEL, MOE_HIDDEN), BF16),
                            pltpu.VMEM((MOE_HIDDEN, D_MODEL), BF16),
                            pltpu.SemaphoreType.DMA((2,)), pltpu.SemaphoreType.DMA((2,))],
        ),
        out_shape=jax.ShapeDtypeStruct((n_assign + MOE_ROWS, D_MODEL), F32),
        compiler_params=_params("arbitrary"),
        name="moe_experts",
    )(block_expert, n_used, src3, src3, dst.reshape(n_blocks, 1, MOE_ROWS), h2, w1, w3, w2)


def _combine_kernel(x1_ref, y0_ref, y1_ref, gate_ref, mod_ref, g_ref, o_ref, *, final):
    f = gate_ref[:, 0:1] * y0_ref[...] + gate_ref[:, 1:2] * y1_ref[...]
    x2 = x1_ref[...] + mod_ref[0, 5:6, :] * f
    if final:
        x2 = _rms(x2) * g_ref[...]
    o_ref[...] = x2


def _combine(n_blocks, x1, y, gate, mod3, g_final, final):
    row = lambda w: pl.BlockSpec((RB, w), lambda i: (i, 0))
    n = n_blocks * RB
    return pl.pallas_call(
        functools.partial(_combine_kernel, final=final),
        grid=(n_blocks,),
        in_specs=[row(D_MODEL), row(D_MODEL), pl.BlockSpec((RB, D_MODEL), lambda i: (i + n_blocks, 0)), row(128),
                  pl.BlockSpec((1, 6, D_MODEL), lambda i: (_mod_row(i), 0, 0)),
                  pl.BlockSpec((1, D_MODEL), lambda i: (0, 0))],
        out_specs=row(D_MODEL),
        out_shape=jax.ShapeDtypeStruct((n, D_MODEL), F32),
        compiler_params=_params("arbitrary"),
        name="moe_combine_final" if final else "moe_combine",
    )(x1, y, y, gate, mod3, g_final)


def _rope_rot_cols(w):
    p = ROPE_PAIRS
    return jnp.concatenate([-w[..., p:2 * p], w[..., 0:p], -w[..., 3 * p:4 * p], w[..., 2 * p:3 * p]], axis=-1)


def _pack_w_in(w_in):
    cuts = np.cumsum([S5_WIDTH, MLA_Q_LORA, MLA_KV_LORA, MLA_ROPE, HG_HEADS * HG_K, HG_HEADS * HG_K,
                      HG_HEADS * HG_K, HG_HEADS * HG_V, HG_HEADS * HG_V, D_MODEL, D_MODEL])
    (w_s5, w_cq, w_ckv, w_kpe, w_hq, w_zf, w_zb, w_hi, w_hg, w_ga, w_gb, w_gc) = jnp.split(w_in, cuts, axis=-1)
    pad = jnp.zeros(w_in.shape[:-1] + (256 - MLA_KV_LORA - 2 * MLA_ROPE,), w_in.dtype)
    packed = jnp.concatenate([w_s5, w_cq, w_ckv, w_kpe, _rope_rot_cols(w_kpe), pad,
                              w_hq, w_zf, w_zb, w_hi, w_hg, w_ga, w_gb, w_gc], axis=-1)
    assert packed.shape[-1] == W_IN_COLS
    return packed.astype(BF16)


def _pack_mla(w_uq, w_uk, w_uv):
    hp, n, r = MLA_HEAD_PAD, MLA_NOPE, MLA_ROPE
    nl = w_uq.shape[0]
    wq = w_uq.reshape(nl, MLA_Q_LORA, MLA_HEADS, n + r)
    zq = jnp.zeros((nl, MLA_Q_LORA, MLA_HEADS, hp - n - r), F32)
    q1 = jnp.concatenate([wq, zq], axis=-1).reshape(nl, MLA_Q_LORA, MLA_HEADS * hp)
    q2 = jnp.concatenate([jnp.zeros((nl, MLA_Q_LORA, MLA_HEADS, n), F32), _rope_rot_cols(wq[..., n:]), zq],
                         axis=-1).reshape(nl, MLA_Q_LORA, MLA_HEADS * hp)
    wq_packed = jnp.concatenate([q1, q2], axis=-1).astype(BF16)

    wk = w_uk.reshape(nl, MLA_KV_LORA, MLA_HEADS, n)
    k_nope = jnp.concatenate([wk, jnp.zeros((nl, MLA_KV_LORA, MLA_HEADS, hp - n), F32)], axis=-1)
    pe_slot = np.concatenate([np.zeros((r, n), np.float32), np.eye(r, dtype=np.float32),
                              np.zeros((r, hp - n - r), np.float32)], axis=-1)
    pe_all = jnp.asarray(np.tile(pe_slot[None, :, None, :], (nl, 1, MLA_HEADS, 1)))
    zero_pe = jnp.zeros_like(pe_all)
    tail = jnp.zeros((nl, 256 - MLA_KV_LORA - 2 * r, MLA_HEADS, hp), F32)
    k1 = jnp.concatenate([k_nope, pe_all, zero_pe, tail], axis=1).reshape(nl, 256, MLA_HEADS * hp)
    k2 = jnp.concatenate([jnp.zeros_like(k_nope), zero_pe, pe_all, tail], axis=1).reshape(nl, 256, MLA_HEADS * hp)
    wk_packed = jnp.concatenate([k1, k2], axis=-1).astype(BF16)
    wv_packed = jnp.concatenate([w_uv, jnp.zeros((nl, 256 - MLA_KV_LORA, MLA_HEADS * MLA_V), F32)],
                                axis=1).astype(BF16)
    return wq_packed, wk_packed, wv_packed


def _rope_tables():
    rows = SEQ // GRID_W
    row = np.repeat(np.arange(rows, dtype=np.float32), GRID_W)
    col = np.tile(np.arange(GRID_W, dtype=np.float32), rows)
    inv = (np.float32(ROPE_BASE) ** (-np.arange(ROPE_PAIRS, dtype=np.float32) / np.float32(ROPE_PAIRS))).astype(np.float32)
    ar, ac = row[:, None] * inv, col[:, None] * inv
    cos = np.concatenate([np.cos(ar), np.cos(ar), np.cos(ac), np.cos(ac)], axis=1)
    sin = np.concatenate([np.sin(ar), np.sin(ar), np.sin(ac), np.sin(ac)], axis=1)
    tail = MLA_HEAD_PAD - MLA_NOPE - MLA_ROPE
    cos_t = np.concatenate([np.ones((SEQ, MLA_NOPE)), cos, np.ones((SEQ, tail))], axis=1)
    sin_t = np.concatenate([np.zeros((SEQ, MLA_NOPE)), sin, np.zeros((SEQ, tail))], axis=1)
    cos_t = np.concatenate([cos_t, np.ones((CTX_LEN, MLA_HEAD_PAD))], axis=0)
    sin_t = np.concatenate([sin_t, np.zeros((CTX_LEN, MLA_HEAD_PAD))], axis=0)
    return jnp.asarray(cos_t, F32), jnp.asarray(sin_t, F32)


def _pack_s5(lam_re, lam_im, b_re, b_im, c_re, c_im, log_step):
    eye = jnp.eye(S5_GROUPS, dtype=F32)
    dt = jnp.exp(log_step)[..., None]
    mag = jnp.exp(lam_re * dt)
    lb_re, lb_im = mag * jnp.cos(lam_im * dt), mag * jnp.sin(lam_im * dt)
    den = lam_re * lam_re + lam_im * lam_im
    fr = ((lb_re - 1) * lam_re + lb_im * lam_im) / den
    fi = (lb_im * lam_re - (lb_re - 1) * lam_im) / den
    bb_re = fr[..., None] * b_re - fi[..., None] * b_im
    bb_im = fr[..., None] * b_im + fi[..., None] * b_re
    lead = lam_re.shape[:2]

    def in_mat(bb):
        return jnp.einsum('ldgph,gk->ldghkp', bb, eye).reshape(lead + (S5_WIDTH, S5_LANES))

    def out_mat(cc):
        return jnp.einsum('ldghp,gk->ldgpkh', cc, eye).reshape(lead + (S5_LANES, S5_WIDTH))

    bmat = jnp.concatenate([in_mat(bb_re), in_mat(bb_im)], axis=-1).astype(BF16)
    cmat = jnp.concatenate([out_mat(c_re), -out_mat(c_im)], axis=-2).astype(BF16)
    a_tab = jnp.stack([jnp.broadcast_to(lb_re.reshape(lead + (1, S5_LANES)), lead + (BATCH, S5_LANES)),
                       jnp.broadcast_to(lb_im.reshape(lead + (1, S5_LANES)), lead + (BATCH, S5_LANES))], axis=2)
    return bmat, cmat, a_tab


def kernel(x, c, ctx, c_ctx, mod_w, mod_b, norm1_g, norm2_g, w_in, s5_lam_re, s5_lam_im, s5_b_re, s5_b_im, s5_c_re, s5_c_im, s5_log_step, s5_d, s5_glu_w, s5_glu_b, mla_qa_g, mla_kva_g, mla_w_uq, mla_w_uk, mla_w_uv, hg_lb_logits, hg_norm_g, w_pa, w_pb, w_pc, w_out, moe_w_group, moe_b_group, moe_w_expert, moe_b_expert, moe_w1, moe_w3, moe_w2, final_norm_g):
    x_lat, x_ctx, ctx_row0 = x.reshape(N_LAT, D_MODEL), ctx.reshape(N_CTX, D_MODEL), 0
    c_rows = jnp.concatenate([c, c_ctx[None, :], jnp.zeros((16 - BATCH - 1, D_MODEL), F32)], axis=0)
    mod = _modulation(c_rows, mod_w, mod_b).reshape(DEPTH, 16, 6, D_MODEL)

    lb_all = jnp.cumsum(jax.nn.softmax(hg_lb_logits.astype(F32)))
    lb_all = lb_all - lb_all[0]
    hg_scal = jnp.stack([jnp.log(lb_all), jnp.log1p(-lb_all), 1.0 - lb_all], axis=1).reshape(3 * DEPTH).astype(F32)
    cos_t, sin_t = _rope_tables()

    vec = lambda a: a[:, None, :]
    w_in_p = _pack_w_in(w_in)
    bmat, cmat, a_tab = _pack_s5(s5_lam_re, s5_lam_im, s5_b_re, s5_b_im, s5_c_re, s5_c_im, s5_log_step)
    wq, wk, wv = _pack_mla(mla_w_uq, mla_w_uk, mla_w_uv)
    w_route = jnp.concatenate([moe_w_group, jnp.zeros((DEPTH, D_MODEL, 32 - MOE_GROUPS), F32), moe_w_expert,
                               jnp.zeros((DEPTH, D_MODEL, 128 - 32 - MOE_EXPERTS), F32)], axis=-1)
    b_route = jnp.concatenate([moe_b_group, jnp.zeros((DEPTH, 32 - MOE_GROUPS), F32), moe_b_expert,
                               jnp.zeros((DEPTH, 128 - 32 - MOE_EXPERTS), F32)], axis=-1)
    w_route_hi = w_route.astype(BF16)
    w_route_lo = (w_route - w_route_hi.astype(F32)).astype(BF16)
    merge_consts = [vec(s5_d), s5_glu_w.astype(BF16), vec(s5_glu_b), vec(hg_norm_g),
                    w_pa.astype(BF16), w_pb.astype(BF16), w_pc.astype(BF16), w_out.astype(BF16), vec(norm2_g),
                    w_route_hi, w_route_lo, vec(b_route)]
    w_tab_np, mask_tab_np = _hgrn_tables()
    w_tab = jnp.asarray(np.concatenate([w_tab_np, w_tab_np], axis=-1), BF16)
    mask_tab = jnp.asarray(np.kron(np.eye(RB // HG_CHUNK, dtype=np.float32), mask_tab_np), F32)
    head_sum = jnp.asarray(np.kron(np.eye(HG_HEADS), np.ones((HG_V, HG_V))), F32)
    wc_np, maskc_np, vmask_np = _hgrn_fast_tables()
    wc_tab, maskc_tab, vmask_tab = jnp.asarray(wc_np, BF16), jnp.asarray(maskc_np, F32), jnp.asarray(vmask_np, BF16)

    out = None
    for layer in range(DEPTH):
        last = layer == DEPTH - 1
        mod3 = mod[layer]
        (s5u, cqn, kvn, hq, lff, lfb, kkf, kkb, hi, hgs, gates) = _input_projection(
            layer, x_lat, x_ctx, ctx_row0, mod3, hg_scal, vec(norm1_g), w_in_p, vec(mla_qa_g), vec(mla_kva_g))

        yf, yb = [_from_time_major(y) for y in _s5_states(layer, _to_time_major(s5u), bmat, cmat, a_tab)]

        q, k, v = _qkv(layer, cqn, kvn, cos_t, sin_t, wq, wk, wv)
        att_lat = _attention(q, k, v, latent=True)
        att_ctx = None if last else _attention(q, k, v, latent=False)

        of, ob = _hgrn(hq, kkf, kkb, lff, lfb, hi, w_tab, mask_tab, wc_tab, maskc_tab, vmask_tab)

        n_blocks = NB_LAT if last else NB_TOK
        x1, h2, eid, gate = _merge(layer, n_blocks, x_lat, x_ctx, ctx_row0, mod3, yf, yb, s5u, of, ob, hgs,
                                   att_lat, att_ctx, gates, head_sum, merge_consts)

        y_moe = _moe(h2, eid[:, :MOE_TOP_K], moe_w1, moe_w3, moe_w2, layer)
        res = _combine(n_blocks, x1, y_moe, gate, mod3, final_norm_g[None, :], final=last)
        if last:
            out = res
        else:
            x_lat, x_ctx, ctx_row0 = res, res, N_LAT
    return out.reshape(BATCH, SEQ, D_MODEL)
```

```python
import functools
import math

import jax
import jax.numpy as jnp
from jax import lax
import numpy as np
from jax.experimental import pallas as pl
from jax.experimental.pallas import tpu as pltpu

F32 = jnp.float32
BF16 = jnp.bfloat16
HIGHEST = lax.Precision.HIGHEST

D_MODEL = 1024
BATCH = 8
SEQ = 2048
DEPTH = 2
GRID_W = 64
CTX_LEN = 256
NORM_EPS = 1e-6

S5_WIDTH = D_MODEL // 4
S5_GROUP_CH = 16
S5_GROUPS = S5_WIDTH // S5_GROUP_CH
S5_STATE = 64
S5_LANES = S5_GROUPS * S5_STATE

MLA_HEADS = D_MODEL // 128
MLA_NOPE = 64
MLA_ROPE = 32
MLA_V = 64
MLA_Q_LORA = D_MODEL // 4
MLA_KV_LORA = D_MODEL // 8
MLA_SCALE = 1.0 / math.sqrt(MLA_NOPE + MLA_ROPE)
MLA_HEAD_PAD = 128
ROPE_PAIRS = MLA_ROPE // 4
ROPE_BASE = 10000.0

HG_HEADS = D_MODEL // 256
HG_K = 128
HG_V = 64
HG_CHUNK = 64
HG_LEVELS = 6
HG_SETS = 2 + HG_LEVELS
HG_FAST_CHUNK = 32
HG_FAST_MIN = -60.0

MOE_GROUPS = 4
MOE_EXPERTS_PER_GROUP = 8
MOE_EXPERTS = MOE_GROUPS * MOE_EXPERTS_PER_GROUP
MOE_TOP_K = 2
MOE_HIDDEN = D_MODEL // 2
MOE_ROWS = 256

N_LAT = BATCH * SEQ
N_CTX = BATCH * CTX_LEN
N_TOK = N_LAT + N_CTX
RB = CTX_LEN
WB = 2 * RB
NB_LAT = N_LAT // RB
NB_CTX = N_CTX // RB
NB_TOK = N_TOK // RB
LAT_BLOCKS = SEQ // RB
SEQ_BLOCKS = LAT_BLOCKS + 1

C_S5 = 0
C_CQ = 256
C_KV = 512
C_HQ = 768
C_ZF = 1280
C_ZB = 1792
C_HI = 2304
C_HG = 2560
C_GATE = 2816
W_IN_COLS = C_GATE + 3 * D_MODEL

VMEM_LIMIT = 56 * 1024 * 1024


def _params(*sem):
    return pltpu.CompilerParams(dimension_semantics=sem, vmem_limit_bytes=VMEM_LIMIT)


def _rms(v):
    return v * lax.rsqrt(jnp.mean(v * v, axis=-1, keepdims=True) + NORM_EPS)


def _mod_row(i, rb=RB):
    return jnp.where(i < N_LAT // rb, i // (SEQ // rb), BATCH)


def _resident(a):
    return pl.BlockSpec(a.shape, lambda *_: (0,) * a.ndim, pipeline_mode=pl.Buffered(1))


def _layer_spec(a, layer):
    return pl.BlockSpec((1,) + a.shape[1:], lambda *_: (layer,) + (0,) * (a.ndim - 1),
                        pipeline_mode=pl.Buffered(1))


def _split_row_specs(rb, w, ctx_row0):
    n_lat = N_LAT // rb
    lat = pl.BlockSpec((rb, w), lambda i: (jnp.minimum(i, n_lat - 1), 0))
    ctx = pl.BlockSpec((rb, w), lambda i: (ctx_row0 // rb + jnp.maximum(i - n_lat, 0), 0))
    return lat, ctx


def _pick_rows(lat_ref, ctx_ref, rb):
    return jnp.where(pl.program_id(0) < N_LAT // rb, lat_ref[...], ctx_ref[...])


def _mod_kernel(c_ref, w_ref, b_ref, o_ref):
    c = c_ref[...]
    s = c * jax.nn.sigmoid(c)
    o_ref[0] = jnp.dot(s, w_ref[0], precision=HIGHEST, preferred_element_type=F32) + b_ref[0]


def _modulation(c_rows, mod_w, mod_b):
    bn = 1536
    return pl.pallas_call(
        _mod_kernel,
        grid=(DEPTH, 6 * D_MODEL // bn),
        in_specs=[
            pl.BlockSpec((16, D_MODEL), lambda l, j: (0, 0)),
            pl.BlockSpec((1, D_MODEL, bn), lambda l, j: (l, 0, j)),
            pl.BlockSpec((1, 1, bn), lambda l, j: (l, 0, j)),
        ],
        out_specs=pl.BlockSpec((1, 16, bn), lambda l, j: (l, 0, j)),
        out_shape=jax.ShapeDtypeStruct((DEPTH, 16, 6 * D_MODEL), F32),
        compiler_params=_params("arbitrary", "arbitrary"),
        name="adaln_mod",
    )(c_rows, mod_w, mod_b.reshape(DEPTH, 1, 6 * D_MODEL))


def _log_sigmoid(z):
    return jnp.minimum(z, 0.0) - jnp.log1p(jnp.exp(-jnp.abs(z)))


def _in_kernel(hs_ref, xl_ref, xc_ref, mod_ref, g1_ref, w_ref, qag_ref, kvg_ref,
               s5u_ref, cqn_ref, kvn_ref, hq_ref, lff_ref, lfb_ref, kkf_ref, kkb_ref,
               hi_ref, hgs_ref, gates_ref, h_ref, *, layer):
    x = _pick_rows(xl_ref, xc_ref, WB)
    xn = _rms(x) * g1_ref[0]
    h_ref[...] = (xn * (1.0 + mod_ref[0, 1:2, :]) + mod_ref[0, 0:1, :]).astype(BF16)

    def mm(a, b):
        return jnp.dot(h_ref[...], w_ref[0, :, a:b], preferred_element_type=F32)

    s5u_ref[...] = mm(C_S5, C_CQ).astype(BF16)
    cqn_ref[...] = (_rms(mm(C_CQ, C_KV)) * qag_ref[0]).astype(BF16)
    kv = mm(C_KV, C_HQ)
    ckvn = _rms(kv[:, :MLA_KV_LORA]) * kvg_ref[0]
    kvn_ref[...] = jnp.concatenate([ckvn, kv[:, MLA_KV_LORA:]], axis=1).astype(BF16)
    hq_ref[...] = mm(C_HQ, C_ZF).astype(BF16)

    log_lb = hs_ref[3 * layer]
    log_1m_lb = hs_ref[3 * layer + 1]
    one_m_lb = hs_ref[3 * layer + 2]
    for c0, lf_ref, kk_ref in ((C_ZF, lff_ref, kkf_ref), (C_ZB, lfb_ref, kkb_ref)):
        z = mm(c0, c0 + HG_HEADS * HG_K)
        b = log_1m_lb + _log_sigmoid(z)
        m = jnp.maximum(b, log_lb)
        lf_ref[...] = m + jnp.log1p(jnp.exp(-jnp.abs(b - log_lb)))
        kk_ref[...] = (one_m_lb * jax.nn.sigmoid(-z)).astype(BF16)

    hi_ref[...] = mm(C_HI, C_HG).astype(BF16)
    g = mm(C_HG, C_GATE)
    hgs_ref[...] = (g * jax.nn.sigmoid(g)).astype(BF16)
    gw = 512
    for j in range(3 * D_MODEL // gw):
        c0 = C_GATE + j * gw
        gates_ref[:, j * gw:(j + 1) * gw] = jax.nn.sigmoid(mm(c0, c0 + gw)).astype(BF16)


def _input_projection(layer, x_lat, x_ctx, ctx_row0, mod3, hg_scal, g1, w_in, qa_g, kva_g):
    row = lambda w: pl.BlockSpec((WB, w), lambda i: (i, 0))
    x_specs = _split_row_specs(WB, D_MODEL, ctx_row0)
    shapes = [
        (S5_WIDTH, BF16), (MLA_Q_LORA, BF16), (256, BF16), (HG_HEADS * HG_K, BF16),
        (HG_HEADS * HG_K, F32), (HG_HEADS * HG_K, F32), (HG_HEADS * HG_K, BF16), (HG_HEADS * HG_K, BF16),
        (HG_HEADS * HG_V, BF16), (HG_HEADS * HG_V, BF16), (3 * D_MODEL, BF16),
    ]
    return pl.pallas_call(
        functools.partial(_in_kernel, layer=layer),
        grid=(N_TOK // WB,),
        in_specs=[
            pl.BlockSpec(memory_space=pltpu.SMEM),
            *x_specs,
            pl.BlockSpec((1, 6, D_MODEL), lambda i: (_mod_row(i, WB), 0, 0)),
            _layer_spec(g1, layer), _layer_spec(w_in, layer), _layer_spec(qa_g, layer), _layer_spec(kva_g, layer),
        ],
        out_specs=[row(w) for w, _ in shapes],
        out_shape=[jax.ShapeDtypeStruct((N_TOK, w), dt) for w, dt in shapes],
        scratch_shapes=[pltpu.VMEM((WB, D_MODEL), BF16)],
        compiler_params=_params("arbitrary"),
        name="input_projection",
    )(hg_scal, x_lat, x_ctx, mod3, g1, w_in, qa_g, kva_g)


def _seq_block(b, j):
    return jnp.where(j == 0, NB_LAT + b, b * LAT_BLOCKS + j - 1)


def _seq_block_rev(b, j):
    return jnp.where(j == 0, NB_LAT + b, b * LAT_BLOCKS + LAT_BLOCKS - j)


S5_TB = 128
S5_ROWS = S5_TB * BATCH
S5_STEPS = (CTX_LEN + SEQ) // S5_TB
S5_CTX_STEPS = CTX_LEN // S5_TB


def _s5_kernel(uf_ref, ub_ref, bmat_ref, cmat_ref, a_ref, yf_ref, yb_ref, bu_ref, carry_ref):
    j = pl.program_id(0)

    @pl.when(j == 0)
    def _():
        carry_ref[...] = jnp.zeros_like(carry_ref)

    for d, (u_ref, y_ref) in enumerate(((uf_ref, yf_ref), (ub_ref, yb_ref))):
        bu_ref[...] = jnp.dot(u_ref[...], bmat_ref[0, d], preferred_element_type=F32)
        are = a_ref[0, d, 0]
        aim = a_ref[0, d, 1]

        def body(t, carry, d=d, are=are, aim=aim):
            hre, him = carry
            tt = (S5_TB - 1 - t) if d == 1 else t
            r0 = pl.multiple_of(tt * BATCH, BATCH)
            nre = are * hre - aim * him + bu_ref[pl.ds(r0, BATCH), 0:S5_LANES]
            nim = are * him + aim * hre + bu_ref[pl.ds(r0, BATCH), S5_LANES:2 * S5_LANES]
            bu_ref[pl.ds(r0, BATCH), 0:S5_LANES] = nre
            bu_ref[pl.ds(r0, BATCH), S5_LANES:2 * S5_LANES] = nim
            return nre, nim

        hre, him = lax.fori_loop(0, S5_TB, body, (carry_ref[d, 0], carry_ref[d, 1]), unroll=4)
        carry_ref[d, 0] = hre
        carry_ref[d, 1] = him
        hl = S5_LANES // 2
        hw = S5_WIDTH // 2
        for half in range(2):
            states = jnp.concatenate([bu_ref[:, half * hl:(half + 1) * hl],
                                      bu_ref[:, S5_LANES + half * hl:S5_LANES + (half + 1) * hl]], axis=1)
            y_ref[:, half * hw:(half + 1) * hw] = jnp.dot(states.astype(BF16), cmat_ref[0, d, half],
                                                          preferred_element_type=F32)


def _s5_block_rev(j):
    return jnp.where(j < S5_CTX_STEPS, S5_CTX_STEPS - 1 - j, S5_STEPS - 1 + S5_CTX_STEPS - j)


def _s5_states(layer, u_tm, bmat, cmat, a_tab):
    full = lambda a: _layer_spec(a, layer)
    fwd = pl.BlockSpec((S5_ROWS, S5_WIDTH), lambda j: (j, 0))
    bwd = pl.BlockSpec((S5_ROWS, S5_WIDTH), lambda j: (_s5_block_rev(j), 0))
    return pl.pallas_call(
        _s5_kernel,
        grid=(S5_STEPS,),
        in_specs=[fwd, bwd, full(bmat), full(cmat), full(a_tab)],
        out_specs=[fwd, bwd],
        out_shape=[jax.ShapeDtypeStruct((S5_STEPS * S5_ROWS, S5_WIDTH), F32)] * 2,
        scratch_shapes=[pltpu.VMEM((S5_ROWS, 2 * S5_LANES), F32), pltpu.VMEM((2, 2, BATCH, S5_LANES), F32)],
        compiler_params=_params("arbitrary"),
        name="s5_scan",
    )(u_tm, u_tm, bmat, cmat, a_tab)


def _to_time_major(a):
    w = a.shape[1]
    seq = jnp.concatenate([a[N_LAT:].reshape(BATCH, CTX_LEN, w), a[:N_LAT].reshape(BATCH, SEQ, w)], axis=1)
    return seq.transpose(1, 0, 2).reshape((CTX_LEN + SEQ) * BATCH, w)


def _from_time_major(a):
    w = a.shape[1]
    seq = a.reshape(CTX_LEN + SEQ, BATCH, w).transpose(1, 0, 2)
    return jnp.concatenate([seq[:, CTX_LEN:].reshape(N_LAT, w), seq[:, :CTX_LEN].reshape(N_CTX, w)], axis=0)


def _qkv_kernel(cqn_ref, kvn_ref, cos_ref, sin_ref, wq_ref, wk_ref, wv_ref, q_ref, k_ref, v_ref):
    cos = jnp.concatenate([cos_ref[...]] * MLA_HEADS, axis=1)
    sin = jnp.concatenate([sin_ref[...]] * MLA_HEADS, axis=1)
    w = MLA_HEADS * MLA_HEAD_PAD
    q2 = jnp.dot(cqn_ref[...], wq_ref[0], preferred_element_type=F32)
    q_ref[...] = ((q2[:, :w] * cos + q2[:, w:] * sin) * MLA_SCALE).astype(BF16)
    kv = kvn_ref[...]
    k2 = jnp.dot(kv, wk_ref[0], preferred_element_type=F32)
    k_ref[...] = (k2[:, :w] * cos + k2[:, w:] * sin).astype(BF16)
    v_ref[...] = jnp.dot(kv, wv_ref[0], preferred_element_type=F32).astype(BF16)


def _qkv(layer, cqn, kvn, cos_t, sin_t, wq, wk, wv):
    row = lambda w: pl.BlockSpec((WB, w), lambda i: (i, 0))
    full = lambda a: _layer_spec(a, layer)
    pos = pl.BlockSpec((WB, MLA_HEAD_PAD),
                       lambda i: (jnp.where(i < N_LAT // WB, i % (SEQ // WB), SEQ // WB), 0))
    w = MLA_HEADS * MLA_HEAD_PAD
    return pl.pallas_call(
        _qkv_kernel,
        grid=(N_TOK // WB,),
        in_specs=[row(MLA_Q_LORA), row(256), pos, pos, full(wq), full(wk), full(wv)],
        out_specs=[row(w), row(w), row(MLA_HEADS * MLA_V)],
        out_shape=[jax.ShapeDtypeStruct((N_TOK, w), BF16), jax.ShapeDtypeStruct((N_TOK, w), BF16),
                   jax.ShapeDtypeStruct((N_TOK, MLA_HEADS * MLA_V), BF16)],
        compiler_params=_params("arbitrary"),
        name="mla_qkv",
    )(cqn, kvn, cos_t, sin_t, wq, wk, wv)


def _attn_kernel(*refs, n_kv):
    q_ref = refs[0]
    k_refs = refs[1:1 + n_kv]
    v_refs = refs[1 + n_kv:1 + 2 * n_kv]
    o_ref = refs[1 + 2 * n_kv]
    nt = (((1,), (1,)), ((), ()))
    for h in range(MLA_HEADS):
        q = q_ref[:, h * MLA_HEAD_PAD:(h + 1) * MLA_HEAD_PAD]
        s = [lax.dot_general(q, k_ref[:, h * MLA_HEAD_PAD:(h + 1) * MLA_HEAD_PAD], nt,
                             preferred_element_type=F32) for k_ref in k_refs]
        m = functools.reduce(jnp.maximum, [jnp.max(si, axis=-1, keepdims=True) for si in s])
        p = [jnp.exp(si - m) for si in s]
        l = functools.reduce(jnp.add, [jnp.sum(pi, axis=-1, keepdims=True) for pi in p])
        o = functools.reduce(jnp.add, [
            jnp.dot(pi.astype(BF16), v_ref[:, h * MLA_V:(h + 1) * MLA_V], preferred_element_type=F32)
            for pi, v_ref in zip(p, v_refs)])
        o_ref[:, h * MLA_V:(h + 1) * MLA_V] = (o / l).astype(BF16)


def _attention(q, k, v, latent):
    w = MLA_HEADS * MLA_HEAD_PAD
    wv = MLA_HEADS * MLA_V
    if latent:
        tq = WB
        nq = SEQ // tq
        q_spec = pl.BlockSpec((tq, w), lambda b, j: (b * nq + j, 0))
        k_specs = [pl.BlockSpec((SEQ, w), lambda b, j: (b, 0)),
                   pl.BlockSpec((CTX_LEN, w), lambda b, j: (NB_LAT + b, 0))]
        v_specs = [pl.BlockSpec((SEQ, wv), lambda b, j: (b, 0)),
                   pl.BlockSpec((CTX_LEN, wv), lambda b, j: (NB_LAT + b, 0))]
        o_spec = pl.BlockSpec((tq, wv), lambda b, j: (b * nq + j, 0))
        n_out = N_LAT
    else:
        tq = CTX_LEN
        nq = 1
        q_spec = pl.BlockSpec((tq, w), lambda b, j: (NB_LAT + b, 0))
        k_specs = [pl.BlockSpec((CTX_LEN, w), lambda b, j: (NB_LAT + b, 0))]
        v_specs = [pl.BlockSpec((CTX_LEN, wv), lambda b, j: (NB_LAT + b, 0))]
        o_spec = pl.BlockSpec((tq, wv), lambda b, j: (b, 0))
        n_out = N_CTX
    n_kv = len(k_specs)
    return pl.pallas_call(
        functools.partial(_attn_kernel, n_kv=n_kv),
        grid=(BATCH, nq),
        in_specs=[q_spec] + k_specs + v_specs,
        out_specs=o_spec,
        out_shape=jax.ShapeDtypeStruct((n_out, wv), BF16),
        compiler_params=_params("arbitrary", "arbitrary"),
        name="mla_attention_lat" if latent else "mla_attention_ctx",
    )(q, *([k] * n_kv), *([v] * n_kv))


def _hgrn_tables():
    c = HG_CHUNK
    w = np.zeros((2, HG_SETS * c, c), np.float32)
    mask = np.zeros((2, HG_LEVELS, c, c), np.float32)
    idx = np.arange(c)
    for r in range(c):
        w[0, r, :r + 1] = 1
        w[1, r, r:] = 1
        w[0, c + r, r + 1:] = 1
        w[1, c + r, :r] = 1
    for l in range(HG_LEVELS):
        m = c >> (l + 1)
        for r in range(c):
            base = (r // (2 * m)) * 2 * m
            mid = base + m
            later = r >= mid
            row = (2 + l) * c + r
            if later:
                w[0, row, mid:r + 1] = 1
                w[1, row, mid:r] = 1
            else:
                w[0, row, r + 1:mid] = 1
                w[1, row, r:mid] = 1
        same = (idx[:, None] // (2 * m)) == (idx[None, :] // (2 * m))
        q_later = (idx[:, None] % (2 * m)) >= m
        k_later = (idx[None, :] % (2 * m)) >= m
        mask[0, l] = same & q_later & ~k_later
        mask[1, l] = same & ~q_later & k_later
    return w, mask


def _hgrn_dir(d, q_ref, k_ref, lf_ref, v_ref, w_ref, mask_ref, st_ref, o_ref):
    c = HG_CHUNK
    n_chunks = RB // c
    nt = (((1,), (1,)), ((), ()))
    tn = (((0,), (0,)), ((), ()))
    last = c - 1 if d == 0 else 0
    wsel = w_ref[d]
    e = []
    for ci in range(n_chunks):
        g = lf_ref[ci * c:(ci + 1) * c, :]
        g_hi = g.astype(BF16)
        g_lo = (g - g_hi.astype(F32)).astype(BF16)
        expo = jnp.dot(wsel, jnp.concatenate([g_hi, g_lo], axis=0), preferred_element_type=F32)
        e.append(jnp.exp(expo))

    def rows(s):
        return jnp.concatenate([e[ci][s * c:(s + 1) * c] for ci in range(n_chunks)], axis=0)

    q = q_ref[...].astype(F32)
    k = k_ref[...].astype(F32)
    q_state = (q * rows(0)).astype(BF16)
    k_state = (k * rows(1)).astype(BF16)
    q_lvl, k_lvl = [], []
    for l in range(HG_LEVELS):
        el = rows(2 + l)
        q_lvl.append((q * el).astype(BF16))
        k_lvl.append((k * el).astype(BF16))
    qk = q * k
    order = range(n_chunks) if d == 0 else range(n_chunks - 1, -1, -1)
    for h in range(HG_HEADS):
        ks = slice(h * HG_K, (h + 1) * HG_K)
        vs = slice(h * HG_V, (h + 1) * HG_V)
        v = v_ref[:, vs]
        scores = jnp.zeros((RB, RB), F32)
        for l in range(HG_LEVELS):
            p = lax.dot_general(q_lvl[l][:, ks], k_lvl[l][:, ks], nt, preferred_element_type=F32)
            scores = scores + p * mask_ref[d, l]
        o = jnp.dot(scores.astype(BF16), v, preferred_element_type=F32)
        o = o + jnp.sum(qk[:, ks], axis=-1, keepdims=True) * v.astype(F32)
        st = st_ref[d, h]
        o_state = [None] * n_chunks
        for ci in order:
            r = slice(ci * c, (ci + 1) * c)
            o_state[ci] = lax.dot_general(q_state[r, ks], st.astype(BF16), nt, preferred_element_type=F32)
            inc = lax.dot_general(v[r], k_state[r, ks], tn, preferred_element_type=F32)
            st = st * e[ci][last:last + 1, ks] + inc
        st_ref[d, h] = st
        o_ref[:, vs] = o + jnp.concatenate(o_state, axis=0)


def _hgrn_fast_tables():
    c = HG_FAST_CHUNK
    idx = np.arange(RB)
    same = (idx[:, None] // c) == (idx[None, :] // c)
    le = idx[None, :] <= idx[:, None]
    ge = idx[None, :] >= idx[:, None]
    wc = np.stack([same & le, same & ge]).astype(np.float32)
    col_chunk = np.arange((RB // c) * HG_V) // HG_V
    vmask = ((idx[:, None] // c) == col_chunk[None, :]).astype(np.float32)
    return np.concatenate([wc, wc], axis=-1), wc, vmask


def _hgrn_dir_fast(d, q_ref, k_ref, cum, tot, v_ref, maskc_ref, vmask_ref, st_ref, o_ref):
    c = HG_FAST_CHUNK
    n_chunks = RB // c
    nt = (((1,), (1,)), ((), ()))
    tn = (((0,), (0,)), ((), ()))
    q = q_ref[...].astype(F32)
    k = k_ref[...].astype(F32)
    qa = (q * jnp.exp(cum)).astype(BF16)
    kb = (k * jnp.exp(-cum)).astype(BF16)
    k_state = (k * jnp.exp(tot - cum)).astype(BF16)
    decay = jnp.exp(tot)
    keep = maskc_ref[d] > 0.5
    vmask = vmask_ref[...]
    order = range(n_chunks) if d == 0 else range(n_chunks - 1, -1, -1)
    for h in range(HG_HEADS):
        ks = slice(h * HG_K, (h + 1) * HG_K)
        vs = slice(h * HG_V, (h + 1) * HG_V)
        v = v_ref[:, vs]
        p = lax.dot_general(qa[:, ks], kb[:, ks], nt, preferred_element_type=F32)
        o = jnp.dot(jnp.where(keep, p, 0.0).astype(BF16), v, preferred_element_type=F32)
        v_by_chunk = jnp.concatenate([v] * n_chunks, axis=1) * vmask
        inc = lax.dot_general(v_by_chunk, k_state[:, ks], tn, preferred_element_type=F32)
        st = st_ref[d, h]
        entering = [None] * n_chunks
        for ci in order:
            entering[ci] = st
            st = st * decay[ci * c:ci * c + 1, ks] + inc[ci * HG_V:(ci + 1) * HG_V]
        st_ref[d, h] = st
        s_all = jnp.concatenate(entering, axis=0).astype(BF16)
        o_all = lax.dot_general(qa[:, ks], s_all, nt, preferred_element_type=F32) * vmask.astype(F32)
        o_fold = functools.reduce(jnp.add, [o_all[:, j * 128:(j + 1) * 128] for j in range(n_chunks * HG_V // 128)])
        o_state = (o_fold + pltpu.roll(o_fold, HG_V, 1))[:, :HG_V]
        o_ref[:, vs] = o + o_state


def _hgrn_kernel(qf_ref, kf_ref, lff_ref, vf_ref, qb_ref, kb_ref, lfb_ref, vb_ref, w_ref, mask_ref,
                 wc_ref, maskc_ref, vmask_ref, of_ref, ob_ref, st_ref):
    j = pl.program_id(1)

    @pl.when(j == 0)
    def _():
        st_ref[...] = jnp.zeros_like(st_ref)

    c = HG_FAST_CHUNK
    dirs = ((0, qf_ref, kf_ref, lff_ref, vf_ref, of_ref), (1, qb_ref, kb_ref, lfb_ref, vb_ref, ob_ref))
    for d, q_ref, k_ref, lf_ref, v_ref, o_ref in dirs:
        g_hi, g_lo = _split_bf16(lf_ref[...])
        cum = jnp.dot(wc_ref[d], jnp.concatenate([g_hi, g_lo], axis=0), preferred_element_type=F32)
        edge = c - 1 if d == 0 else 0
        totals = [cum[ci * c + edge:ci * c + edge + 1] for ci in range(RB // c)]
        tot = jnp.concatenate([jnp.broadcast_to(t, (c, t.shape[1])) for t in totals], axis=0)
        safe = jnp.min(jnp.concatenate(totals, axis=0)) >= HG_FAST_MIN

        @pl.when(safe)
        def _(d=d, q_ref=q_ref, k_ref=k_ref, v_ref=v_ref, o_ref=o_ref, cum=cum, tot=tot):
            _hgrn_dir_fast(d, q_ref, k_ref, cum, tot, v_ref, maskc_ref, vmask_ref, st_ref, o_ref)

        @pl.when(jnp.logical_not(safe))
        def _(d=d, q_ref=q_ref, k_ref=k_ref, lf_ref=lf_ref, v_ref=v_ref, o_ref=o_ref):
            _hgrn_dir(d, q_ref, k_ref, lf_ref, v_ref, w_ref, mask_ref, st_ref, o_ref)


def _hgrn(hq, kkf, kkb, lff, lfb, hi, w_tab, mask_tab, wc_tab, maskc_tab, vmask_tab):
    full = _resident
    fwd = lambda w: pl.BlockSpec((RB, w), lambda b, j: (_seq_block(b, j), 0))
    bwd = lambda w: pl.BlockSpec((RB, w), lambda b, j: (_seq_block_rev(b, j), 0))
    wk = HG_HEADS * HG_K
    wv = HG_HEADS * HG_V
    return pl.pallas_call(
        _hgrn_kernel,
        grid=(BATCH, SEQ_BLOCKS),
        in_specs=[fwd(wk), fwd(wk), fwd(wk), fwd(wv), bwd(wk), bwd(wk), bwd(wk), bwd(wv),
                  full(w_tab), full(mask_tab), full(wc_tab), full(maskc_tab), full(vmask_tab)],
        out_specs=[fwd(wv), bwd(wv)],
        out_shape=[jax.ShapeDtypeStruct((N_TOK, wv), F32)] * 2,
        scratch_shapes=[pltpu.VMEM((2, HG_HEADS, HG_V, HG_K), F32)],
        compiler_params=_params("arbitrary", "arbitrary"),
        name="hgrn2_scan",
    )(hq, kkf, lff, hi, hq, kkb, lfb, hi, w_tab, mask_tab, wc_tab, maskc_tab, vmask_tab)


def _gelu_tanh(x):
    return 0.5 * x * (1.0 + jnp.tanh(math.sqrt(2.0 / math.pi) * (x + 0.044715 * (x * x * x))))


def _split_bf16(a):
    hi = a.astype(BF16)
    return hi, (a - hi.astype(F32)).astype(BF16)


def _merge_kernel(xl_ref, xc_ref, mod_ref, yf_ref, yb_ref, u_ref, of_ref, ob_ref, hgs_ref, attl_ref, attc_ref,
                  gates_ref, s5d_ref, gluw_ref, glub_ref, hgg_ref, hsum_ref, wpa_ref, wpb_ref, wpc_ref, wout_ref,
                  g2_ref, wrh_ref, wrl_ref, br_ref,
                  x1_ref, h2_ref, eid_ref, gate_ref):
    y = yf_ref[...] + yb_ref[...] + s5d_ref[0] * u_ref[...].astype(F32)
    y = _gelu_tanh(y)
    y = y * jax.nn.sigmoid(jnp.dot(y.astype(BF16), gluw_ref[0], preferred_element_type=F32) + glub_ref[0])
    o = of_ref[...] + ob_ref[...]
    sq_hi, sq_lo = _split_bf16(o * o)
    ms = (jnp.dot(sq_hi, hsum_ref[...], preferred_element_type=F32)
          + jnp.dot(sq_lo, hsum_ref[...], preferred_element_type=F32)) * (1.0 / HG_V)
    o = o * lax.rsqrt(ms + NORM_EPS) * hgg_ref[0] * hgs_ref[...].astype(F32)

    d = D_MODEL
    att = _pick_rows(attl_ref, attc_ref, WB)
    merged = (gates_ref[:, 0:d].astype(F32) * jnp.dot(y.astype(BF16), wpa_ref[0], preferred_element_type=F32)
              + gates_ref[:, d:2 * d].astype(F32) * jnp.dot(att, wpb_ref[0], preferred_element_type=F32)
              + gates_ref[:, 2 * d:3 * d].astype(F32) * jnp.dot(o.astype(BF16), wpc_ref[0], preferred_element_type=F32))
    y_out = jnp.dot(merged.astype(BF16), wout_ref[0], preferred_element_type=F32)
    x1 = _pick_rows(xl_ref, xc_ref, WB) + mod_ref[0, 2:3, :] * y_out
    x1_ref[...] = x1
    h2 = _rms(x1) * g2_ref[0] * (1.0 + mod_ref[0, 4:5, :]) + mod_ref[0, 3:4, :]
    h2_ref[...] = h2

    h_hi, h_lo = _split_bf16(h2)
    logits = (jnp.dot(h_hi, wrh_ref[0], preferred_element_type=F32)
              + jnp.dot(h_lo, wrh_ref[0], preferred_element_type=F32)
              + jnp.dot(h_hi, wrl_ref[0], preferred_element_type=F32)) + br_ref[0]
    lane = lax.broadcasted_iota(jnp.int32, logits.shape, 1).astype(F32)
    neg = -jnp.inf
    glog = jnp.where(lane < MOE_GROUPS, logits, neg)
    gmax = jnp.max(glog, axis=-1, keepdims=True)
    gidx = jnp.min(jnp.where(glog == gmax, lane, 1e9), axis=-1, keepdims=True)
    g_w = 1.0 / jnp.sum(jnp.exp(glog - gmax), axis=-1, keepdims=True)
    e_lo = 32.0 + gidx * MOE_EXPERTS_PER_GROUP
    elog = jnp.where((lane >= e_lo) & (lane < e_lo + MOE_EXPERTS_PER_GROUP), logits, neg)
    v1 = jnp.max(elog, axis=-1, keepdims=True)
    i1 = jnp.min(jnp.where(elog == v1, lane, 1e9), axis=-1, keepdims=True)
    elog2 = jnp.where(lane == i1, neg, elog)
    v2 = jnp.max(elog2, axis=-1, keepdims=True)
    i2 = jnp.min(jnp.where(elog2 == v2, lane, 1e9), axis=-1, keepdims=True)
    e2 = jnp.exp(v2 - v1)
    gate1 = g_w / (1.0 + e2)
    gate2 = g_w * e2 / (1.0 + e2)
    eid_ref[...] = jnp.where(lane == 0.0, i1 - 32.0, jnp.where(lane == 1.0, i2 - 32.0, 0.0)).astype(jnp.int32)
    gate_ref[...] = jnp.where(lane == 0.0, gate1, jnp.where(lane == 1.0, gate2, 0.0))


def _merge(layer, n_blocks, x_lat, x_ctx, ctx_row0, mod3, yf, yb, s5u, of, ob, hgs, att_lat, att_ctx, gates,
           head_sum, consts):
    row = lambda w: pl.BlockSpec((WB, w), lambda i: (i, 0))
    n = n_blocks * RB
    wv = MLA_HEADS * MLA_V
    if att_ctx is None:
        att_ctx = att_lat
    const_specs = [_layer_spec(a, layer) for a in consts]
    const_specs.insert(4, _resident(head_sum))
    return pl.pallas_call(
        _merge_kernel,
        grid=(n // WB,),
        in_specs=[*_split_row_specs(WB, D_MODEL, ctx_row0),
                  pl.BlockSpec((1, 6, D_MODEL), lambda i: (_mod_row(i, WB), 0, 0)),
                  row(S5_WIDTH), row(S5_WIDTH), row(S5_WIDTH), row(HG_HEADS * HG_V), row(HG_HEADS * HG_V),
                  row(HG_HEADS * HG_V), *_split_row_specs(WB, wv, 0), row(3 * D_MODEL)] + const_specs,
        out_specs=[row(D_MODEL), row(D_MODEL), row(128), row(128)],
        out_shape=[jax.ShapeDtypeStruct((n, D_MODEL), F32), jax.ShapeDtypeStruct((n, D_MODEL), F32),
                   jax.ShapeDtypeStruct((n, 128), jnp.int32), jax.ShapeDtypeStruct((n, 128), F32)],
        compiler_params=_params("arbitrary"),
        name="merge_router",
    )(x_lat, x_ctx, mod3, yf, yb, s5u, of, ob, hgs, att_lat, att_ctx, gates,
      *consts[:4], head_sum, *consts[4:])


def _moe_kernel(bexp_ref, nblk_ref, src_ref, srcn_ref, dst_ref, h_hbm, w1_ref, w3_ref, w2_ref, y_hbm,
                xbuf, ybuf, w1b, w3b, w2b, gsem, ssem):
    i = pl.program_id(0)
    n_used = nblk_ref[0]
    slot = lax.rem(i, 2)

    def gather_start(idx_ref, s):
        for r in range(MOE_ROWS):
            pltpu.make_async_copy(h_hbm.at[pl.ds(idx_ref[0, 0, r], 1), :], xbuf.at[s, pl.ds(r, 1), :],
                                  gsem.at[s]).start()

    def gather_wait(s):
        pltpu.make_async_copy(h_hbm.at[pl.ds(0, MOE_ROWS), :], xbuf.at[s], gsem.at[s]).wait()

    def scatter_start(s):
        for r in range(MOE_ROWS):
            pltpu.make_async_copy(ybuf.at[s, pl.ds(r, 1), :], y_hbm.at[pl.ds(dst_ref[0, 0, r], 1), :],
                                  ssem.at[s]).start()

    def scatter_wait(s):
        pltpu.make_async_copy(ybuf.at[s], y_hbm.at[pl.ds(0, MOE_ROWS), :], ssem.at[s]).wait()

    @pl.when(i < n_used)
    def _():
        @pl.when(i == 0)
        def _():
            gather_start(src_ref, 0)
            ybuf[1] = jnp.zeros((MOE_ROWS, D_MODEL), F32)
            n_real = y_hbm.shape[0] - MOE_ROWS
            fill = pltpu.make_async_copy(ybuf.at[1], y_hbm.at[pl.ds(n_real, MOE_ROWS), :], ssem.at[1])
            fill.start()
            fill.wait()

        @pl.when(i + 1 < n_used)
        def _():
            gather_start(srcn_ref, 1 - slot)

        @pl.when((i == 0) | (bexp_ref[i] != bexp_ref[jnp.maximum(i - 1, 0)]))
        def _():
            w1b[...] = w1_ref[0, 0].astype(BF16)
            w3b[...] = w3_ref[0, 0].astype(BF16)
            w2b[...] = w2_ref[0, 0].astype(BF16)

        gather_wait(slot)

        @pl.when(i >= 2)
        def _():
            scatter_wait(slot)

        x = xbuf[slot].astype(BF16)
        a = jnp.dot(x, w1b[...], preferred_element_type=F32)
        g = jnp.dot(x, w3b[...], preferred_element_type=F32)
        hid = (a * jax.nn.sigmoid(a) * g).astype(BF16)
        ybuf[slot] = jnp.dot(hid, w2b[...], preferred_element_type=F32)
        scatter_start(slot)

        @pl.when(i == n_used - 1)
        def _():
            @pl.when(i >= 1)
            def _():
                scatter_wait(1 - slot)
            scatter_wait(slot)


def _moe(h2, eid, w1, w3, w2, layer):
    n = h2.shape[0]
    n_assign = n * MOE_TOP_K
    n_blocks = (n_assign + MOE_EXPERTS * (MOE_ROWS - 1) + MOE_ROWS - 1) // MOE_ROWS
    flat_e = eid.reshape(n_assign)
    order = jnp.argsort(flat_e, stable=True).astype(jnp.int32)
    experts = jnp.arange(MOE_EXPERTS, dtype=jnp.int32)
    counts = jnp.sum((flat_e[:, None] == experts[None, :]).astype(jnp.int32), axis=0)
    starts = jnp.cumsum(counts) - counts
    padded = (counts + MOE_ROWS - 1) // MOE_ROWS * MOE_ROWS
    p_ends = jnp.cumsum(padded)
    p_starts = p_ends - padded
    n_used_s = p_ends[-1] // MOE_ROWS
    n_used = n_used_s.astype(jnp.int32).reshape(1)
    blk = jnp.arange(n_blocks, dtype=jnp.int32)
    blk_expert = jnp.sum((p_ends[None, :] <= (blk * MOE_ROWS)[:, None]).astype(jnp.int32), axis=1)
    last_expert = jnp.max(jnp.where(counts > 0, experts, 0))
    block_expert = jnp.where(blk < n_used_s, jnp.minimum(blk_expert, MOE_EXPERTS - 1), last_expert)
    onehot = (block_expert[:, None] == experts[None, :]).astype(jnp.int32)
    pick = lambda table: jnp.sum(onehot * table[None, :], axis=1)
    local = jnp.arange(MOE_ROWS, dtype=jnp.int32)[None, :]
    rank = blk[:, None] * MOE_ROWS + local - pick(p_starts)[:, None]
    valid = (rank < pick(counts)[:, None]) & (blk < n_used_s)[:, None]
    pos = jnp.clip(pick(starts)[:, None] + rank, 0, n_assign - 1)
    assign = order[pos]
    tok = jnp.where(valid, assign // MOE_TOP_K, 0)
    dst = jnp.where(valid, (assign % MOE_TOP_K) * n + assign // MOE_TOP_K, n_assign + local)
    rows_now = pl.BlockSpec((1, 1, MOE_ROWS), lambda i, be, nb: (i, 0, 0), memory_space=pltpu.SMEM)
    rows_next = pl.BlockSpec((1, 1, MOE_ROWS), lambda i, be, nb: (jnp.minimum(i + 1, n_blocks - 1), 0, 0),
                             memory_space=pltpu.SMEM)
    src3 = tok.reshape(n_blocks, 1, MOE_ROWS)
    return pl.pallas_call(
        _moe_kernel,
        grid_spec=pltpu.PrefetchScalarGridSpec(
            num_scalar_prefetch=2,
            grid=(n_blocks,),
            in_specs=[
                rows_now, rows_next, rows_now,
                pl.BlockSpec(memory_space=pl.ANY),
                pl.BlockSpec((1, 1, D_MODEL, MOE_HIDDEN), lambda i, be, nb: (layer, be[i], 0, 0)),
                pl.BlockSpec((1, 1, D_MODEL, MOE_HIDDEN), lambda i, be, nb: (layer, be[i], 0, 0)),
                pl.BlockSpec((1, 1, MOE_HIDDEN, D_MODEL), lambda i, be, nb: (layer, be[i], 0, 0)),
            ],
            out_specs=pl.BlockSpec(memory_space=pl.ANY),
            scratch_shapes=[pltpu.VMEM((2, MOE_ROWS, D_MODEL), F32), pltpu.VMEM((2, MOE_ROWS, D_MODEL), F32),
                            pltpu.VMEM((D_MODEL, MOE_HIDDEN), BF16), pltpu.VMEM((D_MODEL, MOE_HIDDEN), BF16),
                            pltpu.VMEM((MOE_HIDDEN, D_MODEL), BF16),
                            pltpu.SemaphoreType.DMA((2,)), pltpu.SemaphoreType.DMA((2,))],
        ),
        out_shape=jax.ShapeDtypeStruct((n_assign + MOE_ROWS, D_MODEL), F32),
        compiler_params=_params("arbitrary"),
        name="moe_experts",
    )(block_expert, n_used, src3, src3, dst.reshape(n_blocks, 1, MOE_ROWS), h2, w1, w3, w2)


def _combine_kernel(x1_ref, y0_ref, y1_ref, gate_ref, mod_ref, g_ref, o_ref, *, final):
    f = gate_ref[:, 0:1] * y0_ref[...] + gate_ref[:, 1:2] * y1_ref[...]
    x2 = x1_ref[...] + mod_ref[0, 5:6, :] * f
    if final:
        x2 = _rms(x2) * g_ref[...]
    o_ref[...] = x2


def _combine(n_blocks, x1, y, gate, mod3, g_final, final):
    row = lambda w: pl.BlockSpec((RB, w), lambda i: (i, 0))
    n = n_blocks * RB
    return pl.pallas_call(
        functools.partial(_combine_kernel, final=final),
        grid=(n_blocks,),
        in_specs=[row(D_MODEL), row(D_MODEL), pl.BlockSpec((RB, D_MODEL), lambda i: (i + n_blocks, 0)), row(128),
                  pl.BlockSpec((1, 6, D_MODEL), lambda i: (_mod_row(i), 0, 0)),
                  pl.BlockSpec((1, D_MODEL), lambda i: (0, 0))],
        out_specs=row(D_MODEL),
        out_shape=jax.ShapeDtypeStruct((n, D_MODEL), F32),
        compiler_params=_params("arbitrary"),
        name="moe_combine_final" if final else "moe_combine",
    )(x1, y, y, gate, mod3, g_final)


def _rope_rot_cols(w):
    p = ROPE_PAIRS
    return jnp.concatenate([-w[..., p:2 * p], w[..., 0:p], -w[..., 3 * p:4 * p], w[..., 2 * p:3 * p]], axis=-1)


def _pack_w_in(w_in):
    cuts = np.cumsum([S5_WIDTH, MLA_Q_LORA, MLA_KV_LORA, MLA_ROPE, HG_HEADS * HG_K, HG_HEADS * HG_K,
                      HG_HEADS * HG_K, HG_HEADS * HG_V, HG_HEADS * HG_V, D_MODEL, D_MODEL])
    (w_s5, w_cq, w_ckv, w_kpe, w_hq, w_zf, w_zb, w_hi, w_hg, w_ga, w_gb, w_gc) = jnp.split(w_in, cuts, axis=-1)
    pad = jnp.zeros(w_in.shape[:-1] + (256 - MLA_KV_LORA - 2 * MLA_ROPE,), w_in.dtype)
    packed = jnp.concatenate([w_s5, w_cq, w_ckv, w_kpe, _rope_rot_cols(w_kpe), pad,
                              w_hq, w_zf, w_zb, w_hi, w_hg, w_ga, w_gb, w_gc], axis=-1)
    assert packed.shape[-1] == W_IN_COLS
    return packed.astype(BF16)


def _pack_mla(w_uq, w_uk, w_uv):
    hp, n, r = MLA_HEAD_PAD, MLA_NOPE, MLA_ROPE
    nl = w_uq.shape[0]
    wq = w_uq.reshape(nl, MLA_Q_LORA, MLA_HEADS, n + r)
    zq = jnp.zeros((nl, MLA_Q_LORA, MLA_HEADS, hp - n - r), F32)
    q1 = jnp.concatenate([wq, zq], axis=-1).reshape(nl, MLA_Q_LORA, MLA_HEADS * hp)
    q2 = jnp.concatenate([jnp.zeros((nl, MLA_Q_LORA, MLA_HEADS, n), F32), _rope_rot_cols(wq[..., n:]), zq],
                         axis=-1).reshape(nl, MLA_Q_LORA, MLA_HEADS * hp)
    wq_packed = jnp.concatenate([q1, q2], axis=-1).astype(BF16)

    wk = w_uk.reshape(nl, MLA_KV_LORA, MLA_HEADS, n)
    k_nope = jnp.concatenate([wk, jnp.zeros((nl, MLA_KV_LORA, MLA_HEADS, hp - n), F32)], axis=-1)
    pe_slot = np.concatenate([np.zeros((r, n), np.float32), np.eye(r, dtype=np.float32),
                              np.zeros((r, hp - n - r), np.float32)], axis=-1)
    pe_all = jnp.asarray(np.tile(pe_slot[None, :, None, :], (nl, 1, MLA_HEADS, 1)))
    zero_pe = jnp.zeros_like(pe_all)
    tail = jnp.zeros((nl, 256 - MLA_KV_LORA - 2 * r, MLA_HEADS, hp), F32)
    k1 = jnp.concatenate([k_nope, pe_all, zero_pe, tail], axis=1).reshape(nl, 256, MLA_HEADS * hp)
    k2 = jnp.concatenate([jnp.zeros_like(k_nope), zero_pe, pe_all, tail], axis=1).reshape(nl, 256, MLA_HEADS * hp)
    wk_packed = jnp.concatenate([k1, k2], axis=-1).astype(BF16)
    wv_packed = jnp.concatenate([w_uv, jnp.zeros((nl, 256 - MLA_KV_LORA, MLA_HEADS * MLA_V), F32)],
                                axis=1).astype(BF16)
    return wq_packed, wk_packed, wv_packed


def _rope_tables():
    rows = SEQ // GRID_W
    row = np.repeat(np.arange(rows, dtype=np.float32), GRID_W)
    col = np.tile(np.arange(GRID_W, dtype=np.float32), rows)
    inv = (np.float32(ROPE_BASE) ** (-np.arange(ROPE_PAIRS, dtype=np.float32) / np.float32(ROPE_PAIRS))).astype(np.float32)
    ar, ac = row[:, None] * inv, col[:, None] * inv
    cos = np.concatenate([np.cos(ar), np.cos(ar), np.cos(ac), np.cos(ac)], axis=1)
    sin = np.concatenate([np.sin(ar), np.sin(ar), np.sin(ac), np.sin(ac)], axis=1)
    tail = MLA_HEAD_PAD - MLA_NOPE - MLA_ROPE
    cos_t = np.concatenate([np.ones((SEQ, MLA_NOPE)), cos, np.ones((SEQ, tail))], axis=1)
    sin_t = np.concatenate([np.zeros((SEQ, MLA_NOPE)), sin, np.zeros((SEQ, tail))], axis=1)
    cos_t = np.concatenate([cos_t, np.ones((WB, MLA_HEAD_PAD))], axis=0)
    sin_t = np.concatenate([sin_t, np.zeros((WB, MLA_HEAD_PAD))], axis=0)
    return jnp.asarray(cos_t, F32), jnp.asarray(sin_t, F32)


def _pack_s5(lam_re, lam_im, b_re, b_im, c_re, c_im, log_step):
    eye = jnp.eye(S5_GROUPS, dtype=F32)
    dt = jnp.exp(log_step)[..., None]
    mag = jnp.exp(lam_re * dt)
    lb_re, lb_im = mag * jnp.cos(lam_im * dt), mag * jnp.sin(lam_im * dt)
    den = lam_re * lam_re + lam_im * lam_im
    fr = ((lb_re - 1) * lam_re + lb_im * lam_im) / den
    fi = (lb_im * lam_re - (lb_re - 1) * lam_im) / den
    bb_re = fr[..., None] * b_re - fi[..., None] * b_im
    bb_im = fr[..., None] * b_im + fi[..., None] * b_re
    lead = lam_re.shape[:2]

    def in_mat(bb):
        return jnp.einsum('ldgph,gk->ldghkp', bb, eye).reshape(lead + (S5_WIDTH, S5_LANES))

    def out_mat(cc):
        return jnp.einsum('ldghp,gk->ldgpkh', cc, eye).reshape(lead + (S5_LANES, S5_WIDTH))

    bmat = jnp.concatenate([in_mat(bb_re), in_mat(bb_im)], axis=-1).astype(BF16)
    hl, hw = S5_LANES // 2, S5_WIDTH // 2
    cre, cim = out_mat(c_re), -out_mat(c_im)
    cmat = jnp.stack([jnp.concatenate([cre[..., h * hl:(h + 1) * hl, h * hw:(h + 1) * hw],
                                       cim[..., h * hl:(h + 1) * hl, h * hw:(h + 1) * hw]], axis=-2)
                      for h in range(2)], axis=2).astype(BF16)
    a_tab = jnp.stack([jnp.broadcast_to(lb_re.reshape(lead + (1, S5_LANES)), lead + (BATCH, S5_LANES)),
                       jnp.broadcast_to(lb_im.reshape(lead + (1, S5_LANES)), lead + (BATCH, S5_LANES))], axis=2)
    return bmat, cmat, a_tab


def kernel(x, c, ctx, c_ctx, mod_w, mod_b, norm1_g, norm2_g, w_in, s5_lam_re, s5_lam_im, s5_b_re, s5_b_im, s5_c_re, s5_c_im, s5_log_step, s5_d, s5_glu_w, s5_glu_b, mla_qa_g, mla_kva_g, mla_w_uq, mla_w_uk, mla_w_uv, hg_lb_logits, hg_norm_g, w_pa, w_pb, w_pc, w_out, moe_w_group, moe_b_group, moe_w_expert, moe_b_expert, moe_w1, moe_w3, moe_w2, final_norm_g):
    x_lat, x_ctx, ctx_row0 = x.reshape(N_LAT, D_MODEL), ctx.reshape(N_CTX, D_MODEL), 0
    c_rows = jnp.concatenate([c, c_ctx[None, :], jnp.zeros((16 - BATCH - 1, D_MODEL), F32)], axis=0)
    mod = _modulation(c_rows, mod_w, mod_b).reshape(DEPTH, 16, 6, D_MODEL)

    lb_all = jnp.cumsum(jax.nn.softmax(hg_lb_logits.astype(F32)))
    lb_all = lb_all - lb_all[0]
    hg_scal = jnp.stack([jnp.log(lb_all), jnp.log1p(-lb_all), 1.0 - lb_all], axis=1).reshape(3 * DEPTH).astype(F32)
    cos_t, sin_t = _rope_tables()

    vec = lambda a: a[:, None, :]
    w_in_p = _pack_w_in(w_in)
    bmat, cmat, a_tab = _pack_s5(s5_lam_re, s5_lam_im, s5_b_re, s5_b_im, s5_c_re, s5_c_im, s5_log_step)
    wq, wk, wv = _pack_mla(mla_w_uq, mla_w_uk, mla_w_uv)
    w_route = jnp.concatenate([moe_w_group, jnp.zeros((DEPTH, D_MODEL, 32 - MOE_GROUPS), F32), moe_w_expert,
                               jnp.zeros((DEPTH, D_MODEL, 128 - 32 - MOE_EXPERTS), F32)], axis=-1)
    b_route = jnp.concatenate([moe_b_group, jnp.zeros((DEPTH, 32 - MOE_GROUPS), F32), moe_b_expert,
                               jnp.zeros((DEPTH, 128 - 32 - MOE_EXPERTS), F32)], axis=-1)
    w_route_hi = w_route.astype(BF16)
    w_route_lo = (w_route - w_route_hi.astype(F32)).astype(BF16)
    merge_consts = [vec(s5_d), s5_glu_w.astype(BF16), vec(s5_glu_b), vec(hg_norm_g),
                    w_pa.astype(BF16), w_pb.astype(BF16), w_pc.astype(BF16), w_out.astype(BF16), vec(norm2_g),
                    w_route_hi, w_route_lo, vec(b_route)]
    w_tab_np, mask_tab_np = _hgrn_tables()
    w_tab = jnp.asarray(np.concatenate([w_tab_np, w_tab_np], axis=-1), BF16)
    mask_tab = jnp.asarray(np.kron(np.eye(RB // HG_CHUNK, dtype=np.float32), mask_tab_np), F32)
    head_sum = jnp.asarray(np.kron(np.eye(HG_HEADS), np.ones((HG_V, HG_V))), F32)
    wc_np, maskc_np, vmask_np = _hgrn_fast_tables()
    wc_tab, maskc_tab, vmask_tab = jnp.asarray(wc_np, BF16), jnp.asarray(maskc_np, F32), jnp.asarray(vmask_np, BF16)

    out = None
    for layer in range(DEPTH):
        last = layer == DEPTH - 1
        mod3 = mod[layer]
        (s5u, cqn, kvn, hq, lff, lfb, kkf, kkb, hi, hgs, gates) = _input_projection(
            layer, x_lat, x_ctx, ctx_row0, mod3, hg_scal, vec(norm1_g), w_in_p, vec(mla_qa_g), vec(mla_kva_g))

        yf, yb = [_from_time_major(y) for y in _s5_states(layer, _to_time_major(s5u), bmat, cmat, a_tab)]

        q, k, v = _qkv(layer, cqn, kvn, cos_t, sin_t, wq, wk, wv)
        att_lat = _attention(q, k, v, latent=True)
        att_ctx = None if last else _attention(q, k, v, latent=False)

        of, ob = _hgrn(hq, kkf, kkb, lff, lfb, hi, w_tab, mask_tab, wc_tab, maskc_tab, vmask_tab)

        n_blocks = NB_LAT if last else NB_TOK
        x1, h2, eid, gate = _merge(layer, n_blocks, x_lat, x_ctx, ctx_row0, mod3, yf, yb, s5u, of, ob, hgs,
                                   att_lat, att_ctx, gates, head_sum, merge_consts)

        y_moe = _moe(h2, eid[:, :MOE_TOP_K], moe_w1, moe_w3, moe_w2, layer)
        res = _combine(n_blocks, x1, y_moe, gate, mod3, final_norm_g[None, :], final=last)
        if last:
            out = res
        else:
            x_lat, x_ctx, ctx_row0 = res, res, N_LAT
    return out.reshape(BATCH, SEQ, D_MODEL)
```

```python
import functools
import math

import jax
import jax.numpy as jnp
from jax import lax
import numpy as np
from jax.experimental import pallas as pl
from jax.experimental.pallas import tpu as pltpu

F32 = jnp.float32
BF16 = jnp.bfloat16
HIGHEST = lax.Precision.HIGHEST

D_MODEL = 1024
BATCH = 8
SEQ = 2048
DEPTH = 2
GRID_W = 64
CTX_LEN = 256
NORM_EPS = 1e-6

S5_WIDTH = D_MODEL // 4
S5_GROUP_CH = 16
S5_GROUPS = S5_WIDTH // S5_GROUP_CH
S5_STATE = 64
S5_LANES = S5_GROUPS * S5_STATE

MLA_HEADS = D_MODEL // 128
MLA_NOPE = 64
MLA_ROPE = 32
MLA_V = 64
MLA_Q_LORA = D_MODEL // 4
MLA_KV_LORA = D_MODEL // 8
MLA_SCALE = 1.0 / math.sqrt(MLA_NOPE + MLA_ROPE)
MLA_HEAD_PAD = 128
ROPE_PAIRS = MLA_ROPE // 4
ROPE_BASE = 10000.0

HG_HEADS = D_MODEL // 256
HG_K = 128
HG_V = 64
HG_CHUNK = 64
HG_LEVELS = 6
HG_SETS = 2 + HG_LEVELS
HG_FAST_CHUNK = 32
HG_FAST_MIN = -60.0

MOE_GROUPS = 4
MOE_EXPERTS_PER_GROUP = 8
MOE_EXPERTS = MOE_GROUPS * MOE_EXPERTS_PER_GROUP
MOE_TOP_K = 2
MOE_HIDDEN = D_MODEL // 2
MOE_ROWS = 512

N_LAT = BATCH * SEQ
N_CTX = BATCH * CTX_LEN
N_TOK = N_LAT + N_CTX
RB = CTX_LEN
WB = 2 * RB
NB_LAT = N_LAT // RB
NB_CTX = N_CTX // RB
NB_TOK = N_TOK // RB
LAT_BLOCKS = SEQ // RB
SEQ_BLOCKS = LAT_BLOCKS + 1

C_S5 = 0
C_CQ = 256
C_KV = 512
C_HQ = 768
C_ZF = 1280
C_ZB = 1792
C_HI = 2304
C_HG = 2560
C_GATE = 2816
W_IN_COLS = C_GATE + 3 * D_MODEL

VMEM_LIMIT = 56 * 1024 * 1024


def _params(*sem):
    return pltpu.CompilerParams(dimension_semantics=sem, vmem_limit_bytes=VMEM_LIMIT)


def _rms(v):
    return v * lax.rsqrt(jnp.mean(v * v, axis=-1, keepdims=True) + NORM_EPS)


def _mod_row(i, rb=RB):
    return jnp.where(i < N_LAT // rb, i // (SEQ // rb), BATCH)


def _resident(a):
    return pl.BlockSpec(a.shape, lambda *_: (0,) * a.ndim, pipeline_mode=pl.Buffered(1))


def _layer_spec(a, layer):
    return pl.BlockSpec((1,) + a.shape[1:], lambda *_: (layer,) + (0,) * (a.ndim - 1),
                        pipeline_mode=pl.Buffered(1))


def _split_row_specs(rb, w, ctx_row0):
    n_lat = N_LAT // rb
    lat = pl.BlockSpec((rb, w), lambda i, *_: (jnp.minimum(i, n_lat - 1), 0))
    ctx = pl.BlockSpec((rb, w), lambda i, *_: (ctx_row0 // rb + jnp.maximum(i - n_lat, 0), 0))
    return lat, ctx


def _pick_rows(lat_ref, ctx_ref, rb):
    return jnp.where(pl.program_id(0) < N_LAT // rb, lat_ref[...], ctx_ref[...])


def _mod_kernel(c_ref, w_ref, b_ref, o_ref):
    c = c_ref[...]
    s = c * jax.nn.sigmoid(c)
    o_ref[0] = jnp.dot(s, w_ref[0], precision=HIGHEST, preferred_element_type=F32) + b_ref[0]


def _modulation(c_rows, mod_w, mod_b):
    bn = 1536
    return pl.pallas_call(
        _mod_kernel,
        grid=(DEPTH, 6 * D_MODEL // bn),
        in_specs=[
            pl.BlockSpec((16, D_MODEL), lambda l, j: (0, 0)),
            pl.BlockSpec((1, D_MODEL, bn), lambda l, j: (l, 0, j)),
            pl.BlockSpec((1, 1, bn), lambda l, j: (l, 0, j)),
        ],
        out_specs=pl.BlockSpec((1, 16, bn), lambda l, j: (l, 0, j)),
        out_shape=jax.ShapeDtypeStruct((DEPTH, 16, 6 * D_MODEL), F32),
        compiler_params=_params("arbitrary", "arbitrary"),
        name="adaln_mod",
    )(c_rows, mod_w, mod_b.reshape(DEPTH, 1, 6 * D_MODEL))


def _log_sigmoid(z):
    return jnp.minimum(z, 0.0) - jnp.log1p(jnp.exp(-jnp.abs(z)))


def _in_kernel(hs_ref, xl_ref, xc_ref, mod_ref, g1_ref, w_ref, qag_ref, kvg_ref,
               s5u_ref, cqn_ref, kvn_ref, hq_ref, lff_ref, lfb_ref, kkf_ref, kkb_ref,
               hi_ref, hgs_ref, gates_ref, h_ref, *, layer):
    x = _pick_rows(xl_ref, xc_ref, WB)
    xn = _rms(x) * g1_ref[0]
    h_ref[...] = (xn * (1.0 + mod_ref[0, 1:2, :]) + mod_ref[0, 0:1, :]).astype(BF16)

    def mm(a, b):
        return jnp.dot(h_ref[...], w_ref[0, :, a:b], preferred_element_type=F32)

    s5u_ref[...] = mm(C_S5, C_CQ).astype(BF16)
    cqn_ref[...] = (_rms(mm(C_CQ, C_KV)) * qag_ref[0]).astype(BF16)
    kv = mm(C_KV, C_HQ)
    ckvn = _rms(kv[:, :MLA_KV_LORA]) * kvg_ref[0]
    kvn_ref[...] = jnp.concatenate([ckvn, kv[:, MLA_KV_LORA:]], axis=1).astype(BF16)
    hq_ref[...] = mm(C_HQ, C_ZF).astype(BF16)

    log_lb = hs_ref[3 * layer]
    log_1m_lb = hs_ref[3 * layer + 1]
    one_m_lb = hs_ref[3 * layer + 2]
    for c0, lf_ref, kk_ref in ((C_ZF, lff_ref, kkf_ref), (C_ZB, lfb_ref, kkb_ref)):
        z = mm(c0, c0 + HG_HEADS * HG_K)
        b = log_1m_lb + _log_sigmoid(z)
        m = jnp.maximum(b, log_lb)
        lf_ref[...] = m + jnp.log1p(jnp.exp(-jnp.abs(b - log_lb)))
        kk_ref[...] = (one_m_lb * jax.nn.sigmoid(-z)).astype(BF16)

    hi_ref[...] = mm(C_HI, C_HG).astype(BF16)
    g = mm(C_HG, C_GATE)
    hgs_ref[...] = (g * jax.nn.sigmoid(g)).astype(BF16)
    gw = 512
    for j in range(3 * D_MODEL // gw):
        c0 = C_GATE + j * gw
        gates_ref[:, j * gw:(j + 1) * gw] = jax.nn.sigmoid(mm(c0, c0 + gw)).astype(BF16)


def _input_projection(layer, x_lat, x_ctx, ctx_row0, mod3, hg_scal, g1, w_in, qa_g, kva_g):
    row = lambda w: pl.BlockSpec((WB, w), lambda i: (i, 0))
    x_specs = _split_row_specs(WB, D_MODEL, ctx_row0)
    shapes = [
        (S5_WIDTH, BF16), (MLA_Q_LORA, BF16), (256, BF16), (HG_HEADS * HG_K, BF16),
        (HG_HEADS * HG_K, F32), (HG_HEADS * HG_K, F32), (HG_HEADS * HG_K, BF16), (HG_HEADS * HG_K, BF16),
        (HG_HEADS * HG_V, BF16), (HG_HEADS * HG_V, BF16), (3 * D_MODEL, BF16),
    ]
    return pl.pallas_call(
        functools.partial(_in_kernel, layer=layer),
        grid=(N_TOK // WB,),
        in_specs=[
            pl.BlockSpec(memory_space=pltpu.SMEM),
            *x_specs,
            pl.BlockSpec((1, 6, D_MODEL), lambda i: (_mod_row(i, WB), 0, 0)),
            _layer_spec(g1, layer), _layer_spec(w_in, layer), _layer_spec(qa_g, layer), _layer_spec(kva_g, layer),
        ],
        out_specs=[row(w) for w, _ in shapes],
        out_shape=[jax.ShapeDtypeStruct((N_TOK, w), dt) for w, dt in shapes],
        scratch_shapes=[pltpu.VMEM((WB, D_MODEL), BF16)],
        compiler_params=_params("arbitrary"),
        name="input_projection",
    )(hg_scal, x_lat, x_ctx, mod3, g1, w_in, qa_g, kva_g)


def _seq_block(b, j):
    return jnp.where(j == 0, NB_LAT + b, b * LAT_BLOCKS + j - 1)


def _seq_block_rev(b, j):
    return jnp.where(j == 0, NB_LAT + b, b * LAT_BLOCKS + LAT_BLOCKS - j)


S5_TB = 128
S5_ROWS = S5_TB * BATCH
S5_STEPS = (CTX_LEN + SEQ) // S5_TB
S5_CTX_STEPS = CTX_LEN // S5_TB


def _s5_kernel(uf_ref, ub_ref, bmat_ref, cmat_ref, a_ref, yf_ref, yb_ref, bu_ref, carry_ref):
    j = pl.program_id(0)

    @pl.when(j == 0)
    def _():
        carry_ref[...] = jnp.zeros_like(carry_ref)

    for d, (u_ref, y_ref) in enumerate(((uf_ref, yf_ref), (ub_ref, yb_ref))):
        bu_ref[...] = jnp.dot(u_ref[...], bmat_ref[0, d], preferred_element_type=F32)
        are = a_ref[0, d, 0]
        aim = a_ref[0, d, 1]

        def body(t, carry, d=d, are=are, aim=aim):
            hre, him = carry
            tt = (S5_TB - 1 - t) if d == 1 else t
            r0 = pl.multiple_of(tt * BATCH, BATCH)
            nre = are * hre - aim * him + bu_ref[pl.ds(r0, BATCH), 0:S5_LANES]
            nim = are * him + aim * hre + bu_ref[pl.ds(r0, BATCH), S5_LANES:2 * S5_LANES]
            bu_ref[pl.ds(r0, BATCH), 0:S5_LANES] = nre
            bu_ref[pl.ds(r0, BATCH), S5_LANES:2 * S5_LANES] = nim
            return nre, nim

        hre, him = lax.fori_loop(0, S5_TB, body, (carry_ref[d, 0], carry_ref[d, 1]), unroll=4)
        carry_ref[d, 0] = hre
        carry_ref[d, 1] = him
        hl = S5_LANES // 2
        hw = S5_WIDTH // 2
        for half in range(2):
            states = jnp.concatenate([bu_ref[:, half * hl:(half + 1) * hl],
                                      bu_ref[:, S5_LANES + half * hl:S5_LANES + (half + 1) * hl]], axis=1)
            y_ref[:, half * hw:(half + 1) * hw] = jnp.dot(states.astype(BF16), cmat_ref[0, d, half],
                                                          preferred_element_type=F32)


def _s5_block_rev(j):
    return jnp.where(j < S5_CTX_STEPS, S5_CTX_STEPS - 1 - j, S5_STEPS - 1 + S5_CTX_STEPS - j)


def _s5_states(layer, u_tm, bmat, cmat, a_tab):
    full = lambda a: _layer_spec(a, layer)
    fwd = pl.BlockSpec((S5_ROWS, S5_WIDTH), lambda j: (j, 0))
    bwd = pl.BlockSpec((S5_ROWS, S5_WIDTH), lambda j: (_s5_block_rev(j), 0))
    return pl.pallas_call(
        _s5_kernel,
        grid=(S5_STEPS,),
        in_specs=[fwd, bwd, full(bmat), full(cmat), full(a_tab)],
        out_specs=[fwd, bwd],
        out_shape=[jax.ShapeDtypeStruct((S5_STEPS * S5_ROWS, S5_WIDTH), F32)] * 2,
        scratch_shapes=[pltpu.VMEM((S5_ROWS, 2 * S5_LANES), F32), pltpu.VMEM((2, 2, BATCH, S5_LANES), F32)],
        compiler_params=_params("arbitrary"),
        name="s5_scan",
    )(u_tm, u_tm, bmat, cmat, a_tab)


def _to_time_major(a):
    w = a.shape[1]
    seq = jnp.concatenate([a[N_LAT:].reshape(BATCH, CTX_LEN, w), a[:N_LAT].reshape(BATCH, SEQ, w)], axis=1)
    return seq.transpose(1, 0, 2).reshape((CTX_LEN + SEQ) * BATCH, w)


def _from_time_major(a):
    w = a.shape[1]
    seq = a.reshape(CTX_LEN + SEQ, BATCH, w).transpose(1, 0, 2)
    return jnp.concatenate([seq[:, CTX_LEN:].reshape(N_LAT, w), seq[:, :CTX_LEN].reshape(N_CTX, w)], axis=0)


def _qkv_kernel(cqn_ref, kvn_ref, cos_ref, sin_ref, wq_ref, wk_ref, wv_ref, q_ref, k_ref, v_ref):
    cos = jnp.concatenate([cos_ref[...]] * MLA_HEADS, axis=1)
    sin = jnp.concatenate([sin_ref[...]] * MLA_HEADS, axis=1)
    w = MLA_HEADS * MLA_HEAD_PAD
    q2 = jnp.dot(cqn_ref[...], wq_ref[0], preferred_element_type=F32)
    q_ref[...] = ((q2[:, :w] * cos + q2[:, w:] * sin) * MLA_SCALE).astype(BF16)
    kv = kvn_ref[...]
    k2 = jnp.dot(kv, wk_ref[0], preferred_element_type=F32)
    k_ref[...] = (k2[:, :w] * cos + k2[:, w:] * sin).astype(BF16)
    v_ref[...] = jnp.dot(kv, wv_ref[0], preferred_element_type=F32).astype(BF16)


def _qkv(layer, cqn, kvn, cos_t, sin_t, wq, wk, wv):
    row = lambda w: pl.BlockSpec((WB, w), lambda i: (i, 0))
    full = lambda a: _layer_spec(a, layer)
    pos = pl.BlockSpec((WB, MLA_HEAD_PAD),
                       lambda i: (jnp.where(i < N_LAT // WB, i % (SEQ // WB), SEQ // WB), 0))
    w = MLA_HEADS * MLA_HEAD_PAD
    return pl.pallas_call(
        _qkv_kernel,
        grid=(N_TOK // WB,),
        in_specs=[row(MLA_Q_LORA), row(256), pos, pos, full(wq), full(wk), full(wv)],
        out_specs=[row(w), row(w), row(MLA_HEADS * MLA_V)],
        out_shape=[jax.ShapeDtypeStruct((N_TOK, w), BF16), jax.ShapeDtypeStruct((N_TOK, w), BF16),
                   jax.ShapeDtypeStruct((N_TOK, MLA_HEADS * MLA_V), BF16)],
        compiler_params=_params("arbitrary"),
        name="mla_qkv",
    )(cqn, kvn, cos_t, sin_t, wq, wk, wv)


def _attn_kernel(*refs, n_kv):
    q_ref = refs[0]
    k_refs = refs[1:1 + n_kv]
    v_refs = refs[1 + n_kv:1 + 2 * n_kv]
    o_ref = refs[1 + 2 * n_kv]
    nt = (((1,), (1,)), ((), ()))
    for h in range(MLA_HEADS):
        q = q_ref[:, h * MLA_HEAD_PAD:(h + 1) * MLA_HEAD_PAD]
        s = [lax.dot_general(q, k_ref[:, h * MLA_HEAD_PAD:(h + 1) * MLA_HEAD_PAD], nt,
                             preferred_element_type=F32) for k_ref in k_refs]
        m = functools.reduce(jnp.maximum, [jnp.max(si, axis=-1, keepdims=True) for si in s])
        p = [jnp.exp(si - m) for si in s]
        l = functools.reduce(jnp.add, [jnp.sum(pi, axis=-1, keepdims=True) for pi in p])
        o = functools.reduce(jnp.add, [
            jnp.dot(pi.astype(BF16), v_ref[:, h * MLA_V:(h + 1) * MLA_V], preferred_element_type=F32)
            for pi, v_ref in zip(p, v_refs)])
        o_ref[:, h * MLA_V:(h + 1) * MLA_V] = (o / l).astype(BF16)


def _attention(q, k, v, latent):
    w = MLA_HEADS * MLA_HEAD_PAD
    wv = MLA_HEADS * MLA_V
    if latent:
        tq = WB
        nq = SEQ // tq
        q_spec = pl.BlockSpec((tq, w), lambda b, j: (b * nq + j, 0))
        k_specs = [pl.BlockSpec((SEQ, w), lambda b, j: (b, 0)),
                   pl.BlockSpec((CTX_LEN, w), lambda b, j: (NB_LAT + b, 0))]
        v_specs = [pl.BlockSpec((SEQ, wv), lambda b, j: (b, 0)),
                   pl.BlockSpec((CTX_LEN, wv), lambda b, j: (NB_LAT + b, 0))]
        o_spec = pl.BlockSpec((tq, wv), lambda b, j: (b * nq + j, 0))
        n_out = N_LAT
    else:
        tq = CTX_LEN
        nq = 1
        q_spec = pl.BlockSpec((tq, w), lambda b, j: (NB_LAT + b, 0))
        k_specs = [pl.BlockSpec((CTX_LEN, w), lambda b, j: (NB_LAT + b, 0))]
        v_specs = [pl.BlockSpec((CTX_LEN, wv), lambda b, j: (NB_LAT + b, 0))]
        o_spec = pl.BlockSpec((tq, wv), lambda b, j: (b, 0))
        n_out = N_CTX
    n_kv = len(k_specs)
    return pl.pallas_call(
        functools.partial(_attn_kernel, n_kv=n_kv),
        grid=(BATCH, nq),
        in_specs=[q_spec] + k_specs + v_specs,
        out_specs=o_spec,
        out_shape=jax.ShapeDtypeStruct((n_out, wv), BF16),
        compiler_params=_params("arbitrary", "arbitrary"),
        name="mla_attention_lat" if latent else "mla_attention_ctx",
    )(q, *([k] * n_kv), *([v] * n_kv))


def _hgrn_tables():
    c = HG_CHUNK
    w = np.zeros((2, HG_SETS * c, c), np.float32)
    mask = np.zeros((2, HG_LEVELS, c, c), np.float32)
    idx = np.arange(c)
    for r in range(c):
        w[0, r, :r + 1] = 1
        w[1, r, r:] = 1
        w[0, c + r, r + 1:] = 1
        w[1, c + r, :r] = 1
    for l in range(HG_LEVELS):
        m = c >> (l + 1)
        for r in range(c):
            base = (r // (2 * m)) * 2 * m
            mid = base + m
            later = r >= mid
            row = (2 + l) * c + r
            if later:
                w[0, row, mid:r + 1] = 1
                w[1, row, mid:r] = 1
            else:
                w[0, row, r + 1:mid] = 1
                w[1, row, r:mid] = 1
        same = (idx[:, None] // (2 * m)) == (idx[None, :] // (2 * m))
        q_later = (idx[:, None] % (2 * m)) >= m
        k_later = (idx[None, :] % (2 * m)) >= m
        mask[0, l] = same & q_later & ~k_later
        mask[1, l] = same & ~q_later & k_later
    return w, mask


def _hgrn_dir(d, q_ref, k_ref, lf_ref, v_ref, w_ref, mask_ref, st_ref, o_ref):
    c = HG_CHUNK
    n_chunks = RB // c
    nt = (((1,), (1,)), ((), ()))
    tn = (((0,), (0,)), ((), ()))
    last = c - 1 if d == 0 else 0
    wsel = w_ref[d]
    e = []
    for ci in range(n_chunks):
        g = lf_ref[ci * c:(ci + 1) * c, :]
        g_hi = g.astype(BF16)
        g_lo = (g - g_hi.astype(F32)).astype(BF16)
        expo = jnp.dot(wsel, jnp.concatenate([g_hi, g_lo], axis=0), preferred_element_type=F32)
        e.append(jnp.exp(expo))

    def rows(s):
        return jnp.concatenate([e[ci][s * c:(s + 1) * c] for ci in range(n_chunks)], axis=0)

    q = q_ref[...].astype(F32)
    k = k_ref[...].astype(F32)
    q_state = (q * rows(0)).astype(BF16)
    k_state = (k * rows(1)).astype(BF16)
    q_lvl, k_lvl = [], []
    for l in range(HG_LEVELS):
        el = rows(2 + l)
        q_lvl.append((q * el).astype(BF16))
        k_lvl.append((k * el).astype(BF16))
    qk = q * k
    order = range(n_chunks) if d == 0 else range(n_chunks - 1, -1, -1)
    for h in range(HG_HEADS):
        ks = slice(h * HG_K, (h + 1) * HG_K)
        vs = slice(h * HG_V, (h + 1) * HG_V)
        v = v_ref[:, vs]
        scores = jnp.zeros((RB, RB), F32)
        for l in range(HG_LEVELS):
            p = lax.dot_general(q_lvl[l][:, ks], k_lvl[l][:, ks], nt, preferred_element_type=F32)
            scores = scores + p * mask_ref[d, l]
        o = jnp.dot(scores.astype(BF16), v, preferred_element_type=F32)
        o = o + jnp.sum(qk[:, ks], axis=-1, keepdims=True) * v.astype(F32)
        st = st_ref[d, h]
        o_state = [None] * n_chunks
        for ci in order:
            r = slice(ci * c, (ci + 1) * c)
            o_state[ci] = lax.dot_general(q_state[r, ks], st.astype(BF16), nt, preferred_element_type=F32)
            inc = lax.dot_general(v[r], k_state[r, ks], tn, preferred_element_type=F32)
            st = st * e[ci][last:last + 1, ks] + inc
        st_ref[d, h] = st
        o_ref[:, vs] = o + jnp.concatenate(o_state, axis=0)


def _hgrn_fast_tables():
    c = HG_FAST_CHUNK
    idx = np.arange(RB)
    same = (idx[:, None] // c) == (idx[None, :] // c)
    le = idx[None, :] <= idx[:, None]
    ge = idx[None, :] >= idx[:, None]
    wc = np.stack([same & le, same & ge]).astype(np.float32)
    col_chunk = np.arange((RB // c) * HG_V) // HG_V
    vmask = ((idx[:, None] // c) == col_chunk[None, :]).astype(np.float32)
    return np.concatenate([wc, wc], axis=-1), wc, vmask


def _hgrn_dir_fast(d, q_ref, k_ref, cum, tot, v_ref, maskc_ref, vmask_ref, st_ref, o_ref):
    c = HG_FAST_CHUNK
    n_chunks = RB // c
    nt = (((1,), (1,)), ((), ()))
    tn = (((0,), (0,)), ((), ()))
    q = q_ref[...].astype(F32)
    k = k_ref[...].astype(F32)
    qa = (q * jnp.exp(cum)).astype(BF16)
    kb = (k * jnp.exp(-cum)).astype(BF16)
    k_state = (k * jnp.exp(tot - cum)).astype(BF16)
    decay = jnp.exp(tot)
    keep = maskc_ref[d] > 0.5
    vmask = vmask_ref[...]
    order = range(n_chunks) if d == 0 else range(n_chunks - 1, -1, -1)
    for h in range(HG_HEADS):
        ks = slice(h * HG_K, (h + 1) * HG_K)
        vs = slice(h * HG_V, (h + 1) * HG_V)
        v = v_ref[:, vs]
        p = lax.dot_general(qa[:, ks], kb[:, ks], nt, preferred_element_type=F32)
        o = jnp.dot(jnp.where(keep, p, 0.0).astype(BF16), v, preferred_element_type=F32)
        v_by_chunk = jnp.concatenate([v] * n_chunks, axis=1) * vmask
        inc = lax.dot_general(v_by_chunk, k_state[:, ks], tn, preferred_element_type=F32)
        st = st_ref[d, h]
        entering = [None] * n_chunks
        for ci in order:
            entering[ci] = st
            st = st * decay[ci * c:ci * c + 1, ks] + inc[ci * HG_V:(ci + 1) * HG_V]
        st_ref[d, h] = st
        s_all = jnp.concatenate(entering, axis=0).astype(BF16)
        o_all = lax.dot_general(qa[:, ks], s_all, nt, preferred_element_type=F32) * vmask.astype(F32)
        o_fold = functools.reduce(jnp.add, [o_all[:, j * 128:(j + 1) * 128] for j in range(n_chunks * HG_V // 128)])
        o_state = (o_fold + pltpu.roll(o_fold, HG_V, 1))[:, :HG_V]
        o_ref[:, vs] = o + o_state


def _hgrn_kernel(qf_ref, kf_ref, lff_ref, vf_ref, qb_ref, kb_ref, lfb_ref, vb_ref, w_ref, mask_ref,
                 wc_ref, maskc_ref, vmask_ref, of_ref, ob_ref, st_ref):
    j = pl.program_id(1)

    @pl.when(j == 0)
    def _():
        st_ref[...] = jnp.zeros_like(st_ref)

    c = HG_FAST_CHUNK
    dirs = ((0, qf_ref, kf_ref, lff_ref, vf_ref, of_ref), (1, qb_ref, kb_ref, lfb_ref, vb_ref, ob_ref))
    for d, q_ref, k_ref, lf_ref, v_ref, o_ref in dirs:
        g_hi, g_lo = _split_bf16(lf_ref[...])
        cum = jnp.dot(wc_ref[d], jnp.concatenate([g_hi, g_lo], axis=0), preferred_element_type=F32)
        edge = c - 1 if d == 0 else 0
        totals = [cum[ci * c + edge:ci * c + edge + 1] for ci in range(RB // c)]
        tot = jnp.concatenate([jnp.broadcast_to(t, (c, t.shape[1])) for t in totals], axis=0)
        safe = jnp.min(jnp.concatenate(totals, axis=0)) >= HG_FAST_MIN

        @pl.when(safe)
        def _(d=d, q_ref=q_ref, k_ref=k_ref, v_ref=v_ref, o_ref=o_ref, cum=cum, tot=tot):
            _hgrn_dir_fast(d, q_ref, k_ref, cum, tot, v_ref, maskc_ref, vmask_ref, st_ref, o_ref)

        @pl.when(jnp.logical_not(safe))
        def _(d=d, q_ref=q_ref, k_ref=k_ref, lf_ref=lf_ref, v_ref=v_ref, o_ref=o_ref):
            _hgrn_dir(d, q_ref, k_ref, lf_ref, v_ref, w_ref, mask_ref, st_ref, o_ref)


def _hgrn(hq, kkf, kkb, lff, lfb, hi, w_tab, mask_tab, wc_tab, maskc_tab, vmask_tab):
    full = _resident
    fwd = lambda w: pl.BlockSpec((RB, w), lambda b, j: (_seq_block(b, j), 0))
    bwd = lambda w: pl.BlockSpec((RB, w), lambda b, j: (_seq_block_rev(b, j), 0))
    wk = HG_HEADS * HG_K
    wv = HG_HEADS * HG_V
    return pl.pallas_call(
        _hgrn_kernel,
        grid=(BATCH, SEQ_BLOCKS),
        in_specs=[fwd(wk), fwd(wk), fwd(wk), fwd(wv), bwd(wk), bwd(wk), bwd(wk), bwd(wv),
                  full(w_tab), full(mask_tab), full(wc_tab), full(maskc_tab), full(vmask_tab)],
        out_specs=[fwd(wv), bwd(wv)],
        out_shape=[jax.ShapeDtypeStruct((N_TOK, wv), F32)] * 2,
        scratch_shapes=[pltpu.VMEM((2, HG_HEADS, HG_V, HG_K), F32)],
        compiler_params=_params("arbitrary", "arbitrary"),
        name="hgrn2_scan",
    )(hq, kkf, lff, hi, hq, kkb, lfb, hi, w_tab, mask_tab, wc_tab, maskc_tab, vmask_tab)


def _gelu_tanh(x):
    return 0.5 * x * (1.0 + jnp.tanh(math.sqrt(2.0 / math.pi) * (x + 0.044715 * (x * x * x))))


def _split_bf16(a):
    hi = a.astype(BF16)
    return hi, (a - hi.astype(F32)).astype(BF16)


def _merge_kernel(xl_ref, xc_ref, mod_ref, yf_ref, yb_ref, u_ref, of_ref, ob_ref, hgs_ref, attl_ref, attc_ref,
                  gates_ref, s5d_ref, gluw_ref, glub_ref, hgg_ref, hsum_ref, wpa_ref, wpb_ref, wpc_ref, wout_ref,
                  g2_ref, wrh_ref, wrl_ref, br_ref,
                  x1_ref, h2_ref, eid_ref, gate_ref):
    y = yf_ref[...] + yb_ref[...] + s5d_ref[0] * u_ref[...].astype(F32)
    y = _gelu_tanh(y)
    y = y * jax.nn.sigmoid(jnp.dot(y.astype(BF16), gluw_ref[0], preferred_element_type=F32) + glub_ref[0])
    o = of_ref[...] + ob_ref[...]
    sq_hi, sq_lo = _split_bf16(o * o)
    ms = (jnp.dot(sq_hi, hsum_ref[...], preferred_element_type=F32)
          + jnp.dot(sq_lo, hsum_ref[...], preferred_element_type=F32)) * (1.0 / HG_V)
    o = o * lax.rsqrt(ms + NORM_EPS) * hgg_ref[0] * hgs_ref[...].astype(F32)

    d = D_MODEL
    att = _pick_rows(attl_ref, attc_ref, WB)
    merged = (gates_ref[:, 0:d].astype(F32) * jnp.dot(y.astype(BF16), wpa_ref[0], preferred_element_type=F32)
              + gates_ref[:, d:2 * d].astype(F32) * jnp.dot(att, wpb_ref[0], preferred_element_type=F32)
              + gates_ref[:, 2 * d:3 * d].astype(F32) * jnp.dot(o.astype(BF16), wpc_ref[0], preferred_element_type=F32))
    y_out = jnp.dot(merged.astype(BF16), wout_ref[0], preferred_element_type=F32)
    x1 = _pick_rows(xl_ref, xc_ref, WB) + mod_ref[0, 2:3, :] * y_out
    x1_ref[...] = x1
    h2 = _rms(x1) * g2_ref[0] * (1.0 + mod_ref[0, 4:5, :]) + mod_ref[0, 3:4, :]
    h2_ref[...] = h2

    h_hi, h_lo = _split_bf16(h2)
    logits = (jnp.dot(h_hi, wrh_ref[0], preferred_element_type=F32)
              + jnp.dot(h_lo, wrh_ref[0], preferred_element_type=F32)
              + jnp.dot(h_hi, wrl_ref[0], preferred_element_type=F32)) + br_ref[0]
    lane = lax.broadcasted_iota(jnp.int32, logits.shape, 1).astype(F32)
    neg = -jnp.inf
    glog = jnp.where(lane < MOE_GROUPS, logits, neg)
    gmax = jnp.max(glog, axis=-1, keepdims=True)
    gidx = jnp.min(jnp.where(glog == gmax, lane, 1e9), axis=-1, keepdims=True)
    g_w = 1.0 / jnp.sum(jnp.exp(glog - gmax), axis=-1, keepdims=True)
    e_lo = 32.0 + gidx * MOE_EXPERTS_PER_GROUP
    elog = jnp.where((lane >= e_lo) & (lane < e_lo + MOE_EXPERTS_PER_GROUP), logits, neg)
    v1 = jnp.max(elog, axis=-1, keepdims=True)
    i1 = jnp.min(jnp.where(elog == v1, lane, 1e9), axis=-1, keepdims=True)
    elog2 = jnp.where(lane == i1, neg, elog)
    v2 = jnp.max(elog2, axis=-1, keepdims=True)
    i2 = jnp.min(jnp.where(elog2 == v2, lane, 1e9), axis=-1, keepdims=True)
    e2 = jnp.exp(v2 - v1)
    gate1 = g_w / (1.0 + e2)
    gate2 = g_w * e2 / (1.0 + e2)
    eid_ref[...] = jnp.where(lane == 0.0, i1 - 32.0, jnp.where(lane == 1.0, i2 - 32.0, 0.0)).astype(jnp.int32)
    gate_ref[...] = jnp.where(lane == 0.0, gate1, jnp.where(lane == 1.0, gate2, 0.0))


def _merge(layer, n_blocks, x_lat, x_ctx, ctx_row0, mod3, yf, yb, s5u, of, ob, hgs, att_lat, att_ctx, gates,
           head_sum, consts):
    row = lambda w: pl.BlockSpec((WB, w), lambda i: (i, 0))
    n = n_blocks * RB
    wv = MLA_HEADS * MLA_V
    if att_ctx is None:
        att_ctx = att_lat
    const_specs = [_layer_spec(a, layer) for a in consts]
    const_specs.insert(4, _resident(head_sum))
    return pl.pallas_call(
        _merge_kernel,
        grid=(n // WB,),
        in_specs=[*_split_row_specs(WB, D_MODEL, ctx_row0),
                  pl.BlockSpec((1, 6, D_MODEL), lambda i: (_mod_row(i, WB), 0, 0)),
                  row(S5_WIDTH), row(S5_WIDTH), row(S5_WIDTH), row(HG_HEADS * HG_V), row(HG_HEADS * HG_V),
                  row(HG_HEADS * HG_V), *_split_row_specs(WB, wv, 0), row(3 * D_MODEL)] + const_specs,
        out_specs=[row(D_MODEL), row(D_MODEL), row(128), row(128)],
        out_shape=[jax.ShapeDtypeStruct((n, D_MODEL), F32), jax.ShapeDtypeStruct((n, D_MODEL), F32),
                   jax.ShapeDtypeStruct((n, 128), jnp.int32), jax.ShapeDtypeStruct((n, 128), F32)],
        compiler_params=_params("arbitrary"),
        name="merge_router",
    )(x_lat, x_ctx, mod3, yf, yb, s5u, of, ob, hgs, att_lat, att_ctx, gates,
      *consts[:4], head_sum, *consts[4:])


def _moe_kernel(bexp_ref, nblk_ref, src_ref, srcn_ref, dst_ref, h_hbm, w1_ref, w3_ref, w2_ref, y_hbm,
                xbuf, ybuf, w1b, w3b, w2b, gsem, ssem):
    i = pl.program_id(0)
    n_used = nblk_ref[0]
    slot = lax.rem(i, 2)

    def gather_start(idx_ref, s):
        for r in range(MOE_ROWS):
            pltpu.make_async_copy(h_hbm.at[pl.ds(idx_ref[0, 0, r], 1), :], xbuf.at[s, pl.ds(r, 1), :],
                                  gsem.at[s]).start()

    def gather_wait(s):
        pltpu.make_async_copy(h_hbm.at[pl.ds(0, MOE_ROWS), :], xbuf.at[s], gsem.at[s]).wait()

    def scatter_start(s):
        for r in range(MOE_ROWS):
            pltpu.make_async_copy(ybuf.at[s, pl.ds(r, 1), :], y_hbm.at[pl.ds(dst_ref[0, 0, r], 1), :],
                                  ssem.at[s]).start()

    def scatter_wait(s):
        pltpu.make_async_copy(ybuf.at[s], y_hbm.at[pl.ds(0, MOE_ROWS), :], ssem.at[s]).wait()

    @pl.when(i < n_used)
    def _():
        @pl.when(i == 0)
        def _():
            gather_start(src_ref, 0)
            ybuf[1] = jnp.zeros((MOE_ROWS, D_MODEL), F32)
            n_real = y_hbm.shape[0] - 2 * MOE_ROWS
            for half in range(2):
                fill = pltpu.make_async_copy(ybuf.at[1], y_hbm.at[pl.ds(n_real + half * MOE_ROWS, MOE_ROWS), :],
                                             ssem.at[1])
                fill.start()
                fill.wait()

        @pl.when(i + 1 < n_used)
        def _():
            gather_start(srcn_ref, 1 - slot)

        @pl.when((i == 0) | (bexp_ref[i] != bexp_ref[jnp.maximum(i - 1, 0)]))
        def _():
            w1b[...] = w1_ref[0, 0].astype(BF16)
            w3b[...] = w3_ref[0, 0].astype(BF16)
            w2b[...] = w2_ref[0, 0].astype(BF16)

        gather_wait(slot)

        @pl.when(i >= 2)
        def _():
            scatter_wait(slot)

        x = xbuf[slot].astype(BF16)
        a = jnp.dot(x, w1b[...], preferred_element_type=F32)
        g = jnp.dot(x, w3b[...], preferred_element_type=F32)
        hid = (a * jax.nn.sigmoid(a) * g).astype(BF16)
        ybuf[slot] = jnp.dot(hid, w2b[...], preferred_element_type=F32)
        scatter_start(slot)

        @pl.when(i == n_used - 1)
        def _():
            @pl.when(i >= 1)
            def _():
                scatter_wait(1 - slot)
            scatter_wait(slot)


def _moe(h2, eid, w1, w3, w2, layer):
    n = h2.shape[0]
    n_assign = n * MOE_TOP_K
    n_blocks = (n_assign + MOE_EXPERTS * (MOE_ROWS - 1) + MOE_ROWS - 1) // MOE_ROWS
    flat_e = eid.reshape(n_assign)
    order = jnp.argsort(flat_e, stable=True).astype(jnp.int32)
    experts = jnp.arange(MOE_EXPERTS, dtype=jnp.int32)
    counts = jnp.sum((flat_e[:, None] == experts[None, :]).astype(jnp.int32), axis=0)
    starts = jnp.cumsum(counts) - counts
    padded = (counts + MOE_ROWS - 1) // MOE_ROWS * MOE_ROWS
    p_ends = jnp.cumsum(padded)
    p_starts = p_ends - padded
    n_used_s = p_ends[-1] // MOE_ROWS
    n_used = n_used_s.astype(jnp.int32).reshape(1)
    blk = jnp.arange(n_blocks, dtype=jnp.int32)
    blk_expert = jnp.sum((p_ends[None, :] <= (blk * MOE_ROWS)[:, None]).astype(jnp.int32), axis=1)
    last_expert = jnp.max(jnp.where(counts > 0, experts, 0))
    block_expert = jnp.where(blk < n_used_s, jnp.minimum(blk_expert, MOE_EXPERTS - 1), last_expert)
    onehot = (block_expert[:, None] == experts[None, :]).astype(jnp.int32)
    pick = lambda table: jnp.sum(onehot * table[None, :], axis=1)
    local = jnp.arange(MOE_ROWS, dtype=jnp.int32)[None, :]
    rank = blk[:, None] * MOE_ROWS + local - pick(p_starts)[:, None]
    valid = (rank < pick(counts)[:, None]) & (blk < n_used_s)[:, None]
    pos = jnp.clip(pick(starts)[:, None] + rank, 0, n_assign - 1)
    assign = order[pos]
    tok = jnp.where(valid, assign // MOE_TOP_K, 0)
    dst = jnp.where(valid, (assign % MOE_TOP_K) * n + assign // MOE_TOP_K,
                    n_assign + (blk % 2)[:, None] * MOE_ROWS + local)
    rows_now = pl.BlockSpec((1, 1, MOE_ROWS), lambda i, be, nb: (i, 0, 0), memory_space=pltpu.SMEM)
    rows_next = pl.BlockSpec((1, 1, MOE_ROWS), lambda i, be, nb: (jnp.minimum(i + 1, n_blocks - 1), 0, 0),
                             memory_space=pltpu.SMEM)
    src3 = tok.reshape(n_blocks, 1, MOE_ROWS)
    return pl.pallas_call(
        _moe_kernel,
        grid_spec=pltpu.PrefetchScalarGridSpec(
            num_scalar_prefetch=2,
            grid=(n_blocks,),
            in_specs=[
                rows_now, rows_next, rows_now,
                pl.BlockSpec(memory_space=pl.ANY),
                pl.BlockSpec((1, 1, D_MODEL, MOE_HIDDEN), lambda i, be, nb: (layer, be[i], 0, 0)),
                pl.BlockSpec((1, 1, D_MODEL, MOE_HIDDEN), lambda i, be, nb: (layer, be[i], 0, 0)),
                pl.BlockSpec((1, 1, MOE_HIDDEN, D_MODEL), lambda i, be, nb: (layer, be[i], 0, 0)),
            ],
            out_specs=pl.BlockSpec(memory_space=pl.ANY),
            scratch_shapes=[pltpu.VMEM((2, MOE_ROWS, D_MODEL), F32), pltpu.VMEM((2, MOE_ROWS, D_MODEL), F32),
                            pltpu.VMEM((D_MODEL, MOE_HIDDEN), BF16), pltpu.VMEM((D_MODEL, MOE_HIDDEN), BF16),
                            pltpu.VMEM((MOE_HIDDEN, D_MODEL), BF16),
                            pltpu.SemaphoreType.DMA((2,)), pltpu.SemaphoreType.DMA((2,))],
        ),
        out_shape=jax.ShapeDtypeStruct((n_assign + 2 * MOE_ROWS, D_MODEL), F32),
        compiler_params=_params("arbitrary"),
        name="moe_experts",
    )(block_expert, n_used, src3, src3, dst.reshape(n_blocks, 1, MOE_ROWS), h2, w1, w3, w2)


def _combine_kernel(x1_ref, y0_ref, y1_ref, gate_ref, mod_ref, g_ref, o_ref, *, final):
    f = gate_ref[:, 0:1] * y0_ref[...] + gate_ref[:, 1:2] * y1_ref[...]
    x2 = x1_ref[...] + mod_ref[0, 5:6, :] * f
    if final:
        x2 = _rms(x2) * g_ref[...]
    o_ref[...] = x2


def _combine(n_blocks, x1, y, gate, mod3, g_final, final):
    row = lambda w: pl.BlockSpec((RB, w), lambda i: (i, 0))
    n = n_blocks * RB
    return pl.pallas_call(
        functools.partial(_combine_kernel, final=final),
        grid=(n_blocks,),
        in_specs=[row(D_MODEL), row(D_MODEL), pl.BlockSpec((RB, D_MODEL), lambda i: (i + n_blocks, 0)), row(128),
                  pl.BlockSpec((1, 6, D_MODEL), lambda i: (_mod_row(i), 0, 0)),
                  pl.BlockSpec((1, D_MODEL), lambda i: (0, 0))],
        out_specs=row(D_MODEL),
        out_shape=jax.ShapeDtypeStruct((n, D_MODEL), F32),
        compiler_params=_params("arbitrary"),
        name="moe_combine_final" if final else "moe_combine",
    )(x1, y, y, gate, mod3, g_final)


def _rope_rot_cols(w):
    p = ROPE_PAIRS
    return jnp.concatenate([-w[..., p:2 * p], w[..., 0:p], -w[..., 3 * p:4 * p], w[..., 2 * p:3 * p]], axis=-1)


def _pack_w_in(w_in):
    cuts = np.cumsum([S5_WIDTH, MLA_Q_LORA, MLA_KV_LORA, MLA_ROPE, HG_HEADS * HG_K, HG_HEADS * HG_K,
                      HG_HEADS * HG_K, HG_HEADS * HG_V, HG_HEADS * HG_V, D_MODEL, D_MODEL])
    (w_s5, w_cq, w_ckv, w_kpe, w_hq, w_zf, w_zb, w_hi, w_hg, w_ga, w_gb, w_gc) = jnp.split(w_in, cuts, axis=-1)
    pad = jnp.zeros(w_in.shape[:-1] + (256 - MLA_KV_LORA - 2 * MLA_ROPE,), w_in.dtype)
    packed = jnp.concatenate([w_s5, w_cq, w_ckv, w_kpe, _rope_rot_cols(w_kpe), pad,
                              w_hq, w_zf, w_zb, w_hi, w_hg, w_ga, w_gb, w_gc], axis=-1)
    assert packed.shape[-1] == W_IN_COLS
    return packed.astype(BF16)


def _pack_mla(w_uq, w_uk, w_uv):
    hp, n, r = MLA_HEAD_PAD, MLA_NOPE, MLA_ROPE
    nl = w_uq.shape[0]
    wq = w_uq.reshape(nl, MLA_Q_LORA, MLA_HEADS, n + r)
    zq = jnp.zeros((nl, MLA_Q_LORA, MLA_HEADS, hp - n - r), F32)
    q1 = jnp.concatenate([wq, zq], axis=-1).reshape(nl, MLA_Q_LORA, MLA_HEADS * hp)
    q2 = jnp.concatenate([jnp.zeros((nl, MLA_Q_LORA, MLA_HEADS, n), F32), _rope_rot_cols(wq[..., n:]), zq],
                         axis=-1).reshape(nl, MLA_Q_LORA, MLA_HEADS * hp)
    wq_packed = jnp.concatenate([q1, q2], axis=-1).astype(BF16)

    wk = w_uk.reshape(nl, MLA_KV_LORA, MLA_HEADS, n)
    k_nope = jnp.concatenate([wk, jnp.zeros((nl, MLA_KV_LORA, MLA_HEADS, hp - n), F32)], axis=-1)
    pe_slot = np.concatenate([np.zeros((r, n), np.float32), np.eye(r, dtype=np.float32),
                              np.zeros((r, hp - n - r), np.float32)], axis=-1)
    pe_all = jnp.asarray(np.tile(pe_slot[None, :, None, :], (nl, 1, MLA_HEADS, 1)))
    zero_pe = jnp.zeros_like(pe_all)
    tail = jnp.zeros((nl, 256 - MLA_KV_LORA - 2 * r, MLA_HEADS, hp), F32)
    k1 = jnp.concatenate([k_nope, pe_all, zero_pe, tail], axis=1).reshape(nl, 256, MLA_HEADS * hp)
    k2 = jnp.concatenate([jnp.zeros_like(k_nope), zero_pe, pe_all, tail], axis=1).reshape(nl, 256, MLA_HEADS * hp)
    wk_packed = jnp.concatenate([k1, k2], axis=-1).astype(BF16)
    wv_packed = jnp.concatenate([w_uv, jnp.zeros((nl, 256 - MLA_KV_LORA, MLA_HEADS * MLA_V), F32)],
                                axis=1).astype(BF16)
    return wq_packed, wk_packed, wv_packed


def _rope_tables():
    rows = SEQ // GRID_W
    row = np.repeat(np.arange(rows, dtype=np.float32), GRID_W)
    col = np.tile(np.arange(GRID_W, dtype=np.float32), rows)
    inv = (np.float32(ROPE_BASE) ** (-np.arange(ROPE_PAIRS, dtype=np.float32) / np.float32(ROPE_PAIRS))).astype(np.float32)
    ar, ac = row[:, None] * inv, col[:, None] * inv
    cos = np.concatenate([np.cos(ar), np.cos(ar), np.cos(ac), np.cos(ac)], axis=1)
    sin = np.concatenate([np.sin(ar), np.sin(ar), np.sin(ac), np.sin(ac)], axis=1)
    tail = MLA_HEAD_PAD - MLA_NOPE - MLA_ROPE
    cos_t = np.concatenate([np.ones((SEQ, MLA_NOPE)), cos, np.ones((SEQ, tail))], axis=1)
    sin_t = np.concatenate([np.zeros((SEQ, MLA_NOPE)), sin, np.zeros((SEQ, tail))], axis=1)
    cos_t = np.concatenate([cos_t, np.ones((WB, MLA_HEAD_PAD))], axis=0)
    sin_t = np.concatenate([sin_t, np.zeros((WB, MLA_HEAD_PAD))], axis=0)
    return jnp.asarray(cos_t, F32), jnp.asarray(sin_t, F32)


def _pack_s5(lam_re, lam_im, b_re, b_im, c_re, c_im, log_step):
    eye = jnp.eye(S5_GROUPS, dtype=F32)
    dt = jnp.exp(log_step)[..., None]
    mag = jnp.exp(lam_re * dt)
    lb_re, lb_im = mag * jnp.cos(lam_im * dt), mag * jnp.sin(lam_im * dt)
    den = lam_re * lam_re + lam_im * lam_im
    fr = ((lb_re - 1) * lam_re + lb_im * lam_im) / den
    fi = (lb_im * lam_re - (lb_re - 1) * lam_im) / den
    bb_re = fr[..., None] * b_re - fi[..., None] * b_im
    bb_im = fr[..., None] * b_im + fi[..., None] * b_re
    lead = lam_re.shape[:2]

    def in_mat(bb):
        return jnp.einsum('ldgph,gk->ldghkp', bb, eye).reshape(lead + (S5_WIDTH, S5_LANES))

    def out_mat(cc):
        return jnp.einsum('ldghp,gk->ldgpkh', cc, eye).reshape(lead + (S5_LANES, S5_WIDTH))

    bmat = jnp.concatenate([in_mat(bb_re), in_mat(bb_im)], axis=-1).astype(BF16)
    hl, hw = S5_LANES // 2, S5_WIDTH // 2
    cre, cim = out_mat(c_re), -out_mat(c_im)
    cmat = jnp.stack([jnp.concatenate([cre[..., h * hl:(h + 1) * hl, h * hw:(h + 1) * hw],
                                       cim[..., h * hl:(h + 1) * hl, h * hw:(h + 1) * hw]], axis=-2)
                      for h in range(2)], axis=2).astype(BF16)
    a_tab = jnp.stack([jnp.broadcast_to(lb_re.reshape(lead + (1, S5_LANES)), lead + (BATCH, S5_LANES)),
                       jnp.broadcast_to(lb_im.reshape(lead + (1, S5_LANES)), lead + (BATCH, S5_LANES))], axis=2)
    return bmat, cmat, a_tab


def kernel(x, c, ctx, c_ctx, mod_w, mod_b, norm1_g, norm2_g, w_in, s5_lam_re, s5_lam_im, s5_b_re, s5_b_im, s5_c_re, s5_c_im, s5_log_step, s5_d, s5_glu_w, s5_glu_b, mla_qa_g, mla_kva_g, mla_w_uq, mla_w_uk, mla_w_uv, hg_lb_logits, hg_norm_g, w_pa, w_pb, w_pc, w_out, moe_w_group, moe_b_group, moe_w_expert, moe_b_expert, moe_w1, moe_w3, moe_w2, final_norm_g):
    x_lat, x_ctx, ctx_row0 = x.reshape(N_LAT, D_MODEL), ctx.reshape(N_CTX, D_MODEL), 0
    c_rows = jnp.concatenate([c, c_ctx[None, :], jnp.zeros((16 - BATCH - 1, D_MODEL), F32)], axis=0)
    mod = _modulation(c_rows, mod_w, mod_b).reshape(DEPTH, 16, 6, D_MODEL)

    lb_all = jnp.cumsum(jax.nn.softmax(hg_lb_logits.astype(F32)))
    lb_all = lb_all - lb_all[0]
    hg_scal = jnp.stack([jnp.log(lb_all), jnp.log1p(-lb_all), 1.0 - lb_all], axis=1).reshape(3 * DEPTH).astype(F32)
    cos_t, sin_t = _rope_tables()

    vec = lambda a: a[:, None, :]
    w_in_p = _pack_w_in(w_in)
    bmat, cmat, a_tab = _pack_s5(s5_lam_re, s5_lam_im, s5_b_re, s5_b_im, s5_c_re, s5_c_im, s5_log_step)
    wq, wk, wv = _pack_mla(mla_w_uq, mla_w_uk, mla_w_uv)
    w_route = jnp.concatenate([moe_w_group, jnp.zeros((DEPTH, D_MODEL, 32 - MOE_GROUPS), F32), moe_w_expert,
                               jnp.zeros((DEPTH, D_MODEL, 128 - 32 - MOE_EXPERTS), F32)], axis=-1)
    b_route = jnp.concatenate([moe_b_group, jnp.zeros((DEPTH, 32 - MOE_GROUPS), F32), moe_b_expert,
                               jnp.zeros((DEPTH, 128 - 32 - MOE_EXPERTS), F32)], axis=-1)
    w_route_hi = w_route.astype(BF16)
    w_route_lo = (w_route - w_route_hi.astype(F32)).astype(BF16)
    merge_consts = [vec(s5_d), s5_glu_w.astype(BF16), vec(s5_glu_b), vec(hg_norm_g),
                    w_pa.astype(BF16), w_pb.astype(BF16), w_pc.astype(BF16), w_out.astype(BF16), vec(norm2_g),
                    w_route_hi, w_route_lo, vec(b_route)]
    w_tab_np, mask_tab_np = _hgrn_tables()
    w_tab = jnp.asarray(np.concatenate([w_tab_np, w_tab_np], axis=-1), BF16)
    mask_tab = jnp.asarray(np.kron(np.eye(RB // HG_CHUNK, dtype=np.float32), mask_tab_np), F32)
    head_sum = jnp.asarray(np.kron(np.eye(HG_HEADS), np.ones((HG_V, HG_V))), F32)
    wc_np, maskc_np, vmask_np = _hgrn_fast_tables()
    wc_tab, maskc_tab, vmask_tab = jnp.asarray(wc_np, BF16), jnp.asarray(maskc_np, F32), jnp.asarray(vmask_np, BF16)

    out = None
    for layer in range(DEPTH):
        last = layer == DEPTH - 1
        mod3 = mod[layer]
        (s5u, cqn, kvn, hq, lff, lfb, kkf, kkb, hi, hgs, gates) = _input_projection(
            layer, x_lat, x_ctx, ctx_row0, mod3, hg_scal, vec(norm1_g), w_in_p, vec(mla_qa_g), vec(mla_kva_g))

        yf, yb = [_from_time_major(y) for y in _s5_states(layer, _to_time_major(s5u), bmat, cmat, a_tab)]

        q, k, v = _qkv(layer, cqn, kvn, cos_t, sin_t, wq, wk, wv)
        att_lat = _attention(q, k, v, latent=True)
        att_ctx = None if last else _attention(q, k, v, latent=False)

        of, ob = _hgrn(hq, kkf, kkb, lff, lfb, hi, w_tab, mask_tab, wc_tab, maskc_tab, vmask_tab)

        n_blocks = NB_LAT if last else NB_TOK
        x1, h2, eid, gate = _merge(layer, n_blocks, x_lat, x_ctx, ctx_row0, mod3, yf, yb, s5u, of, ob, hgs,
                                   att_lat, att_ctx, gates, head_sum, merge_consts)

        y_moe = _moe(h2, eid[:, :MOE_TOP_K], moe_w1, moe_w3, moe_w2, layer)
        res = _combine(n_blocks, x1, y_moe, gate, mod3, final_norm_g[None, :], final=last)
        if last:
            out = res
        else:
            x_lat, x_ctx, ctx_row0 = res, res, N_LAT
    return out.reshape(BATCH, SEQ, D_MODEL)
```

```python
import functools
import math

import jax
import jax.numpy as jnp
from jax import lax
import numpy as np
from jax.experimental import pallas as pl
from jax.experimental.pallas import tpu as pltpu

F32 = jnp.float32
BF16 = jnp.bfloat16
HIGHEST = lax.Precision.HIGHEST

D_MODEL = 1024
BATCH = 8
SEQ = 2048
DEPTH = 2
GRID_W = 64
CTX_LEN = 256
NORM_EPS = 1e-6

S5_WIDTH = D_MODEL // 4
S5_GROUP_CH = 16
S5_GROUPS = S5_WIDTH // S5_GROUP_CH
S5_STATE = 64
S5_LANES = S5_GROUPS * S5_STATE

MLA_HEADS = D_MODEL // 128
MLA_NOPE = 64
MLA_ROPE = 32
MLA_V = 64
MLA_Q_LORA = D_MODEL // 4
MLA_KV_LORA = D_MODEL // 8
MLA_SCALE = 1.0 / math.sqrt(MLA_NOPE + MLA_ROPE)
MLA_HEAD_PAD = 128
ROPE_PAIRS = MLA_ROPE // 4
ROPE_BASE = 10000.0

HG_HEADS = D_MODEL // 256
HG_K = 128
HG_V = 64
HG_CHUNK = 64
HG_LEVELS = 6
HG_SETS = 2 + HG_LEVELS
HG_FAST_CHUNK = 32
HG_FAST_MIN = -60.0

MOE_GROUPS = 4
MOE_EXPERTS_PER_GROUP = 8
MOE_EXPERTS = MOE_GROUPS * MOE_EXPERTS_PER_GROUP
MOE_TOP_K = 2
MOE_HIDDEN = D_MODEL // 2
MOE_ROWS = 256

N_LAT = BATCH * SEQ
N_CTX = BATCH * CTX_LEN
N_TOK = N_LAT + N_CTX
RB = CTX_LEN
WB = 2 * RB
NB_LAT = N_LAT // RB
NB_CTX = N_CTX // RB
NB_TOK = N_TOK // RB
LAT_BLOCKS = SEQ // RB
SEQ_BLOCKS = LAT_BLOCKS + 1

C_S5 = 0
C_CQ = 256
C_KV = 512
C_HQ = 768
C_ZF = 1280
C_ZB = 1792
C_HI = 2304
C_HG = 2560
C_GATE = 2816
W_IN_COLS = C_GATE + 3 * D_MODEL

VMEM_LIMIT = 56 * 1024 * 1024


def _params(*sem):
    return pltpu.CompilerParams(dimension_semantics=sem, vmem_limit_bytes=VMEM_LIMIT)


def _rms(v):
    return v * lax.rsqrt(jnp.mean(v * v, axis=-1, keepdims=True) + NORM_EPS)


def _mod_row(i, rb=RB):
    return jnp.where(i < N_LAT // rb, i // (SEQ // rb), BATCH)


def _resident(a):
    return pl.BlockSpec(a.shape, lambda *_: (0,) * a.ndim, pipeline_mode=pl.Buffered(1))


def _layer_spec(a, layer):
    return pl.BlockSpec((1,) + a.shape[1:], lambda *_: (layer,) + (0,) * (a.ndim - 1),
                        pipeline_mode=pl.Buffered(1))


def _split_row_specs(rb, w, ctx_row0):
    n_lat = N_LAT // rb
    lat = pl.BlockSpec((rb, w), lambda i, *_: (jnp.minimum(i, n_lat - 1), 0))
    ctx = pl.BlockSpec((rb, w), lambda i, *_: (ctx_row0 // rb + jnp.maximum(i - n_lat, 0), 0))
    return lat, ctx


def _pick_rows(lat_ref, ctx_ref, rb):
    return jnp.where(pl.program_id(0) < N_LAT // rb, lat_ref[...], ctx_ref[...])


def _mod_kernel(c_ref, w_ref, b_ref, o_ref):
    c = c_ref[...]
    s = c * jax.nn.sigmoid(c)
    o_ref[0] = jnp.dot(s, w_ref[0], precision=HIGHEST, preferred_element_type=F32) + b_ref[0]


def _modulation(c_rows, mod_w, mod_b):
    bn = 1536
    return pl.pallas_call(
        _mod_kernel,
        grid=(DEPTH, 6 * D_MODEL // bn),
        in_specs=[
            pl.BlockSpec((16, D_MODEL), lambda l, j: (0, 0)),
            pl.BlockSpec((1, D_MODEL, bn), lambda l, j: (l, 0, j)),
            pl.BlockSpec((1, 1, bn), lambda l, j: (l, 0, j)),
        ],
        out_specs=pl.BlockSpec((1, 16, bn), lambda l, j: (l, 0, j)),
        out_shape=jax.ShapeDtypeStruct((DEPTH, 16, 6 * D_MODEL), F32),
        compiler_params=_params("arbitrary", "arbitrary"),
        name="adaln_mod",
    )(c_rows, mod_w, mod_b.reshape(DEPTH, 1, 6 * D_MODEL))


def _log_sigmoid(z):
    return jnp.minimum(z, 0.0) - jnp.log1p(jnp.exp(-jnp.abs(z)))


def _in_kernel(hs_ref, xl_ref, xc_ref, mod_ref, g1_ref, w_ref, qag_ref, kvg_ref,
               s5u_ref, cqn_ref, kvn_ref, hq_ref, lff_ref, lfb_ref, kkf_ref, kkb_ref,
               hi_ref, hgs_ref, gates_ref, h_ref, *, layer):
    x = _pick_rows(xl_ref, xc_ref, WB)
    xn = _rms(x) * g1_ref[0]
    h_ref[...] = (xn * (1.0 + mod_ref[0, 1:2, :]) + mod_ref[0, 0:1, :]).astype(BF16)

    def mm(a, b):
        return jnp.dot(h_ref[...], w_ref[0, :, a:b], preferred_element_type=F32)

    s5u_ref[...] = mm(C_S5, C_CQ).astype(BF16)
    cqn_ref[...] = (_rms(mm(C_CQ, C_KV)) * qag_ref[0]).astype(BF16)
    kv = mm(C_KV, C_HQ)
    ckvn = _rms(kv[:, :MLA_KV_LORA]) * kvg_ref[0]
    kvn_ref[...] = jnp.concatenate([ckvn, kv[:, MLA_KV_LORA:]], axis=1).astype(BF16)
    hq_ref[...] = mm(C_HQ, C_ZF).astype(BF16)

    log_lb = hs_ref[3 * layer]
    log_1m_lb = hs_ref[3 * layer + 1]
    one_m_lb = hs_ref[3 * layer + 2]
    for c0, lf_ref, kk_ref in ((C_ZF, lff_ref, kkf_ref), (C_ZB, lfb_ref, kkb_ref)):
        z = mm(c0, c0 + HG_HEADS * HG_K)
        b = log_1m_lb + _log_sigmoid(z)
        m = jnp.maximum(b, log_lb)
        lf_ref[...] = m + jnp.log1p(jnp.exp(-jnp.abs(b - log_lb)))
        kk_ref[...] = (one_m_lb * jax.nn.sigmoid(-z)).astype(BF16)

    hi_ref[...] = mm(C_HI, C_HG).astype(BF16)
    g = mm(C_HG, C_GATE)
    hgs_ref[...] = (g * jax.nn.sigmoid(g)).astype(BF16)
    gw = 512
    for j in range(3 * D_MODEL // gw):
        c0 = C_GATE + j * gw
        gates_ref[:, j * gw:(j + 1) * gw] = jax.nn.sigmoid(mm(c0, c0 + gw)).astype(BF16)


def _input_projection(layer, x_lat, x_ctx, ctx_row0, mod3, hg_scal, g1, w_in, qa_g, kva_g):
    row = lambda w: pl.BlockSpec((WB, w), lambda i: (i, 0))
    x_specs = _split_row_specs(WB, D_MODEL, ctx_row0)
    shapes = [
        (S5_WIDTH, BF16), (MLA_Q_LORA, BF16), (256, BF16), (HG_HEADS * HG_K, BF16),
        (HG_HEADS * HG_K, F32), (HG_HEADS * HG_K, F32), (HG_HEADS * HG_K, BF16), (HG_HEADS * HG_K, BF16),
        (HG_HEADS * HG_V, BF16), (HG_HEADS * HG_V, BF16), (3 * D_MODEL, BF16),
    ]
    return pl.pallas_call(
        functools.partial(_in_kernel, layer=layer),
        grid=(N_TOK // WB,),
        in_specs=[
            pl.BlockSpec(memory_space=pltpu.SMEM),
            *x_specs,
            pl.BlockSpec((1, 6, D_MODEL), lambda i: (_mod_row(i, WB), 0, 0)),
            _layer_spec(g1, layer), _layer_spec(w_in, layer), _layer_spec(qa_g, layer), _layer_spec(kva_g, layer),
        ],
        out_specs=[row(w) for w, _ in shapes],
        out_shape=[jax.ShapeDtypeStruct((N_TOK, w), dt) for w, dt in shapes],
        scratch_shapes=[pltpu.VMEM((WB, D_MODEL), BF16)],
        compiler_params=_params("arbitrary"),
        name="input_projection",
    )(hg_scal, x_lat, x_ctx, mod3, g1, w_in, qa_g, kva_g)


def _seq_block(b, j):
    return jnp.where(j == 0, NB_LAT + b, b * LAT_BLOCKS + j - 1)


def _seq_block_rev(b, j):
    return jnp.where(j == 0, NB_LAT + b, b * LAT_BLOCKS + LAT_BLOCKS - j)


S5_TB = 128
S5_ROWS = S5_TB * BATCH
S5_STEPS = (CTX_LEN + SEQ) // S5_TB
S5_CTX_STEPS = CTX_LEN // S5_TB


def _s5_kernel(uf_ref, ub_ref, bmat_ref, cmat_ref, a_ref, yf_ref, yb_ref, bu_ref, carry_ref):
    j = pl.program_id(0)

    @pl.when(j == 0)
    def _():
        carry_ref[...] = jnp.zeros_like(carry_ref)

    for d, (u_ref, y_ref) in enumerate(((uf_ref, yf_ref), (ub_ref, yb_ref))):
        bu_ref[...] = jnp.dot(u_ref[...], bmat_ref[0, d], preferred_element_type=F32)
        are = a_ref[0, d, 0]
        aim = a_ref[0, d, 1]

        def body(t, carry, d=d, are=are, aim=aim):
            hre, him = carry
            tt = (S5_TB - 1 - t) if d == 1 else t
            r0 = pl.multiple_of(tt * BATCH, BATCH)
            nre = are * hre - aim * him + bu_ref[pl.ds(r0, BATCH), 0:S5_LANES]
            nim = are * him + aim * hre + bu_ref[pl.ds(r0, BATCH), S5_LANES:2 * S5_LANES]
            bu_ref[pl.ds(r0, BATCH), 0:S5_LANES] = nre
            bu_ref[pl.ds(r0, BATCH), S5_LANES:2 * S5_LANES] = nim
            return nre, nim

        hre, him = lax.fori_loop(0, S5_TB, body, (carry_ref[d, 0], carry_ref[d, 1]), unroll=4)
        carry_ref[d, 0] = hre
        carry_ref[d, 1] = him
        hl = S5_LANES // 2
        hw = S5_WIDTH // 2
        for half in range(2):
            states = jnp.concatenate([bu_ref[:, half * hl:(half + 1) * hl],
                                      bu_ref[:, S5_LANES + half * hl:S5_LANES + (half + 1) * hl]], axis=1)
            y_ref[:, half * hw:(half + 1) * hw] = jnp.dot(states.astype(BF16), cmat_ref[0, d, half],
                                                          preferred_element_type=F32)


def _s5_block_rev(j):
    return jnp.where(j < S5_CTX_STEPS, S5_CTX_STEPS - 1 - j, S5_STEPS - 1 + S5_CTX_STEPS - j)


def _s5_states(layer, u_tm, bmat, cmat, a_tab):
    full = lambda a: _layer_spec(a, layer)
    fwd = pl.BlockSpec((S5_ROWS, S5_WIDTH), lambda j: (j, 0))
    bwd = pl.BlockSpec((S5_ROWS, S5_WIDTH), lambda j: (_s5_block_rev(j), 0))
    return pl.pallas_call(
        _s5_kernel,
        grid=(S5_STEPS,),
        in_specs=[fwd, bwd, full(bmat), full(cmat), full(a_tab)],
        out_specs=[fwd, bwd],
        out_shape=[jax.ShapeDtypeStruct((S5_STEPS * S5_ROWS, S5_WIDTH), F32)] * 2,
        scratch_shapes=[pltpu.VMEM((S5_ROWS, 2 * S5_LANES), F32), pltpu.VMEM((2, 2, BATCH, S5_LANES), F32)],
        compiler_params=_params("arbitrary"),
        name="s5_scan",
    )(u_tm, u_tm, bmat, cmat, a_tab)


def _to_time_major(a):
    w = a.shape[1]
    seq = jnp.concatenate([a[N_LAT:].reshape(BATCH, CTX_LEN, w), a[:N_LAT].reshape(BATCH, SEQ, w)], axis=1)
    return seq.transpose(1, 0, 2).reshape((CTX_LEN + SEQ) * BATCH, w)


def _from_time_major(a):
    w = a.shape[1]
    seq = a.reshape(CTX_LEN + SEQ, BATCH, w).transpose(1, 0, 2)
    return jnp.concatenate([seq[:, CTX_LEN:].reshape(N_LAT, w), seq[:, :CTX_LEN].reshape(N_CTX, w)], axis=0)


def _qkv_kernel(cqn_ref, kvn_ref, cos_ref, sin_ref, wq_ref, wk_ref, wv_ref, q_ref, k_ref, v_ref):
    cos = jnp.concatenate([cos_ref[...]] * MLA_HEADS, axis=1)
    sin = jnp.concatenate([sin_ref[...]] * MLA_HEADS, axis=1)
    w = MLA_HEADS * MLA_HEAD_PAD
    q2 = jnp.dot(cqn_ref[...], wq_ref[0], preferred_element_type=F32)
    q_ref[...] = ((q2[:, :w] * cos + q2[:, w:] * sin) * MLA_SCALE).astype(BF16)
    kv = kvn_ref[...]
    k2 = jnp.dot(kv, wk_ref[0], preferred_element_type=F32)
    k_ref[...] = (k2[:, :w] * cos + k2[:, w:] * sin).astype(BF16)
    v_ref[...] = jnp.dot(kv, wv_ref[0], preferred_element_type=F32).astype(BF16)


def _qkv(layer, cqn, kvn, cos_t, sin_t, wq, wk, wv):
    row = lambda w: pl.BlockSpec((WB, w), lambda i: (i, 0))
    full = lambda a: _layer_spec(a, layer)
    pos = pl.BlockSpec((WB, MLA_HEAD_PAD),
                       lambda i: (jnp.where(i < N_LAT // WB, i % (SEQ // WB), SEQ // WB), 0))
    w = MLA_HEADS * MLA_HEAD_PAD
    return pl.pallas_call(
        _qkv_kernel,
        grid=(N_TOK // WB,),
        in_specs=[row(MLA_Q_LORA), row(256), pos, pos, full(wq), full(wk), full(wv)],
        out_specs=[row(w), row(w), row(MLA_HEADS * MLA_V)],
        out_shape=[jax.ShapeDtypeStruct((N_TOK, w), BF16), jax.ShapeDtypeStruct((N_TOK, w), BF16),
                   jax.ShapeDtypeStruct((N_TOK, MLA_HEADS * MLA_V), BF16)],
        compiler_params=_params("arbitrary"),
        name="mla_qkv",
    )(cqn, kvn, cos_t, sin_t, wq, wk, wv)


def _attn_kernel(*refs, n_kv):
    q_ref = refs[0]
    k_refs = refs[1:1 + n_kv]
    v_refs = refs[1 + n_kv:1 + 2 * n_kv]
    o_ref = refs[1 + 2 * n_kv]
    nt = (((1,), (1,)), ((), ()))
    for h in range(MLA_HEADS):
        q = q_ref[:, h * MLA_HEAD_PAD:(h + 1) * MLA_HEAD_PAD]
        s = [lax.dot_general(q, k_ref[:, h * MLA_HEAD_PAD:(h + 1) * MLA_HEAD_PAD], nt,
                             preferred_element_type=F32) for k_ref in k_refs]
        m = functools.reduce(jnp.maximum, [jnp.max(si, axis=-1, keepdims=True) for si in s])
        p = [jnp.exp(si - m) for si in s]
        l = functools.reduce(jnp.add, [jnp.sum(pi, axis=-1, keepdims=True) for pi in p])
        o = functools.reduce(jnp.add, [
            jnp.dot(pi.astype(BF16), v_ref[:, h * MLA_V:(h + 1) * MLA_V], preferred_element_type=F32)
            for pi, v_ref in zip(p, v_refs)])
        o_ref[:, h * MLA_V:(h + 1) * MLA_V] = (o / l).astype(BF16)


def _attention(q, k, v, latent):
    w = MLA_HEADS * MLA_HEAD_PAD
    wv = MLA_HEADS * MLA_V
    if latent:
        tq = WB
        nq = SEQ // tq
        q_spec = pl.BlockSpec((tq, w), lambda b, j: (b * nq + j, 0))
        k_specs = [pl.BlockSpec((SEQ, w), lambda b, j: (b, 0)),
                   pl.BlockSpec((CTX_LEN, w), lambda b, j: (NB_LAT + b, 0))]
        v_specs = [pl.BlockSpec((SEQ, wv), lambda b, j: (b, 0)),
                   pl.BlockSpec((CTX_LEN, wv), lambda b, j: (NB_LAT + b, 0))]
        o_spec = pl.BlockSpec((tq, wv), lambda b, j: (b * nq + j, 0))
        n_out = N_LAT
    else:
        tq = CTX_LEN
        nq = 1
        q_spec = pl.BlockSpec((tq, w), lambda b, j: (NB_LAT + b, 0))
        k_specs = [pl.BlockSpec((CTX_LEN, w), lambda b, j: (NB_LAT + b, 0))]
        v_specs = [pl.BlockSpec((CTX_LEN, wv), lambda b, j: (NB_LAT + b, 0))]
        o_spec = pl.BlockSpec((tq, wv), lambda b, j: (b, 0))
        n_out = N_CTX
    n_kv = len(k_specs)
    return pl.pallas_call(
        functools.partial(_attn_kernel, n_kv=n_kv),
        grid=(BATCH, nq),
        in_specs=[q_spec] + k_specs + v_specs,
        out_specs=o_spec,
        out_shape=jax.ShapeDtypeStruct((n_out, wv), BF16),
        compiler_params=_params("arbitrary", "arbitrary"),
        name="mla_attention_lat" if latent else "mla_attention_ctx",
    )(q, *([k] * n_kv), *([v] * n_kv))


def _hgrn_tables():
    c = HG_CHUNK
    w = np.zeros((2, HG_SETS * c, c), np.float32)
    mask = np.zeros((2, HG_LEVELS, c, c), np.float32)
    idx = np.arange(c)
    for r in range(c):
        w[0, r, :r + 1] = 1
        w[1, r, r:] = 1
        w[0, c + r, r + 1:] = 1
        w[1, c + r, :r] = 1
    for l in range(HG_LEVELS):
        m = c >> (l + 1)
        for r in range(c):
            base = (r // (2 * m)) * 2 * m
            mid = base + m
            later = r >= mid
            row = (2 + l) * c + r
            if later:
                w[0, row, mid:r + 1] = 1
                w[1, row, mid:r] = 1
            else:
                w[0, row, r + 1:mid] = 1
                w[1, row, r:mid] = 1
        same = (idx[:, None] // (2 * m)) == (idx[None, :] // (2 * m))
        q_later = (idx[:, None] % (2 * m)) >= m
        k_later = (idx[None, :] % (2 * m)) >= m
        mask[0, l] = same & q_later & ~k_later
        mask[1, l] = same & ~q_later & k_later
    return w, mask


def _hgrn_dir(d, q_ref, k_ref, lf_ref, v_ref, w_ref, mask_ref, st_ref, o_ref):
    c = HG_CHUNK
    n_chunks = RB // c
    nt = (((1,), (1,)), ((), ()))
    tn = (((0,), (0,)), ((), ()))
    last = c - 1 if d == 0 else 0
    wsel = w_ref[d]
    e = []
    for ci in range(n_chunks):
        g = lf_ref[ci * c:(ci + 1) * c, :]
        g_hi = g.astype(BF16)
        g_lo = (g - g_hi.astype(F32)).astype(BF16)
        expo = jnp.dot(wsel, jnp.concatenate([g_hi, g_lo], axis=0), preferred_element_type=F32)
        e.append(jnp.exp(expo))

    def rows(s):
        return jnp.concatenate([e[ci][s * c:(s + 1) * c] for ci in range(n_chunks)], axis=0)

    q = q_ref[...].astype(F32)
    k = k_ref[...].astype(F32)
    q_state = (q * rows(0)).astype(BF16)
    k_state = (k * rows(1)).astype(BF16)
    q_lvl, k_lvl = [], []
    for l in range(HG_LEVELS):
        el = rows(2 + l)
        q_lvl.append((q * el).astype(BF16))
        k_lvl.append((k * el).astype(BF16))
    qk = q * k
    order = range(n_chunks) if d == 0 else range(n_chunks - 1, -1, -1)
    for h in range(HG_HEADS):
        ks = slice(h * HG_K, (h + 1) * HG_K)
        vs = slice(h * HG_V, (h + 1) * HG_V)
        v = v_ref[:, vs]
        scores = jnp.zeros((RB, RB), F32)
        for l in range(HG_LEVELS):
            p = lax.dot_general(q_lvl[l][:, ks], k_lvl[l][:, ks], nt, preferred_element_type=F32)
            scores = scores + p * mask_ref[d, l]
        o = jnp.dot(scores.astype(BF16), v, preferred_element_type=F32)
        o = o + jnp.sum(qk[:, ks], axis=-1, keepdims=True) * v.astype(F32)
        st = st_ref[d, h]
        o_state = [None] * n_chunks
        for ci in order:
            r = slice(ci * c, (ci + 1) * c)
            o_state[ci] = lax.dot_general(q_state[r, ks], st.astype(BF16), nt, preferred_element_type=F32)
            inc = lax.dot_general(v[r], k_state[r, ks], tn, preferred_element_type=F32)
            st = st * e[ci][last:last + 1, ks] + inc
        st_ref[d, h] = st
        o_ref[:, vs] = o + jnp.concatenate(o_state, axis=0)


def _hgrn_fast_tables():
    c = HG_FAST_CHUNK
    idx = np.arange(RB)
    same = (idx[:, None] // c) == (idx[None, :] // c)
    le = idx[None, :] <= idx[:, None]
    ge = idx[None, :] >= idx[:, None]
    wc = np.stack([same & le, same & ge]).astype(np.float32)
    col_chunk = np.arange((RB // c) * HG_V) // HG_V
    vmask = ((idx[:, None] // c) == col_chunk[None, :]).astype(np.float32)
    return np.concatenate([wc, wc], axis=-1), wc, vmask


def _hgrn_dir_fast(d, q_ref, k_ref, cum, tot, v_ref, maskc_ref, vmask_ref, st_ref, o_ref):
    c = HG_FAST_CHUNK
    n_chunks = RB // c
    nt = (((1,), (1,)), ((), ()))
    tn = (((0,), (0,)), ((), ()))
    q = q_ref[...].astype(F32)
    k = k_ref[...].astype(F32)
    qa = (q * jnp.exp(cum)).astype(BF16)
    kb = (k * jnp.exp(-cum)).astype(BF16)
    k_state = (k * jnp.exp(tot - cum)).astype(BF16)
    decay = jnp.exp(tot)
    keep = maskc_ref[d] > 0.5
    vmask = vmask_ref[...]
    order = range(n_chunks) if d == 0 else range(n_chunks - 1, -1, -1)
    for h in range(HG_HEADS):
        ks = slice(h * HG_K, (h + 1) * HG_K)
        vs = slice(h * HG_V, (h + 1) * HG_V)
        v = v_ref[:, vs]
        p = lax.dot_general(qa[:, ks], kb[:, ks], nt, preferred_element_type=F32)
        o = jnp.dot(jnp.where(keep, p, 0.0).astype(BF16), v, preferred_element_type=F32)
        v_by_chunk = jnp.concatenate([v] * n_chunks, axis=1) * vmask
        inc = lax.dot_general(v_by_chunk, k_state[:, ks], tn, preferred_element_type=F32)
        st = st_ref[d, h]
        entering = [None] * n_chunks
        for ci in order:
            entering[ci] = st
            st = st * decay[ci * c:ci * c + 1, ks] + inc[ci * HG_V:(ci + 1) * HG_V]
        st_ref[d, h] = st
        s_all = jnp.concatenate(entering, axis=0).astype(BF16)
        o_all = lax.dot_general(qa[:, ks], s_all, nt, preferred_element_type=F32) * vmask.astype(F32)
        o_fold = functools.reduce(jnp.add, [o_all[:, j * 128:(j + 1) * 128] for j in range(n_chunks * HG_V // 128)])
        o_state = (o_fold + pltpu.roll(o_fold, HG_V, 1))[:, :HG_V]
        o_ref[:, vs] = o + o_state


def _hgrn_kernel(qf_ref, kf_ref, lff_ref, vf_ref, qb_ref, kb_ref, lfb_ref, vb_ref, w_ref, mask_ref,
                 wc_ref, maskc_ref, vmask_ref, of_ref, ob_ref, st_ref):
    j = pl.program_id(1)

    @pl.when(j == 0)
    def _():
        st_ref[...] = jnp.zeros_like(st_ref)

    c = HG_FAST_CHUNK
    dirs = ((0, qf_ref, kf_ref, lff_ref, vf_ref, of_ref), (1, qb_ref, kb_ref, lfb_ref, vb_ref, ob_ref))
    for d, q_ref, k_ref, lf_ref, v_ref, o_ref in dirs:
        g_hi, g_lo = _split_bf16(lf_ref[...])
        cum = jnp.dot(wc_ref[d], jnp.concatenate([g_hi, g_lo], axis=0), preferred_element_type=F32)
        edge = c - 1 if d == 0 else 0
        totals = [cum[ci * c + edge:ci * c + edge + 1] for ci in range(RB // c)]
        tot = jnp.concatenate([jnp.broadcast_to(t, (c, t.shape[1])) for t in totals], axis=0)
        safe = jnp.min(jnp.concatenate(totals, axis=0)) >= HG_FAST_MIN

        @pl.when(safe)
        def _(d=d, q_ref=q_ref, k_ref=k_ref, v_ref=v_ref, o_ref=o_ref, cum=cum, tot=tot):
            _hgrn_dir_fast(d, q_ref, k_ref, cum, tot, v_ref, maskc_ref, vmask_ref, st_ref, o_ref)

        @pl.when(jnp.logical_not(safe))
        def _(d=d, q_ref=q_ref, k_ref=k_ref, lf_ref=lf_ref, v_ref=v_ref, o_ref=o_ref):
            _hgrn_dir(d, q_ref, k_ref, lf_ref, v_ref, w_ref, mask_ref, st_ref, o_ref)


def _hgrn(hq, kkf, kkb, lff, lfb, hi, w_tab, mask_tab, wc_tab, maskc_tab, vmask_tab):
    full = _resident
    fwd = lambda w: pl.BlockSpec((RB, w), lambda b, j: (_seq_block(b, j), 0))
    bwd = lambda w: pl.BlockSpec((RB, w), lambda b, j: (_seq_block_rev(b, j), 0))
    wk = HG_HEADS * HG_K
    wv = HG_HEADS * HG_V
    return pl.pallas_call(
        _hgrn_kernel,
        grid=(BATCH, SEQ_BLOCKS),
        in_specs=[fwd(wk), fwd(wk), fwd(wk), fwd(wv), bwd(wk), bwd(wk), bwd(wk), bwd(wv),
                  full(w_tab), full(mask_tab), full(wc_tab), full(maskc_tab), full(vmask_tab)],
        out_specs=[fwd(wv), bwd(wv)],
        out_shape=[jax.ShapeDtypeStruct((N_TOK, wv), F32)] * 2,
        scratch_shapes=[pltpu.VMEM((2, HG_HEADS, HG_V, HG_K), F32)],
        compiler_params=_params("arbitrary", "arbitrary"),
        name="hgrn2_scan",
    )(hq, kkf, lff, hi, hq, kkb, lfb, hi, w_tab, mask_tab, wc_tab, maskc_tab, vmask_tab)


def _gelu_tanh(x):
    return 0.5 * x * (1.0 + jnp.tanh(math.sqrt(2.0 / math.pi) * (x + 0.044715 * (x * x * x))))


def _split_bf16(a):
    hi = a.astype(BF16)
    return hi, (a - hi.astype(F32)).astype(BF16)


def _merge_kernel(xl_ref, xc_ref, mod_ref, yf_ref, yb_ref, u_ref, of_ref, ob_ref, hgs_ref, attl_ref, attc_ref,
                  gates_ref, s5d_ref, gluw_ref, glub_ref, hgg_ref, hsum_ref, wpa_ref, wpb_ref, wpc_ref, wout_ref,
                  g2_ref, wrh_ref, wrl_ref, br_ref,
                  x1_ref, h2_ref, eid_ref, gate_ref):
    y = yf_ref[...] + yb_ref[...] + s5d_ref[0] * u_ref[...].astype(F32)
    y = _gelu_tanh(y)
    y = y * jax.nn.sigmoid(jnp.dot(y.astype(BF16), gluw_ref[0], preferred_element_type=F32) + glub_ref[0])
    o = of_ref[...] + ob_ref[...]
    sq_hi, sq_lo = _split_bf16(o * o)
    ms = (jnp.dot(sq_hi, hsum_ref[...], preferred_element_type=F32)
          + jnp.dot(sq_lo, hsum_ref[...], preferred_element_type=F32)) * (1.0 / HG_V)
    o = o * lax.rsqrt(ms + NORM_EPS) * hgg_ref[0] * hgs_ref[...].astype(F32)

    d = D_MODEL
    att = _pick_rows(attl_ref, attc_ref, WB)
    merged = (gates_ref[:, 0:d].astype(F32) * jnp.dot(y.astype(BF16), wpa_ref[0], preferred_element_type=F32)
              + gates_ref[:, d:2 * d].astype(F32) * jnp.dot(att, wpb_ref[0], preferred_element_type=F32)
              + gates_ref[:, 2 * d:3 * d].astype(F32) * jnp.dot(o.astype(BF16), wpc_ref[0], preferred_element_type=F32))
    y_out = jnp.dot(merged.astype(BF16), wout_ref[0], preferred_element_type=F32)
    x1 = _pick_rows(xl_ref, xc_ref, WB) + mod_ref[0, 2:3, :] * y_out
    x1_ref[...] = x1
    h2 = _rms(x1) * g2_ref[0] * (1.0 + mod_ref[0, 4:5, :]) + mod_ref[0, 3:4, :]
    h2_ref[...] = h2

    h_hi, h_lo = _split_bf16(h2)
    logits = (jnp.dot(h_hi, wrh_ref[0], preferred_element_type=F32)
              + jnp.dot(h_lo, wrh_ref[0], preferred_element_type=F32)
              + jnp.dot(h_hi, wrl_ref[0], preferred_element_type=F32)) + br_ref[0]
    lane = lax.broadcasted_iota(jnp.int32, logits.shape, 1).astype(F32)
    neg = -jnp.inf
    glog = jnp.where(lane < MOE_GROUPS, logits, neg)
    gmax = jnp.max(glog, axis=-1, keepdims=True)
    gidx = jnp.min(jnp.where(glog == gmax, lane, 1e9), axis=-1, keepdims=True)
    g_w = 1.0 / jnp.sum(jnp.exp(glog - gmax), axis=-1, keepdims=True)
    e_lo = 32.0 + gidx * MOE_EXPERTS_PER_GROUP
    elog = jnp.where((lane >= e_lo) & (lane < e_lo + MOE_EXPERTS_PER_GROUP), logits, neg)
    v1 = jnp.max(elog, axis=-1, keepdims=True)
    i1 = jnp.min(jnp.where(elog == v1, lane, 1e9), axis=-1, keepdims=True)
    elog2 = jnp.where(lane == i1, neg, elog)
    v2 = jnp.max(elog2, axis=-1, keepdims=True)
    i2 = jnp.min(jnp.where(elog2 == v2, lane, 1e9), axis=-1, keepdims=True)
    e2 = jnp.exp(v2 - v1)
    gate1 = g_w / (1.0 + e2)
    gate2 = g_w * e2 / (1.0 + e2)
    eid_ref[...] = jnp.where(lane == 0.0, i1 - 32.0, jnp.where(lane == 1.0, i2 - 32.0, 0.0)).astype(jnp.int32)
    gate_ref[...] = jnp.where(lane == 0.0, gate1, jnp.where(lane == 1.0, gate2, 0.0))


def _merge(layer, n_blocks, x_lat, x_ctx, ctx_row0, mod3, yf, yb, s5u, of, ob, hgs, att_lat, att_ctx, gates,
           head_sum, consts):
    row = lambda w: pl.BlockSpec((WB, w), lambda i: (i, 0))
    n = n_blocks * RB
    wv = MLA_HEADS * MLA_V
    if att_ctx is None:
        att_ctx = att_lat
    const_specs = [_layer_spec(a, layer) for a in consts]
    const_specs.insert(4, _resident(head_sum))
    return pl.pallas_call(
        _merge_kernel,
        grid=(n // WB,),
        in_specs=[*_split_row_specs(WB, D_MODEL, ctx_row0),
                  pl.BlockSpec((1, 6, D_MODEL), lambda i: (_mod_row(i, WB), 0, 0)),
                  row(S5_WIDTH), row(S5_WIDTH), row(S5_WIDTH), row(HG_HEADS * HG_V), row(HG_HEADS * HG_V),
                  row(HG_HEADS * HG_V), *_split_row_specs(WB, wv, 0), row(3 * D_MODEL)] + const_specs,
        out_specs=[row(D_MODEL), row(D_MODEL), row(128), row(128)],
        out_shape=[jax.ShapeDtypeStruct((n, D_MODEL), F32), jax.ShapeDtypeStruct((n, D_MODEL), F32),
                   jax.ShapeDtypeStruct((n, 128), jnp.int32), jax.ShapeDtypeStruct((n, 128), F32)],
        compiler_params=_params("arbitrary"),
        name="merge_router",
    )(x_lat, x_ctx, mod3, yf, yb, s5u, of, ob, hgs, att_lat, att_ctx, gates,
      *consts[:4], head_sum, *consts[4:])


def _moe_kernel(bexp_ref, nblk_ref, src_ref, srcn_ref, dst_ref, h_hbm, w1_ref, w3_ref, w2_ref, y_hbm,
                xbuf, ybuf, w1b, w3b, w2b, gsem, ssem):
    i = pl.program_id(0)
    n_used = nblk_ref[0]
    slot = lax.rem(i, 2)

    def gather_start(idx_ref, s):
        for r in range(MOE_ROWS):
            pltpu.make_async_copy(h_hbm.at[pl.ds(idx_ref[0, 0, r], 1), :], xbuf.at[s, pl.ds(r, 1), :],
                                  gsem.at[s]).start()

    def gather_wait(s):
        pltpu.make_async_copy(h_hbm.at[pl.ds(0, MOE_ROWS), :], xbuf.at[s], gsem.at[s]).wait()

    def scatter_start(s):
        for r in range(MOE_ROWS):
            pltpu.make_async_copy(ybuf.at[s, pl.ds(r, 1), :], y_hbm.at[pl.ds(dst_ref[0, 0, r], 1), :],
                                  ssem.at[s]).start()

    def scatter_wait(s):
        pltpu.make_async_copy(ybuf.at[s], y_hbm.at[pl.ds(0, MOE_ROWS), :], ssem.at[s]).wait()

    @pl.when(i < n_used)
    def _():
        @pl.when(i == 0)
        def _():
            gather_start(src_ref, 0)
            ybuf[1] = jnp.zeros((MOE_ROWS, D_MODEL), F32)
            n_real = y_hbm.shape[0] - 2 * MOE_ROWS
            for half in range(2):
                fill = pltpu.make_async_copy(ybuf.at[1], y_hbm.at[pl.ds(n_real + half * MOE_ROWS, MOE_ROWS), :],
                                             ssem.at[1])
                fill.start()
                fill.wait()

        @pl.when(i + 1 < n_used)
        def _():
            gather_start(srcn_ref, 1 - slot)

        @pl.when((i == 0) | (bexp_ref[i] != bexp_ref[jnp.maximum(i - 1, 0)]))
        def _():
            w1b[...] = w1_ref[0, 0].astype(BF16)
            w3b[...] = w3_ref[0, 0].astype(BF16)
            w2b[...] = w2_ref[0, 0].astype(BF16)

        gather_wait(slot)

        @pl.when(i >= 2)
        def _():
            scatter_wait(slot)

        x = xbuf[slot].astype(BF16)
        a = jnp.dot(x, w1b[...], preferred_element_type=F32)
        g = jnp.dot(x, w3b[...], preferred_element_type=F32)
        hid = (a * jax.nn.sigmoid(a) * g).astype(BF16)
        ybuf[slot] = jnp.dot(hid, w2b[...], preferred_element_type=F32)
        scatter_start(slot)

        @pl.when(i == n_used - 1)
        def _():
            @pl.when(i >= 1)
            def _():
                scatter_wait(1 - slot)
            scatter_wait(slot)


def _moe(h2, eid, w1, w3, w2, layer):
    n = h2.shape[0]
    n_assign = n * MOE_TOP_K
    n_blocks = (n_assign + MOE_EXPERTS * (MOE_ROWS - 1) + MOE_ROWS - 1) // MOE_ROWS
    flat_e = eid.reshape(n_assign)
    order = jnp.argsort(flat_e, stable=True).astype(jnp.int32)
    experts = jnp.arange(MOE_EXPERTS, dtype=jnp.int32)
    counts = jnp.sum((flat_e[:, None] == experts[None, :]).astype(jnp.int32), axis=0)
    starts = jnp.cumsum(counts) - counts
    padded = (counts + MOE_ROWS - 1) // MOE_ROWS * MOE_ROWS
    p_ends = jnp.cumsum(padded)
    p_starts = p_ends - padded
    n_used_s = p_ends[-1] // MOE_ROWS
    n_used = n_used_s.astype(jnp.int32).reshape(1)
    blk = jnp.arange(n_blocks, dtype=jnp.int32)
    blk_expert = jnp.sum((p_ends[None, :] <= (blk * MOE_ROWS)[:, None]).astype(jnp.int32), axis=1)
    last_expert = jnp.max(jnp.where(counts > 0, experts, 0))
    block_expert = jnp.where(blk < n_used_s, jnp.minimum(blk_expert, MOE_EXPERTS - 1), last_expert)
    onehot = (block_expert[:, None] == experts[None, :]).astype(jnp.int32)
    pick = lambda table: jnp.sum(onehot * table[None, :], axis=1)
    local = jnp.arange(MOE_ROWS, dtype=jnp.int32)[None, :]
    rank = blk[:, None] * MOE_ROWS + local - pick(p_starts)[:, None]
    valid = (rank < pick(counts)[:, None]) & (blk < n_used_s)[:, None]
    pos = jnp.clip(pick(starts)[:, None] + rank, 0, n_assign - 1)
    assign = order[pos]
    tok = jnp.where(valid, assign // MOE_TOP_K, 0)
    dst = jnp.where(valid, (assign % MOE_TOP_K) * n + assign // MOE_TOP_K,
                    n_assign + (blk % 2)[:, None] * MOE_ROWS + local)
    rows_now = pl.BlockSpec((1, 1, MOE_ROWS), lambda i, be, nb: (i, 0, 0), memory_space=pltpu.SMEM)
    rows_next = pl.BlockSpec((1, 1, MOE_ROWS), lambda i, be, nb: (jnp.minimum(i + 1, n_blocks - 1), 0, 0),
                             memory_space=pltpu.SMEM)
    src3 = tok.reshape(n_blocks, 1, MOE_ROWS)
    return pl.pallas_call(
        _moe_kernel,
        grid_spec=pltpu.PrefetchScalarGridSpec(
            num_scalar_prefetch=2,
            grid=(n_blocks,),
            in_specs=[
                rows_now, rows_next, rows_now,
                pl.BlockSpec(memory_space=pl.ANY),
                pl.BlockSpec((1, 1, D_MODEL, MOE_HIDDEN), lambda i, be, nb: (layer, be[i], 0, 0)),
                pl.BlockSpec((1, 1, D_MODEL, MOE_HIDDEN), lambda i, be, nb: (layer, be[i], 0, 0)),
                pl.BlockSpec((1, 1, MOE_HIDDEN, D_MODEL), lambda i, be, nb: (layer, be[i], 0, 0)),
            ],
            out_specs=pl.BlockSpec(memory_space=pl.ANY),
            scratch_shapes=[pltpu.VMEM((2, MOE_ROWS, D_MODEL), F32), pltpu.VMEM((2, MOE_ROWS, D_MODEL), F32),
                            pltpu.VMEM((D_MODEL, MOE_HIDDEN), BF16), pltpu.VMEM((D_MODEL, MOE_HIDDEN), BF16),
                            pltpu.VMEM((MOE_HIDDEN, D_MODEL), BF16),
                            pltpu.SemaphoreType.DMA((2,)), pltpu.SemaphoreType.DMA((2,))],
        ),
        out_shape=jax.ShapeDtypeStruct((n_assign + 2 * MOE_ROWS, D_MODEL), F32),
        compiler_params=_params("arbitrary"),
        name="moe_experts",
    )(block_expert, n_used, src3, src3, dst.reshape(n_blocks, 1, MOE_ROWS), h2, w1, w3, w2)


def _combine_kernel(x1_ref, y0_ref, y1_ref, gate_ref, mod_ref, g_ref, o_ref, *, final):
    f = gate_ref[:, 0:1] * y0_ref[...] + gate_ref[:, 1:2] * y1_ref[...]
    x2 = x1_ref[...] + mod_ref[0, 5:6, :] * f
    if final:
        x2 = _rms(x2) * g_ref[...]
    o_ref[...] = x2


def _combine(n_blocks, x1, y, gate, mod3, g_final, final):
    row = lambda w: pl.BlockSpec((RB, w), lambda i: (i, 0))
    n = n_blocks * RB
    return pl.pallas_call(
        functools.partial(_combine_kernel, final=final),
        grid=(n_blocks,),
        in_specs=[row(D_MODEL), row(D_MODEL), pl.BlockSpec((RB, D_MODEL), lambda i: (i + n_blocks, 0)), row(128),
                  pl.BlockSpec((1, 6, D_MODEL), lambda i: (_mod_row(i), 0, 0)),
                  pl.BlockSpec((1, D_MODEL), lambda i: (0, 0))],
        out_specs=row(D_MODEL),
        out_shape=jax.ShapeDtypeStruct((n, D_MODEL), F32),
        compiler_params=_params("arbitrary"),
        name="moe_combine_final" if final else "moe_combine",
    )(x1, y, y, gate, mod3, g_final)


def _rope_rot_cols(w):
    p = ROPE_PAIRS
    return jnp.concatenate([-w[..., p:2 * p], w[..., 0:p], -w[..., 3 * p:4 * p], w[..., 2 * p:3 * p]], axis=-1)


def _pack_w_in(w_in):
    cuts = np.cumsum([S5_WIDTH, MLA_Q_LORA, MLA_KV_LORA, MLA_ROPE, HG_HEADS * HG_K, HG_HEADS * HG_K,
                      HG_HEADS * HG_K, HG_HEADS * HG_V, HG_HEADS * HG_V, D_MODEL, D_MODEL])
    (w_s5, w_cq, w_ckv, w_kpe, w_hq, w_zf, w_zb, w_hi, w_hg, w_ga, w_gb, w_gc) = jnp.split(w_in, cuts, axis=-1)
    pad = jnp.zeros(w_in.shape[:-1] + (256 - MLA_KV_LORA - 2 * MLA_ROPE,), w_in.dtype)
    packed = jnp.concatenate([w_s5, w_cq, w_ckv, w_kpe, _rope_rot_cols(w_kpe), pad,
                              w_hq, w_zf, w_zb, w_hi, w_hg, w_ga, w_gb, w_gc], axis=-1)
    assert packed.shape[-1] == W_IN_COLS
    return packed.astype(BF16)


def _pack_mla(w_uq, w_uk, w_uv):
    hp, n, r = MLA_HEAD_PAD, MLA_NOPE, MLA_ROPE
    nl = w_uq.shape[0]
    wq = w_uq.reshape(nl, MLA_Q_LORA, MLA_HEADS, n + r)
    zq = jnp.zeros((nl, MLA_Q_LORA, MLA_HEADS, hp - n - r), F32)
    q1 = jnp.concatenate([wq, zq], axis=-1).reshape(nl, MLA_Q_LORA, MLA_HEADS * hp)
    q2 = jnp.concatenate([jnp.zeros((nl, MLA_Q_LORA, MLA_HEADS, n), F32), _rope_rot_cols(wq[..., n:]), zq],
                         axis=-1).reshape(nl, MLA_Q_LORA, MLA_HEADS * hp)
    wq_packed = jnp.concatenate([q1, q2], axis=-1).astype(BF16)

    wk = w_uk.reshape(nl, MLA_KV_LORA, MLA_HEADS, n)
    k_nope = jnp.concatenate([wk, jnp.zeros((nl, MLA_KV_LORA, MLA_HEADS, hp - n), F32)], axis=-1)
    pe_slot = np.concatenate([np.zeros((r, n), np.float32), np.eye(r, dtype=np.float32),
                              np.zeros((r, hp - n - r), np.float32)], axis=-1)
    pe_all = jnp.asarray(np.tile(pe_slot[None, :, None, :], (nl, 1, MLA_HEADS, 1)))
    zero_pe = jnp.zeros_like(pe_all)
    tail = jnp.zeros((nl, 256 - MLA_KV_LORA - 2 * r, MLA_HEADS, hp), F32)
    k1 = jnp.concatenate([k_nope, pe_all, zero_pe, tail], axis=1).reshape(nl, 256, MLA_HEADS * hp)
    k2 = jnp.concatenate([jnp.zeros_like(k_nope), zero_pe, pe_all, tail], axis=1).reshape(nl, 256, MLA_HEADS * hp)
    wk_packed = jnp.concatenate([k1, k2], axis=-1).astype(BF16)
    wv_packed = jnp.concatenate([w_uv, jnp.zeros((nl, 256 - MLA_KV_LORA, MLA_HEADS * MLA_V), F32)],
                                axis=1).astype(BF16)
    return wq_packed, wk_packed, wv_packed


def _rope_tables():
    rows = SEQ // GRID_W
    row = np.repeat(np.arange(rows, dtype=np.float32), GRID_W)
    col = np.tile(np.arange(GRID_W, dtype=np.float32), rows)
    inv = (np.float32(ROPE_BASE) ** (-np.arange(ROPE_PAIRS, dtype=np.float32) / np.float32(ROPE_PAIRS))).astype(np.float32)
    ar, ac = row[:, None] * inv, col[:, None] * inv
    cos = np.concatenate([np.cos(ar), np.cos(ar), np.cos(ac), np.cos(ac)], axis=1)
    sin = np.concatenate([np.sin(ar), np.sin(ar), np.sin(ac), np.sin(ac)], axis=1)
    tail = MLA_HEAD_PAD - MLA_NOPE - MLA_ROPE
    cos_t = np.concatenate([np.ones((SEQ, MLA_NOPE)), cos, np.ones((SEQ, tail))], axis=1)
    sin_t = np.concatenate([np.zeros((SEQ, MLA_NOPE)), sin, np.zeros((SEQ, tail))], axis=1)
    cos_t = np.concatenate([cos_t, np.ones((WB, MLA_HEAD_PAD))], axis=0)
    sin_t = np.concatenate([sin_t, np.zeros((WB, MLA_HEAD_PAD))], axis=0)
    return jnp.asarray(cos_t, F32), jnp.asarray(sin_t, F32)


def _pack_s5(lam_re, lam_im, b_re, b_im, c_re, c_im, log_step):
    eye = jnp.eye(S5_GROUPS, dtype=F32)
    dt = jnp.exp(log_step)[..., None]
    mag = jnp.exp(lam_re * dt)
    lb_re, lb_im = mag * jnp.cos(lam_im * dt), mag * jnp.sin(lam_im * dt)
    den = lam_re * lam_re + lam_im * lam_im
    fr = ((lb_re - 1) * lam_re + lb_im * lam_im) / den
    fi = (lb_im * lam_re - (lb_re - 1) * lam_im) / den
    bb_re = fr[..., None] * b_re - fi[..., None] * b_im
    bb_im = fr[..., None] * b_im + fi[..., None] * b_re
    lead = lam_re.shape[:2]

    def in_mat(bb):
        return jnp.einsum('ldgph,gk->ldghkp', bb, eye).reshape(lead + (S5_WIDTH, S5_LANES))

    def out_mat(cc):
        return jnp.einsum('ldghp,gk->ldgpkh', cc, eye).reshape(lead + (S5_LANES, S5_WIDTH))

    bmat = jnp.concatenate([in_mat(bb_re), in_mat(bb_im)], axis=-1).astype(BF16)
    hl, hw = S5_LANES // 2, S5_WIDTH // 2
    cre, cim = out_mat(c_re), -out_mat(c_im)
    cmat = jnp.stack([jnp.concatenate([cre[..., h * hl:(h + 1) * hl, h * hw:(h + 1) * hw],
                                       cim[..., h * hl:(h + 1) * hl, h * hw:(h + 1) * hw]], axis=-2)
                      for h in range(2)], axis=2).astype(BF16)
    a_tab = jnp.stack([jnp.broadcast_to(lb_re.reshape(lead + (1, S5_LANES)), lead + (BATCH, S5_LANES)),
                       jnp.broadcast_to(lb_im.reshape(lead + (1, S5_LANES)), lead + (BATCH, S5_LANES))], axis=2)
    return bmat, cmat, a_tab


def kernel(x, c, ctx, c_ctx, mod_w, mod_b, norm1_g, norm2_g, w_in, s5_lam_re, s5_lam_im, s5_b_re, s5_b_im, s5_c_re, s5_c_im, s5_log_step, s5_d, s5_glu_w, s5_glu_b, mla_qa_g, mla_kva_g, mla_w_uq, mla_w_uk, mla_w_uv, hg_lb_logits, hg_norm_g, w_pa, w_pb, w_pc, w_out, moe_w_group, moe_b_group, moe_w_expert, moe_b_expert, moe_w1, moe_w3, moe_w2, final_norm_g):
    x_lat, x_ctx, ctx_row0 = x.reshape(N_LAT, D_MODEL), ctx.reshape(N_CTX, D_MODEL), 0
    c_rows = jnp.concatenate([c, c_ctx[None, :], jnp.zeros((16 - BATCH - 1, D_MODEL), F32)], axis=0)
    mod = _modulation(c_rows, mod_w, mod_b).reshape(DEPTH, 16, 6, D_MODEL)

    lb_all = jnp.cumsum(jax.nn.softmax(hg_lb_logits.astype(F32)))
    lb_all = lb_all - lb_all[0]
    hg_scal = jnp.stack([jnp.log(lb_all), jnp.log1p(-lb_all), 1.0 - lb_all], axis=1).reshape(3 * DEPTH).astype(F32)
    cos_t, sin_t = _rope_tables()

    vec = lambda a: a[:, None, :]
    w_in_p = _pack_w_in(w_in)
    bmat, cmat, a_tab = _pack_s5(s5_lam_re, s5_lam_im, s5_b_re, s5_b_im, s5_c_re, s5_c_im, s5_log_step)
    wq, wk, wv = _pack_mla(mla_w_uq, mla_w_uk, mla_w_uv)
    w_route = jnp.concatenate([moe_w_group, jnp.zeros((DEPTH, D_MODEL, 32 - MOE_GROUPS), F32), moe_w_expert,
                               jnp.zeros((DEPTH, D_MODEL, 128 - 32 - MOE_EXPERTS), F32)], axis=-1)
    b_route = jnp.concatenate([moe_b_group, jnp.zeros((DEPTH, 32 - MOE_GROUPS), F32), moe_b_expert,
                               jnp.zeros((DEPTH, 128 - 32 - MOE_EXPERTS), F32)], axis=-1)
    w_route_hi = w_route.astype(BF16)
    w_route_lo = (w_route - w_route_hi.astype(F32)).astype(BF16)
    merge_consts = [vec(s5_d), s5_glu_w.astype(BF16), vec(s5_glu_b), vec(hg_norm_g),
                    w_pa.astype(BF16), w_pb.astype(BF16), w_pc.astype(BF16), w_out.astype(BF16), vec(norm2_g),
                    w_route_hi, w_route_lo, vec(b_route)]
    w_tab_np, mask_tab_np = _hgrn_tables()
    w_tab = jnp.asarray(np.concatenate([w_tab_np, w_tab_np], axis=-1), BF16)
    mask_tab = jnp.asarray(np.kron(np.eye(RB // HG_CHUNK, dtype=np.float32), mask_tab_np), F32)
    head_sum = jnp.asarray(np.kron(np.eye(HG_HEADS), np.ones((HG_V, HG_V))), F32)
    wc_np, maskc_np, vmask_np = _hgrn_fast_tables()
    wc_tab, maskc_tab, vmask_tab = jnp.asarray(wc_np, BF16), jnp.asarray(maskc_np, F32), jnp.asarray(vmask_np, BF16)

    out = None
    for layer in range(DEPTH):
        last = layer == DEPTH - 1
        mod3 = mod[layer]
        (s5u, cqn, kvn, hq, lff, lfb, kkf, kkb, hi, hgs, gates) = _input_projection(
            layer, x_lat, x_ctx, ctx_row0, mod3, hg_scal, vec(norm1_g), w_in_p, vec(mla_qa_g), vec(mla_kva_g))

        yf, yb = [_from_time_major(y) for y in _s5_states(layer, _to_time_major(s5u), bmat, cmat, a_tab)]

        q, k, v = _qkv(layer, cqn, kvn, cos_t, sin_t, wq, wk, wv)
        att_lat = _attention(q, k, v, latent=True)
        att_ctx = None if last else _attention(q, k, v, latent=False)

        of, ob = _hgrn(hq, kkf, kkb, lff, lfb, hi, w_tab, mask_tab, wc_tab, maskc_tab, vmask_tab)

        n_blocks = NB_LAT if last else NB_TOK
        x1, h2, eid, gate = _merge(layer, n_blocks, x_lat, x_ctx, ctx_row0, mod3, yf, yb, s5u, of, ob, hgs,
                                   att_lat, att_ctx, gates, head_sum, merge_consts)

        y_moe = _moe(h2, eid[:, :MOE_TOP_K], moe_w1, moe_w3, moe_w2, layer)
        res = _combine(n_blocks, x1, y_moe, gate, mod3, final_norm_g[None, :], final=last)
        if last:
            out = res
        else:
            x_lat, x_ctx, ctx_row0 = res, res, N_LAT
    return out.reshape(BATCH, SEQ, D_MODEL)
```

```python
import functools
import math

import jax
import jax.numpy as jnp
from jax import lax
import numpy as np
from jax.experimental import pallas as pl
from jax.experimental.pallas import tpu as pltpu

F32 = jnp.float32
BF16 = jnp.bfloat16
HIGHEST = lax.Precision.HIGHEST

D_MODEL = 1024
BATCH = 8
SEQ = 2048
DEPTH = 2
GRID_W = 64
CTX_LEN = 256
NORM_EPS = 1e-6

S5_WIDTH = D_MODEL // 4
S5_GROUP_CH = 16
S5_GROUPS = S5_WIDTH // S5_GROUP_CH
S5_STATE = 64
S5_LANES = S5_GROUPS * S5_STATE

MLA_HEADS = D_MODEL // 128
MLA_NOPE = 64
MLA_ROPE = 32
MLA_V = 64
MLA_Q_LORA = D_MODEL // 4
MLA_KV_LORA = D_MODEL // 8
MLA_SCALE = 1.0 / math.sqrt(MLA_NOPE + MLA_ROPE)
MLA_HEAD_PAD = 128
ROPE_PAIRS = MLA_ROPE // 4
ROPE_BASE = 10000.0

HG_HEADS = D_MODEL // 256
HG_K = 128
HG_V = 64
HG_CHUNK = 64
HG_LEVELS = 6
HG_SETS = 2 + HG_LEVELS
HG_FAST_CHUNK = 32
HG_FAST_MIN = -60.0

MOE_GROUPS = 4
MOE_EXPERTS_PER_GROUP = 8
MOE_EXPERTS = MOE_GROUPS * MOE_EXPERTS_PER_GROUP
MOE_TOP_K = 2
MOE_HIDDEN = D_MODEL // 2
MOE_ROWS = 256
MOE_DMA_CHUNK = 32

N_LAT = BATCH * SEQ
N_CTX = BATCH * CTX_LEN
N_TOK = N_LAT + N_CTX
RB = CTX_LEN
WB = 2 * RB
NB_LAT = N_LAT // RB
NB_CTX = N_CTX // RB
NB_TOK = N_TOK // RB
LAT_BLOCKS = SEQ // RB
SEQ_BLOCKS = LAT_BLOCKS + 1

C_S5 = 0
C_CQ = 256
C_KV = 512
C_HQ = 768
C_ZF = 1280
C_ZB = 1792
C_HI = 2304
C_HG = 2560
C_GATE = 2816
W_IN_COLS = C_GATE + 3 * D_MODEL

VMEM_LIMIT = 56 * 1024 * 1024


def _params(*sem):
    return pltpu.CompilerParams(dimension_semantics=sem, vmem_limit_bytes=VMEM_LIMIT)


def _rms(v):
    return v * lax.rsqrt(jnp.mean(v * v, axis=-1, keepdims=True) + NORM_EPS)


def _mod_row(i, rb=RB):
    return jnp.where(i < N_LAT // rb, i // (SEQ // rb), BATCH)


def _resident(a):
    return pl.BlockSpec(a.shape, lambda *_: (0,) * a.ndim, pipeline_mode=pl.Buffered(1))


def _layer_spec(a, layer):
    return pl.BlockSpec((1,) + a.shape[1:], lambda *_: (layer,) + (0,) * (a.ndim - 1),
                        pipeline_mode=pl.Buffered(1))


def _split_row_specs(rb, w, ctx_row0):
    n_lat = N_LAT // rb
    lat = pl.BlockSpec((rb, w), lambda i, *_: (jnp.minimum(i, n_lat - 1), 0))
    ctx = pl.BlockSpec((rb, w), lambda i, *_: (ctx_row0 // rb + jnp.maximum(i - n_lat, 0), 0))
    return lat, ctx


def _pick_rows(lat_ref, ctx_ref, rb):
    return jnp.where(pl.program_id(0) < N_LAT // rb, lat_ref[...], ctx_ref[...])


def _mod_kernel(c_ref, w_ref, b_ref, o_ref):
    c = c_ref[...]
    s = c * jax.nn.sigmoid(c)
    o_ref[0] = jnp.dot(s, w_ref[0], precision=HIGHEST, preferred_element_type=F32) + b_ref[0]


def _modulation(c_rows, mod_w, mod_b):
    bn = 1536
    return pl.pallas_call(
        _mod_kernel,
        grid=(DEPTH, 6 * D_MODEL // bn),
        in_specs=[
            pl.BlockSpec((16, D_MODEL), lambda l, j: (0, 0)),
            pl.BlockSpec((1, D_MODEL, bn), lambda l, j: (l, 0, j)),
            pl.BlockSpec((1, 1, bn), lambda l, j: (l, 0, j)),
        ],
        out_specs=pl.BlockSpec((1, 16, bn), lambda l, j: (l, 0, j)),
        out_shape=jax.ShapeDtypeStruct((DEPTH, 16, 6 * D_MODEL), F32),
        compiler_params=_params("arbitrary", "arbitrary"),
        name="adaln_mod",
    )(c_rows, mod_w, mod_b.reshape(DEPTH, 1, 6 * D_MODEL))


def _log_sigmoid(z):
    return jnp.minimum(z, 0.0) - jnp.log1p(jnp.exp(-jnp.abs(z)))


def _in_kernel(hs_ref, xl_ref, xc_ref, mod_ref, g1_ref, w_ref, qag_ref, kvg_ref,
               s5u_ref, cqn_ref, kvn_ref, hq_ref, lff_ref, lfb_ref, kkf_ref, kkb_ref,
               hi_ref, hgs_ref, gates_ref, h_ref, *, layer):
    x = _pick_rows(xl_ref, xc_ref, WB)
    xn = _rms(x) * g1_ref[0]
    h_ref[...] = (xn * (1.0 + mod_ref[0, 1:2, :]) + mod_ref[0, 0:1, :]).astype(BF16)

    def mm(a, b):
        return jnp.dot(h_ref[...], w_ref[0, :, a:b], preferred_element_type=F32)

    s5u_ref[...] = mm(C_S5, C_CQ).astype(BF16)
    cqn_ref[...] = (_rms(mm(C_CQ, C_KV)) * qag_ref[0]).astype(BF16)
    kv = mm(C_KV, C_HQ)
    ckvn = _rms(kv[:, :MLA_KV_LORA]) * kvg_ref[0]
    kvn_ref[...] = jnp.concatenate([ckvn, kv[:, MLA_KV_LORA:]], axis=1).astype(BF16)
    hq_ref[...] = mm(C_HQ, C_ZF).astype(BF16)

    log_lb = hs_ref[3 * layer]
    log_1m_lb = hs_ref[3 * layer + 1]
    one_m_lb = hs_ref[3 * layer + 2]
    for c0, lf_ref, kk_ref in ((C_ZF, lff_ref, kkf_ref), (C_ZB, lfb_ref, kkb_ref)):
        z = mm(c0, c0 + HG_HEADS * HG_K)
        b = log_1m_lb + _log_sigmoid(z)
        m = jnp.maximum(b, log_lb)
        lf_ref[...] = m + jnp.log1p(jnp.exp(-jnp.abs(b - log_lb)))
        kk_ref[...] = (one_m_lb * jax.nn.sigmoid(-z)).astype(BF16)

    hi_ref[...] = mm(C_HI, C_HG).astype(BF16)
    g = mm(C_HG, C_GATE)
    hgs_ref[...] = (g * jax.nn.sigmoid(g)).astype(BF16)
    gw = 512
    for j in range(3 * D_MODEL // gw):
        c0 = C_GATE + j * gw
        gates_ref[:, j * gw:(j + 1) * gw] = jax.nn.sigmoid(mm(c0, c0 + gw)).astype(BF16)


def _input_projection(layer, x_lat, x_ctx, ctx_row0, mod3, hg_scal, g1, w_in, qa_g, kva_g):
    row = lambda w: pl.BlockSpec((WB, w), lambda i: (i, 0))
    x_specs = _split_row_specs(WB, D_MODEL, ctx_row0)
    shapes = [
        (S5_WIDTH, BF16), (MLA_Q_LORA, BF16), (256, BF16), (HG_HEADS * HG_K, BF16),
        (HG_HEADS * HG_K, F32), (HG_HEADS * HG_K, F32), (HG_HEADS * HG_K, BF16), (HG_HEADS * HG_K, BF16),
        (HG_HEADS * HG_V, BF16), (HG_HEADS * HG_V, BF16), (3 * D_MODEL, BF16),
    ]
    return pl.pallas_call(
        functools.partial(_in_kernel, layer=layer),
        grid=(N_TOK // WB,),
        in_specs=[
            pl.BlockSpec(memory_space=pltpu.SMEM),
            *x_specs,
            pl.BlockSpec((1, 6, D_MODEL), lambda i: (_mod_row(i, WB), 0, 0)),
            _layer_spec(g1, layer), _layer_spec(w_in, layer), _layer_spec(qa_g, layer), _layer_spec(kva_g, layer),
        ],
        out_specs=[row(w) for w, _ in shapes],
        out_shape=[jax.ShapeDtypeStruct((N_TOK, w), dt) for w, dt in shapes],
        scratch_shapes=[pltpu.VMEM((WB, D_MODEL), BF16)],
        compiler_params=_params("arbitrary"),
        name="input_projection",
    )(hg_scal, x_lat, x_ctx, mod3, g1, w_in, qa_g, kva_g)


def _seq_block(b, j):
    return jnp.where(j == 0, NB_LAT + b, b * LAT_BLOCKS + j - 1)


def _seq_block_rev(b, j):
    return jnp.where(j == 0, NB_LAT + b, b * LAT_BLOCKS + LAT_BLOCKS - j)


S5_TB = 128
S5_ROWS = S5_TB * BATCH
S5_STEPS = (CTX_LEN + SEQ) // S5_TB
S5_CTX_STEPS = CTX_LEN // S5_TB


def _s5_kernel(uf_ref, ub_ref, bmat_ref, cmat_ref, a_ref, yf_ref, yb_ref, bu_ref, carry_ref):
    j = pl.program_id(0)

    @pl.when(j == 0)
    def _():
        carry_ref[...] = jnp.zeros_like(carry_ref)

    for d, (u_ref, y_ref) in enumerate(((uf_ref, yf_ref), (ub_ref, yb_ref))):
        bu_ref[...] = jnp.dot(u_ref[...], bmat_ref[0, d], preferred_element_type=F32)
        are = a_ref[0, d, 0]
        aim = a_ref[0, d, 1]

        def body(t, carry, d=d, are=are, aim=aim):
            hre, him = carry
            tt = (S5_TB - 1 - t) if d == 1 else t
            r0 = pl.multiple_of(tt * BATCH, BATCH)
            nre = are * hre - aim * him + bu_ref[pl.ds(r0, BATCH), 0:S5_LANES]
            nim = are * him + aim * hre + bu_ref[pl.ds(r0, BATCH), S5_LANES:2 * S5_LANES]
            bu_ref[pl.ds(r0, BATCH), 0:S5_LANES] = nre
            bu_ref[pl.ds(r0, BATCH), S5_LANES:2 * S5_LANES] = nim
            return nre, nim

        hre, him = lax.fori_loop(0, S5_TB, body, (carry_ref[d, 0], carry_ref[d, 1]), unroll=4)
        carry_ref[d, 0] = hre
        carry_ref[d, 1] = him
        hl = S5_LANES // 2
        hw = S5_WIDTH // 2
        for half in range(2):
            states = jnp.concatenate([bu_ref[:, half * hl:(half + 1) * hl],
                                      bu_ref[:, S5_LANES + half * hl:S5_LANES + (half + 1) * hl]], axis=1)
            y_ref[:, half * hw:(half + 1) * hw] = jnp.dot(states.astype(BF16), cmat_ref[0, d, half],
                                                          preferred_element_type=F32)


def _s5_block_rev(j):
    return jnp.where(j < S5_CTX_STEPS, S5_CTX_STEPS - 1 - j, S5_STEPS - 1 + S5_CTX_STEPS - j)


def _s5_states(layer, u_tm, bmat, cmat, a_tab):
    full = lambda a: _layer_spec(a, layer)
    fwd = pl.BlockSpec((S5_ROWS, S5_WIDTH), lambda j: (j, 0))
    bwd = pl.BlockSpec((S5_ROWS, S5_WIDTH), lambda j: (_s5_block_rev(j), 0))
    return pl.pallas_call(
        _s5_kernel,
        grid=(S5_STEPS,),
        in_specs=[fwd, bwd, full(bmat), full(cmat), full(a_tab)],
        out_specs=[fwd, bwd],
        out_shape=[jax.ShapeDtypeStruct((S5_STEPS * S5_ROWS, S5_WIDTH), F32)] * 2,
        scratch_shapes=[pltpu.VMEM((S5_ROWS, 2 * S5_LANES), F32), pltpu.VMEM((2, 2, BATCH, S5_LANES), F32)],
        compiler_params=_params("arbitrary"),
        name="s5_scan",
    )(u_tm, u_tm, bmat, cmat, a_tab)


def _to_time_major(a):
    w = a.shape[1]
    seq = jnp.concatenate([a[N_LAT:].reshape(BATCH, CTX_LEN, w), a[:N_LAT].reshape(BATCH, SEQ, w)], axis=1)
    return seq.transpose(1, 0, 2).reshape((CTX_LEN + SEQ) * BATCH, w)


def _from_time_major(a):
    w = a.shape[1]
    seq = a.reshape(CTX_LEN + SEQ, BATCH, w).transpose(1, 0, 2)
    return jnp.concatenate([seq[:, CTX_LEN:].reshape(N_LAT, w), seq[:, :CTX_LEN].reshape(N_CTX, w)], axis=0)


def _qkv_kernel(cqn_ref, kvn_ref, cos_ref, sin_ref, wq_ref, wk_ref, wv_ref, q_ref, k_ref, v_ref):
    cos = jnp.concatenate([cos_ref[...]] * MLA_HEADS, axis=1)
    sin = jnp.concatenate([sin_ref[...]] * MLA_HEADS, axis=1)
    w = MLA_HEADS * MLA_HEAD_PAD
    q2 = jnp.dot(cqn_ref[...], wq_ref[0], preferred_element_type=F32)
    q_ref[...] = ((q2[:, :w] * cos + q2[:, w:] * sin) * MLA_SCALE).astype(BF16)
    kv = kvn_ref[...]
    k2 = jnp.dot(kv, wk_ref[0], preferred_element_type=F32)
    k_ref[...] = (k2[:, :w] * cos + k2[:, w:] * sin).astype(BF16)
    v_ref[...] = jnp.dot(kv, wv_ref[0], preferred_element_type=F32).astype(BF16)


def _qkv(layer, cqn, kvn, cos_t, sin_t, wq, wk, wv):
    row = lambda w: pl.BlockSpec((WB, w), lambda i: (i, 0))
    full = lambda a: _layer_spec(a, layer)
    pos = pl.BlockSpec((WB, MLA_HEAD_PAD),
                       lambda i: (jnp.where(i < N_LAT // WB, i % (SEQ // WB), SEQ // WB), 0))
    w = MLA_HEADS * MLA_HEAD_PAD
    return pl.pallas_call(
        _qkv_kernel,
        grid=(N_TOK // WB,),
        in_specs=[row(MLA_Q_LORA), row(256), pos, pos, full(wq), full(wk), full(wv)],
        out_specs=[row(w), row(w), row(MLA_HEADS * MLA_V)],
        out_shape=[jax.ShapeDtypeStruct((N_TOK, w), BF16), jax.ShapeDtypeStruct((N_TOK, w), BF16),
                   jax.ShapeDtypeStruct((N_TOK, MLA_HEADS * MLA_V), BF16)],
        compiler_params=_params("arbitrary"),
        name="mla_qkv",
    )(cqn, kvn, cos_t, sin_t, wq, wk, wv)


def _attn_kernel(*refs, n_kv):
    q_ref = refs[0]
    k_refs = refs[1:1 + n_kv]
    v_refs = refs[1 + n_kv:1 + 2 * n_kv]
    o_ref = refs[1 + 2 * n_kv]
    nt = (((1,), (1,)), ((), ()))
    for h in range(MLA_HEADS):
        q = q_ref[:, h * MLA_HEAD_PAD:(h + 1) * MLA_HEAD_PAD]
        s = [lax.dot_general(q, k_ref[:, h * MLA_HEAD_PAD:(h + 1) * MLA_HEAD_PAD], nt,
                             preferred_element_type=F32) for k_ref in k_refs]
        m = functools.reduce(jnp.maximum, [jnp.max(si, axis=-1, keepdims=True) for si in s])
        p = [jnp.exp(si - m) for si in s]
        l = functools.reduce(jnp.add, [jnp.sum(pi, axis=-1, keepdims=True) for pi in p])
        o = functools.reduce(jnp.add, [
            jnp.dot(pi.astype(BF16), v_ref[:, h * MLA_V:(h + 1) * MLA_V], preferred_element_type=F32)
            for pi, v_ref in zip(p, v_refs)])
        o_ref[:, h * MLA_V:(h + 1) * MLA_V] = (o / l).astype(BF16)


def _attention(q, k, v, latent):
    w = MLA_HEADS * MLA_HEAD_PAD
    wv = MLA_HEADS * MLA_V
    if latent:
        tq = WB
        nq = SEQ // tq
        q_spec = pl.BlockSpec((tq, w), lambda b, j: (b * nq + j, 0))
        k_specs = [pl.BlockSpec((SEQ, w), lambda b, j: (b, 0)),
                   pl.BlockSpec((CTX_LEN, w), lambda b, j: (NB_LAT + b, 0))]
        v_specs = [pl.BlockSpec((SEQ, wv), lambda b, j: (b, 0)),
                   pl.BlockSpec((CTX_LEN, wv), lambda b, j: (NB_LAT + b, 0))]
        o_spec = pl.BlockSpec((tq, wv), lambda b, j: (b * nq + j, 0))
        n_out = N_LAT
    else:
        tq = CTX_LEN
        nq = 1
        q_spec = pl.BlockSpec((tq, w), lambda b, j: (NB_LAT + b, 0))
        k_specs = [pl.BlockSpec((CTX_LEN, w), lambda b, j: (NB_LAT + b, 0))]
        v_specs = [pl.BlockSpec((CTX_LEN, wv), lambda b, j: (NB_LAT + b, 0))]
        o_spec = pl.BlockSpec((tq, wv), lambda b, j: (b, 0))
        n_out = N_CTX
    n_kv = len(k_specs)
    return pl.pallas_call(
        functools.partial(_attn_kernel, n_kv=n_kv),
        grid=(BATCH, nq),
        in_specs=[q_spec] + k_specs + v_specs,
        out_specs=o_spec,
        out_shape=jax.ShapeDtypeStruct((n_out, wv), BF16),
        compiler_params=_params("arbitrary", "arbitrary"),
        name="mla_attention_lat" if latent else "mla_attention_ctx",
    )(q, *([k] * n_kv), *([v] * n_kv))


def _hgrn_tables():
    c = HG_CHUNK
    w = np.zeros((2, HG_SETS * c, c), np.float32)
    mask = np.zeros((2, HG_LEVELS, c, c), np.float32)
    idx = np.arange(c)
    for r in range(c):
        w[0, r, :r + 1] = 1
        w[1, r, r:] = 1
        w[0, c + r, r + 1:] = 1
        w[1, c + r, :r] = 1
    for l in range(HG_LEVELS):
        m = c >> (l + 1)
        for r in range(c):
            base = (r // (2 * m)) * 2 * m
            mid = base + m
            later = r >= mid
            row = (2 + l) * c + r
            if later:
                w[0, row, mid:r + 1] = 1
                w[1, row, mid:r] = 1
            else:
                w[0, row, r + 1:mid] = 1
                w[1, row, r:mid] = 1
        same = (idx[:, None] // (2 * m)) == (idx[None, :] // (2 * m))
        q_later = (idx[:, None] % (2 * m)) >= m
        k_later = (idx[None, :] % (2 * m)) >= m
        mask[0, l] = same & q_later & ~k_later
        mask[1, l] = same & ~q_later & k_later
    return w, mask


def _hgrn_dir(d, q_ref, k_ref, lf_ref, v_ref, w_ref, mask_ref, st_ref, o_ref):
    c = HG_CHUNK
    n_chunks = RB // c
    nt = (((1,), (1,)), ((), ()))
    tn = (((0,), (0,)), ((), ()))
    last = c - 1 if d == 0 else 0
    wsel = w_ref[d]
    e = []
    for ci in range(n_chunks):
        g = lf_ref[ci * c:(ci + 1) * c, :]
        g_hi = g.astype(BF16)
        g_lo = (g - g_hi.astype(F32)).astype(BF16)
        expo = jnp.dot(wsel, jnp.concatenate([g_hi, g_lo], axis=0), preferred_element_type=F32)
        e.append(jnp.exp(expo))

    def rows(s):
        return jnp.concatenate([e[ci][s * c:(s + 1) * c] for ci in range(n_chunks)], axis=0)

    q = q_ref[...].astype(F32)
    k = k_ref[...].astype(F32)
    q_state = (q * rows(0)).astype(BF16)
    k_state = (k * rows(1)).astype(BF16)
    q_lvl, k_lvl = [], []
    for l in range(HG_LEVELS):
        el = rows(2 + l)
        q_lvl.append((q * el).astype(BF16))
        k_lvl.append((k * el).astype(BF16))
    qk = q * k
    order = range(n_chunks) if d == 0 else range(n_chunks - 1, -1, -1)
    for h in range(HG_HEADS):
        ks = slice(h * HG_K, (h + 1) * HG_K)
        vs = slice(h * HG_V, (h + 1) * HG_V)
        v = v_ref[:, vs]
        scores = jnp.zeros((RB, RB), F32)
        for l in range(HG_LEVELS):
            p = lax.dot_general(q_lvl[l][:, ks], k_lvl[l][:, ks], nt, preferred_element_type=F32)
            scores = scores + p * mask_ref[d, l]
        o = jnp.dot(scores.astype(BF16), v, preferred_element_type=F32)
        o = o + jnp.sum(qk[:, ks], axis=-1, keepdims=True) * v.astype(F32)
        st = st_ref[d, h]
        o_state = [None] * n_chunks
        for ci in order:
            r = slice(ci * c, (ci + 1) * c)
            o_state[ci] = lax.dot_general(q_state[r, ks], st.astype(BF16), nt, preferred_element_type=F32)
            inc = lax.dot_general(v[r], k_state[r, ks], tn, preferred_element_type=F32)
            st = st * e[ci][last:last + 1, ks] + inc
        st_ref[d, h] = st
        o_ref[:, vs] = o + jnp.concatenate(o_state, axis=0)


def _hgrn_fast_tables():
    c = HG_FAST_CHUNK
    idx = np.arange(RB)
    same = (idx[:, None] // c) == (idx[None, :] // c)
    le = idx[None, :] <= idx[:, None]
    ge = idx[None, :] >= idx[:, None]
    wc = np.stack([same & le, same & ge]).astype(np.float32)
    col_chunk = np.arange((RB // c) * HG_V) // HG_V
    vmask = ((idx[:, None] // c) == col_chunk[None, :]).astype(np.float32)
    return np.concatenate([wc, wc], axis=-1), wc, vmask


def _hgrn_dir_fast(d, q_ref, k_ref, cum, tot, v_ref, maskc_ref, vmask_ref, st_ref, o_ref):
    c = HG_FAST_CHUNK
    n_chunks = RB // c
    nt = (((1,), (1,)), ((), ()))
    tn = (((0,), (0,)), ((), ()))
    q = q_ref[...].astype(F32)
    k = k_ref[...].astype(F32)
    qa = (q * jnp.exp(cum)).astype(BF16)
    kb = (k * jnp.exp(-cum)).astype(BF16)
    k_state = (k * jnp.exp(tot - cum)).astype(BF16)
    decay = jnp.exp(tot)
    keep = maskc_ref[d] > 0.5
    vmask = vmask_ref[...]
    order = range(n_chunks) if d == 0 else range(n_chunks - 1, -1, -1)
    for h in range(HG_HEADS):
        ks = slice(h * HG_K, (h + 1) * HG_K)
        vs = slice(h * HG_V, (h + 1) * HG_V)
        v = v_ref[:, vs]
        p = lax.dot_general(qa[:, ks], kb[:, ks], nt, preferred_element_type=F32)
        o = jnp.dot(jnp.where(keep, p, 0.0).astype(BF16), v, preferred_element_type=F32)
        v_by_chunk = jnp.concatenate([v] * n_chunks, axis=1) * vmask
        inc = lax.dot_general(v_by_chunk, k_state[:, ks], tn, preferred_element_type=F32)
        st = st_ref[d, h]
        entering = [None] * n_chunks
        for ci in order:
            entering[ci] = st
            st = st * decay[ci * c:ci * c + 1, ks] + inc[ci * HG_V:(ci + 1) * HG_V]
        st_ref[d, h] = st
        s_all = jnp.concatenate(entering, axis=0).astype(BF16)
        o_all = lax.dot_general(qa[:, ks], s_all, nt, preferred_element_type=F32) * vmask.astype(F32)
        o_fold = functools.reduce(jnp.add, [o_all[:, j * 128:(j + 1) * 128] for j in range(n_chunks * HG_V // 128)])
        o_state = (o_fold + pltpu.roll(o_fold, HG_V, 1))[:, :HG_V]
        o_ref[:, vs] = o + o_state


def _hgrn_kernel(qf_ref, kf_ref, lff_ref, vf_ref, qb_ref, kb_ref, lfb_ref, vb_ref, w_ref, mask_ref,
                 wc_ref, maskc_ref, vmask_ref, of_ref, ob_ref, st_ref):
    j = pl.program_id(1)

    @pl.when(j == 0)
    def _():
        st_ref[...] = jnp.zeros_like(st_ref)

    c = HG_FAST_CHUNK
    dirs = ((0, qf_ref, kf_ref, lff_ref, vf_ref, of_ref), (1, qb_ref, kb_ref, lfb_ref, vb_ref, ob_ref))
    for d, q_ref, k_ref, lf_ref, v_ref, o_ref in dirs:
        g_hi, g_lo = _split_bf16(lf_ref[...])
        cum = jnp.dot(wc_ref[d], jnp.concatenate([g_hi, g_lo], axis=0), preferred_element_type=F32)
        edge = c - 1 if d == 0 else 0
        totals = [cum[ci * c + edge:ci * c + edge + 1] for ci in range(RB // c)]
        tot = jnp.concatenate([jnp.broadcast_to(t, (c, t.shape[1])) for t in totals], axis=0)
        safe = jnp.min(jnp.concatenate(totals, axis=0)) >= HG_FAST_MIN

        @pl.when(safe)
        def _(d=d, q_ref=q_ref, k_ref=k_ref, v_ref=v_ref, o_ref=o_ref, cum=cum, tot=tot):
            _hgrn_dir_fast(d, q_ref, k_ref, cum, tot, v_ref, maskc_ref, vmask_ref, st_ref, o_ref)

        @pl.when(jnp.logical_not(safe))
        def _(d=d, q_ref=q_ref, k_ref=k_ref, lf_ref=lf_ref, v_ref=v_ref, o_ref=o_ref):
            _hgrn_dir(d, q_ref, k_ref, lf_ref, v_ref, w_ref, mask_ref, st_ref, o_ref)


def _hgrn(hq, kkf, kkb, lff, lfb, hi, w_tab, mask_tab, wc_tab, maskc_tab, vmask_tab):
    full = _resident
    fwd = lambda w: pl.BlockSpec((RB, w), lambda b, j: (_seq_block(b, j), 0))
    bwd = lambda w: pl.BlockSpec((RB, w), lambda b, j: (_seq_block_rev(b, j), 0))
    wk = HG_HEADS * HG_K
    wv = HG_HEADS * HG_V
    return pl.pallas_call(
        _hgrn_kernel,
        grid=(BATCH, SEQ_BLOCKS),
        in_specs=[fwd(wk), fwd(wk), fwd(wk), fwd(wv), bwd(wk), bwd(wk), bwd(wk), bwd(wv),
                  full(w_tab), full(mask_tab), full(wc_tab), full(maskc_tab), full(vmask_tab)],
        out_specs=[fwd(wv), bwd(wv)],
        out_shape=[jax.ShapeDtypeStruct((N_TOK, wv), F32)] * 2,
        scratch_shapes=[pltpu.VMEM((2, HG_HEADS, HG_V, HG_K), F32)],
        compiler_params=_params("arbitrary", "arbitrary"),
        name="hgrn2_scan",
    )(hq, kkf, lff, hi, hq, kkb, lfb, hi, w_tab, mask_tab, wc_tab, maskc_tab, vmask_tab)


def _gelu_tanh(x):
    return 0.5 * x * (1.0 + jnp.tanh(math.sqrt(2.0 / math.pi) * (x + 0.044715 * (x * x * x))))


def _split_bf16(a):
    hi = a.astype(BF16)
    return hi, (a - hi.astype(F32)).astype(BF16)


def _merge_kernel(xl_ref, xc_ref, mod_ref, yf_ref, yb_ref, u_ref, of_ref, ob_ref, hgs_ref, attl_ref, attc_ref,
                  gates_ref, s5d_ref, gluw_ref, glub_ref, hgg_ref, hsum_ref, wpa_ref, wpb_ref, wpc_ref, wout_ref,
                  g2_ref, wrh_ref, wrl_ref, br_ref,
                  x1_ref, h2_ref, eid_ref, gate_ref):
    y = yf_ref[...] + yb_ref[...] + s5d_ref[0] * u_ref[...].astype(F32)
    y = _gelu_tanh(y)
    y = y * jax.nn.sigmoid(jnp.dot(y.astype(BF16), gluw_ref[0], preferred_element_type=F32) + glub_ref[0])
    o = of_ref[...] + ob_ref[...]
    sq_hi, sq_lo = _split_bf16(o * o)
    ms = (jnp.dot(sq_hi, hsum_ref[...], preferred_element_type=F32)
          + jnp.dot(sq_lo, hsum_ref[...], preferred_element_type=F32)) * (1.0 / HG_V)
    o = o * lax.rsqrt(ms + NORM_EPS) * hgg_ref[0] * hgs_ref[...].astype(F32)

    d = D_MODEL
    att = _pick_rows(attl_ref, attc_ref, WB)
    merged = (gates_ref[:, 0:d].astype(F32) * jnp.dot(y.astype(BF16), wpa_ref[0], preferred_element_type=F32)
              + gates_ref[:, d:2 * d].astype(F32) * jnp.dot(att, wpb_ref[0], preferred_element_type=F32)
              + gates_ref[:, 2 * d:3 * d].astype(F32) * jnp.dot(o.astype(BF16), wpc_ref[0], preferred_element_type=F32))
    y_out = jnp.dot(merged.astype(BF16), wout_ref[0], preferred_element_type=F32)
    x1 = _pick_rows(xl_ref, xc_ref, WB) + mod_ref[0, 2:3, :] * y_out
    x1_ref[...] = x1
    h2 = _rms(x1) * g2_ref[0] * (1.0 + mod_ref[0, 4:5, :]) + mod_ref[0, 3:4, :]
    h2_ref[...] = h2

    h_hi, h_lo = _split_bf16(h2)
    logits = (jnp.dot(h_hi, wrh_ref[0], preferred_element_type=F32)
              + jnp.dot(h_lo, wrh_ref[0], preferred_element_type=F32)
              + jnp.dot(h_hi, wrl_ref[0], preferred_element_type=F32)) + br_ref[0]
    lane = lax.broadcasted_iota(jnp.int32, logits.shape, 1).astype(F32)
    neg = -jnp.inf
    glog = jnp.where(lane < MOE_GROUPS, logits, neg)
    gmax = jnp.max(glog, axis=-1, keepdims=True)
    gidx = jnp.min(jnp.where(glog == gmax, lane, 1e9), axis=-1, keepdims=True)
    g_w = 1.0 / jnp.sum(jnp.exp(glog - gmax), axis=-1, keepdims=True)
    e_lo = 32.0 + gidx * MOE_EXPERTS_PER_GROUP
    elog = jnp.where((lane >= e_lo) & (lane < e_lo + MOE_EXPERTS_PER_GROUP), logits, neg)
    v1 = jnp.max(elog, axis=-1, keepdims=True)
    i1 = jnp.min(jnp.where(elog == v1, lane, 1e9), axis=-1, keepdims=True)
    elog2 = jnp.where(lane == i1, neg, elog)
    v2 = jnp.max(elog2, axis=-1, keepdims=True)
    i2 = jnp.min(jnp.where(elog2 == v2, lane, 1e9), axis=-1, keepdims=True)
    e2 = jnp.exp(v2 - v1)
    gate1 = g_w / (1.0 + e2)
    gate2 = g_w * e2 / (1.0 + e2)
    eid_ref[...] = jnp.where(lane == 0.0, i1 - 32.0, jnp.where(lane == 1.0, i2 - 32.0, 0.0)).astype(jnp.int32)
    gate_ref[...] = jnp.where(lane == 0.0, gate1, jnp.where(lane == 1.0, gate2, 0.0))


def _merge(layer, n_blocks, x_lat, x_ctx, ctx_row0, mod3, yf, yb, s5u, of, ob, hgs, att_lat, att_ctx, gates,
           head_sum, consts):
    row = lambda w: pl.BlockSpec((WB, w), lambda i: (i, 0))
    n = n_blocks * RB
    wv = MLA_HEADS * MLA_V
    if att_ctx is None:
        att_ctx = att_lat
    const_specs = [_layer_spec(a, layer) for a in consts]
    const_specs.insert(4, _resident(head_sum))
    return pl.pallas_call(
        _merge_kernel,
        grid=(n // WB,),
        in_specs=[*_split_row_specs(WB, D_MODEL, ctx_row0),
                  pl.BlockSpec((1, 6, D_MODEL), lambda i: (_mod_row(i, WB), 0, 0)),
                  row(S5_WIDTH), row(S5_WIDTH), row(S5_WIDTH), row(HG_HEADS * HG_V), row(HG_HEADS * HG_V),
                  row(HG_HEADS * HG_V), *_split_row_specs(WB, wv, 0), row(3 * D_MODEL)] + const_specs,
        out_specs=[row(D_MODEL), row(D_MODEL), row(128), row(128)],
        out_shape=[jax.ShapeDtypeStruct((n, D_MODEL), F32), jax.ShapeDtypeStruct((n, D_MODEL), F32),
                   jax.ShapeDtypeStruct((n, 128), jnp.int32), jax.ShapeDtypeStruct((n, 128), F32)],
        compiler_params=_params("arbitrary"),
        name="merge_router",
    )(x_lat, x_ctx, mod3, yf, yb, s5u, of, ob, hgs, att_lat, att_ctx, gates,
      *consts[:4], head_sum, *consts[4:])


def _moe_kernel(bexp_ref, nblk_ref, nv_ref, src_ref, srcn_ref, dst_ref, h_hbm, w1_ref, w3_ref, w2_ref, y_hbm,
                xbuf, ybuf, w1b, w3b, w2b, gsem, ssem):
    i = pl.program_id(0)
    n_used = nblk_ref[0]
    slot = lax.rem(i, 2)
    last_blk = pl.num_programs(0) - 1

    def valid_rows(blk):
        return nv_ref[jnp.clip(blk, 0, last_blk)]

    def each_chunk(nv, fn):
        for c0 in range(0, MOE_ROWS, MOE_DMA_CHUNK):
            @pl.when(c0 < nv)
            def _(c0=c0):
                fn(c0)

    def gather_start(idx_ref, s, nv):
        def issue(c0):
            for r in range(c0, c0 + MOE_DMA_CHUNK):
                pltpu.make_async_copy(h_hbm.at[pl.ds(idx_ref[0, 0, r], 1), :], xbuf.at[s, pl.ds(r, 1), :],
                                      gsem.at[s]).start()
        each_chunk(nv, issue)

    def gather_wait(s, nv):
        each_chunk(nv, lambda c0: pltpu.make_async_copy(
            h_hbm.at[pl.ds(0, MOE_DMA_CHUNK), :], xbuf.at[s, pl.ds(c0, MOE_DMA_CHUNK), :], gsem.at[s]).wait())

    def scatter_start(s, nv):
        def issue(c0):
            for r in range(c0, c0 + MOE_DMA_CHUNK):
                pltpu.make_async_copy(ybuf.at[s, pl.ds(r, 1), :], y_hbm.at[pl.ds(dst_ref[0, 0, r], 1), :],
                                      ssem.at[s]).start()
        each_chunk(nv, issue)

    def scatter_wait(s, nv):
        each_chunk(nv, lambda c0: pltpu.make_async_copy(
            ybuf.at[s, pl.ds(c0, MOE_DMA_CHUNK), :], y_hbm.at[pl.ds(0, MOE_DMA_CHUNK), :], ssem.at[s]).wait())

    @pl.when(i < n_used)
    def _():
        @pl.when(i == 0)
        def _():
            xbuf[...] = jnp.zeros_like(xbuf)
            gather_start(src_ref, 0, valid_rows(0))
            ybuf[1] = jnp.zeros((MOE_ROWS, D_MODEL), F32)
            n_real = y_hbm.shape[0] - 2 * MOE_ROWS
            for half in range(2):
                fill = pltpu.make_async_copy(ybuf.at[1], y_hbm.at[pl.ds(n_real + half * MOE_ROWS, MOE_ROWS), :],
                                             ssem.at[1])
                fill.start()
                fill.wait()

        @pl.when(i + 1 < n_used)
        def _():
            gather_start(srcn_ref, 1 - slot, valid_rows(i + 1))

        @pl.when((i == 0) | (bexp_ref[i] != bexp_ref[jnp.maximum(i - 1, 0)]))
        def _():
            w1b[...] = w1_ref[0, 0].astype(BF16)
            w3b[...] = w3_ref[0, 0].astype(BF16)
            w2b[...] = w2_ref[0, 0].astype(BF16)

        gather_wait(slot, valid_rows(i))

        @pl.when(i >= 2)
        def _():
            scatter_wait(slot, valid_rows(i - 2))

        x = xbuf[slot].astype(BF16)
        a = jnp.dot(x, w1b[...], preferred_element_type=F32)
        g = jnp.dot(x, w3b[...], preferred_element_type=F32)
        hid = (a * jax.nn.sigmoid(a) * g).astype(BF16)
        ybuf[slot] = jnp.dot(hid, w2b[...], preferred_element_type=F32)
        scatter_start(slot, valid_rows(i))

        @pl.when(i == n_used - 1)
        def _():
            @pl.when(i >= 1)
            def _():
                scatter_wait(1 - slot, valid_rows(i - 1))
            scatter_wait(slot, valid_rows(i))


def _moe(h2, eid, w1, w3, w2, layer):
    n = h2.shape[0]
    n_assign = n * MOE_TOP_K
    n_blocks = (n_assign + MOE_EXPERTS * (MOE_ROWS - 1) + MOE_ROWS - 1) // MOE_ROWS
    flat_e = eid.reshape(n_assign)
    order = jnp.argsort(flat_e, stable=True).astype(jnp.int32)
    experts = jnp.arange(MOE_EXPERTS, dtype=jnp.int32)
    counts = jnp.sum((flat_e[:, None] == experts[None, :]).astype(jnp.int32), axis=0)
    starts = jnp.cumsum(counts) - counts
    padded = (counts + MOE_ROWS - 1) // MOE_ROWS * MOE_ROWS
    p_ends = jnp.cumsum(padded)
    p_starts = p_ends - padded
    n_used_s = p_ends[-1] // MOE_ROWS
    n_used = n_used_s.astype(jnp.int32).reshape(1)
    blk = jnp.arange(n_blocks, dtype=jnp.int32)
    blk_expert = jnp.sum((p_ends[None, :] <= (blk * MOE_ROWS)[:, None]).astype(jnp.int32), axis=1)
    last_expert = jnp.max(jnp.where(counts > 0, experts, 0))
    block_expert = jnp.where(blk < n_used_s, jnp.minimum(blk_expert, MOE_EXPERTS - 1), last_expert)
    onehot = (block_expert[:, None] == experts[None, :]).astype(jnp.int32)
    pick = lambda table: jnp.sum(onehot * table[None, :], axis=1)
    local = jnp.arange(MOE_ROWS, dtype=jnp.int32)[None, :]
    rank = blk[:, None] * MOE_ROWS + local - pick(p_starts)[:, None]
    valid = (rank < pick(counts)[:, None]) & (blk < n_used_s)[:, None]
    pos = jnp.clip(pick(starts)[:, None] + rank, 0, n_assign - 1)
    assign = order[pos]
    tok = jnp.where(valid, assign // MOE_TOP_K, 0)
    dst = jnp.where(valid, (assign % MOE_TOP_K) * n + assign // MOE_TOP_K,
                    n_assign + (blk % 2)[:, None] * MOE_ROWS + local)
    n_valid = jnp.sum(valid.astype(jnp.int32), axis=1)
    rows_now = pl.BlockSpec((1, 1, MOE_ROWS), lambda i, *_: (i, 0, 0), memory_space=pltpu.SMEM)
    rows_next = pl.BlockSpec((1, 1, MOE_ROWS), lambda i, *_: (jnp.minimum(i + 1, n_blocks - 1), 0, 0),
                             memory_space=pltpu.SMEM)
    src3 = tok.reshape(n_blocks, 1, MOE_ROWS)
    return pl.pallas_call(
        _moe_kernel,
        grid_spec=pltpu.PrefetchScalarGridSpec(
            num_scalar_prefetch=3,
            grid=(n_blocks,),
            in_specs=[
                rows_now, rows_next, rows_now,
                pl.BlockSpec(memory_space=pl.ANY),
                pl.BlockSpec((1, 1, D_MODEL, MOE_HIDDEN), lambda i, be, *_: (layer, be[i], 0, 0)),
                pl.BlockSpec((1, 1, D_MODEL, MOE_HIDDEN), lambda i, be, *_: (layer, be[i], 0, 0)),
                pl.BlockSpec((1, 1, MOE_HIDDEN, D_MODEL), lambda i, be, *_: (layer, be[i], 0, 0)),
            ],
            out_specs=pl.BlockSpec(memory_space=pl.ANY),
            scratch_shapes=[pltpu.VMEM((2, MOE_ROWS, D_MODEL), F32), pltpu.VMEM((2, MOE_ROWS, D_MODEL), F32),
                            pltpu.VMEM((D_MODEL, MOE_HIDDEN), BF16), pltpu.VMEM((D_MODEL, MOE_HIDDEN), BF16),
                            pltpu.VMEM((MOE_HIDDEN, D_MODEL), BF16),
                            pltpu.SemaphoreType.DMA((2,)), pltpu.SemaphoreType.DMA((2,))],
        ),
        out_shape=jax.ShapeDtypeStruct((n_assign + 2 * MOE_ROWS, D_MODEL), F32),
        compiler_params=_params("arbitrary"),
        name="moe_experts",
    )(block_expert, n_used, n_valid, src3, src3, dst.reshape(n_blocks, 1, MOE_ROWS), h2, w1, w3, w2)


def _combine_kernel(x1_ref, y0_ref, y1_ref, gate_ref, mod_ref, g_ref, o_ref, *, final):
    f = gate_ref[:, 0:1] * y0_ref[...] + gate_ref[:, 1:2] * y1_ref[...]
    x2 = x1_ref[...] + mod_ref[0, 5:6, :] * f
    if final:
        x2 = _rms(x2) * g_ref[...]
    o_ref[...] = x2


def _combine(n_blocks, x1, y, gate, mod3, g_final, final):
    row = lambda w: pl.BlockSpec((RB, w), lambda i: (i, 0))
    n = n_blocks * RB
    return pl.pallas_call(
        functools.partial(_combine_kernel, final=final),
        grid=(n_blocks,),
        in_specs=[row(D_MODEL), row(D_MODEL), pl.BlockSpec((RB, D_MODEL), lambda i: (i + n_blocks, 0)), row(128),
                  pl.BlockSpec((1, 6, D_MODEL), lambda i: (_mod_row(i), 0, 0)),
                  pl.BlockSpec((1, D_MODEL), lambda i: (0, 0))],
        out_specs=row(D_MODEL),
        out_shape=jax.ShapeDtypeStruct((n, D_MODEL), F32),
        compiler_params=_params("arbitrary"),
        name="moe_combine_final" if final else "moe_combine",
    )(x1, y, y, gate, mod3, g_final)


def _rope_rot_cols(w):
    p = ROPE_PAIRS
    return jnp.concatenate([-w[..., p:2 * p], w[..., 0:p], -w[..., 3 * p:4 * p], w[..., 2 * p:3 * p]], axis=-1)


def _pack_w_in(w_in):
    cuts = np.cumsum([S5_WIDTH, MLA_Q_LORA, MLA_KV_LORA, MLA_ROPE, HG_HEADS * HG_K, HG_HEADS * HG_K,
                      HG_HEADS * HG_K, HG_HEADS * HG_V, HG_HEADS * HG_V, D_MODEL, D_MODEL])
    (w_s5, w_cq, w_ckv, w_kpe, w_hq, w_zf, w_zb, w_hi, w_hg, w_ga, w_gb, w_gc) = jnp.split(w_in, cuts, axis=-1)
    pad = jnp.zeros(w_in.shape[:-1] + (256 - MLA_KV_LORA - 2 * MLA_ROPE,), w_in.dtype)
    packed = jnp.concatenate([w_s5, w_cq, w_ckv, w_kpe, _rope_rot_cols(w_kpe), pad,
                              w_hq, w_zf, w_zb, w_hi, w_hg, w_ga, w_gb, w_gc], axis=-1)
    assert packed.shape[-1] == W_IN_COLS
    return packed.astype(BF16)


def _pack_mla(w_uq, w_uk, w_uv):
    hp, n, r = MLA_HEAD_PAD, MLA_NOPE, MLA_ROPE
    nl = w_uq.shape[0]
    wq = w_uq.reshape(nl, MLA_Q_LORA, MLA_HEADS, n + r)
    zq = jnp.zeros((nl, MLA_Q_LORA, MLA_HEADS, hp - n - r), F32)
    q1 = jnp.concatenate([wq, zq], axis=-1).reshape(nl, MLA_Q_LORA, MLA_HEADS * hp)
    q2 = jnp.concatenate([jnp.zeros((nl, MLA_Q_LORA, MLA_HEADS, n), F32), _rope_rot_cols(wq[..., n:]), zq],
                         axis=-1).reshape(nl, MLA_Q_LORA, MLA_HEADS * hp)
    wq_packed = jnp.concatenate([q1, q2], axis=-1).astype(BF16)

    wk = w_uk.reshape(nl, MLA_KV_LORA, MLA_HEADS, n)
    k_nope = jnp.concatenate([wk, jnp.zeros((nl, MLA_KV_LORA, MLA_HEADS, hp - n), F32)], axis=-1)
    pe_slot = np.concatenate([np.zeros((r, n), np.float32), np.eye(r, dtype=np.float32),
                              np.zeros((r, hp - n - r), np.float32)], axis=-1)
    pe_all = jnp.asarray(np.tile(pe_slot[None, :, None, :], (nl, 1, MLA_HEADS, 1)))
    zero_pe = jnp.zeros_like(pe_all)
    tail = jnp.zeros((nl, 256 - MLA_KV_LORA - 2 * r, MLA_HEADS, hp), F32)
    k1 = jnp.concatenate([k_nope, pe_all, zero_pe, tail], axis=1).reshape(nl, 256, MLA_HEADS * hp)
    k2 = jnp.concatenate([jnp.zeros_like(k_nope), zero_pe, pe_all, tail], axis=1).reshape(nl, 256, MLA_HEADS * hp)
    wk_packed = jnp.concatenate([k1, k2], axis=-1).astype(BF16)
    wv_packed = jnp.concatenate([w_uv, jnp.zeros((nl, 256 - MLA_KV_LORA, MLA_HEADS * MLA_V), F32)],
                                axis=1).astype(BF16)
    return wq_packed, wk_packed, wv_packed


def _rope_tables():
    rows = SEQ // GRID_W
    row = np.repeat(np.arange(rows, dtype=np.float32), GRID_W)
    col = np.tile(np.arange(GRID_W, dtype=np.float32), rows)
    inv = (np.float32(ROPE_BASE) ** (-np.arange(ROPE_PAIRS, dtype=np.float32) / np.float32(ROPE_PAIRS))).astype(np.float32)
    ar, ac = row[:, None] * inv, col[:, None] * inv
    cos = np.concatenate([np.cos(ar), np.cos(ar), np.cos(ac), np.cos(ac)], axis=1)
    sin = np.concatenate([np.sin(ar), np.sin(ar), np.sin(ac), np.sin(ac)], axis=1)
    tail = MLA_HEAD_PAD - MLA_NOPE - MLA_ROPE
    cos_t = np.concatenate([np.ones((SEQ, MLA_NOPE)), cos, np.ones((SEQ, tail))], axis=1)
    sin_t = np.concatenate([np.zeros((SEQ, MLA_NOPE)), sin, np.zeros((SEQ, tail))], axis=1)
    cos_t = np.concatenate([cos_t, np.ones((WB, MLA_HEAD_PAD))], axis=0)
    sin_t = np.concatenate([sin_t, np.zeros((WB, MLA_HEAD_PAD))], axis=0)
    return jnp.asarray(cos_t, F32), jnp.asarray(sin_t, F32)


def _pack_s5(lam_re, lam_im, b_re, b_im, c_re, c_im, log_step):
    eye = jnp.eye(S5_GROUPS, dtype=F32)
    dt = jnp.exp(log_step)[..., None]
    mag = jnp.exp(lam_re * dt)
    lb_re, lb_im = mag * jnp.cos(lam_im * dt), mag * jnp.sin(lam_im * dt)
    den = lam_re * lam_re + lam_im * lam_im
    fr = ((lb_re - 1) * lam_re + lb_im * lam_im) / den
    fi = (lb_im * lam_re - (lb_re - 1) * lam_im) / den
    bb_re = fr[..., None] * b_re - fi[..., None] * b_im
    bb_im = fr[..., None] * b_im + fi[..., None] * b_re
    lead = lam_re.shape[:2]

    def in_mat(bb):
        return jnp.einsum('ldgph,gk->ldghkp', bb, eye).reshape(lead + (S5_WIDTH, S5_LANES))

    def out_mat(cc):
        return jnp.einsum('ldghp,gk->ldgpkh', cc, eye).reshape(lead + (S5_LANES, S5_WIDTH))

    bmat = jnp.concatenate([in_mat(bb_re), in_mat(bb_im)], axis=-1).astype(BF16)
    hl, hw = S5_LANES // 2, S5_WIDTH // 2
    cre, cim = out_mat(c_re), -out_mat(c_im)
    cmat = jnp.stack([jnp.concatenate([cre[..., h * hl:(h + 1) * hl, h * hw:(h + 1) * hw],
                                       cim[..., h * hl:(h + 1) * hl, h * hw:(h + 1) * hw]], axis=-2)
                      for h in range(2)], axis=2).astype(BF16)
    a_tab = jnp.stack([jnp.broadcast_to(lb_re.reshape(lead + (1, S5_LANES)), lead + (BATCH, S5_LANES)),
                       jnp.broadcast_to(lb_im.reshape(lead + (1, S5_LANES)), lead + (BATCH, S5_LANES))], axis=2)
    return bmat, cmat, a_tab


def kernel(x, c, ctx, c_ctx, mod_w, mod_b, norm1_g, norm2_g, w_in, s5_lam_re, s5_lam_im, s5_b_re, s5_b_im, s5_c_re, s5_c_im, s5_log_step, s5_d, s5_glu_w, s5_glu_b, mla_qa_g, mla_kva_g, mla_w_uq, mla_w_uk, mla_w_uv, hg_lb_logits, hg_norm_g, w_pa, w_pb, w_pc, w_out, moe_w_group, moe_b_group, moe_w_expert, moe_b_expert, moe_w1, moe_w3, moe_w2, final_norm_g):
    x_lat, x_ctx, ctx_row0 = x.reshape(N_LAT, D_MODEL), ctx.reshape(N_CTX, D_MODEL), 0
    c_rows = jnp.concatenate([c, c_ctx[None, :], jnp.zeros((16 - BATCH - 1, D_MODEL), F32)], axis=0)
    mod = _modulation(c_rows, mod_w, mod_b).reshape(DEPTH, 16, 6, D_MODEL)

    lb_all = jnp.cumsum(jax.nn.softmax(hg_lb_logits.astype(F32)))
    lb_all = lb_all - lb_all[0]
    hg_scal = jnp.stack([jnp.log(lb_all), jnp.log1p(-lb_all), 1.0 - lb_all], axis=1).reshape(3 * DEPTH).astype(F32)
    cos_t, sin_t = _rope_tables()

    vec = lambda a: a[:, None, :]
    w_in_p = _pack_w_in(w_in)
    bmat, cmat, a_tab = _pack_s5(s5_lam_re, s5_lam_im, s5_b_re, s5_b_im, s5_c_re, s5_c_im, s5_log_step)
    wq, wk, wv = _pack_mla(mla_w_uq, mla_w_uk, mla_w_uv)
    w_route = jnp.concatenate([moe_w_group, jnp.zeros((DEPTH, D_MODEL, 32 - MOE_GROUPS), F32), moe_w_expert,
                               jnp.zeros((DEPTH, D_MODEL, 128 - 32 - MOE_EXPERTS), F32)], axis=-1)
    b_route = jnp.concatenate([moe_b_group, jnp.zeros((DEPTH, 32 - MOE_GROUPS), F32), moe_b_expert,
                               jnp.zeros((DEPTH, 128 - 32 - MOE_EXPERTS), F32)], axis=-1)
    w_route_hi = w_route.astype(BF16)
    w_route_lo = (w_route - w_route_hi.astype(F32)).astype(BF16)
    merge_consts = [vec(s5_d), s5_glu_w.astype(BF16), vec(s5_glu_b), vec(hg_norm_g),
                    w_pa.astype(BF16), w_pb.astype(BF16), w_pc.astype(BF16), w_out.astype(BF16), vec(norm2_g),
                    w_route_hi, w_route_lo, vec(b_route)]
    w_tab_np, mask_tab_np = _hgrn_tables()
    w_tab = jnp.asarray(np.concatenate([w_tab_np, w_tab_np], axis=-1), BF16)
    mask_tab = jnp.asarray(np.kron(np.eye(RB // HG_CHUNK, dtype=np.float32), mask_tab_np), F32)
    head_sum = jnp.asarray(np.kron(np.eye(HG_HEADS), np.ones((HG_V, HG_V))), F32)
    wc_np, maskc_np, vmask_np = _hgrn_fast_tables()
    wc_tab, maskc_tab, vmask_tab = jnp.asarray(wc_np, BF16), jnp.asarray(maskc_np, F32), jnp.asarray(vmask_np, BF16)

    out = None
    for layer in range(DEPTH):
        last = layer == DEPTH - 1
        mod3 = mod[layer]
        (s5u, cqn, kvn, hq, lff, lfb, kkf, kkb, hi, hgs, gates) = _input_projection(
            layer, x_lat, x_ctx, ctx_row0, mod3, hg_scal, vec(norm1_g), w_in_p, vec(mla_qa_g), vec(mla_kva_g))

        yf, yb = [_from_time_major(y) for y in _s5_states(layer, _to_time_major(s5u), bmat, cmat, a_tab)]

        q, k, v = _qkv(layer, cqn, kvn, cos_t, sin_t, wq, wk, wv)
        att_lat = _attention(q, k, v, latent=True)
        att_ctx = None if last else _attention(q, k, v, latent=False)

        of, ob = _hgrn(hq, kkf, kkb, lff, lfb, hi, w_tab, mask_tab, wc_tab, maskc_tab, vmask_tab)

        n_blocks = NB_LAT if last else NB_TOK
        x1, h2, eid, gate = _merge(layer, n_blocks, x_lat, x_ctx, ctx_row0, mod3, yf, yb, s5u, of, ob, hgs,
                                   att_lat, att_ctx, gates, head_sum, merge_consts)

        y_moe = _moe(h2, eid[:, :MOE_TOP_K], moe_w1, moe_w3, moe_w2, layer)
        res = _combine(n_blocks, x1, y_moe, gate, mod3, final_norm_g[None, :], final=last)
        if last:
            out = res
        else:
            x_lat, x_ctx, ctx_row0 = res, res, N_LAT
    return out.reshape(BATCH, SEQ, D_MODEL)
```

```python
import functools
import math

import jax
import jax.numpy as jnp
from jax import lax
import numpy as np
from jax.experimental import pallas as pl
from jax.experimental.pallas import tpu as pltpu

F32 = jnp.float32
BF16 = jnp.bfloat16
HIGHEST = lax.Precision.HIGHEST

D_MODEL = 1024
BATCH = 8
SEQ = 2048
DEPTH = 2
GRID_W = 64
CTX_LEN = 256
NORM_EPS = 1e-6

S5_WIDTH = D_MODEL // 4
S5_GROUP_CH = 16
S5_GROUPS = S5_WIDTH // S5_GROUP_CH
S5_STATE = 64
S5_LANES = S5_GROUPS * S5_STATE

MLA_HEADS = D_MODEL // 128
MLA_NOPE = 64
MLA_ROPE = 32
MLA_V = 64
MLA_Q_LORA = D_MODEL // 4
MLA_KV_LORA = D_MODEL // 8
MLA_SCALE = 1.0 / math.sqrt(MLA_NOPE + MLA_ROPE)
MLA_HEAD_PAD = 128
ROPE_PAIRS = MLA_ROPE // 4
ROPE_BASE = 10000.0

HG_HEADS = D_MODEL // 256
HG_K = 128
HG_V = 64
HG_CHUNK = 64
HG_LEVELS = 6
HG_SETS = 2 + HG_LEVELS
HG_FAST_CHUNK = 32
HG_FAST_MIN = -60.0

MOE_GROUPS = 4
MOE_EXPERTS_PER_GROUP = 8
MOE_EXPERTS = MOE_GROUPS * MOE_EXPERTS_PER_GROUP
MOE_TOP_K = 2
MOE_HIDDEN = D_MODEL // 2
MOE_ROWS = 256
MOE_DMA_CHUNK = 16

N_LAT = BATCH * SEQ
N_CTX = BATCH * CTX_LEN
N_TOK = N_LAT + N_CTX
RB = CTX_LEN
WB = 2 * RB
NB_LAT = N_LAT // RB
NB_CTX = N_CTX // RB
NB_TOK = N_TOK // RB
LAT_BLOCKS = SEQ // RB
SEQ_BLOCKS = LAT_BLOCKS + 1

C_S5 = 0
C_CQ = 256
C_KV = 512
C_HQ = 768
C_ZF = 1280
C_ZB = 1792
C_HI = 2304
C_HG = 2560
C_GATE = 2816
W_IN_COLS = C_GATE + 3 * D_MODEL

VMEM_LIMIT = 56 * 1024 * 1024


def _params(*sem):
    return pltpu.CompilerParams(dimension_semantics=sem, vmem_limit_bytes=VMEM_LIMIT)


def _rms(v):
    return v * lax.rsqrt(jnp.mean(v * v, axis=-1, keepdims=True) + NORM_EPS)


def _mod_row(i, rb=RB):
    return jnp.where(i < N_LAT // rb, i // (SEQ // rb), BATCH)


def _resident(a):
    return pl.BlockSpec(a.shape, lambda *_: (0,) * a.ndim, pipeline_mode=pl.Buffered(1))


def _layer_spec(a, layer):
    return pl.BlockSpec((1,) + a.shape[1:], lambda *_: (layer,) + (0,) * (a.ndim - 1),
                        pipeline_mode=pl.Buffered(1))


def _split_row_specs(rb, w, ctx_row0):
    n_lat = N_LAT // rb
    lat = pl.BlockSpec((rb, w), lambda i, *_: (jnp.minimum(i, n_lat - 1), 0))
    ctx = pl.BlockSpec((rb, w), lambda i, *_: (ctx_row0 // rb + jnp.maximum(i - n_lat, 0), 0))
    return lat, ctx


def _pick_rows(lat_ref, ctx_ref, rb):
    return jnp.where(pl.program_id(0) < N_LAT // rb, lat_ref[...], ctx_ref[...])


def _mod_kernel(c_ref, w_ref, b_ref, o_ref):
    c = c_ref[...]
    s = c * jax.nn.sigmoid(c)
    o_ref[0] = jnp.dot(s, w_ref[0], precision=HIGHEST, preferred_element_type=F32) + b_ref[0]


def _modulation(c_rows, mod_w, mod_b):
    bn = 1536
    return pl.pallas_call(
        _mod_kernel,
        grid=(DEPTH, 6 * D_MODEL // bn),
        in_specs=[
            pl.BlockSpec((16, D_MODEL), lambda l, j: (0, 0)),
            pl.BlockSpec((1, D_MODEL, bn), lambda l, j: (l, 0, j)),
            pl.BlockSpec((1, 1, bn), lambda l, j: (l, 0, j)),
        ],
        out_specs=pl.BlockSpec((1, 16, bn), lambda l, j: (l, 0, j)),
        out_shape=jax.ShapeDtypeStruct((DEPTH, 16, 6 * D_MODEL), F32),
        compiler_params=_params("arbitrary", "arbitrary"),
        name="adaln_mod",
    )(c_rows, mod_w, mod_b.reshape(DEPTH, 1, 6 * D_MODEL))


def _log_sigmoid(z):
    return jnp.minimum(z, 0.0) - jnp.log1p(jnp.exp(-jnp.abs(z)))


def _in_kernel(hs_ref, xl_ref, xc_ref, mod_ref, g1_ref, w_ref, qag_ref, kvg_ref,
               s5u_ref, cqn_ref, kvn_ref, hq_ref, lff_ref, lfb_ref, kkf_ref, kkb_ref,
               hi_ref, hgs_ref, gates_ref, h_ref, *, layer):
    x = _pick_rows(xl_ref, xc_ref, WB)
    xn = _rms(x) * g1_ref[0]
    h_ref[...] = (xn * (1.0 + mod_ref[0, 1:2, :]) + mod_ref[0, 0:1, :]).astype(BF16)

    def mm(a, b):
        return jnp.dot(h_ref[...], w_ref[0, :, a:b], preferred_element_type=F32)

    s5u_ref[...] = mm(C_S5, C_CQ).astype(BF16)
    cqn_ref[...] = (_rms(mm(C_CQ, C_KV)) * qag_ref[0]).astype(BF16)
    kv = mm(C_KV, C_HQ)
    ckvn = _rms(kv[:, :MLA_KV_LORA]) * kvg_ref[0]
    kvn_ref[...] = jnp.concatenate([ckvn, kv[:, MLA_KV_LORA:]], axis=1).astype(BF16)
    hq_ref[...] = mm(C_HQ, C_ZF).astype(BF16)

    log_lb = hs_ref[3 * layer]
    log_1m_lb = hs_ref[3 * layer + 1]
    one_m_lb = hs_ref[3 * layer + 2]
    for c0, lf_ref, kk_ref in ((C_ZF, lff_ref, kkf_ref), (C_ZB, lfb_ref, kkb_ref)):
        z = mm(c0, c0 + HG_HEADS * HG_K)
        b = log_1m_lb + _log_sigmoid(z)
        m = jnp.maximum(b, log_lb)
        lf_ref[...] = m + jnp.log1p(jnp.exp(-jnp.abs(b - log_lb)))
        kk_ref[...] = (one_m_lb * jax.nn.sigmoid(-z)).astype(BF16)

    hi_ref[...] = mm(C_HI, C_HG).astype(BF16)
    g = mm(C_HG, C_GATE)
    hgs_ref[...] = (g * jax.nn.sigmoid(g)).astype(BF16)
    gw = 512
    for j in range(3 * D_MODEL // gw):
        c0 = C_GATE + j * gw
        gates_ref[:, j * gw:(j + 1) * gw] = jax.nn.sigmoid(mm(c0, c0 + gw)).astype(BF16)


def _input_projection(layer, x_lat, x_ctx, ctx_row0, mod3, hg_scal, g1, w_in, qa_g, kva_g):
    row = lambda w: pl.BlockSpec((WB, w), lambda i: (i, 0))
    x_specs = _split_row_specs(WB, D_MODEL, ctx_row0)
    shapes = [
        (S5_WIDTH, BF16), (MLA_Q_LORA, BF16), (256, BF16), (HG_HEADS * HG_K, BF16),
        (HG_HEADS * HG_K, F32), (HG_HEADS * HG_K, F32), (HG_HEADS * HG_K, BF16), (HG_HEADS * HG_K, BF16),
        (HG_HEADS * HG_V, BF16), (HG_HEADS * HG_V, BF16), (3 * D_MODEL, BF16),
    ]
    return pl.pallas_call(
        functools.partial(_in_kernel, layer=layer),
        grid=(N_TOK // WB,),
        in_specs=[
            pl.BlockSpec(memory_space=pltpu.SMEM),
            *x_specs,
            pl.BlockSpec((1, 6, D_MODEL), lambda i: (_mod_row(i, WB), 0, 0)),
            _layer_spec(g1, layer), _layer_spec(w_in, layer), _layer_spec(qa_g, layer), _layer_spec(kva_g, layer),
        ],
        out_specs=[row(w) for w, _ in shapes],
        out_shape=[jax.ShapeDtypeStruct((N_TOK, w), dt) for w, dt in shapes],
        scratch_shapes=[pltpu.VMEM((WB, D_MODEL), BF16)],
        compiler_params=_params("arbitrary"),
        name="input_projection",
    )(hg_scal, x_lat, x_ctx, mod3, g1, w_in, qa_g, kva_g)


def _seq_block(b, j):
    return jnp.where(j == 0, NB_LAT + b, b * LAT_BLOCKS + j - 1)


def _seq_block_rev(b, j):
    return jnp.where(j == 0, NB_LAT + b, b * LAT_BLOCKS + LAT_BLOCKS - j)


S5_TB = 128
S5_ROWS = S5_TB * BATCH
S5_STEPS = (CTX_LEN + SEQ) // S5_TB
S5_CTX_STEPS = CTX_LEN // S5_TB


def _s5_kernel(uf_ref, ub_ref, bmat_ref, cmat_ref, a_ref, yf_ref, yb_ref, bu_ref, carry_ref):
    j = pl.program_id(0)

    @pl.when(j == 0)
    def _():
        carry_ref[...] = jnp.zeros_like(carry_ref)

    for d, (u_ref, y_ref) in enumerate(((uf_ref, yf_ref), (ub_ref, yb_ref))):
        bu_ref[...] = jnp.dot(u_ref[...], bmat_ref[0, d], preferred_element_type=F32)
        are = a_ref[0, d, 0]
        aim = a_ref[0, d, 1]

        def body(t, carry, d=d, are=are, aim=aim):
            hre, him = carry
            tt = (S5_TB - 1 - t) if d == 1 else t
            r0 = pl.multiple_of(tt * BATCH, BATCH)
            nre = are * hre - aim * him + bu_ref[pl.ds(r0, BATCH), 0:S5_LANES]
            nim = are * him + aim * hre + bu_ref[pl.ds(r0, BATCH), S5_LANES:2 * S5_LANES]
            bu_ref[pl.ds(r0, BATCH), 0:S5_LANES] = nre
            bu_ref[pl.ds(r0, BATCH), S5_LANES:2 * S5_LANES] = nim
            return nre, nim

        hre, him = lax.fori_loop(0, S5_TB, body, (carry_ref[d, 0], carry_ref[d, 1]), unroll=4)
        carry_ref[d, 0] = hre
        carry_ref[d, 1] = him
        hl = S5_LANES // 2
        hw = S5_WIDTH // 2
        for half in range(2):
            states = jnp.concatenate([bu_ref[:, half * hl:(half + 1) * hl],
                                      bu_ref[:, S5_LANES + half * hl:S5_LANES + (half + 1) * hl]], axis=1)
            y_ref[:, half * hw:(half + 1) * hw] = jnp.dot(states.astype(BF16), cmat_ref[0, d, half],
                                                          preferred_element_type=F32)


def _s5_block_rev(j):
    return jnp.where(j < S5_CTX_STEPS, S5_CTX_STEPS - 1 - j, S5_STEPS - 1 + S5_CTX_STEPS - j)


def _s5_states(layer, u_tm, bmat, cmat, a_tab):
    full = lambda a: _layer_spec(a, layer)
    fwd = pl.BlockSpec((S5_ROWS, S5_WIDTH), lambda j: (j, 0))
    bwd = pl.BlockSpec((S5_ROWS, S5_WIDTH), lambda j: (_s5_block_rev(j), 0))
    return pl.pallas_call(
        _s5_kernel,
        grid=(S5_STEPS,),
        in_specs=[fwd, bwd, full(bmat), full(cmat), full(a_tab)],
        out_specs=[fwd, bwd],
        out_shape=[jax.ShapeDtypeStruct((S5_STEPS * S5_ROWS, S5_WIDTH), F32)] * 2,
        scratch_shapes=[pltpu.VMEM((S5_ROWS, 2 * S5_LANES), F32), pltpu.VMEM((2, 2, BATCH, S5_LANES), F32)],
        compiler_params=_params("arbitrary"),
        name="s5_scan",
    )(u_tm, u_tm, bmat, cmat, a_tab)


def _to_time_major(a):
    w = a.shape[1]
    seq = jnp.concatenate([a[N_LAT:].reshape(BATCH, CTX_LEN, w), a[:N_LAT].reshape(BATCH, SEQ, w)], axis=1)
    return seq.transpose(1, 0, 2).reshape((CTX_LEN + SEQ) * BATCH, w)


def _from_time_major(a):
    w = a.shape[1]
    seq = a.reshape(CTX_LEN + SEQ, BATCH, w).transpose(1, 0, 2)
    return jnp.concatenate([seq[:, CTX_LEN:].reshape(N_LAT, w), seq[:, :CTX_LEN].reshape(N_CTX, w)], axis=0)


def _qkv_kernel(cqn_ref, kvn_ref, cos_ref, sin_ref, wq_ref, wk_ref, wv_ref, q_ref, k_ref, v_ref):
    cos = jnp.concatenate([cos_ref[...]] * MLA_HEADS, axis=1)
    sin = jnp.concatenate([sin_ref[...]] * MLA_HEADS, axis=1)
    w = MLA_HEADS * MLA_HEAD_PAD
    q2 = jnp.dot(cqn_ref[...], wq_ref[0], preferred_element_type=F32)
    q_ref[...] = ((q2[:, :w] * cos + q2[:, w:] * sin) * MLA_SCALE).astype(BF16)
    kv = kvn_ref[...]
    k2 = jnp.dot(kv, wk_ref[0], preferred_element_type=F32)
    k_ref[...] = (k2[:, :w] * cos + k2[:, w:] * sin).astype(BF16)
    v_ref[...] = jnp.dot(kv, wv_ref[0], preferred_element_type=F32).astype(BF16)


def _qkv(layer, cqn, kvn, cos_t, sin_t, wq, wk, wv):
    row = lambda w: pl.BlockSpec((WB, w), lambda i: (i, 0))
    full = lambda a: _layer_spec(a, layer)
    pos = pl.BlockSpec((WB, MLA_HEAD_PAD),
                       lambda i: (jnp.where(i < N_LAT // WB, i % (SEQ // WB), SEQ // WB), 0))
    w = MLA_HEADS * MLA_HEAD_PAD
    return pl.pallas_call(
        _qkv_kernel,
        grid=(N_TOK // WB,),
        in_specs=[row(MLA_Q_LORA), row(256), pos, pos, full(wq), full(wk), full(wv)],
        out_specs=[row(w), row(w), row(MLA_HEADS * MLA_V)],
        out_shape=[jax.ShapeDtypeStruct((N_TOK, w), BF16), jax.ShapeDtypeStruct((N_TOK, w), BF16),
                   jax.ShapeDtypeStruct((N_TOK, MLA_HEADS * MLA_V), BF16)],
        compiler_params=_params("arbitrary"),
        name="mla_qkv",
    )(cqn, kvn, cos_t, sin_t, wq, wk, wv)


def _attn_kernel(*refs, n_kv):
    q_ref = refs[0]
    k_refs = refs[1:1 + n_kv]
    v_refs = refs[1 + n_kv:1 + 2 * n_kv]
    o_ref = refs[1 + 2 * n_kv]
    nt = (((1,), (1,)), ((), ()))
    for h in range(MLA_HEADS):
        q = q_ref[:, h * MLA_HEAD_PAD:(h + 1) * MLA_HEAD_PAD]
        s = [lax.dot_general(q, k_ref[:, h * MLA_HEAD_PAD:(h + 1) * MLA_HEAD_PAD], nt,
                             preferred_element_type=F32) for k_ref in k_refs]
        m = functools.reduce(jnp.maximum, [jnp.max(si, axis=-1, keepdims=True) for si in s])
        p = [jnp.exp(si - m) for si in s]
        l = functools.reduce(jnp.add, [jnp.sum(pi, axis=-1, keepdims=True) for pi in p])
        o = functools.reduce(jnp.add, [
            jnp.dot(pi.astype(BF16), v_ref[:, h * MLA_V:(h + 1) * MLA_V], preferred_element_type=F32)
            for pi, v_ref in zip(p, v_refs)])
        o_ref[:, h * MLA_V:(h + 1) * MLA_V] = (o / l).astype(BF16)


def _attention(q, k, v, latent):
    w = MLA_HEADS * MLA_HEAD_PAD
    wv = MLA_HEADS * MLA_V
    if latent:
        tq = WB
        nq = SEQ // tq
        q_spec = pl.BlockSpec((tq, w), lambda b, j: (b * nq + j, 0))
        k_specs = [pl.BlockSpec((SEQ, w), lambda b, j: (b, 0)),
                   pl.BlockSpec((CTX_LEN, w), lambda b, j: (NB_LAT + b, 0))]
        v_specs = [pl.BlockSpec((SEQ, wv), lambda b, j: (b, 0)),
                   pl.BlockSpec((CTX_LEN, wv), lambda b, j: (NB_LAT + b, 0))]
        o_spec = pl.BlockSpec((tq, wv), lambda b, j: (b * nq + j, 0))
        n_out = N_LAT
    else:
        tq = CTX_LEN
        nq = 1
        q_spec = pl.BlockSpec((tq, w), lambda b, j: (NB_LAT + b, 0))
        k_specs = [pl.BlockSpec((CTX_LEN, w), lambda b, j: (NB_LAT + b, 0))]
        v_specs = [pl.BlockSpec((CTX_LEN, wv), lambda b, j: (NB_LAT + b, 0))]
        o_spec = pl.BlockSpec((tq, wv), lambda b, j: (b, 0))
        n_out = N_CTX
    n_kv = len(k_specs)
    return pl.pallas_call(
        functools.partial(_attn_kernel, n_kv=n_kv),
        grid=(BATCH, nq),
        in_specs=[q_spec] + k_specs + v_specs,
        out_specs=o_spec,
        out_shape=jax.ShapeDtypeStruct((n_out, wv), BF16),
        compiler_params=_params("arbitrary", "arbitrary"),
        name="mla_attention_lat" if latent else "mla_attention_ctx",
    )(q, *([k] * n_kv), *([v] * n_kv))


def _hgrn_tables():
    c = HG_CHUNK
    w = np.zeros((2, HG_SETS * c, c), np.float32)
    mask = np.zeros((2, HG_LEVELS, c, c), np.float32)
    idx = np.arange(c)
    for r in range(c):
        w[0, r, :r + 1] = 1
        w[1, r, r:] = 1
        w[0, c + r, r + 1:] = 1
        w[1, c + r, :r] = 1
    for l in range(HG_LEVELS):
        m = c >> (l + 1)
        for r in range(c):
            base = (r // (2 * m)) * 2 * m
            mid = base + m
            later = r >= mid
            row = (2 + l) * c + r
            if later:
                w[0, row, mid:r + 1] = 1
                w[1, row, mid:r] = 1
            else:
                w[0, row, r + 1:mid] = 1
                w[1, row, r:mid] = 1
        same = (idx[:, None] // (2 * m)) == (idx[None, :] // (2 * m))
        q_later = (idx[:, None] % (2 * m)) >= m
        k_later = (idx[None, :] % (2 * m)) >= m
        mask[0, l] = same & q_later & ~k_later
        mask[1, l] = same & ~q_later & k_later
    return w, mask


def _hgrn_dir(d, q_ref, k_ref, lf_ref, v_ref, w_ref, mask_ref, st_ref, o_ref):
    c = HG_CHUNK
    n_chunks = RB // c
    nt = (((1,), (1,)), ((), ()))
    tn = (((0,), (0,)), ((), ()))
    last = c - 1 if d == 0 else 0
    wsel = w_ref[d]
    e = []
    for ci in range(n_chunks):
        g = lf_ref[ci * c:(ci + 1) * c, :]
        g_hi = g.astype(BF16)
        g_lo = (g - g_hi.astype(F32)).astype(BF16)
        expo = jnp.dot(wsel, jnp.concatenate([g_hi, g_lo], axis=0), preferred_element_type=F32)
        e.append(jnp.exp(expo))

    def rows(s):
        return jnp.concatenate([e[ci][s * c:(s + 1) * c] for ci in range(n_chunks)], axis=0)

    q = q_ref[...].astype(F32)
    k = k_ref[...].astype(F32)
    q_state = (q * rows(0)).astype(BF16)
    k_state = (k * rows(1)).astype(BF16)
    q_lvl, k_lvl = [], []
    for l in range(HG_LEVELS):
        el = rows(2 + l)
        q_lvl.append((q * el).astype(BF16))
        k_lvl.append((k * el).astype(BF16))
    qk = q * k
    order = range(n_chunks) if d == 0 else range(n_chunks - 1, -1, -1)
    for h in range(HG_HEADS):
        ks = slice(h * HG_K, (h + 1) * HG_K)
        vs = slice(h * HG_V, (h + 1) * HG_V)
        v = v_ref[:, vs]
        scores = jnp.zeros((RB, RB), F32)
        for l in range(HG_LEVELS):
            p = lax.dot_general(q_lvl[l][:, ks], k_lvl[l][:, ks], nt, preferred_element_type=F32)
            scores = scores + p * mask_ref[d, l]
        o = jnp.dot(scores.astype(BF16), v, preferred_element_type=F32)
        o = o + jnp.sum(qk[:, ks], axis=-1, keepdims=True) * v.astype(F32)
        st = st_ref[d, h]
        o_state = [None] * n_chunks
        for ci in order:
            r = slice(ci * c, (ci + 1) * c)
            o_state[ci] = lax.dot_general(q_state[r, ks], st.astype(BF16), nt, preferred_element_type=F32)
            inc = lax.dot_general(v[r], k_state[r, ks], tn, preferred_element_type=F32)
            st = st * e[ci][last:last + 1, ks] + inc
        st_ref[d, h] = st
        o_ref[:, vs] = o + jnp.concatenate(o_state, axis=0)


def _hgrn_fast_tables():
    c = HG_FAST_CHUNK
    idx = np.arange(RB)
    same = (idx[:, None] // c) == (idx[None, :] // c)
    le = idx[None, :] <= idx[:, None]
    ge = idx[None, :] >= idx[:, None]
    wc = np.stack([same & le, same & ge]).astype(np.float32)
    col_chunk = np.arange((RB // c) * HG_V) // HG_V
    vmask = ((idx[:, None] // c) == col_chunk[None, :]).astype(np.float32)
    return np.concatenate([wc, wc], axis=-1), wc, vmask


def _hgrn_dir_fast(d, q_ref, k_ref, cum, tot, v_ref, maskc_ref, vmask_ref, st_ref, o_ref):
    c = HG_FAST_CHUNK
    n_chunks = RB // c
    nt = (((1,), (1,)), ((), ()))
    tn = (((0,), (0,)), ((), ()))
    q = q_ref[...].astype(F32)
    k = k_ref[...].astype(F32)
    qa = (q * jnp.exp(cum)).astype(BF16)
    kb = (k * jnp.exp(-cum)).astype(BF16)
    k_state = (k * jnp.exp(tot - cum)).astype(BF16)
    decay = jnp.exp(tot)
    keep = maskc_ref[d] > 0.5
    vmask = vmask_ref[...]
    order = range(n_chunks) if d == 0 else range(n_chunks - 1, -1, -1)
    for h in range(HG_HEADS):
        ks = slice(h * HG_K, (h + 1) * HG_K)
        vs = slice(h * HG_V, (h + 1) * HG_V)
        v = v_ref[:, vs]
        p = lax.dot_general(qa[:, ks], kb[:, ks], nt, preferred_element_type=F32)
        o = jnp.dot(jnp.where(keep, p, 0.0).astype(BF16), v, preferred_element_type=F32)
        v_by_chunk = jnp.concatenate([v] * n_chunks, axis=1) * vmask
        inc = lax.dot_general(v_by_chunk, k_state[:, ks], tn, preferred_element_type=F32)
        st = st_ref[d, h]
        entering = [None] * n_chunks
        for ci in order:
            entering[ci] = st
            st = st * decay[ci * c:ci * c + 1, ks] + inc[ci * HG_V:(ci + 1) * HG_V]
        st_ref[d, h] = st
        s_all = jnp.concatenate(entering, axis=0).astype(BF16)
        o_all = lax.dot_general(qa[:, ks], s_all, nt, preferred_element_type=F32) * vmask.astype(F32)
        o_fold = functools.reduce(jnp.add, [o_all[:, j * 128:(j + 1) * 128] for j in range(n_chunks * HG_V // 128)])
        o_state = (o_fold + pltpu.roll(o_fold, HG_V, 1))[:, :HG_V]
        o_ref[:, vs] = o + o_state


def _hgrn_kernel(qf_ref, kf_ref, lff_ref, vf_ref, qb_ref, kb_ref, lfb_ref, vb_ref, w_ref, mask_ref,
                 wc_ref, maskc_ref, vmask_ref, of_ref, ob_ref, st_ref):
    j = pl.program_id(1)

    @pl.when(j == 0)
    def _():
        st_ref[...] = jnp.zeros_like(st_ref)

    c = HG_FAST_CHUNK
    dirs = ((0, qf_ref, kf_ref, lff_ref, vf_ref, of_ref), (1, qb_ref, kb_ref, lfb_ref, vb_ref, ob_ref))
    for d, q_ref, k_ref, lf_ref, v_ref, o_ref in dirs:
        g_hi, g_lo = _split_bf16(lf_ref[...])
        cum = jnp.dot(wc_ref[d], jnp.concatenate([g_hi, g_lo], axis=0), preferred_element_type=F32)
        edge = c - 1 if d == 0 else 0
        totals = [cum[ci * c + edge:ci * c + edge + 1] for ci in range(RB // c)]
        tot = jnp.concatenate([jnp.broadcast_to(t, (c, t.shape[1])) for t in totals], axis=0)
        safe = jnp.min(jnp.concatenate(totals, axis=0)) >= HG_FAST_MIN

        @pl.when(safe)
        def _(d=d, q_ref=q_ref, k_ref=k_ref, v_ref=v_ref, o_ref=o_ref, cum=cum, tot=tot):
            _hgrn_dir_fast(d, q_ref, k_ref, cum, tot, v_ref, maskc_ref, vmask_ref, st_ref, o_ref)

        @pl.when(jnp.logical_not(safe))
        def _(d=d, q_ref=q_ref, k_ref=k_ref, lf_ref=lf_ref, v_ref=v_ref, o_ref=o_ref):
            _hgrn_dir(d, q_ref, k_ref, lf_ref, v_ref, w_ref, mask_ref, st_ref, o_ref)


def _hgrn(hq, kkf, kkb, lff, lfb, hi, w_tab, mask_tab, wc_tab, maskc_tab, vmask_tab):
    full = _resident
    fwd = lambda w: pl.BlockSpec((RB, w), lambda b, j: (_seq_block(b, j), 0))
    bwd = lambda w: pl.BlockSpec((RB, w), lambda b, j: (_seq_block_rev(b, j), 0))
    wk = HG_HEADS * HG_K
    wv = HG_HEADS * HG_V
    return pl.pallas_call(
        _hgrn_kernel,
        grid=(BATCH, SEQ_BLOCKS),
        in_specs=[fwd(wk), fwd(wk), fwd(wk), fwd(wv), bwd(wk), bwd(wk), bwd(wk), bwd(wv),
                  full(w_tab), full(mask_tab), full(wc_tab), full(maskc_tab), full(vmask_tab)],
        out_specs=[fwd(wv), bwd(wv)],
        out_shape=[jax.ShapeDtypeStruct((N_TOK, wv), F32)] * 2,
        scratch_shapes=[pltpu.VMEM((2, HG_HEADS, HG_V, HG_K), F32)],
        compiler_params=_params("arbitrary", "arbitrary"),
        name="hgrn2_scan",
    )(hq, kkf, lff, hi, hq, kkb, lfb, hi, w_tab, mask_tab, wc_tab, maskc_tab, vmask_tab)


def _gelu_tanh(x):
    return 0.5 * x * (1.0 + jnp.tanh(math.sqrt(2.0 / math.pi) * (x + 0.044715 * (x * x * x))))


def _split_bf16(a):
    hi = a.astype(BF16)
    return hi, (a - hi.astype(F32)).astype(BF16)


def _merge_kernel(xl_ref, xc_ref, mod_ref, yf_ref, yb_ref, u_ref, of_ref, ob_ref, hgs_ref, attl_ref, attc_ref,
                  gates_ref, s5d_ref, gluw_ref, glub_ref, hgg_ref, hsum_ref, wpa_ref, wpb_ref, wpc_ref, wout_ref,
                  g2_ref, wrh_ref, wrl_ref, br_ref,
                  x1_ref, h2_ref, eid_ref, gate_ref):
    y = yf_ref[...] + yb_ref[...] + s5d_ref[0] * u_ref[...].astype(F32)
    y = _gelu_tanh(y)
    y = y * jax.nn.sigmoid(jnp.dot(y.astype(BF16), gluw_ref[0], preferred_element_type=F32) + glub_ref[0])
    o = of_ref[...] + ob_ref[...]
    sq_hi, sq_lo = _split_bf16(o * o)
    ms = (jnp.dot(sq_hi, hsum_ref[...], preferred_element_type=F32)
          + jnp.dot(sq_lo, hsum_ref[...], preferred_element_type=F32)) * (1.0 / HG_V)
    o = o * lax.rsqrt(ms + NORM_EPS) * hgg_ref[0] * hgs_ref[...].astype(F32)

    d = D_MODEL
    att = _pick_rows(attl_ref, attc_ref, WB)
    merged = (gates_ref[:, 0:d].astype(F32) * jnp.dot(y.astype(BF16), wpa_ref[0], preferred_element_type=F32)
              + gates_ref[:, d:2 * d].astype(F32) * jnp.dot(att, wpb_ref[0], preferred_element_type=F32)
              + gates_ref[:, 2 * d:3 * d].astype(F32) * jnp.dot(o.astype(BF16), wpc_ref[0], preferred_element_type=F32))
    y_out = jnp.dot(merged.astype(BF16), wout_ref[0], preferred_element_type=F32)
    x1 = _pick_rows(xl_ref, xc_ref, WB) + mod_ref[0, 2:3, :] * y_out
    x1_ref[...] = x1
    h2 = _rms(x1) * g2_ref[0] * (1.0 + mod_ref[0, 4:5, :]) + mod_ref[0, 3:4, :]
    h2_ref[...] = h2

    h_hi, h_lo = _split_bf16(h2)
    logits = (jnp.dot(h_hi, wrh_ref[0], preferred_element_type=F32)
              + jnp.dot(h_lo, wrh_ref[0], preferred_element_type=F32)
              + jnp.dot(h_hi, wrl_ref[0], preferred_element_type=F32)) + br_ref[0]
    lane = lax.broadcasted_iota(jnp.int32, logits.shape, 1).astype(F32)
    neg = -jnp.inf
    glog = jnp.where(lane < MOE_GROUPS, logits, neg)
    gmax = jnp.max(glog, axis=-1, keepdims=True)
    gidx = jnp.min(jnp.where(glog == gmax, lane, 1e9), axis=-1, keepdims=True)
    g_w = 1.0 / jnp.sum(jnp.exp(glog - gmax), axis=-1, keepdims=True)
    e_lo = 32.0 + gidx * MOE_EXPERTS_PER_GROUP
    elog = jnp.where((lane >= e_lo) & (lane < e_lo + MOE_EXPERTS_PER_GROUP), logits, neg)
    v1 = jnp.max(elog, axis=-1, keepdims=True)
    i1 = jnp.min(jnp.where(elog == v1, lane, 1e9), axis=-1, keepdims=True)
    elog2 = jnp.where(lane == i1, neg, elog)
    v2 = jnp.max(elog2, axis=-1, keepdims=True)
    i2 = jnp.min(jnp.where(elog2 == v2, lane, 1e9), axis=-1, keepdims=True)
    e2 = jnp.exp(v2 - v1)
    gate1 = g_w / (1.0 + e2)
    gate2 = g_w * e2 / (1.0 + e2)
    eid_ref[...] = jnp.where(lane == 0.0, i1 - 32.0, jnp.where(lane == 1.0, i2 - 32.0, 0.0)).astype(jnp.int32)
    gate_ref[...] = jnp.where(lane == 0.0, gate1, jnp.where(lane == 1.0, gate2, 0.0))


def _merge(layer, n_blocks, x_lat, x_ctx, ctx_row0, mod3, yf, yb, s5u, of, ob, hgs, att_lat, att_ctx, gates,
           head_sum, consts):
    row = lambda w: pl.BlockSpec((WB, w), lambda i: (i, 0))
    n = n_blocks * RB
    wv = MLA_HEADS * MLA_V
    if att_ctx is None:
        att_ctx = att_lat
    const_specs = [_layer_spec(a, layer) for a in consts]
    const_specs.insert(4, _resident(head_sum))
    return pl.pallas_call(
        _merge_kernel,
        grid=(n // WB,),
        in_specs=[*_split_row_specs(WB, D_MODEL, ctx_row0),
                  pl.BlockSpec((1, 6, D_MODEL), lambda i: (_mod_row(i, WB), 0, 0)),
                  row(S5_WIDTH), row(S5_WIDTH), row(S5_WIDTH), row(HG_HEADS * HG_V), row(HG_HEADS * HG_V),
                  row(HG_HEADS * HG_V), *_split_row_specs(WB, wv, 0), row(3 * D_MODEL)] + const_specs,
        out_specs=[row(D_MODEL), row(D_MODEL), row(128), row(128)],
        out_shape=[jax.ShapeDtypeStruct((n, D_MODEL), F32), jax.ShapeDtypeStruct((n, D_MODEL), F32),
                   jax.ShapeDtypeStruct((n, 128), jnp.int32), jax.ShapeDtypeStruct((n, 128), F32)],
        compiler_params=_params("arbitrary"),
        name="merge_router",
    )(x_lat, x_ctx, mod3, yf, yb, s5u, of, ob, hgs, att_lat, att_ctx, gates,
      *consts[:4], head_sum, *consts[4:])


def _moe_kernel(bexp_ref, nblk_ref, nv_ref, src_ref, srcn_ref, dst_ref, h_hbm, w1_ref, w3_ref, w2_ref, y_hbm,
                xbuf, ybuf, w1b, w3b, w2b, gsem, ssem):
    i = pl.program_id(0)
    n_used = nblk_ref[0]
    slot = lax.rem(i, 2)
    last_blk = pl.num_programs(0) - 1

    def valid_rows(blk):
        return nv_ref[jnp.clip(blk, 0, last_blk)]

    def each_chunk(nv, fn):
        for c0 in range(0, MOE_ROWS, MOE_DMA_CHUNK):
            @pl.when(c0 < nv)
            def _(c0=c0):
                fn(c0)

    def gather_start(idx_ref, s, nv):
        def issue(c0):
            for r in range(c0, c0 + MOE_DMA_CHUNK):
                pltpu.make_async_copy(h_hbm.at[pl.ds(idx_ref[0, 0, r], 1), :], xbuf.at[s, pl.ds(r, 1), :],
                                      gsem.at[s]).start()
        each_chunk(nv, issue)

    def gather_wait(s, nv):
        each_chunk(nv, lambda c0: pltpu.make_async_copy(
            h_hbm.at[pl.ds(0, MOE_DMA_CHUNK), :], xbuf.at[s, pl.ds(c0, MOE_DMA_CHUNK), :], gsem.at[s]).wait())

    def scatter_start(s, nv):
        def issue(c0):
            for r in range(c0, c0 + MOE_DMA_CHUNK):
                pltpu.make_async_copy(ybuf.at[s, pl.ds(r, 1), :], y_hbm.at[pl.ds(dst_ref[0, 0, r], 1), :],
                                      ssem.at[s]).start()
        each_chunk(nv, issue)

    def scatter_wait(s, nv):
        each_chunk(nv, lambda c0: pltpu.make_async_copy(
            ybuf.at[s, pl.ds(c0, MOE_DMA_CHUNK), :], y_hbm.at[pl.ds(0, MOE_DMA_CHUNK), :], ssem.at[s]).wait())

    @pl.when(i < n_used)
    def _():
        @pl.when(i == 0)
        def _():
            xbuf[...] = jnp.zeros_like(xbuf)
            gather_start(src_ref, 0, valid_rows(0))
            ybuf[1] = jnp.zeros((MOE_ROWS, D_MODEL), F32)
            n_real = y_hbm.shape[0] - 2 * MOE_ROWS
            for half in range(2):
                fill = pltpu.make_async_copy(ybuf.at[1], y_hbm.at[pl.ds(n_real + half * MOE_ROWS, MOE_ROWS), :],
                                             ssem.at[1])
                fill.start()
                fill.wait()

        @pl.when(i + 1 < n_used)
        def _():
            gather_start(srcn_ref, 1 - slot, valid_rows(i + 1))

        @pl.when((i == 0) | (bexp_ref[i] != bexp_ref[jnp.maximum(i - 1, 0)]))
        def _():
            w1b[...] = w1_ref[0, 0].astype(BF16)
            w3b[...] = w3_ref[0, 0].astype(BF16)
            w2b[...] = w2_ref[0, 0].astype(BF16)

        gather_wait(slot, valid_rows(i))

        @pl.when(i >= 2)
        def _():
            scatter_wait(slot, valid_rows(i - 2))

        x = xbuf[slot].astype(BF16)
        a = jnp.dot(x, w1b[...], preferred_element_type=F32)
        g = jnp.dot(x, w3b[...], preferred_element_type=F32)
        hid = (a * jax.nn.sigmoid(a) * g).astype(BF16)
        ybuf[slot] = jnp.dot(hid, w2b[...], preferred_element_type=F32)
        scatter_start(slot, valid_rows(i))

        @pl.when(i == n_used - 1)
        def _():
            @pl.when(i >= 1)
            def _():
                scatter_wait(1 - slot, valid_rows(i - 1))
            scatter_wait(slot, valid_rows(i))


def _moe(h2, eid, w1, w3, w2, layer):
    n = h2.shape[0]
    n_assign = n * MOE_TOP_K
    n_blocks = (n_assign + MOE_EXPERTS * (MOE_ROWS - 1) + MOE_ROWS - 1) // MOE_ROWS
    flat_e = eid.reshape(n_assign)
    order = jnp.argsort(flat_e, stable=True).astype(jnp.int32)
    experts = jnp.arange(MOE_EXPERTS, dtype=jnp.int32)
    counts = jnp.sum((flat_e[:, None] == experts[None, :]).astype(jnp.int32), axis=0)
    starts = jnp.cumsum(counts) - counts
    padded = (counts + MOE_ROWS - 1) // MOE_ROWS * MOE_ROWS
    p_ends = jnp.cumsum(padded)
    p_starts = p_ends - padded
    n_used_s = p_ends[-1] // MOE_ROWS
    n_used = n_used_s.astype(jnp.int32).reshape(1)
    blk = jnp.arange(n_blocks, dtype=jnp.int32)
    blk_expert = jnp.sum((p_ends[None, :] <= (blk * MOE_ROWS)[:, None]).astype(jnp.int32), axis=1)
    last_expert = jnp.max(jnp.where(counts > 0, experts, 0))
    block_expert = jnp.where(blk < n_used_s, jnp.minimum(blk_expert, MOE_EXPERTS - 1), last_expert)
    onehot = (block_expert[:, None] == experts[None, :]).astype(jnp.int32)
    pick = lambda table: jnp.sum(onehot * table[None, :], axis=1)
    local = jnp.arange(MOE_ROWS, dtype=jnp.int32)[None, :]
    rank = blk[:, None] * MOE_ROWS + local - pick(p_starts)[:, None]
    valid = (rank < pick(counts)[:, None]) & (blk < n_used_s)[:, None]
    pos = jnp.clip(pick(starts)[:, None] + rank, 0, n_assign - 1)
    assign = order[pos]
    tok = jnp.where(valid, assign // MOE_TOP_K, 0)
    dst = jnp.where(valid, (assign % MOE_TOP_K) * n + assign // MOE_TOP_K,
                    n_assign + (blk % 2)[:, None] * MOE_ROWS + local)
    n_valid = jnp.sum(valid.astype(jnp.int32), axis=1)
    rows_now = pl.BlockSpec((1, 1, MOE_ROWS), lambda i, *_: (i, 0, 0), memory_space=pltpu.SMEM)
    rows_next = pl.BlockSpec((1, 1, MOE_ROWS), lambda i, *_: (jnp.minimum(i + 1, n_blocks - 1), 0, 0),
                             memory_space=pltpu.SMEM)
    src3 = tok.reshape(n_blocks, 1, MOE_ROWS)
    return pl.pallas_call(
        _moe_kernel,
        grid_spec=pltpu.PrefetchScalarGridSpec(
            num_scalar_prefetch=3,
            grid=(n_blocks,),
            in_specs=[
                rows_now, rows_next, rows_now,
                pl.BlockSpec(memory_space=pl.ANY),
                pl.BlockSpec((1, 1, D_MODEL, MOE_HIDDEN), lambda i, be, *_: (layer, be[i], 0, 0)),
                pl.BlockSpec((1, 1, D_MODEL, MOE_HIDDEN), lambda i, be, *_: (layer, be[i], 0, 0)),
                pl.BlockSpec((1, 1, MOE_HIDDEN, D_MODEL), lambda i, be, *_: (layer, be[i], 0, 0)),
            ],
            out_specs=pl.BlockSpec(memory_space=pl.ANY),
            scratch_shapes=[pltpu.VMEM((2, MOE_ROWS, D_MODEL), F32), pltpu.VMEM((2, MOE_ROWS, D_MODEL), F32),
                            pltpu.VMEM((D_MODEL, MOE_HIDDEN), BF16), pltpu.VMEM((D_MODEL, MOE_HIDDEN), BF16),
                            pltpu.VMEM((MOE_HIDDEN, D_MODEL), BF16),
                            pltpu.SemaphoreType.DMA((2,)), pltpu.SemaphoreType.DMA((2,))],
        ),
        out_shape=jax.ShapeDtypeStruct((n_assign + 2 * MOE_ROWS, D_MODEL), F32),
        compiler_params=_params("arbitrary"),
        name="moe_experts",
    )(block_expert, n_used, n_valid, src3, src3, dst.reshape(n_blocks, 1, MOE_ROWS), h2, w1, w3, w2)


def _combine_kernel(x1_ref, y0_ref, y1_ref, gate_ref, mod_ref, g_ref, o_ref, *, final):
    f = gate_ref[:, 0:1] * y0_ref[...] + gate_ref[:, 1:2] * y1_ref[...]
    x2 = x1_ref[...] + mod_ref[0, 5:6, :] * f
    if final:
        x2 = _rms(x2) * g_ref[...]
    o_ref[...] = x2


def _combine(n_blocks, x1, y, gate, mod3, g_final, final):
    row = lambda w: pl.BlockSpec((RB, w), lambda i: (i, 0))
    n = n_blocks * RB
    return pl.pallas_call(
        functools.partial(_combine_kernel, final=final),
        grid=(n_blocks,),
        in_specs=[row(D_MODEL), row(D_MODEL), pl.BlockSpec((RB, D_MODEL), lambda i: (i + n_blocks, 0)), row(128),
                  pl.BlockSpec((1, 6, D_MODEL), lambda i: (_mod_row(i), 0, 0)),
                  pl.BlockSpec((1, D_MODEL), lambda i: (0, 0))],
        out_specs=row(D_MODEL),
        out_shape=jax.ShapeDtypeStruct((n, D_MODEL), F32),
        compiler_params=_params("arbitrary"),
        name="moe_combine_final" if final else "moe_combine",
    )(x1, y, y, gate, mod3, g_final)


def _rope_rot_cols(w):
    p = ROPE_PAIRS
    return jnp.concatenate([-w[..., p:2 * p], w[..., 0:p], -w[..., 3 * p:4 * p], w[..., 2 * p:3 * p]], axis=-1)


def _pack_w_in(w_in):
    cuts = np.cumsum([S5_WIDTH, MLA_Q_LORA, MLA_KV_LORA, MLA_ROPE, HG_HEADS * HG_K, HG_HEADS * HG_K,
                      HG_HEADS * HG_K, HG_HEADS * HG_V, HG_HEADS * HG_V, D_MODEL, D_MODEL])
    (w_s5, w_cq, w_ckv, w_kpe, w_hq, w_zf, w_zb, w_hi, w_hg, w_ga, w_gb, w_gc) = jnp.split(w_in, cuts, axis=-1)
    pad = jnp.zeros(w_in.shape[:-1] + (256 - MLA_KV_LORA - 2 * MLA_ROPE,), w_in.dtype)
    packed = jnp.concatenate([w_s5, w_cq, w_ckv, w_kpe, _rope_rot_cols(w_kpe), pad,
                              w_hq, w_zf, w_zb, w_hi, w_hg, w_ga, w_gb, w_gc], axis=-1)
    assert packed.shape[-1] == W_IN_COLS
    return packed.astype(BF16)


def _pack_mla(w_uq, w_uk, w_uv):
    hp, n, r = MLA_HEAD_PAD, MLA_NOPE, MLA_ROPE
    nl = w_uq.shape[0]
    wq = w_uq.reshape(nl, MLA_Q_LORA, MLA_HEADS, n + r)
    zq = jnp.zeros((nl, MLA_Q_LORA, MLA_HEADS, hp - n - r), F32)
    q1 = jnp.concatenate([wq, zq], axis=-1).reshape(nl, MLA_Q_LORA, MLA_HEADS * hp)
    q2 = jnp.concatenate([jnp.zeros((nl, MLA_Q_LORA, MLA_HEADS, n), F32), _rope_rot_cols(wq[..., n:]), zq],
                         axis=-1).reshape(nl, MLA_Q_LORA, MLA_HEADS * hp)
    wq_packed = jnp.concatenate([q1, q2], axis=-1).astype(BF16)

    wk = w_uk.reshape(nl, MLA_KV_LORA, MLA_HEADS, n)
    k_nope = jnp.concatenate([wk, jnp.zeros((nl, MLA_KV_LORA, MLA_HEADS, hp - n), F32)], axis=-1)
    pe_slot = np.concatenate([np.zeros((r, n), np.float32), np.eye(r, dtype=np.float32),
                              np.zeros((r, hp - n - r), np.float32)], axis=-1)
    pe_all = jnp.asarray(np.tile(pe_slot[None, :, None, :], (nl, 1, MLA_HEADS, 1)))
    zero_pe = jnp.zeros_like(pe_all)
    tail = jnp.zeros((nl, 256 - MLA_KV_LORA - 2 * r, MLA_HEADS, hp), F32)
    k1 = jnp.concatenate([k_nope, pe_all, zero_pe, tail], axis=1).reshape(nl, 256, MLA_HEADS * hp)
    k2 = jnp.concatenate([jnp.zeros_like(k_nope), zero_pe, pe_all, tail], axis=1).reshape(nl, 256, MLA_HEADS * hp)
    wk_packed = jnp.concatenate([k1, k2], axis=-1).astype(BF16)
    wv_packed = jnp.concatenate([w_uv, jnp.zeros((nl, 256 - MLA_KV_LORA, MLA_HEADS * MLA_V), F32)],
                                axis=1).astype(BF16)
    return wq_packed, wk_packed, wv_packed


def _rope_tables():
    rows = SEQ // GRID_W
    row = np.repeat(np.arange(rows, dtype=np.float32), GRID_W)
    col = np.tile(np.arange(GRID_W, dtype=np.float32), rows)
    inv = (np.float32(ROPE_BASE) ** (-np.arange(ROPE_PAIRS, dtype=np.float32) / np.float32(ROPE_PAIRS))).astype(np.float32)
    ar, ac = row[:, None] * inv, col[:, None] * inv
    cos = np.concatenate([np.cos(ar), np.cos(ar), np.cos(ac), np.cos(ac)], axis=1)
    sin = np.concatenate([np.sin(ar), np.sin(ar), np.sin(ac), np.sin(ac)], axis=1)
    tail = MLA_HEAD_PAD - MLA_NOPE - MLA_ROPE
    cos_t = np.concatenate([np.ones((SEQ, MLA_NOPE)), cos, np.ones((SEQ, tail))], axis=1)
    sin_t = np.concatenate([np.zeros((SEQ, MLA_NOPE)), sin, np.zeros((SEQ, tail))], axis=1)
    cos_t = np.concatenate([cos_t, np.ones((WB, MLA_HEAD_PAD))], axis=0)
    sin_t = np.concatenate([sin_t, np.zeros((WB, MLA_HEAD_PAD))], axis=0)
    return jnp.asarray(cos_t, F32), jnp.asarray(sin_t, F32)


def _pack_s5(lam_re, lam_im, b_re, b_im, c_re, c_im, log_step):
    eye = jnp.eye(S5_GROUPS, dtype=F32)
    dt = jnp.exp(log_step)[..., None]
    mag = jnp.exp(lam_re * dt)
    lb_re, lb_im = mag * jnp.cos(lam_im * dt), mag * jnp.sin(lam_im * dt)
    den = lam_re * lam_re + lam_im * lam_im
    fr = ((lb_re - 1) * lam_re + lb_im * lam_im) / den
    fi = (lb_im * lam_re - (lb_re - 1) * lam_im) / den
    bb_re = fr[..., None] * b_re - fi[..., None] * b_im
    bb_im = fr[..., None] * b_im + fi[..., None] * b_re
    lead = lam_re.shape[:2]

    def in_mat(bb):
        return jnp.einsum('ldgph,gk->ldghkp', bb, eye).reshape(lead + (S5_WIDTH, S5_LANES))

    def out_mat(cc):
        return jnp.einsum('ldghp,gk->ldgpkh', cc, eye).reshape(lead + (S5_LANES, S5_WIDTH))

    bmat = jnp.concatenate([in_mat(bb_re), in_mat(bb_im)], axis=-1).astype(BF16)
    hl, hw = S5_LANES // 2, S5_WIDTH // 2
    cre, cim = out_mat(c_re), -out_mat(c_im)
    cmat = jnp.stack([jnp.concatenate([cre[..., h * hl:(h + 1) * hl, h * hw:(h + 1) * hw],
                                       cim[..., h * hl:(h + 1) * hl, h * hw:(h + 1) * hw]], axis=-2)
                      for h in range(2)], axis=2).astype(BF16)
    a_tab = jnp.stack([jnp.broadcast_to(lb_re.reshape(lead + (1, S5_LANES)), lead + (BATCH, S5_LANES)),
                       jnp.broadcast_to(lb_im.reshape(lead + (1, S5_LANES)), lead + (BATCH, S5_LANES))], axis=2)
    return bmat, cmat, a_tab


def kernel(x, c, ctx, c_ctx, mod_w, mod_b, norm1_g, norm2_g, w_in, s5_lam_re, s5_lam_im, s5_b_re, s5_b_im, s5_c_re, s5_c_im, s5_log_step, s5_d, s5_glu_w, s5_glu_b, mla_qa_g, mla_kva_g, mla_w_uq, mla_w_uk, mla_w_uv, hg_lb_logits, hg_norm_g, w_pa, w_pb, w_pc, w_out, moe_w_group, moe_b_group, moe_w_expert, moe_b_expert, moe_w1, moe_w3, moe_w2, final_norm_g):
    x_lat, x_ctx, ctx_row0 = x.reshape(N_LAT, D_MODEL), ctx.reshape(N_CTX, D_MODEL), 0
    c_rows = jnp.concatenate([c, c_ctx[None, :], jnp.zeros((16 - BATCH - 1, D_MODEL), F32)], axis=0)
    mod = _modulation(c_rows, mod_w, mod_b).reshape(DEPTH, 16, 6, D_MODEL)

    lb_all = jnp.cumsum(jax.nn.softmax(hg_lb_logits.astype(F32)))
    lb_all = lb_all - lb_all[0]
    hg_scal = jnp.stack([jnp.log(lb_all), jnp.log1p(-lb_all), 1.0 - lb_all], axis=1).reshape(3 * DEPTH).astype(F32)
    cos_t, sin_t = _rope_tables()

    vec = lambda a: a[:, None, :]
    w_in_p = _pack_w_in(w_in)
    bmat, cmat, a_tab = _pack_s5(s5_lam_re, s5_lam_im, s5_b_re, s5_b_im, s5_c_re, s5_c_im, s5_log_step)
    wq, wk, wv = _pack_mla(mla_w_uq, mla_w_uk, mla_w_uv)
    w_route = jnp.concatenate([moe_w_group, jnp.zeros((DEPTH, D_MODEL, 32 - MOE_GROUPS), F32), moe_w_expert,
                               jnp.zeros((DEPTH, D_MODEL, 128 - 32 - MOE_EXPERTS), F32)], axis=-1)
    b_route = jnp.concatenate([moe_b_group, jnp.zeros((DEPTH, 32 - MOE_GROUPS), F32), moe_b_expert,
                               jnp.zeros((DEPTH, 128 - 32 - MOE_EXPERTS), F32)], axis=-1)
    w_route_hi = w_route.astype(BF16)
    w_route_lo = (w_route - w_route_hi.astype(F32)).astype(BF16)
    merge_consts = [vec(s5_d), s5_glu_w.astype(BF16), vec(s5_glu_b), vec(hg_norm_g),
                    w_pa.astype(BF16), w_pb.astype(BF16), w_pc.astype(BF16), w_out.astype(BF16), vec(norm2_g),
                    w_route_hi, w_route_lo, vec(b_route)]
    w_tab_np, mask_tab_np = _hgrn_tables()
    w_tab = jnp.asarray(np.concatenate([w_tab_np, w_tab_np], axis=-1), BF16)
    mask_tab = jnp.asarray(np.kron(np.eye(RB // HG_CHUNK, dtype=np.float32), mask_tab_np), F32)
    head_sum = jnp.asarray(np.kron(np.eye(HG_HEADS), np.ones((HG_V, HG_V))), F32)
    wc_np, maskc_np, vmask_np = _hgrn_fast_tables()
    wc_tab, maskc_tab, vmask_tab = jnp.asarray(wc_np, BF16), jnp.asarray(maskc_np, F32), jnp.asarray(vmask_np, BF16)

    out = None
    for layer in range(DEPTH):
        last = layer == DEPTH - 1
        mod3 = mod[layer]
        (s5u, cqn, kvn, hq, lff, lfb, kkf, kkb, hi, hgs, gates) = _input_projection(
            layer, x_lat, x_ctx, ctx_row0, mod3, hg_scal, vec(norm1_g), w_in_p, vec(mla_qa_g), vec(mla_kva_g))

        yf, yb = [_from_time_major(y) for y in _s5_states(layer, _to_time_major(s5u), bmat, cmat, a_tab)]

        q, k, v = _qkv(layer, cqn, kvn, cos_t, sin_t, wq, wk, wv)
        att_lat = _attention(q, k, v, latent=True)
        att_ctx = None if last else _attention(q, k, v, latent=False)

        of, ob = _hgrn(hq, kkf, kkb, lff, lfb, hi, w_tab, mask_tab, wc_tab, maskc_tab, vmask_tab)

        n_blocks = NB_LAT if last else NB_TOK
        x1, h2, eid, gate = _merge(layer, n_blocks, x_lat, x_ctx, ctx_row0, mod3, yf, yb, s5u, of, ob, hgs,
                                   att_lat, att_ctx, gates, head_sum, merge_consts)

        y_moe = _moe(h2, eid[:, :MOE_TOP_K], moe_w1, moe_w3, moe_w2, layer)
        res = _combine(n_blocks, x1, y_moe, gate, mod3, final_norm_g[None, :], final=last)
        if last:
            out = res
        else:
            x_lat, x_ctx, ctx_row0 = res, res, N_LAT
    return out.reshape(BATCH, SEQ, D_MODEL)
```

```python
import functools
import math

import jax
import jax.numpy as jnp
from jax import lax
import numpy as np
from jax.experimental import pallas as pl
from jax.experimental.pallas import tpu as pltpu

F32 = jnp.float32
BF16 = jnp.bfloat16
HIGHEST = lax.Precision.HIGHEST

D_MODEL = 1024
BATCH = 8
SEQ = 2048
DEPTH = 2
GRID_W = 64
CTX_LEN = 256
NORM_EPS = 1e-6

S5_WIDTH = D_MODEL // 4
S5_GROUP_CH = 16
S5_GROUPS = S5_WIDTH // S5_GROUP_CH
S5_STATE = 64
S5_LANES = S5_GROUPS * S5_STATE

MLA_HEADS = D_MODEL // 128
MLA_NOPE = 64
MLA_ROPE = 32
MLA_V = 64
MLA_Q_LORA = D_MODEL // 4
MLA_KV_LORA = D_MODEL // 8
MLA_SCALE = 1.0 / math.sqrt(MLA_NOPE + MLA_ROPE)
MLA_HEAD_PAD = 128
ROPE_PAIRS = MLA_ROPE // 4
ROPE_BASE = 10000.0

HG_HEADS = D_MODEL // 256
HG_K = 128
HG_V = 64
HG_CHUNK = 64
HG_LEVELS = 6
HG_SETS = 2 + HG_LEVELS
HG_FAST_CHUNK = 32
HG_FAST_MIN = -60.0

MOE_GROUPS = 4
MOE_EXPERTS_PER_GROUP = 8
MOE_EXPERTS = MOE_GROUPS * MOE_EXPERTS_PER_GROUP
MOE_TOP_K = 2
MOE_HIDDEN = D_MODEL // 2
MOE_ROWS = 256
MOE_DMA_CHUNK = 32

N_LAT = BATCH * SEQ
N_CTX = BATCH * CTX_LEN
N_TOK = N_LAT + N_CTX
RB = CTX_LEN
WB = 2 * RB
NB_LAT = N_LAT // RB
NB_CTX = N_CTX // RB
NB_TOK = N_TOK // RB
LAT_BLOCKS = SEQ // RB
SEQ_BLOCKS = LAT_BLOCKS + 1

C_S5 = 0
C_CQ = 256
C_KV = 512
C_HQ = 768
C_ZF = 1280
C_ZB = 1792
C_HI = 2304
C_HG = 2560
C_GATE = 2816
W_IN_COLS = C_GATE + 3 * D_MODEL

VMEM_LIMIT = 56 * 1024 * 1024


def _params(*sem):
    return pltpu.CompilerParams(dimension_semantics=sem, vmem_limit_bytes=VMEM_LIMIT)


def _rms(v):
    return v * lax.rsqrt(jnp.mean(v * v, axis=-1, keepdims=True) + NORM_EPS)


def _mod_row(i, rb=RB):
    return jnp.where(i < N_LAT // rb, i // (SEQ // rb), BATCH)


def _resident(a):
    return pl.BlockSpec(a.shape, lambda *_: (0,) * a.ndim, pipeline_mode=pl.Buffered(1))


def _layer_spec(a, layer):
    return pl.BlockSpec((1,) + a.shape[1:], lambda *_: (layer,) + (0,) * (a.ndim - 1),
                        pipeline_mode=pl.Buffered(1))


def _split_row_specs(rb, w, ctx_row0):
    n_lat = N_LAT // rb
    lat = pl.BlockSpec((rb, w), lambda i, *_: (jnp.minimum(i, n_lat - 1), 0))
    ctx = pl.BlockSpec((rb, w), lambda i, *_: (ctx_row0 // rb + jnp.maximum(i - n_lat, 0), 0))
    return lat, ctx


def _pick_rows(lat_ref, ctx_ref, rb):
    return jnp.where(pl.program_id(0) < N_LAT // rb, lat_ref[...], ctx_ref[...])


def _mod_kernel(c_ref, w_ref, b_ref, o_ref):
    c = c_ref[...]
    s = c * jax.nn.sigmoid(c)
    o_ref[0] = jnp.dot(s, w_ref[0], precision=HIGHEST, preferred_element_type=F32) + b_ref[0]


def _modulation(c_rows, mod_w, mod_b):
    bn = 1536
    return pl.pallas_call(
        _mod_kernel,
        grid=(DEPTH, 6 * D_MODEL // bn),
        in_specs=[
            pl.BlockSpec((16, D_MODEL), lambda l, j: (0, 0)),
            pl.BlockSpec((1, D_MODEL, bn), lambda l, j: (l, 0, j)),
            pl.BlockSpec((1, 1, bn), lambda l, j: (l, 0, j)),
        ],
        out_specs=pl.BlockSpec((1, 16, bn), lambda l, j: (l, 0, j)),
        out_shape=jax.ShapeDtypeStruct((DEPTH, 16, 6 * D_MODEL), F32),
        compiler_params=_params("arbitrary", "arbitrary"),
        name="adaln_mod",
    )(c_rows, mod_w, mod_b.reshape(DEPTH, 1, 6 * D_MODEL))


def _log_sigmoid(z):
    return jnp.minimum(z, 0.0) - jnp.log1p(jnp.exp(-jnp.abs(z)))


def _in_kernel(hs_ref, xl_ref, xc_ref, mod_ref, g1_ref, w_ref, qag_ref, kvg_ref,
               s5u_ref, cqn_ref, kvn_ref, hq_ref, lff_ref, lfb_ref, kkf_ref, kkb_ref,
               hi_ref, hgs_ref, gates_ref, h_ref, *, layer):
    x = _pick_rows(xl_ref, xc_ref, WB)
    xn = _rms(x) * g1_ref[0]
    h_ref[...] = (xn * (1.0 + mod_ref[0, 1:2, :]) + mod_ref[0, 0:1, :]).astype(BF16)

    def mm(a, b):
        return jnp.dot(h_ref[...], w_ref[0, :, a:b], preferred_element_type=F32)

    s5u_ref[...] = mm(C_S5, C_CQ).astype(BF16)
    cqn_ref[...] = (_rms(mm(C_CQ, C_KV)) * qag_ref[0]).astype(BF16)
    kv = mm(C_KV, C_HQ)
    ckvn = _rms(kv[:, :MLA_KV_LORA]) * kvg_ref[0]
    kvn_ref[...] = jnp.concatenate([ckvn, kv[:, MLA_KV_LORA:]], axis=1).astype(BF16)
    hq_ref[...] = mm(C_HQ, C_ZF).astype(BF16)

    log_lb = hs_ref[3 * layer]
    log_1m_lb = hs_ref[3 * layer + 1]
    one_m_lb = hs_ref[3 * layer + 2]
    for c0, lf_ref, kk_ref in ((C_ZF, lff_ref, kkf_ref), (C_ZB, lfb_ref, kkb_ref)):
        z = mm(c0, c0 + HG_HEADS * HG_K)
        b = log_1m_lb + _log_sigmoid(z)
        m = jnp.maximum(b, log_lb)
        lf_ref[...] = m + jnp.log1p(jnp.exp(-jnp.abs(b - log_lb)))
        kk_ref[...] = (one_m_lb * jax.nn.sigmoid(-z)).astype(BF16)

    hi_ref[...] = mm(C_HI, C_HG).astype(BF16)
    g = mm(C_HG, C_GATE)
    hgs_ref[...] = (g * jax.nn.sigmoid(g)).astype(BF16)
    gw = 512
    for j in range(3 * D_MODEL // gw):
        c0 = C_GATE + j * gw
        gates_ref[:, j * gw:(j + 1) * gw] = jax.nn.sigmoid(mm(c0, c0 + gw)).astype(BF16)


def _input_projection(layer, x_lat, x_ctx, ctx_row0, mod3, hg_scal, g1, w_in, qa_g, kva_g):
    row = lambda w: pl.BlockSpec((WB, w), lambda i: (i, 0))
    x_specs = _split_row_specs(WB, D_MODEL, ctx_row0)
    shapes = [
        (S5_WIDTH, BF16), (MLA_Q_LORA, BF16), (256, BF16), (HG_HEADS * HG_K, BF16),
        (HG_HEADS * HG_K, F32), (HG_HEADS * HG_K, F32), (HG_HEADS * HG_K, BF16), (HG_HEADS * HG_K, BF16),
        (HG_HEADS * HG_V, BF16), (HG_HEADS * HG_V, BF16), (3 * D_MODEL, BF16),
    ]
    return pl.pallas_call(
        functools.partial(_in_kernel, layer=layer),
        grid=(N_TOK // WB,),
        in_specs=[
            pl.BlockSpec(memory_space=pltpu.SMEM),
            *x_specs,
            pl.BlockSpec((1, 6, D_MODEL), lambda i: (_mod_row(i, WB), 0, 0)),
            _layer_spec(g1, layer), _layer_spec(w_in, layer), _layer_spec(qa_g, layer), _layer_spec(kva_g, layer),
        ],
        out_specs=[row(w) for w, _ in shapes],
        out_shape=[jax.ShapeDtypeStruct((N_TOK, w), dt) for w, dt in shapes],
        scratch_shapes=[pltpu.VMEM((WB, D_MODEL), BF16)],
        compiler_params=_params("arbitrary"),
        name="input_projection",
    )(hg_scal, x_lat, x_ctx, mod3, g1, w_in, qa_g, kva_g)


def _seq_block(b, j):
    return jnp.where(j == 0, NB_LAT + b, b * LAT_BLOCKS + j - 1)


def _seq_block_rev(b, j):
    return jnp.where(j == 0, NB_LAT + b, b * LAT_BLOCKS + LAT_BLOCKS - j)


S5_TB = 128
S5_ROWS = S5_TB * BATCH
S5_STEPS = (CTX_LEN + SEQ) // S5_TB
S5_CTX_STEPS = CTX_LEN // S5_TB


def _s5_kernel(uf_ref, ub_ref, bmat_ref, cmat_ref, a_ref, yf_ref, yb_ref, bu_ref, carry_ref):
    j = pl.program_id(0)

    @pl.when(j == 0)
    def _():
        carry_ref[...] = jnp.zeros_like(carry_ref)

    for d, (u_ref, y_ref) in enumerate(((uf_ref, yf_ref), (ub_ref, yb_ref))):
        bu_ref[...] = jnp.dot(u_ref[...], bmat_ref[0, d], preferred_element_type=F32)
        are = a_ref[0, d, 0]
        aim = a_ref[0, d, 1]

        def body(t, carry, d=d, are=are, aim=aim):
            hre, him = carry
            tt = (S5_TB - 1 - t) if d == 1 else t
            r0 = pl.multiple_of(tt * BATCH, BATCH)
            nre = are * hre - aim * him + bu_ref[pl.ds(r0, BATCH), 0:S5_LANES]
            nim = are * him + aim * hre + bu_ref[pl.ds(r0, BATCH), S5_LANES:2 * S5_LANES]
            bu_ref[pl.ds(r0, BATCH), 0:S5_LANES] = nre
            bu_ref[pl.ds(r0, BATCH), S5_LANES:2 * S5_LANES] = nim
            return nre, nim

        hre, him = lax.fori_loop(0, S5_TB, body, (carry_ref[d, 0], carry_ref[d, 1]), unroll=4)
        carry_ref[d, 0] = hre
        carry_ref[d, 1] = him
        hl = S5_LANES // 2
        hw = S5_WIDTH // 2
        for half in range(2):
            states = jnp.concatenate([bu_ref[:, half * hl:(half + 1) * hl],
                                      bu_ref[:, S5_LANES + half * hl:S5_LANES + (half + 1) * hl]], axis=1)
            y_ref[:, half * hw:(half + 1) * hw] = jnp.dot(states.astype(BF16), cmat_ref[0, d, half],
                                                          preferred_element_type=F32)


def _s5_block_rev(j):
    return jnp.where(j < S5_CTX_STEPS, S5_CTX_STEPS - 1 - j, S5_STEPS - 1 + S5_CTX_STEPS - j)


def _s5_states(layer, u_tm, bmat, cmat, a_tab):
    full = lambda a: _layer_spec(a, layer)
    fwd = pl.BlockSpec((S5_ROWS, S5_WIDTH), lambda j: (j, 0))
    bwd = pl.BlockSpec((S5_ROWS, S5_WIDTH), lambda j: (_s5_block_rev(j), 0))
    return pl.pallas_call(
        _s5_kernel,
        grid=(S5_STEPS,),
        in_specs=[fwd, bwd, full(bmat), full(cmat), full(a_tab)],
        out_specs=[fwd, bwd],
        out_shape=[jax.ShapeDtypeStruct((S5_STEPS * S5_ROWS, S5_WIDTH), F32)] * 2,
        scratch_shapes=[pltpu.VMEM((S5_ROWS, 2 * S5_LANES), F32), pltpu.VMEM((2, 2, BATCH, S5_LANES), F32)],
        compiler_params=_params("arbitrary"),
        name="s5_scan",
    )(u_tm, u_tm, bmat, cmat, a_tab)


def _to_time_major(a):
    w = a.shape[1]
    seq = jnp.concatenate([a[N_LAT:].reshape(BATCH, CTX_LEN, w), a[:N_LAT].reshape(BATCH, SEQ, w)], axis=1)
    return seq.transpose(1, 0, 2).reshape((CTX_LEN + SEQ) * BATCH, w)


def _from_time_major(a):
    w = a.shape[1]
    seq = a.reshape(CTX_LEN + SEQ, BATCH, w).transpose(1, 0, 2)
    return jnp.concatenate([seq[:, CTX_LEN:].reshape(N_LAT, w), seq[:, :CTX_LEN].reshape(N_CTX, w)], axis=0)


def _qkv_kernel(cqn_ref, kvn_ref, cos_ref, sin_ref, wq_ref, wk_ref, wv_ref, q_ref, k_ref, v_ref):
    cos = jnp.concatenate([cos_ref[...]] * MLA_HEADS, axis=1)
    sin = jnp.concatenate([sin_ref[...]] * MLA_HEADS, axis=1)
    w = MLA_HEADS * MLA_HEAD_PAD
    q2 = jnp.dot(cqn_ref[...], wq_ref[0], preferred_element_type=F32)
    q_ref[...] = ((q2[:, :w] * cos + q2[:, w:] * sin) * MLA_SCALE).astype(BF16)
    kv = kvn_ref[...]
    k2 = jnp.dot(kv, wk_ref[0], preferred_element_type=F32)
    k_ref[...] = (k2[:, :w] * cos + k2[:, w:] * sin).astype(BF16)
    v_ref[...] = jnp.dot(kv, wv_ref[0], preferred_element_type=F32).astype(BF16)


def _qkv(layer, cqn, kvn, cos_t, sin_t, wq, wk, wv):
    row = lambda w: pl.BlockSpec((WB, w), lambda i: (i, 0))
    full = lambda a: _layer_spec(a, layer)
    pos = pl.BlockSpec((WB, MLA_HEAD_PAD),
                       lambda i: (jnp.where(i < N_LAT // WB, i % (SEQ // WB), SEQ // WB), 0))
    w = MLA_HEADS * MLA_HEAD_PAD
    return pl.pallas_call(
        _qkv_kernel,
        grid=(N_TOK // WB,),
        in_specs=[row(MLA_Q_LORA), row(256), pos, pos, full(wq), full(wk), full(wv)],
        out_specs=[row(w), row(w), row(MLA_HEADS * MLA_V)],
        out_shape=[jax.ShapeDtypeStruct((N_TOK, w), BF16), jax.ShapeDtypeStruct((N_TOK, w), BF16),
                   jax.ShapeDtypeStruct((N_TOK, MLA_HEADS * MLA_V), BF16)],
        compiler_params=_params("arbitrary"),
        name="mla_qkv",
    )(cqn, kvn, cos_t, sin_t, wq, wk, wv)


def _attn_kernel(*refs, n_kv):
    q_ref = refs[0]
    k_refs = refs[1:1 + n_kv]
    v_refs = refs[1 + n_kv:1 + 2 * n_kv]
    o_ref = refs[1 + 2 * n_kv]
    nt = (((1,), (1,)), ((), ()))
    for h in range(MLA_HEADS):
        q = q_ref[:, h * MLA_HEAD_PAD:(h + 1) * MLA_HEAD_PAD]
        s = [lax.dot_general(q, k_ref[:, h * MLA_HEAD_PAD:(h + 1) * MLA_HEAD_PAD], nt,
                             preferred_element_type=F32) for k_ref in k_refs]
        m = functools.reduce(jnp.maximum, [jnp.max(si, axis=-1, keepdims=True) for si in s])
        p = [jnp.exp(si - m) for si in s]
        l = functools.reduce(jnp.add, [jnp.sum(pi, axis=-1, keepdims=True) for pi in p])
        o = functools.reduce(jnp.add, [
            jnp.dot(pi.astype(BF16), v_ref[:, h * MLA_V:(h + 1) * MLA_V], preferred_element_type=F32)
            for pi, v_ref in zip(p, v_refs)])
        o_ref[:, h * MLA_V:(h + 1) * MLA_V] = (o / l).astype(BF16)


def _attention(q, k, v, latent):
    w = MLA_HEADS * MLA_HEAD_PAD
    wv = MLA_HEADS * MLA_V
    if latent:
        tq = WB
        nq = SEQ // tq
        q_spec = pl.BlockSpec((tq, w), lambda b, j: (b * nq + j, 0))
        k_specs = [pl.BlockSpec((SEQ, w), lambda b, j: (b, 0)),
                   pl.BlockSpec((CTX_LEN, w), lambda b, j: (NB_LAT + b, 0))]
        v_specs = [pl.BlockSpec((SEQ, wv), lambda b, j: (b, 0)),
                   pl.BlockSpec((CTX_LEN, wv), lambda b, j: (NB_LAT + b, 0))]
        o_spec = pl.BlockSpec((tq, wv), lambda b, j: (b * nq + j, 0))
        n_out = N_LAT
    else:
        tq = CTX_LEN
        nq = 1
        q_spec = pl.BlockSpec((tq, w), lambda b, j: (NB_LAT + b, 0))
        k_specs = [pl.BlockSpec((CTX_LEN, w), lambda b, j: (NB_LAT + b, 0))]
        v_specs = [pl.BlockSpec((CTX_LEN, wv), lambda b, j: (NB_LAT + b, 0))]
        o_spec = pl.BlockSpec((tq, wv), lambda b, j: (b, 0))
        n_out = N_CTX
    n_kv = len(k_specs)
    return pl.pallas_call(
        functools.partial(_attn_kernel, n_kv=n_kv),
        grid=(BATCH, nq),
        in_specs=[q_spec] + k_specs + v_specs,
        out_specs=o_spec,
        out_shape=jax.ShapeDtypeStruct((n_out, wv), BF16),
        compiler_params=_params("arbitrary", "arbitrary"),
        name="mla_attention_lat" if latent else "mla_attention_ctx",
    )(q, *([k] * n_kv), *([v] * n_kv))


def _hgrn_tables():
    c = HG_CHUNK
    w = np.zeros((2, HG_SETS * c, c), np.float32)
    mask = np.zeros((2, HG_LEVELS, c, c), np.float32)
    idx = np.arange(c)
    for r in range(c):
        w[0, r, :r + 1] = 1
        w[1, r, r:] = 1
        w[0, c + r, r + 1:] = 1
        w[1, c + r, :r] = 1
    for l in range(HG_LEVELS):
        m = c >> (l + 1)
        for r in range(c):
            base = (r // (2 * m)) * 2 * m
            mid = base + m
            later = r >= mid
            row = (2 + l) * c + r
            if later:
                w[0, row, mid:r + 1] = 1
                w[1, row, mid:r] = 1
            else:
                w[0, row, r + 1:mid] = 1
                w[1, row, r:mid] = 1
        same = (idx[:, None] // (2 * m)) == (idx[None, :] // (2 * m))
        q_later = (idx[:, None] % (2 * m)) >= m
        k_later = (idx[None, :] % (2 * m)) >= m
        mask[0, l] = same & q_later & ~k_later
        mask[1, l] = same & ~q_later & k_later
    return w, mask


def _hgrn_dir(d, q_ref, k_ref, lf_ref, v_ref, w_ref, mask_ref, st_ref, o_ref):
    c = HG_CHUNK
    n_chunks = RB // c
    nt = (((1,), (1,)), ((), ()))
    tn = (((0,), (0,)), ((), ()))
    last = c - 1 if d == 0 else 0
    wsel = w_ref[d]
    e = []
    for ci in range(n_chunks):
        g = lf_ref[ci * c:(ci + 1) * c, :]
        g_hi = g.astype(BF16)
        g_lo = (g - g_hi.astype(F32)).astype(BF16)
        expo = jnp.dot(wsel, jnp.concatenate([g_hi, g_lo], axis=0), preferred_element_type=F32)
        e.append(jnp.exp(expo))

    def rows(s):
        return jnp.concatenate([e[ci][s * c:(s + 1) * c] for ci in range(n_chunks)], axis=0)

    q = q_ref[...].astype(F32)
    k = k_ref[...].astype(F32)
    q_state = (q * rows(0)).astype(BF16)
    k_state = (k * rows(1)).astype(BF16)
    q_lvl, k_lvl = [], []
    for l in range(HG_LEVELS):
        el = rows(2 + l)
        q_lvl.append((q * el).astype(BF16))
        k_lvl.append((k * el).astype(BF16))
    qk = q * k
    order = range(n_chunks) if d == 0 else range(n_chunks - 1, -1, -1)
    for h in range(HG_HEADS):
        ks = slice(h * HG_K, (h + 1) * HG_K)
        vs = slice(h * HG_V, (h + 1) * HG_V)
        v = v_ref[:, vs]
        scores = jnp.zeros((RB, RB), F32)
        for l in range(HG_LEVELS):
            p = lax.dot_general(q_lvl[l][:, ks], k_lvl[l][:, ks], nt, preferred_element_type=F32)
            scores = scores + p * mask_ref[d, l]
        o = jnp.dot(scores.astype(BF16), v, preferred_element_type=F32)
        o = o + jnp.sum(qk[:, ks], axis=-1, keepdims=True) * v.astype(F32)
        st = st_ref[d, h]
        o_state = [None] * n_chunks
        for ci in order:
            r = slice(ci * c, (ci + 1) * c)
            o_state[ci] = lax.dot_general(q_state[r, ks], st.astype(BF16), nt, preferred_element_type=F32)
            inc = lax.dot_general(v[r], k_state[r, ks], tn, preferred_element_type=F32)
            st = st * e[ci][last:last + 1, ks] + inc
        st_ref[d, h] = st
        o_ref[:, vs] = o + jnp.concatenate(o_state, axis=0)


def _hgrn_fast_tables():
    c = HG_FAST_CHUNK
    idx = np.arange(RB)
    same = (idx[:, None] // c) == (idx[None, :] // c)
    le = idx[None, :] <= idx[:, None]
    ge = idx[None, :] >= idx[:, None]
    wc = np.stack([same & le, same & ge]).astype(np.float32)
    col_chunk = np.arange((RB // c) * HG_V) // HG_V
    vmask = ((idx[:, None] // c) == col_chunk[None, :]).astype(np.float32)
    return np.concatenate([wc, wc], axis=-1), wc, vmask


def _hgrn_dir_fast(d, q_ref, k_ref, cum, tot, v_ref, maskc_ref, vmask_ref, st_ref, o_ref):
    c = HG_FAST_CHUNK
    n_chunks = RB // c
    nt = (((1,), (1,)), ((), ()))
    tn = (((0,), (0,)), ((), ()))
    q = q_ref[...].astype(F32)
    k = k_ref[...].astype(F32)
    qa = (q * jnp.exp(cum)).astype(BF16)
    kb = (k * jnp.exp(-cum)).astype(BF16)
    k_state = (k * jnp.exp(tot - cum)).astype(BF16)
    decay = jnp.exp(tot)
    keep = maskc_ref[d] > 0.5
    vmask = vmask_ref[...]
    order = range(n_chunks) if d == 0 else range(n_chunks - 1, -1, -1)
    for h in range(HG_HEADS):
        ks = slice(h * HG_K, (h + 1) * HG_K)
        vs = slice(h * HG_V, (h + 1) * HG_V)
        v = v_ref[:, vs]
        p = lax.dot_general(qa[:, ks], kb[:, ks], nt, preferred_element_type=F32)
        o = jnp.dot(jnp.where(keep, p, 0.0).astype(BF16), v, preferred_element_type=F32)
        v_by_chunk = jnp.concatenate([v] * n_chunks, axis=1) * vmask
        inc = lax.dot_general(v_by_chunk, k_state[:, ks], tn, preferred_element_type=F32)
        st = st_ref[d, h]
        entering = [None] * n_chunks
        for ci in order:
            entering[ci] = st
            st = st * decay[ci * c:ci * c + 1, ks] + inc[ci * HG_V:(ci + 1) * HG_V]
        st_ref[d, h] = st
        s_all = jnp.concatenate(entering, axis=0).astype(BF16)
        o_all = lax.dot_general(qa[:, ks], s_all, nt, preferred_element_type=F32) * vmask.astype(F32)
        o_fold = functools.reduce(jnp.add, [o_all[:, j * 128:(j + 1) * 128] for j in range(n_chunks * HG_V // 128)])
        o_state = (o_fold + pltpu.roll(o_fold, HG_V, 1))[:, :HG_V]
        o_ref[:, vs] = o + o_state


def _hgrn_kernel(qf_ref, kf_ref, lff_ref, vf_ref, qb_ref, kb_ref, lfb_ref, vb_ref, w_ref, mask_ref,
                 wc_ref, maskc_ref, vmask_ref, of_ref, ob_ref, st_ref):
    j = pl.program_id(1)

    @pl.when(j == 0)
    def _():
        st_ref[...] = jnp.zeros_like(st_ref)

    c = HG_FAST_CHUNK
    dirs = ((0, qf_ref, kf_ref, lff_ref, vf_ref, of_ref), (1, qb_ref, kb_ref, lfb_ref, vb_ref, ob_ref))
    for d, q_ref, k_ref, lf_ref, v_ref, o_ref in dirs:
        g_hi, g_lo = _split_bf16(lf_ref[...])
        cum = jnp.dot(wc_ref[d], jnp.concatenate([g_hi, g_lo], axis=0), preferred_element_type=F32)
        edge = c - 1 if d == 0 else 0
        totals = [cum[ci * c + edge:ci * c + edge + 1] for ci in range(RB // c)]
        tot = jnp.concatenate([jnp.broadcast_to(t, (c, t.shape[1])) for t in totals], axis=0)
        safe = jnp.min(jnp.concatenate(totals, axis=0)) >= HG_FAST_MIN

        @pl.when(safe)
        def _(d=d, q_ref=q_ref, k_ref=k_ref, v_ref=v_ref, o_ref=o_ref, cum=cum, tot=tot):
            _hgrn_dir_fast(d, q_ref, k_ref, cum, tot, v_ref, maskc_ref, vmask_ref, st_ref, o_ref)

        @pl.when(jnp.logical_not(safe))
        def _(d=d, q_ref=q_ref, k_ref=k_ref, lf_ref=lf_ref, v_ref=v_ref, o_ref=o_ref):
            _hgrn_dir(d, q_ref, k_ref, lf_ref, v_ref, w_ref, mask_ref, st_ref, o_ref)


def _hgrn(hq, kkf, kkb, lff, lfb, hi, w_tab, mask_tab, wc_tab, maskc_tab, vmask_tab):
    full = _resident
    fwd = lambda w: pl.BlockSpec((RB, w), lambda b, j: (_seq_block(b, j), 0))
    bwd = lambda w: pl.BlockSpec((RB, w), lambda b, j: (_seq_block_rev(b, j), 0))
    wk = HG_HEADS * HG_K
    wv = HG_HEADS * HG_V
    return pl.pallas_call(
        _hgrn_kernel,
        grid=(BATCH, SEQ_BLOCKS),
        in_specs=[fwd(wk), fwd(wk), fwd(wk), fwd(wv), bwd(wk), bwd(wk), bwd(wk), bwd(wv),
                  full(w_tab), full(mask_tab), full(wc_tab), full(maskc_tab), full(vmask_tab)],
        out_specs=[fwd(wv), bwd(wv)],
        out_shape=[jax.ShapeDtypeStruct((N_TOK, wv), F32)] * 2,
        scratch_shapes=[pltpu.VMEM((2, HG_HEADS, HG_V, HG_K), F32)],
        compiler_params=_params("arbitrary", "arbitrary"),
        name="hgrn2_scan",
    )(hq, kkf, lff, hi, hq, kkb, lfb, hi, w_tab, mask_tab, wc_tab, maskc_tab, vmask_tab)


def _gelu_tanh(x):
    return 0.5 * x * (1.0 + jnp.tanh(math.sqrt(2.0 / math.pi) * (x + 0.044715 * (x * x * x))))


def _split_bf16(a):
    hi = a.astype(BF16)
    return hi, (a - hi.astype(F32)).astype(BF16)


def _merge_kernel(xl_ref, xc_ref, mod_ref, yf_ref, yb_ref, u_ref, of_ref, ob_ref, hgs_ref, attl_ref, attc_ref,
                  gates_ref, s5d_ref, gluw_ref, glub_ref, hgg_ref, hsum_ref, wpa_ref, wpb_ref, wpc_ref, wout_ref,
                  g2_ref, wrh_ref, wrl_ref, br_ref,
                  x1_ref, h2_ref, eid_ref, gate_ref):
    y = yf_ref[...] + yb_ref[...] + s5d_ref[0] * u_ref[...].astype(F32)
    y = _gelu_tanh(y)
    y = y * jax.nn.sigmoid(jnp.dot(y.astype(BF16), gluw_ref[0], preferred_element_type=F32) + glub_ref[0])
    o = of_ref[...] + ob_ref[...]
    sq_hi, sq_lo = _split_bf16(o * o)
    ms = (jnp.dot(sq_hi, hsum_ref[...], preferred_element_type=F32)
          + jnp.dot(sq_lo, hsum_ref[...], preferred_element_type=F32)) * (1.0 / HG_V)
    o = o * lax.rsqrt(ms + NORM_EPS) * hgg_ref[0] * hgs_ref[...].astype(F32)

    d = D_MODEL
    att = _pick_rows(attl_ref, attc_ref, WB)
    merged = (gates_ref[:, 0:d].astype(F32) * jnp.dot(y.astype(BF16), wpa_ref[0], preferred_element_type=F32)
              + gates_ref[:, d:2 * d].astype(F32) * jnp.dot(att, wpb_ref[0], preferred_element_type=F32)
              + gates_ref[:, 2 * d:3 * d].astype(F32) * jnp.dot(o.astype(BF16), wpc_ref[0], preferred_element_type=F32))
    y_out = jnp.dot(merged.astype(BF16), wout_ref[0], preferred_element_type=F32)
    x1 = _pick_rows(xl_ref, xc_ref, WB) + mod_ref[0, 2:3, :] * y_out
    x1_ref[...] = x1
    h2 = _rms(x1) * g2_ref[0] * (1.0 + mod_ref[0, 4:5, :]) + mod_ref[0, 3:4, :]
    h2_ref[...] = h2

    h_hi, h_lo = _split_bf16(h2)
    logits = (jnp.dot(h_hi, wrh_ref[0], preferred_element_type=F32)
              + jnp.dot(h_lo, wrh_ref[0], preferred_element_type=F32)
              + jnp.dot(h_hi, wrl_ref[0], preferred_element_type=F32)) + br_ref[0]
    lane = lax.broadcasted_iota(jnp.int32, logits.shape, 1).astype(F32)
    neg = -jnp.inf
    glog = jnp.where(lane < MOE_GROUPS, logits, neg)
    gmax = jnp.max(glog, axis=-1, keepdims=True)
    gidx = jnp.min(jnp.where(glog == gmax, lane, 1e9), axis=-1, keepdims=True)
    g_w = 1.0 / jnp.sum(jnp.exp(glog - gmax), axis=-1, keepdims=True)
    e_lo = 32.0 + gidx * MOE_EXPERTS_PER_GROUP
    elog = jnp.where((lane >= e_lo) & (lane < e_lo + MOE_EXPERTS_PER_GROUP), logits, neg)
    v1 = jnp.max(elog, axis=-1, keepdims=True)
    i1 = jnp.min(jnp.where(elog == v1, lane, 1e9), axis=-1, keepdims=True)
    elog2 = jnp.where(lane == i1, neg, elog)
    v2 = jnp.max(elog2, axis=-1, keepdims=True)
    i2 = jnp.min(jnp.where(elog2 == v2, lane, 1e9), axis=-1, keepdims=True)
    e2 = jnp.exp(v2 - v1)
    gate1 = g_w / (1.0 + e2)
    gate2 = g_w * e2 / (1.0 + e2)
    eid_ref[...] = jnp.where(lane == 0.0, i1 - 32.0, jnp.where(lane == 1.0, i2 - 32.0, 0.0)).astype(jnp.int32)
    gate_ref[...] = jnp.where(lane == 0.0, gate1, jnp.where(lane == 1.0, gate2, 0.0))


def _merge(layer, n_blocks, x_lat, x_ctx, ctx_row0, mod3, yf, yb, s5u, of, ob, hgs, att_lat, att_ctx, gates,
           head_sum, consts):
    row = lambda w: pl.BlockSpec((WB, w), lambda i: (i, 0))
    n = n_blocks * RB
    wv = MLA_HEADS * MLA_V
    if att_ctx is None:
        att_ctx = att_lat
    const_specs = [_layer_spec(a, layer) for a in consts]
    const_specs.insert(4, _resident(head_sum))
    return pl.pallas_call(
        _merge_kernel,
        grid=(n // WB,),
        in_specs=[*_split_row_specs(WB, D_MODEL, ctx_row0),
                  pl.BlockSpec((1, 6, D_MODEL), lambda i: (_mod_row(i, WB), 0, 0)),
                  row(S5_WIDTH), row(S5_WIDTH), row(S5_WIDTH), row(HG_HEADS * HG_V), row(HG_HEADS * HG_V),
                  row(HG_HEADS * HG_V), *_split_row_specs(WB, wv, 0), row(3 * D_MODEL)] + const_specs,
        out_specs=[row(D_MODEL), row(D_MODEL), row(128), row(128)],
        out_shape=[jax.ShapeDtypeStruct((n, D_MODEL), F32), jax.ShapeDtypeStruct((n, D_MODEL), F32),
                   jax.ShapeDtypeStruct((n, 128), jnp.int32), jax.ShapeDtypeStruct((n, 128), F32)],
        compiler_params=_params("arbitrary"),
        name="merge_router",
    )(x_lat, x_ctx, mod3, yf, yb, s5u, of, ob, hgs, att_lat, att_ctx, gates,
      *consts[:4], head_sum, *consts[4:])


def _moe_kernel(bexp_ref, nblk_ref, nv_ref, src_ref, srcn_ref, dst_ref, h_hbm, w1_ref, w3_ref, w2_ref, y_hbm,
                xbuf, ybuf, w1b, w3b, w2b, gsem, ssem):
    i = pl.program_id(0)
    n_used = nblk_ref[0]
    slot = lax.rem(i, 2)
    last_blk = pl.num_programs(0) - 1

    def valid_rows(blk):
        return nv_ref[jnp.clip(blk, 0, last_blk)]

    def each_chunk(nv, fn):
        for c0 in range(0, MOE_ROWS, MOE_DMA_CHUNK):
            @pl.when(c0 < nv)
            def _(c0=c0):
                fn(c0)

    def gather_start(idx_ref, s, nv):
        def issue(c0):
            for r in range(c0, c0 + MOE_DMA_CHUNK):
                pltpu.make_async_copy(h_hbm.at[pl.ds(idx_ref[0, 0, r], 1), :], xbuf.at[s, pl.ds(r, 1), :],
                                      gsem.at[s]).start()
        each_chunk(nv, issue)

    def gather_wait(s, nv):
        each_chunk(nv, lambda c0: pltpu.make_async_copy(
            h_hbm.at[pl.ds(0, MOE_DMA_CHUNK), :], xbuf.at[s, pl.ds(c0, MOE_DMA_CHUNK), :], gsem.at[s]).wait())

    def scatter_start(s, nv):
        def issue(c0):
            for r in range(c0, c0 + MOE_DMA_CHUNK):
                pltpu.make_async_copy(ybuf.at[s, pl.ds(r, 1), :], y_hbm.at[pl.ds(dst_ref[0, 0, r], 1), :],
                                      ssem.at[s]).start()
        each_chunk(nv, issue)

    def scatter_wait(s, nv):
        each_chunk(nv, lambda c0: pltpu.make_async_copy(
            ybuf.at[s, pl.ds(c0, MOE_DMA_CHUNK), :], y_hbm.at[pl.ds(0, MOE_DMA_CHUNK), :], ssem.at[s]).wait())

    @pl.when(i < n_used)
    def _():
        @pl.when(i == 0)
        def _():
            xbuf[...] = jnp.zeros_like(xbuf)
            gather_start(src_ref, 0, valid_rows(0))
            ybuf[1] = jnp.zeros((MOE_ROWS, D_MODEL), F32)
            n_real = y_hbm.shape[0] - 2 * MOE_ROWS
            for half in range(2):
                fill = pltpu.make_async_copy(ybuf.at[1], y_hbm.at[pl.ds(n_real + half * MOE_ROWS, MOE_ROWS), :],
                                             ssem.at[1])
                fill.start()
                fill.wait()

        @pl.when((i == 0) | (bexp_ref[i] != bexp_ref[jnp.maximum(i - 1, 0)]))
        def _():
            w1b[...] = w1_ref[0, 0].astype(BF16)
            w3b[...] = w3_ref[0, 0].astype(BF16)
            w2b[...] = w2_ref[0, 0].astype(BF16)

        gather_wait(slot, valid_rows(i))

        @pl.when(i >= 2)
        def _():
            scatter_wait(slot, valid_rows(i - 2))

        x = xbuf[slot].astype(BF16)
        a = jnp.dot(x, w1b[...], preferred_element_type=F32)
        g = jnp.dot(x, w3b[...], preferred_element_type=F32)
        hid = (a * jax.nn.sigmoid(a) * g).astype(BF16)

        @pl.when(i + 1 < n_used)
        def _():
            gather_start(srcn_ref, 1 - slot, valid_rows(i + 1))

        ybuf[slot] = jnp.dot(hid, w2b[...], preferred_element_type=F32)
        scatter_start(slot, valid_rows(i))

        @pl.when(i == n_used - 1)
        def _():
            @pl.when(i >= 1)
            def _():
                scatter_wait(1 - slot, valid_rows(i - 1))
            scatter_wait(slot, valid_rows(i))


def _moe(h2, eid, w1, w3, w2, layer):
    n = h2.shape[0]
    n_assign = n * MOE_TOP_K
    n_blocks = (n_assign + MOE_EXPERTS * (MOE_ROWS - 1) + MOE_ROWS - 1) // MOE_ROWS
    flat_e = eid.reshape(n_assign)
    order = jnp.argsort(flat_e, stable=True).astype(jnp.int32)
    experts = jnp.arange(MOE_EXPERTS, dtype=jnp.int32)
    counts = jnp.sum((flat_e[:, None] == experts[None, :]).astype(jnp.int32), axis=0)
    starts = jnp.cumsum(counts) - counts
    padded = (counts + MOE_ROWS - 1) // MOE_ROWS * MOE_ROWS
    p_ends = jnp.cumsum(padded)
    p_starts = p_ends - padded
    n_used_s = p_ends[-1] // MOE_ROWS
    n_used = n_used_s.astype(jnp.int32).reshape(1)
    blk = jnp.arange(n_blocks, dtype=jnp.int32)
    blk_expert = jnp.sum((p_ends[None, :] <= (blk * MOE_ROWS)[:, None]).astype(jnp.int32), axis=1)
    last_expert = jnp.max(jnp.where(counts > 0, experts, 0))
    block_expert = jnp.where(blk < n_used_s, jnp.minimum(blk_expert, MOE_EXPERTS - 1), last_expert)
    onehot = (block_expert[:, None] == experts[None, :]).astype(jnp.int32)
    pick = lambda table: jnp.sum(onehot * table[None, :], axis=1)
    local = jnp.arange(MOE_ROWS, dtype=jnp.int32)[None, :]
    rank = blk[:, None] * MOE_ROWS + local - pick(p_starts)[:, None]
    valid = (rank < pick(counts)[:, None]) & (blk < n_used_s)[:, None]
    pos = jnp.clip(pick(starts)[:, None] + rank, 0, n_assign - 1)
    assign = order[pos]
    tok = jnp.where(valid, assign // MOE_TOP_K, 0)
    dst = jnp.where(valid, (assign % MOE_TOP_K) * n + assign // MOE_TOP_K,
                    n_assign + (blk % 2)[:, None] * MOE_ROWS + local)
    n_valid = jnp.sum(valid.astype(jnp.int32), axis=1)
    rows_now = pl.BlockSpec((1, 1, MOE_ROWS), lambda i, *_: (i, 0, 0), memory_space=pltpu.SMEM)
    rows_next = pl.BlockSpec((1, 1, MOE_ROWS), lambda i, *_: (jnp.minimum(i + 1, n_blocks - 1), 0, 0),
                             memory_space=pltpu.SMEM)
    src3 = tok.reshape(n_blocks, 1, MOE_ROWS)
    return pl.pallas_call(
        _moe_kernel,
        grid_spec=pltpu.PrefetchScalarGridSpec(
            num_scalar_prefetch=3,
            grid=(n_blocks,),
            in_specs=[
                rows_now, rows_next, rows_now,
                pl.BlockSpec(memory_space=pl.ANY),
                pl.BlockSpec((1, 1, D_MODEL, MOE_HIDDEN), lambda i, be, *_: (layer, be[i], 0, 0)),
                pl.BlockSpec((1, 1, D_MODEL, MOE_HIDDEN), lambda i, be, *_: (layer, be[i], 0, 0)),
                pl.BlockSpec((1, 1, MOE_HIDDEN, D_MODEL), lambda i, be, *_: (layer, be[i], 0, 0)),
            ],
            out_specs=pl.BlockSpec(memory_space=pl.ANY),
            scratch_shapes=[pltpu.VMEM((2, MOE_ROWS, D_MODEL), F32), pltpu.VMEM((2, MOE_ROWS, D_MODEL), F32),
                            pltpu.VMEM((D_MODEL, MOE_HIDDEN), BF16), pltpu.VMEM((D_MODEL, MOE_HIDDEN), BF16),
                            pltpu.VMEM((MOE_HIDDEN, D_MODEL), BF16),
                            pltpu.SemaphoreType.DMA((2,)), pltpu.SemaphoreType.DMA((2,))],
        ),
        out_shape=jax.ShapeDtypeStruct((n_assign + 2 * MOE_ROWS, D_MODEL), F32),
        compiler_params=_params("arbitrary"),
        name="moe_experts",
    )(block_expert, n_used, n_valid, src3, src3, dst.reshape(n_blocks, 1, MOE_ROWS), h2, w1, w3, w2)


def _combine_kernel(x1_ref, y0_ref, y1_ref, gate_ref, mod_ref, g_ref, o_ref, *, final):
    f = gate_ref[:, 0:1] * y0_ref[...] + gate_ref[:, 1:2] * y1_ref[...]
    x2 = x1_ref[...] + mod_ref[0, 5:6, :] * f
    if final:
        x2 = _rms(x2) * g_ref[...]
    o_ref[...] = x2


def _combine(n_blocks, x1, y, gate, mod3, g_final, final):
    row = lambda w: pl.BlockSpec((RB, w), lambda i: (i, 0))
    n = n_blocks * RB
    return pl.pallas_call(
        functools.partial(_combine_kernel, final=final),
        grid=(n_blocks,),
        in_specs=[row(D_MODEL), row(D_MODEL), pl.BlockSpec((RB, D_MODEL), lambda i: (i + n_blocks, 0)), row(128),
                  pl.BlockSpec((1, 6, D_MODEL), lambda i: (_mod_row(i), 0, 0)),
                  pl.BlockSpec((1, D_MODEL), lambda i: (0, 0))],
        out_specs=row(D_MODEL),
        out_shape=jax.ShapeDtypeStruct((n, D_MODEL), F32),
        compiler_params=_params("arbitrary"),
        name="moe_combine_final" if final else "moe_combine",
    )(x1, y, y, gate, mod3, g_final)


def _rope_rot_cols(w):
    p = ROPE_PAIRS
    return jnp.concatenate([-w[..., p:2 * p], w[..., 0:p], -w[..., 3 * p:4 * p], w[..., 2 * p:3 * p]], axis=-1)


def _pack_w_in(w_in):
    cuts = np.cumsum([S5_WIDTH, MLA_Q_LORA, MLA_KV_LORA, MLA_ROPE, HG_HEADS * HG_K, HG_HEADS * HG_K,
                      HG_HEADS * HG_K, HG_HEADS * HG_V, HG_HEADS * HG_V, D_MODEL, D_MODEL])
    (w_s5, w_cq, w_ckv, w_kpe, w_hq, w_zf, w_zb, w_hi, w_hg, w_ga, w_gb, w_gc) = jnp.split(w_in, cuts, axis=-1)
    pad = jnp.zeros(w_in.shape[:-1] + (256 - MLA_KV_LORA - 2 * MLA_ROPE,), w_in.dtype)
    packed = jnp.concatenate([w_s5, w_cq, w_ckv, w_kpe, _rope_rot_cols(w_kpe), pad,
                              w_hq, w_zf, w_zb, w_hi, w_hg, w_ga, w_gb, w_gc], axis=-1)
    assert packed.shape[-1] == W_IN_COLS
    return packed.astype(BF16)


def _pack_mla(w_uq, w_uk, w_uv):
    hp, n, r = MLA_HEAD_PAD, MLA_NOPE, MLA_ROPE
    nl = w_uq.shape[0]
    wq = w_uq.reshape(nl, MLA_Q_LORA, MLA_HEADS, n + r)
    zq = jnp.zeros((nl, MLA_Q_LORA, MLA_HEADS, hp - n - r), F32)
    q1 = jnp.concatenate([wq, zq], axis=-1).reshape(nl, MLA_Q_LORA, MLA_HEADS * hp)
    q2 = jnp.concatenate([jnp.zeros((nl, MLA_Q_LORA, MLA_HEADS, n), F32), _rope_rot_cols(wq[..., n:]), zq],
                         axis=-1).reshape(nl, MLA_Q_LORA, MLA_HEADS * hp)
    wq_packed = jnp.concatenate([q1, q2], axis=-1).astype(BF16)

    wk = w_uk.reshape(nl, MLA_KV_LORA, MLA_HEADS, n)
    k_nope = jnp.concatenate([wk, jnp.zeros((nl, MLA_KV_LORA, MLA_HEADS, hp - n), F32)], axis=-1)
    pe_slot = np.concatenate([np.zeros((r, n), np.float32), np.eye(r, dtype=np.float32),
                              np.zeros((r, hp - n - r), np.float32)], axis=-1)
    pe_all = jnp.asarray(np.tile(pe_slot[None, :, None, :], (nl, 1, MLA_HEADS, 1)))
    zero_pe = jnp.zeros_like(pe_all)
    tail = jnp.zeros((nl, 256 - MLA_KV_LORA - 2 * r, MLA_HEADS, hp), F32)
    k1 = jnp.concatenate([k_nope, pe_all, zero_pe, tail], axis=1).reshape(nl, 256, MLA_HEADS * hp)
    k2 = jnp.concatenate([jnp.zeros_like(k_nope), zero_pe, pe_all, tail], axis=1).reshape(nl, 256, MLA_HEADS * hp)
    wk_packed = jnp.concatenate([k1, k2], axis=-1).astype(BF16)
    wv_packed = jnp.concatenate([w_uv, jnp.zeros((nl, 256 - MLA_KV_LORA, MLA_HEADS * MLA_V), F32)],
                                axis=1).astype(BF16)
    return wq_packed, wk_packed, wv_packed


def _rope_tables():
    rows = SEQ // GRID_W
    row = np.repeat(np.arange(rows, dtype=np.float32), GRID_W)
    col = np.tile(np.arange(GRID_W, dtype=np.float32), rows)
    inv = (np.float32(ROPE_BASE) ** (-np.arange(ROPE_PAIRS, dtype=np.float32) / np.float32(ROPE_PAIRS))).astype(np.float32)
    ar, ac = row[:, None] * inv, col[:, None] * inv
    cos = np.concatenate([np.cos(ar), np.cos(ar), np.cos(ac), np.cos(ac)], axis=1)
    sin = np.concatenate([np.sin(ar), np.sin(ar), np.sin(ac), np.sin(ac)], axis=1)
    tail = MLA_HEAD_PAD - MLA_NOPE - MLA_ROPE
    cos_t = np.concatenate([np.ones((SEQ, MLA_NOPE)), cos, np.ones((SEQ, tail))], axis=1)
    sin_t = np.concatenate([np.zeros((SEQ, MLA_NOPE)), sin, np.zeros((SEQ, tail))], axis=1)
    cos_t = np.concatenate([cos_t, np.ones((WB, MLA_HEAD_PAD))], axis=0)
    sin_t = np.concatenate([sin_t, np.zeros((WB, MLA_HEAD_PAD))], axis=0)
    return jnp.asarray(cos_t, F32), jnp.asarray(sin_t, F32)


def _pack_s5(lam_re, lam_im, b_re, b_im, c_re, c_im, log_step):
    eye = jnp.eye(S5_GROUPS, dtype=F32)
    dt = jnp.exp(log_step)[..., None]
    mag = jnp.exp(lam_re * dt)
    lb_re, lb_im = mag * jnp.cos(lam_im * dt), mag * jnp.sin(lam_im * dt)
    den = lam_re * lam_re + lam_im * lam_im
    fr = ((lb_re - 1) * lam_re + lb_im * lam_im) / den
    fi = (lb_im * lam_re - (lb_re - 1) * lam_im) / den
    bb_re = fr[..., None] * b_re - fi[..., None] * b_im
    bb_im = fr[..., None] * b_im + fi[..., None] * b_re
    lead = lam_re.shape[:2]

    def in_mat(bb):
        return jnp.einsum('ldgph,gk->ldghkp', bb, eye).reshape(lead + (S5_WIDTH, S5_LANES))

    def out_mat(cc):
        return jnp.einsum('ldghp,gk->ldgpkh', cc, eye).reshape(lead + (S5_LANES, S5_WIDTH))

    bmat = jnp.concatenate([in_mat(bb_re), in_mat(bb_im)], axis=-1).astype(BF16)
    hl, hw = S5_LANES // 2, S5_WIDTH // 2
    cre, cim = out_mat(c_re), -out_mat(c_im)
    cmat = jnp.stack([jnp.concatenate([cre[..., h * hl:(h + 1) * hl, h * hw:(h + 1) * hw],
                                       cim[..., h * hl:(h + 1) * hl, h * hw:(h + 1) * hw]], axis=-2)
                      for h in range(2)], axis=2).astype(BF16)
    a_tab = jnp.stack([jnp.broadcast_to(lb_re.reshape(lead + (1, S5_LANES)), lead + (BATCH, S5_LANES)),
                       jnp.broadcast_to(lb_im.reshape(lead + (1, S5_LANES)), lead + (BATCH, S5_LANES))], axis=2)
    return bmat, cmat, a_tab


def kernel(x, c, ctx, c_ctx, mod_w, mod_b, norm1_g, norm2_g, w_in, s5_lam_re, s5_lam_im, s5_b_re, s5_b_im, s5_c_re, s5_c_im, s5_log_step, s5_d, s5_glu_w, s5_glu_b, mla_qa_g, mla_kva_g, mla_w_uq, mla_w_uk, mla_w_uv, hg_lb_logits, hg_norm_g, w_pa, w_pb, w_pc, w_out, moe_w_group, moe_b_group, moe_w_expert, moe_b_expert, moe_w1, moe_w3, moe_w2, final_norm_g):
    x_lat, x_ctx, ctx_row0 = x.reshape(N_LAT, D_MODEL), ctx.reshape(N_CTX, D_MODEL), 0
    c_rows = jnp.concatenate([c, c_ctx[None, :], jnp.zeros((16 - BATCH - 1, D_MODEL), F32)], axis=0)
    mod = _modulation(c_rows, mod_w, mod_b).reshape(DEPTH, 16, 6, D_MODEL)

    lb_all = jnp.cumsum(jax.nn.softmax(hg_lb_logits.astype(F32)))
    lb_all = lb_all - lb_all[0]
    hg_scal = jnp.stack([jnp.log(lb_all), jnp.log1p(-lb_all), 1.0 - lb_all], axis=1).reshape(3 * DEPTH).astype(F32)
    cos_t, sin_t = _rope_tables()

    vec = lambda a: a[:, None, :]
    w_in_p = _pack_w_in(w_in)
    bmat, cmat, a_tab = _pack_s5(s5_lam_re, s5_lam_im, s5_b_re, s5_b_im, s5_c_re, s5_c_im, s5_log_step)
    wq, wk, wv = _pack_mla(mla_w_uq, mla_w_uk, mla_w_uv)
    w_route = jnp.concatenate([moe_w_group, jnp.zeros((DEPTH, D_MODEL, 32 - MOE_GROUPS), F32), moe_w_expert,
                               jnp.zeros((DEPTH, D_MODEL, 128 - 32 - MOE_EXPERTS), F32)], axis=-1)
    b_route = jnp.concatenate([moe_b_group, jnp.zeros((DEPTH, 32 - MOE_GROUPS), F32), moe_b_expert,
                               jnp.zeros((DEPTH, 128 - 32 - MOE_EXPERTS), F32)], axis=-1)
    w_route_hi = w_route.astype(BF16)
    w_route_lo = (w_route - w_route_hi.astype(F32)).astype(BF16)
    merge_consts = [vec(s5_d), s5_glu_w.astype(BF16), vec(s5_glu_b), vec(hg_norm_g),
                    w_pa.astype(BF16), w_pb.astype(BF16), w_pc.astype(BF16), w_out.astype(BF16), vec(norm2_g),
                    w_route_hi, w_route_lo, vec(b_route)]
    w_tab_np, mask_tab_np = _hgrn_tables()
    w_tab = jnp.asarray(np.concatenate([w_tab_np, w_tab_np], axis=-1), BF16)
    mask_tab = jnp.asarray(np.kron(np.eye(RB // HG_CHUNK, dtype=np.float32), mask_tab_np), F32)
    head_sum = jnp.asarray(np.kron(np.eye(HG_HEADS), np.ones((HG_V, HG_V))), F32)
    wc_np, maskc_np, vmask_np = _hgrn_fast_tables()
    wc_tab, maskc_tab, vmask_tab = jnp.asarray(wc_np, BF16), jnp.asarray(maskc_np, F32), jnp.asarray(vmask_np, BF16)

    out = None
    for layer in range(DEPTH):
        last = layer == DEPTH - 1
        mod3 = mod[layer]
        (s5u, cqn, kvn, hq, lff, lfb, kkf, kkb, hi, hgs, gates) = _input_projection(
            layer, x_lat, x_ctx, ctx_row0, mod3, hg_scal, vec(norm1_g), w_in_p, vec(mla_qa_g), vec(mla_kva_g))

        yf, yb = [_from_time_major(y) for y in _s5_states(layer, _to_time_major(s5u), bmat, cmat, a_tab)]

        q, k, v = _qkv(layer, cqn, kvn, cos_t, sin_t, wq, wk, wv)
        att_lat = _attention(q, k, v, latent=True)
        att_ctx = None if last else _attention(q, k, v, latent=False)

        of, ob = _hgrn(hq, kkf, kkb, lff, lfb, hi, w_tab, mask_tab, wc_tab, maskc_tab, vmask_tab)

        n_blocks = NB_LAT if last else NB_TOK
        x1, h2, eid, gate = _merge(layer, n_blocks, x_lat, x_ctx, ctx_row0, mod3, yf, yb, s5u, of, ob, hgs,
                                   att_lat, att_ctx, gates, head_sum, merge_consts)

        y_moe = _moe(h2, eid[:, :MOE_TOP_K], moe_w1, moe_w3, moe_w2, layer)
        res = _combine(n_blocks, x1, y_moe, gate, mod3, final_norm_g[None, :], final=last)
        if last:
            out = res
        else:
            x_lat, x_ctx, ctx_row0 = res, res, N_LAT
    return out.reshape(BATCH, SEQ, D_MODEL)
```
